```python
import math
import jax, jax.numpy as jnp
from jax import lax
import numpy as np

D_MODEL = 1024
BATCH = 8
SEQ = 8192
DEPTH = 2

HEAD_DIM = 64
N_HEADS = D_MODEL // HEAD_DIM
N_FOX = N_HEADS // 2
N_SB = N_HEADS - N_FOX
FOX_W = N_FOX * HEAD_DIM
SB_W = N_SB * HEAD_DIM
EVEN_IN = 3 * FOX_W + 3 * SB_W + N_FOX
N_Q = N_HEADS
N_KV = 4
GROUP = N_Q // N_KV
ODD_IN = N_Q * HEAD_DIM + 2 * N_KV * HEAD_DIM
WINDOW = 128
BLOCK_Q = 128
ROPE_THETA = 10000.0
D_FF = ((8 * D_MODEL // 3 + 255) // 256) * 256
PLE_DIM = 256
N_EVEN = (DEPTH + 1) // 2
N_ODD = DEPTH // 2
EPS = 1e-6
NEG_INF = -1e30

kernel_name = "hybrid_fox_stickbreak_swa_sink_block"


def _rmsnorm(x, g):
    xf = x.astype(jnp.float32)
    y = xf * lax.rsqrt(jnp.mean(xf * xf, axis=-1, keepdims=True) + EPS)
    return (y * g.astype(jnp.float32)).astype(x.dtype)


def _rope(x, pos):
    half = x.shape[-1] // 2
    inv = ROPE_THETA ** (-jnp.arange(half, dtype=jnp.float32) / half)
    ang = pos.astype(jnp.float32)[..., None] * inv
    cos = jnp.cos(ang)[:, :, None, :]
    sin = jnp.sin(ang)[:, :, None, :]
    xf = x.astype(jnp.float32)
    x1, x2 = xf[..., :half], xf[..., half:]
    out = jnp.concatenate([x1 * cos - x2 * sin, x2 * cos + x1 * sin], axis=-1)
    return out.astype(x.dtype)


def _forgetting_attention(q, k, v, log_f):
    S, d = q.shape[1], q.shape[-1]
    scale = d ** -0.5
    cum = jnp.cumsum(log_f, axis=1).transpose(0, 2, 1)
    outs = []
    for i in range(S // BLOCK_Q):
        q0, q1 = i * BLOCK_Q, (i + 1) * BLOCK_Q
        s = jnp.einsum('bqhd,bkhd->bhqk', q[:, q0:q1], k[:, :q1],
                       preferred_element_type=jnp.float32) * scale
        s = s + cum[:, :, q0:q1, None] - cum[:, :, None, :q1]
        causal = jnp.arange(q1)[None, :] <= jnp.arange(q0, q1)[:, None]
        s = jnp.where(causal, s, NEG_INF)
        w = jax.nn.softmax(s, axis=-1)
        outs.append(jnp.einsum('bhqk,bkhd->bqhd', w.astype(v.dtype), v[:, :q1]))
    return jnp.concatenate(outs, axis=1)


def _stick_breaking_attention(q, k, v):
    S, d = q.shape[1], q.shape[-1]
    scale = d ** -0.5
    outs = []
    for i in range(S // BLOCK_Q):
        q0, q1 = i * BLOCK_Q, (i + 1) * BLOCK_Q
        z = jnp.einsum('bqhd,bkhd->bhqk', q[:, q0:q1], k[:, :q1],
                       preferred_element_type=jnp.float32) * scale
        strict = jnp.arange(q1)[None, :] < jnp.arange(q0, q1)[:, None]
        log_1mb = jnp.where(strict, jax.nn.log_sigmoid(-z), 0.0)
        suffix = lax.cumsum(log_1mb, axis=3, reverse=True) - log_1mb
        a = jnp.where(strict, jnp.exp(jax.nn.log_sigmoid(z) + suffix), 0.0)
        outs.append(jnp.einsum('bhqk,bkhd->bqhd', a.astype(v.dtype), v[:, :q1]))
    return jnp.concatenate(outs, axis=1)


def _sliding_window_sink_attention(q, k, v, sinks):
    B, S, _, d = q.shape
    nb = S // WINDOW
    scale = d ** -0.5
    qb = q.reshape(B, nb, WINDOW, N_KV, GROUP, d)

    def band(x):
        xb = x.reshape(B, nb, WINDOW, N_KV, d)
        prev = jnp.concatenate([jnp.zeros_like(xb[:, :1]), xb[:, :-1]], axis=1)
        return jnp.concatenate([prev, xb], axis=2)

    kb, vb = band(k), band(v)
    s = jnp.einsum('bnqhgd,bnkhd->bnhgqk', qb, kb,
                   preferred_element_type=jnp.float32) * scale
    qi = jnp.arange(WINDOW)[:, None]
    kj = jnp.arange(2 * WINDOW)[None, :]
    rel = qi + WINDOW - kj
    valid = (rel >= 0) & (rel < WINDOW)
    first = (jnp.arange(nb)[:, None, None] == 0) & (kj[None] < WINDOW)
    mask = valid[None] & ~first
    s = jnp.where(mask[None, :, None, None], s, NEG_INF)
    sink = jnp.broadcast_to(
        sinks.astype(jnp.float32).reshape(N_KV, GROUP)[None, None, :, :, None, None],
        s.shape[:-1] + (1,))
    w = jax.nn.softmax(jnp.concatenate([s, sink], axis=-1), axis=-1)[..., :-1]
    o = jnp.einsum('bnhgqk,bnkhd->bnqhgd', w.astype(v.dtype), vb)
    return o.reshape(B, S, N_Q, d)


def _swiglu(x, w_gate, w_up, w_down):
    return (jax.nn.silu(x @ w_gate) * (x @ w_up)) @ w_down


def _fwd_setup_inputs(seed: int = 0) -> dict:
    key = jax.random.key(seed)
    ks = jax.random.split(key, 20)
    f32 = jnp.float32

    def w(k, shape, fan_in):
        return jax.random.normal(k, shape, f32) * fan_in ** -0.5

    def gain(k, shape):
        return 1.0 + 0.02 * jax.random.normal(k, shape, f32)

    return {
        "x": jax.random.normal(ks[0], (BATCH, SEQ, D_MODEL), f32),
        "p": jax.random.normal(ks[1], (DEPTH, BATCH, SEQ, PLE_DIM), f32),
        "positions": jnp.broadcast_to(jnp.arange(SEQ, dtype=jnp.int32), (BATCH, SEQ)),
        "norm_mix": gain(ks[2], (DEPTH, D_MODEL)),
        "norm_ffn": gain(ks[3], (DEPTH, D_MODEL)),
        "norm_ple": gain(ks[4], (DEPTH, D_MODEL)),
        "norm_final": gain(ks[5], (D_MODEL,)),
        "ev_w_in": w(ks[6], (N_EVEN, D_MODEL, EVEN_IN), D_MODEL),
        "ev_b_f": 0.1 * jax.random.normal(ks[7], (N_EVEN, N_FOX), f32),
        "ev_w_out": w(ks[8], (N_EVEN, FOX_W + SB_W, D_MODEL), FOX_W + SB_W),
        "od_w_in": w(ks[9], (N_ODD, D_MODEL, ODD_IN), D_MODEL),
        "od_sinks": 0.5 * jax.random.normal(ks[10], (N_ODD, N_Q), f32),
        "od_w_out": w(ks[11], (N_ODD, N_Q * HEAD_DIM, D_MODEL), N_Q * HEAD_DIM),
        "ffn_w_gate": w(ks[12], (DEPTH, D_MODEL, D_FF), D_MODEL),
        "ffn_w_up": w(ks[13], (DEPTH, D_MODEL, D_FF), D_MODEL),
        "ffn_w_down": w(ks[14], (DEPTH, D_FF, D_MODEL), D_FF),
        "ple_w_proj": w(ks[15], (DEPTH, PLE_DIM, D_MODEL), PLE_DIM),
        "ple_w_gate": w(ks[16], (DEPTH, D_MODEL, D_MODEL), D_MODEL),
    }


def _fwd_reference(x, p, positions, norm_mix, norm_ffn, norm_ple, norm_final,
              ev_w_in, ev_b_f, ev_w_out, od_w_in, od_sinks, od_w_out,
              ffn_w_gate, ffn_w_up, ffn_w_down, ple_w_proj, ple_w_gate):
    B, S, _ = x.shape
    h = x
    for i in range(DEPTH):
        hn = _rmsnorm(h, norm_mix[i])
        if i % 2 == 0:
            j = i // 2
            proj = hn @ ev_w_in[j]
            c = [0, FOX_W, 2 * FOX_W, 3 * FOX_W,
                 3 * FOX_W + SB_W, 3 * FOX_W + 2 * SB_W, 3 * FOX_W + 3 * SB_W]
            qa = proj[..., c[0]:c[1]].reshape(B, S, N_FOX, HEAD_DIM)
            ka = proj[..., c[1]:c[2]].reshape(B, S, N_FOX, HEAD_DIM)
            va = proj[..., c[2]:c[3]].reshape(B, S, N_FOX, HEAD_DIM)
            qs = proj[..., c[3]:c[4]].reshape(B, S, N_SB, HEAD_DIM)
            ks_ = proj[..., c[4]:c[5]].reshape(B, S, N_SB, HEAD_DIM)
            vs = proj[..., c[5]:c[6]].reshape(B, S, N_SB, HEAD_DIM)
            log_f = jax.nn.log_sigmoid(
                (proj[..., c[6]:] + ev_b_f[j]).astype(jnp.float32))
            o_fox = _forgetting_attention(qa, ka, va, log_f)
            o_sb = _stick_breaking_attention(qs, ks_, vs)
            mix = jnp.concatenate([o_fox.reshape(B, S, FOX_W),
                                   o_sb.reshape(B, S, SB_W)], axis=-1) @ ev_w_out[j]
        else:
            j = i // 2
            proj = hn @ od_w_in[j]
            qw, kw = N_Q * HEAD_DIM, N_KV * HEAD_DIM
            q = _rope(proj[..., :qw].reshape(B, S, N_Q, HEAD_DIM), positions)
            k = _rope(proj[..., qw:qw + kw].reshape(B, S, N_KV, HEAD_DIM), positions)
            v = proj[..., qw + kw:].reshape(B, S, N_KV, HEAD_DIM)
            o = _sliding_window_sink_attention(q, k, v, od_sinks[j])
            mix = o.reshape(B, S, qw) @ od_w_out[j]
        h = h + mix
        h = h + _swiglu(_rmsnorm(h, norm_ffn[i]), ffn_w_gate[i], ffn_w_up[i], ffn_w_down[i])
        gate = jax.nn.sigmoid(_rmsnorm(h, norm_ple[i]) @ ple_w_gate[i])
        h = h + gate * (p[i] @ ple_w_proj[i])
    return _rmsnorm(h, norm_final)


import jax as _jax
import jax.numpy as _jnp

TWIN_FORMAT = 'train_step'
FWD_PARAMS = ['x', 'p', 'positions', 'norm_mix', 'norm_ffn', 'norm_ple', 'norm_final', 'ev_w_in', 'ev_b_f', 'ev_w_out', 'od_w_in', 'od_sinks', 'od_w_out', 'ffn_w_gate', 'ffn_w_up', 'ffn_w_down', 'ple_w_proj', 'ple_w_gate']
TWIN_WEIGHTS = ['norm_mix', 'norm_ffn', 'norm_ple', 'norm_final', 'ev_w_in', 'ev_b_f', 'ev_w_out', 'od_w_in', 'od_sinks', 'od_w_out', 'ffn_w_gate', 'ffn_w_up', 'ffn_w_down', 'ple_w_proj', 'ple_w_gate']
TWIN_DIFF_INPUT = 'x'
TWIN_INPUTS = ['x', 'p', 'positions', 'norm_mix', 'norm_ffn', 'norm_ple', 'norm_final', 'ev_w_in', 'ev_b_f', 'ev_w_out', 'od_w_in', 'od_sinks', 'od_w_out', 'ffn_w_gate', 'ffn_w_up', 'ffn_w_down', 'ple_w_proj', 'ple_w_gate', 'loss_target', 'm_norm_mix', 'm_norm_ffn', 'm_norm_ple', 'm_norm_final', 'm_ev_w_in', 'm_ev_b_f', 'm_ev_w_out', 'm_od_w_in', 'm_od_sinks', 'm_od_w_out', 'm_ffn_w_gate', 'm_ffn_w_up', 'm_ffn_w_down', 'm_ple_w_proj', 'm_ple_w_gate', 'v_norm_mix', 'v_norm_ffn', 'v_norm_ple', 'v_norm_final', 'v_ev_w_in', 'v_ev_b_f', 'v_ev_w_out', 'v_od_w_in', 'v_od_sinks', 'v_od_w_out', 'v_ffn_w_gate', 'v_ffn_w_up', 'v_ffn_w_down', 'v_ple_w_proj', 'v_ple_w_gate']
TWIN_OUTPUTS = ['loss', 'grad_x', 'grad_norm_mix', 'grad_norm_ffn', 'grad_norm_ple', 'grad_norm_final', 'grad_ev_w_in', 'grad_ev_b_f', 'grad_ev_w_out', 'grad_od_w_in', 'grad_od_sinks', 'grad_od_w_out', 'grad_ffn_w_gate', 'grad_ffn_w_up', 'grad_ffn_w_down', 'grad_ple_w_proj', 'grad_ple_w_gate', 'delta_norm_mix', 'delta_norm_ffn', 'delta_norm_ple', 'delta_norm_final', 'delta_ev_w_in', 'delta_ev_b_f', 'delta_ev_w_out', 'delta_od_w_in', 'delta_od_sinks', 'delta_od_w_out', 'delta_ffn_w_gate', 'delta_ffn_w_up', 'delta_ffn_w_down', 'delta_ple_w_proj', 'delta_ple_w_gate', 'new_m_norm_mix', 'new_m_norm_ffn', 'new_m_norm_ple', 'new_m_norm_final', 'new_m_ev_w_in', 'new_m_ev_b_f', 'new_m_ev_w_out', 'new_m_od_w_in', 'new_m_od_sinks', 'new_m_od_w_out', 'new_m_ffn_w_gate', 'new_m_ffn_w_up', 'new_m_ffn_w_down', 'new_m_ple_w_proj', 'new_m_ple_w_gate', 'new_v_norm_mix', 'new_v_norm_ffn', 'new_v_norm_ple', 'new_v_norm_final', 'new_v_ev_w_in', 'new_v_ev_b_f', 'new_v_ev_w_out', 'new_v_od_w_in', 'new_v_od_sinks', 'new_v_od_w_out', 'new_v_ffn_w_gate', 'new_v_ffn_w_up', 'new_v_ffn_w_down', 'new_v_ple_w_proj', 'new_v_ple_w_gate']
TWIN_LEAF_KINDS = {'loss': 'loss', 'grad_x': 'grad_x', 'grad_norm_mix': 'grad_w', 'grad_norm_ffn': 'grad_w', 'grad_norm_ple': 'grad_w', 'grad_norm_final': 'grad_w', 'grad_ev_w_in': 'grad_w', 'grad_ev_b_f': 'grad_w', 'grad_ev_w_out': 'grad_w', 'grad_od_w_in': 'grad_w', 'grad_od_sinks': 'grad_w', 'grad_od_w_out': 'grad_w', 'grad_ffn_w_gate': 'grad_w', 'grad_ffn_w_up': 'grad_w', 'grad_ffn_w_down': 'grad_w', 'grad_ple_w_proj': 'grad_w', 'grad_ple_w_gate': 'grad_w', 'delta_norm_mix': 'delta_w', 'delta_norm_ffn': 'delta_w', 'delta_norm_ple': 'delta_w', 'delta_norm_final': 'delta_w', 'delta_ev_w_in': 'delta_w', 'delta_ev_b_f': 'delta_w', 'delta_ev_w_out': 'delta_w', 'delta_od_w_in': 'delta_w', 'delta_od_sinks': 'delta_w', 'delta_od_w_out': 'delta_w', 'delta_ffn_w_gate': 'delta_w', 'delta_ffn_w_up': 'delta_w', 'delta_ffn_w_down': 'delta_w', 'delta_ple_w_proj': 'delta_w', 'delta_ple_w_gate': 'delta_w', 'new_m_norm_mix': 'new_m', 'new_m_norm_ffn': 'new_m', 'new_m_norm_ple': 'new_m', 'new_m_norm_final': 'new_m', 'new_m_ev_w_in': 'new_m', 'new_m_ev_b_f': 'new_m', 'new_m_ev_w_out': 'new_m', 'new_m_od_w_in': 'new_m', 'new_m_od_sinks': 'new_m', 'new_m_od_w_out': 'new_m', 'new_m_ffn_w_gate': 'new_m', 'new_m_ffn_w_up': 'new_m', 'new_m_ffn_w_down': 'new_m', 'new_m_ple_w_proj': 'new_m', 'new_m_ple_w_gate': 'new_m', 'new_v_norm_mix': 'new_v', 'new_v_norm_ffn': 'new_v', 'new_v_norm_ple': 'new_v', 'new_v_norm_final': 'new_v', 'new_v_ev_w_in': 'new_v', 'new_v_ev_b_f': 'new_v', 'new_v_ev_w_out': 'new_v', 'new_v_od_w_in': 'new_v', 'new_v_od_sinks': 'new_v', 'new_v_od_w_out': 'new_v', 'new_v_ffn_w_gate': 'new_v', 'new_v_ffn_w_up': 'new_v', 'new_v_ffn_w_down': 'new_v', 'new_v_ple_w_proj': 'new_v', 'new_v_ple_w_gate': 'new_v'}


def _forward(args):
    return _fwd_reference(*[args[k] for k in FWD_PARAMS])


def _output_shape():
    def fwd():
        inp = _fwd_setup_inputs(0)
        return _fwd_reference(*[inp[k] for k in FWD_PARAMS])
    out = _jax.eval_shape(fwd)
    return out.shape, out.dtype

N_MICROBATCH = 1
ADAM_LR = 0.001
ADAM_B1 = 0.9
ADAM_B2 = 0.999
ADAM_EPS = 1e-08
ADAM_WD = 0.01
ADAM_STEP = 10
PER_EXAMPLE_BATCH_AXIS = {'x': 0, 'p': 1, 'positions': 0, 'loss_target': 0}
SHARED_INPUTS = []
_WEIGHT_DTYPES = {'norm_mix': _jnp.float32, 'norm_ffn': _jnp.float32, 'norm_ple': _jnp.float32, 'norm_final': _jnp.float32, 'ev_w_in': _jnp.float32, 'ev_b_f': _jnp.float32, 'ev_w_out': _jnp.float32, 'od_w_in': _jnp.float32, 'od_sinks': _jnp.float32, 'od_w_out': _jnp.float32, 'ffn_w_gate': _jnp.float32, 'ffn_w_up': _jnp.float32, 'ffn_w_down': _jnp.float32, 'ple_w_proj': _jnp.float32, 'ple_w_gate': _jnp.float32}
MOMENT_SCALE = {'norm_mix': 1.394625e-01, 'norm_ffn': 1.547620e-01, 'norm_ple': 3.545097e-02, 'norm_final': 6.397168e+01, 'ev_w_in': 1.057386e-01, 'ev_b_f': 9.168691e-01, 'ev_w_out': 1.483917e-01, 'od_w_in': 4.719088e-02, 'od_sinks': 2.485423e-02, 'od_w_out': 3.766741e-02, 'ffn_w_gate': 6.227042e-02, 'ffn_w_up': 6.032993e-02, 'ffn_w_down': 9.996962e-02, 'ple_w_proj': 9.026315e-02, 'ple_w_gate': 3.518646e-02}


def _to_microbatches(a, axis):
    t = _jnp.moveaxis(a, axis, 0)
    t = t.reshape((N_MICROBATCH, t.shape[0] // N_MICROBATCH) + t.shape[1:])
    return _jnp.moveaxis(t, 1, axis + 1)


def setup_inputs(seed: int = 0) -> dict:
    inp = _fwd_setup_inputs(seed)
    key = _jax.random.fold_in(_jax.random.key(seed), 7919)
    shape, _ = _output_shape()
    out = dict(inp)
    out["loss_target"] = _jax.random.normal(_jax.random.fold_in(key, 0), shape, _jnp.float32)
    for i, name in enumerate(TWIN_WEIGHTS):
        w = inp[name].astype(_jnp.float32)
        if MOMENT_SCALE is None:
            s = _jnp.sqrt(_jnp.mean(_jnp.square(w)) + 1e-30)
        else:
            s = MOMENT_SCALE[name]
        km, kv = _jax.random.split(_jax.random.fold_in(key, i + 1))
        out[name] = w
        out["m_" + name] = s * _jax.random.normal(km, w.shape, _jnp.float32)
        out["v_" + name] = (s * s) * _jax.random.uniform(kv, w.shape, _jnp.float32, 0.5, 1.5)
    if N_MICROBATCH > 1:
        for name, axis in PER_EXAMPLE_BATCH_AXIS.items():
            out[name] = _to_microbatches(out[name], axis)
    return {'x': out['x'], 'p': out['p'], 'positions': out['positions'], 'norm_mix': out['norm_mix'], 'norm_ffn': out['norm_ffn'], 'norm_ple': out['norm_ple'], 'norm_final': out['norm_final'], 'ev_w_in': out['ev_w_in'], 'ev_b_f': out['ev_b_f'], 'ev_w_out': out['ev_w_out'], 'od_w_in': out['od_w_in'], 'od_sinks': out['od_sinks'], 'od_w_out': out['od_w_out'], 'ffn_w_gate': out['ffn_w_gate'], 'ffn_w_up': out['ffn_w_up'], 'ffn_w_down': out['ffn_w_down'], 'ple_w_proj': out['ple_w_proj'], 'ple_w_gate': out['ple_w_gate'], 'loss_target': out['loss_target'], 'm_norm_mix': out['m_norm_mix'], 'm_norm_ffn': out['m_norm_ffn'], 'm_norm_ple': out['m_norm_ple'], 'm_norm_final': out['m_norm_final'], 'm_ev_w_in': out['m_ev_w_in'], 'm_ev_b_f': out['m_ev_b_f'], 'm_ev_w_out': out['m_ev_w_out'], 'm_od_w_in': out['m_od_w_in'], 'm_od_sinks': out['m_od_sinks'], 'm_od_w_out': out['m_od_w_out'], 'm_ffn_w_gate': out['m_ffn_w_gate'], 'm_ffn_w_up': out['m_ffn_w_up'], 'm_ffn_w_down': out['m_ffn_w_down'], 'm_ple_w_proj': out['m_ple_w_proj'], 'm_ple_w_gate': out['m_ple_w_gate'], 'v_norm_mix': out['v_norm_mix'], 'v_norm_ffn': out['v_norm_ffn'], 'v_norm_ple': out['v_norm_ple'], 'v_norm_final': out['v_norm_final'], 'v_ev_w_in': out['v_ev_w_in'], 'v_ev_b_f': out['v_ev_b_f'], 'v_ev_w_out': out['v_ev_w_out'], 'v_od_w_in': out['v_od_w_in'], 'v_od_sinks': out['v_od_sinks'], 'v_od_w_out': out['v_od_w_out'], 'v_ffn_w_gate': out['v_ffn_w_gate'], 'v_ffn_w_up': out['v_ffn_w_up'], 'v_ffn_w_down': out['v_ffn_w_down'], 'v_ple_w_proj': out['v_ple_w_proj'], 'v_ple_w_gate': out['v_ple_w_gate']}


def _loss(weights, diff, rest, loss_target):
    with _jax.named_scope("forward"):
        args = {**rest, TWIN_DIFF_INPUT: diff, **{k: w.astype(_WEIGHT_DTYPES[k]) for k, w in weights.items()}}
        y = _forward(args)
    with _jax.named_scope("loss_head"):
        err = _jnp.square(y.astype(_jnp.float32) - loss_target)
        return 0.5 * _jnp.sum(_jnp.mean(err, axis=-1)) if err.ndim else 0.5 * err


def _adamw(w, g, m, v):
    m = ADAM_B1 * m + (1.0 - ADAM_B1) * g
    v = ADAM_B2 * v + (1.0 - ADAM_B2) * _jnp.square(g)
    m_hat = m / (1.0 - ADAM_B1 ** ADAM_STEP)
    v_hat = v / (1.0 - ADAM_B2 ** ADAM_STEP)
    delta = -ADAM_LR * (m_hat / (_jnp.sqrt(v_hat) + ADAM_EPS) + ADAM_WD * w)
    return delta, m, v


def reference(x, p, positions, norm_mix, norm_ffn, norm_ple, norm_final, ev_w_in, ev_b_f, ev_w_out, od_w_in, od_sinks, od_w_out, ffn_w_gate, ffn_w_up, ffn_w_down, ple_w_proj, ple_w_gate, loss_target, m_norm_mix, m_norm_ffn, m_norm_ple, m_norm_final, m_ev_w_in, m_ev_b_f, m_ev_w_out, m_od_w_in, m_od_sinks, m_od_w_out, m_ffn_w_gate, m_ffn_w_up, m_ffn_w_down, m_ple_w_proj, m_ple_w_gate, v_norm_mix, v_norm_ffn, v_norm_ple, v_norm_final, v_ev_w_in, v_ev_b_f, v_ev_w_out, v_od_w_in, v_od_sinks, v_od_w_out, v_ffn_w_gate, v_ffn_w_up, v_ffn_w_down, v_ple_w_proj, v_ple_w_gate):
    given = dict(x=x, p=p, positions=positions, norm_mix=norm_mix, norm_ffn=norm_ffn, norm_ple=norm_ple, norm_final=norm_final, ev_w_in=ev_w_in, ev_b_f=ev_b_f, ev_w_out=ev_w_out, od_w_in=od_w_in, od_sinks=od_sinks, od_w_out=od_w_out, ffn_w_gate=ffn_w_gate, ffn_w_up=ffn_w_up, ffn_w_down=ffn_w_down, ple_w_proj=ple_w_proj, ple_w_gate=ple_w_gate, loss_target=loss_target, m_norm_mix=m_norm_mix, m_norm_ffn=m_norm_ffn, m_norm_ple=m_norm_ple, m_norm_final=m_norm_final, m_ev_w_in=m_ev_w_in, m_ev_b_f=m_ev_b_f, m_ev_w_out=m_ev_w_out, m_od_w_in=m_od_w_in, m_od_sinks=m_od_sinks, m_od_w_out=m_od_w_out, m_ffn_w_gate=m_ffn_w_gate, m_ffn_w_up=m_ffn_w_up, m_ffn_w_down=m_ffn_w_down, m_ple_w_proj=m_ple_w_proj, m_ple_w_gate=m_ple_w_gate, v_norm_mix=v_norm_mix, v_norm_ffn=v_norm_ffn, v_norm_ple=v_norm_ple, v_norm_final=v_norm_final, v_ev_w_in=v_ev_w_in, v_ev_b_f=v_ev_b_f, v_ev_w_out=v_ev_w_out, v_od_w_in=v_od_w_in, v_od_sinks=v_od_sinks, v_od_w_out=v_od_w_out, v_ffn_w_gate=v_ffn_w_gate, v_ffn_w_up=v_ffn_w_up, v_ffn_w_down=v_ffn_w_down, v_ple_w_proj=v_ple_w_proj, v_ple_w_gate=v_ple_w_gate)
    weights = {n: given[n] for n in TWIN_WEIGHTS}
    shared = {n: given[n] for n in SHARED_INPUTS}
    per_example = {n: given[n] for n in ['x', 'p', 'positions']}
    grad_fn = _jax.value_and_grad(_loss, argnums=(0, 1))

    def one_microbatch(ex, loss_target):
        ex = dict(ex)
        diff = ex.pop(TWIN_DIFF_INPUT)
        return grad_fn(weights, diff, {**shared, **ex}, loss_target)

    if N_MICROBATCH == 1:
        loss, (grad_w, grad_x) = one_microbatch(per_example, given["loss_target"])
    else:
        def body(carry, xs):
            loss_sum, grad_sum = carry
            l_k, (gw_k, gx_k) = one_microbatch(xs[0], xs[1])
            with _jax.named_scope("update"):
                return (loss_sum + l_k, _jax.tree.map(_jnp.add, grad_sum, gw_k)), gx_k

        init = (_jnp.zeros((), _jnp.float32), _jax.tree.map(_jnp.zeros_like, weights))
        (loss, grad_w), grad_x = _jax.lax.scan(body, init, (per_example, given["loss_target"]))
    with _jax.named_scope("update"):
        delta_w, new_m, new_v = {}, {}, {}
        for n in TWIN_WEIGHTS:
            delta_w[n], new_m[n], new_v[n] = _adamw(weights[n], grad_w[n], given["m_" + n], given["v_" + n])
    return (loss, grad_x, *[grad_w[n] for n in TWIN_WEIGHTS], *[delta_w[n] for n in TWIN_WEIGHTS],
            *[new_m[n] for n in TWIN_WEIGHTS], *[new_v[n] for n in TWIN_WEIGHTS])
```

```python
import functools

import jax
import jax.numpy as jnp
from jax import lax
from jax.experimental import pallas as pl
from jax.experimental.pallas import tpu as pltpu

F32 = jnp.float32
BF16 = jnp.bfloat16

D_MODEL = 1024
HEAD_DIM = 64
N_FOX = 8
N_SB = 8
FOX_W = N_FOX * HEAD_DIM
SB_W = N_SB * HEAD_DIM
QKV_W = 3 * FOX_W + 3 * SB_W
EVEN_IN = QKV_W + N_FOX
N_Q = 16
N_KV = 4
GROUP = N_Q // N_KV
ODD_IN = N_Q * HEAD_DIM + 2 * N_KV * HEAD_DIM
WINDOW = 128
ROPE_THETA = 10000.0
D_FF = 2816
PLE_DIM = 256
EPS = 1e-6
NEG_INF = -1e30
SCALE = HEAD_DIM ** -0.5

ADAM_LR = 0.001
ADAM_B1 = 0.9
ADAM_B2 = 0.999
ADAM_EPS = 1e-08
ADAM_WD = 0.01
ADAM_STEP = 10

N_DEV = 8
LANES = 128
FLAT_COLS = 1024
FLAT_ROW_ALIGN = 256
ATTN_BLOCK = 256

MESH = pl.DeviceIdType.MESH

_WSPEC = (
    ("ev_w_in", (1, 1024, 385), 2),
    ("ev_w_out", (1, 128, 1024), 1),
    ("od_w_in", (1, 1024, 192), 2),
    ("od_w_out", (1, 128, 1024), 1),
    ("ffn_w_gate", (2, 1024, 352), 2),
    ("ffn_w_up", (2, 1024, 352), 2),
    ("ffn_w_down", (2, 352, 1024), 1),
    ("ple_w_proj", (2, 256, 128), 2),
    ("ple_w_gate", (2, 128, 1024), 1),
)


def _size(shape):
    n = 1
    for s in shape:
        n *= s
    return n


_FLAT_N = sum(_size(s) for _, s, _ in _WSPEC)
_FLAT_ROWS = -(-_FLAT_N // (FLAT_COLS * FLAT_ROW_ALIGN)) * FLAT_ROW_ALIGN


def _pick(n, prefs):
    for t in prefs:
        if n % t == 0:
            return t
    return n


def _pack_shards(shards, dtype):
    flat = jnp.concatenate([s.reshape(-1).astype(dtype) for s in shards])
    flat = jnp.pad(flat, (0, _FLAT_ROWS * FLAT_COLS - _FLAT_N))
    return flat.reshape(_FLAT_ROWS, FLAT_COLS)


def _unpack_shards(flat):
    flat = flat.reshape(-1)
    out, off = [], 0
    for _, shape, _ in _WSPEC:
        n = _size(shape)
        out.append(flat[off:off + n].reshape(shape))
        off += n
    return out


def _unpack_gathered(g):
    g = g.reshape(N_DEV, -1)
    out, off = {}, 0
    for name, shape, axis in _WSPEC:
        n = _size(shape)
        blk = g[:, off:off + n].reshape((N_DEV,) + shape)
        off += n
        l, r, c = shape
        if axis == 2:
            out[name] = blk.transpose(1, 2, 0, 3).reshape(l, r, N_DEV * c)
        else:
            out[name] = blk.transpose(1, 0, 2, 3).reshape(l, N_DEV * r, c)
    return out


def _pack_full_grads(grads):
    parts = []
    for name, shape, axis in _WSPEC:
        l, r, c = shape
        gfull = grads[name]
        if axis == 2:
            blk = gfull.reshape(l, r, N_DEV, c).transpose(2, 0, 1, 3)
        else:
            blk = gfull.reshape(l, N_DEV, r, c).transpose(1, 0, 2, 3)
        parts.append(blk.reshape(N_DEV, -1))
    flat = jnp.concatenate(parts, axis=1)
    flat = jnp.pad(flat, ((0, 0), (0, _FLAT_ROWS * FLAT_COLS - _FLAT_N)))
    return flat.reshape(N_DEV, _FLAT_ROWS, FLAT_COLS)


_ANY = pl.BlockSpec(memory_space=pl.ANY)


def _all_gather_weights(shard):
    rows, cols = shard.shape

    def body(x_ref, out_ref, send_sems, recv_sems, local_sem):
        x, y, c = lax.axis_index("x"), lax.axis_index("y"), lax.axis_index("c")
        me, sibling = (x, y, c), (x, y, 1 - c)
        chips = [(1 - x, y), (x, 1 - y), (1 - x, 1 - y)]

        def slot(px, py, pc):
            return out_ref.at[4 * px + 2 * py + pc]

        def copy(k, block, to, src=None):
            return pltpu.make_async_remote_copy(
                src_ref=slot(*block) if src is None else src, dst_ref=slot(*block),
                send_sem=send_sems.at[k], recv_sem=recv_sems.at[k],
                device_id=to, device_id_type=MESH)

        mine = pltpu.make_async_copy(x_ref, slot(*me), local_sem)
        mine.start()
        first = [copy(0, me, sibling, src=x_ref)]
        first += [copy(1 + j, me, (*chip, c), src=x_ref) for j, chip in enumerate(chips)]
        for cp in first:
            cp.start()
        passed = [copy(4 + j, (*chip, c), sibling) for j, chip in enumerate(chips)]
        for j, chip in enumerate(chips):
            copy(1 + j, (*chip, c), me).wait_recv()
            passed[j].start()
        copy(0, sibling, me).wait_recv()
        for j, chip in enumerate(chips):
            copy(4 + j, (*chip, 1 - c), me).wait_recv()
        for cp in first + passed:
            cp.wait_send()
        mine.wait()

    return pl.pallas_call(
        body, name="ag_weights",
        out_shape=jax.ShapeDtypeStruct((N_DEV, rows, cols), shard.dtype),
        in_specs=[_ANY], out_specs=_ANY,
        scratch_shapes=[pltpu.SemaphoreType.DMA((7,)), pltpu.SemaphoreType.DMA((7,)), pltpu.SemaphoreType.DMA(())],
    )(shard)


def _rs_sibling_exchange(gp):
    _, rows, cols = gp.shape

    def body(g_ref, out_ref, send_sems, recv_sems):
        x, y, c = lax.axis_index("x"), lax.axis_index("y"), lax.axis_index("c")
        sibling = (x, y, 1 - c)
        copies = []
        for k in range(4):
            copies.append(pltpu.make_async_remote_copy(
                src_ref=g_ref.at[2 * k + (1 - c)], dst_ref=out_ref.at[k],
                send_sem=send_sems.at[k], recv_sem=recv_sems.at[k],
                device_id=sibling, device_id_type=MESH))
        for cp in copies:
            cp.start()
        for cp in copies:
            cp.wait_recv()
        for cp in copies:
            cp.wait_send()

    return pl.pallas_call(
        body, name="rs_sibling_exchange",
        out_shape=jax.ShapeDtypeStruct((4, rows, cols), gp.dtype),
        in_specs=[_ANY], out_specs=_ANY,
        scratch_shapes=[pltpu.SemaphoreType.DMA((4,)), pltpu.SemaphoreType.DMA((4,))],
    )(gp)


def _rs_chip_sum(core, gp, recv):
    _, rows, cols = gp.shape
    tr = FLAT_ROW_ALIGN

    def body(core_ref, a_ref, b_ref, o_ref):
        o_ref[...] = (a_ref[...] + b_ref[...]).astype(BF16)

    return pl.pallas_call(
        body, name="rs_chip_sum",
        out_shape=jax.ShapeDtypeStruct((4, rows, cols), BF16),
        grid_spec=pltpu.PrefetchScalarGridSpec(
            num_scalar_prefetch=1, grid=(4, rows // tr),
            in_specs=[pl.BlockSpec((1, tr, cols), lambda k, r, cr: (2 * k + cr[0], r, 0)),
                      pl.BlockSpec((1, tr, cols), lambda k, r, cr: (k, r, 0))],
            out_specs=pl.BlockSpec((1, tr, cols), lambda k, r, cr: (k, r, 0))),
    )(core, gp, recv)


def _rs_chip_exchange(part):
    _, rows, cols = part.shape

    def body(p_ref, out_ref, send_sems, recv_sems, local_sem):
        x, y, c = lax.axis_index("x"), lax.axis_index("y"), lax.axis_index("c")
        my_chip = 2 * x + y
        mine = pltpu.make_async_copy(p_ref.at[my_chip], out_ref.at[my_chip], local_sem)
        mine.start()
        copies = []
        for j, (px, py) in enumerate([(1 - x, y), (x, 1 - y), (1 - x, 1 - y)]):
            copies.append(pltpu.make_async_remote_copy(
                src_ref=p_ref.at[2 * px + py], dst_ref=out_ref.at[my_chip],
                send_sem=send_sems.at[j], recv_sem=recv_sems.at[j],
                device_id=(px, py, c), device_id_type=MESH))
        for cp in copies:
            cp.start()
        for cp in copies:
            cp.wait_recv()
        for cp in copies:
            cp.wait_send()
        mine.wait()

    return pl.pallas_call(
        body, name="rs_chip_exchange",
        out_shape=jax.ShapeDtypeStruct((4, rows, cols), part.dtype),
        in_specs=[_ANY], out_specs=_ANY,
        scratch_shapes=[pltpu.SemaphoreType.DMA((3,)), pltpu.SemaphoreType.DMA((3,)), pltpu.SemaphoreType.DMA(())],
    )(part)


def _adamw(w, g, m, v):
    m = ADAM_B1 * m + (1.0 - ADAM_B1) * g
    v = ADAM_B2 * v + (1.0 - ADAM_B2) * (g * g)
    m_hat = m / (1.0 - ADAM_B1 ** ADAM_STEP)
    v_hat = v / (1.0 - ADAM_B2 ** ADAM_STEP)
    delta = -ADAM_LR * (m_hat / (jnp.sqrt(v_hat) + ADAM_EPS) + ADAM_WD * w)
    return delta, m, v


def _rs_sum_adamw(recv, w, m, v):
    _, rows, cols = recv.shape
    tr = FLAT_ROW_ALIGN

    def body(r_ref, w_ref, m_ref, v_ref, g_out, d_out, m_out, v_out):
        g = r_ref[0].astype(F32)
        for k in range(1, 4):
            g = g + r_ref[k].astype(F32)
        delta, m_new, v_new = _adamw(w_ref[...], g, m_ref[...], v_ref[...])
        g_out[...] = g
        d_out[...] = delta
        m_out[...] = m_new
        v_out[...] = v_new

    flat = pl.BlockSpec((tr, cols), lambda r: (r, 0))
    shp = jax.ShapeDtypeStruct((rows, cols), F32)
    return pl.pallas_call(
        body, name="rs_sum_adamw", grid=(rows // tr,),
        out_shape=(shp, shp, shp, shp),
        in_specs=[pl.BlockSpec((4, tr, cols), lambda r: (0, r, 0)), flat, flat, flat],
        out_specs=(flat, flat, flat, flat),
    )(recv, w, m, v)


def _small_allreduce_adamw(vec, w, m, v):
    rows, cols = vec.shape

    def body(x_ref, w_ref, m_ref, v_ref, g_out, d_out, m_out, v_out, gather, send_sems, recv_sems):
        x, y, c = lax.axis_index("x"), lax.axis_index("y"), lax.axis_index("c")
        me = 4 * x + 2 * y + c
        copies = []
        for d in range(1, N_DEV):
            dx, dy, dc = (d >> 2) & 1, (d >> 1) & 1, d & 1
            peer = (x ^ dx if dx else x, y ^ dy if dy else y, c ^ dc if dc else c)
            copies.append(pltpu.make_async_remote_copy(
                src_ref=x_ref, dst_ref=gather.at[me],
                send_sem=send_sems.at[d - 1], recv_sem=recv_sems.at[d - 1],
                device_id=peer, device_id_type=MESH))
        for cp in copies:
            cp.start()
        gather[me] = x_ref[...]
        for cp in copies:
            cp.wait_recv()
        for cp in copies:
            cp.wait_send()
        g = gather[0]
        for k in range(1, N_DEV):
            g = g + gather[k]
        delta, m_new, v_new = _adamw(w_ref[...], g, m_ref[...], v_ref[...])
        g_out[...] = g
        d_out[...] = delta
        m_out[...] = m_new
        v_out[...] = v_new

    vm = pl.BlockSpec(memory_space=pltpu.VMEM)
    shp = jax.ShapeDtypeStruct((rows, cols), F32)
    return pl.pallas_call(
        body, name="small_allreduce_adamw",
        out_shape=(shp, shp, shp, shp),
        in_specs=[vm, vm, vm, vm], out_specs=(vm, vm, vm, vm),
        scratch_shapes=[pltpu.VMEM((N_DEV, rows, cols), F32),
                        pltpu.SemaphoreType.DMA((N_DEV - 1,)), pltpu.SemaphoreType.DMA((N_DEV - 1,))],
    )(vec, w, m, v)


def _mm(a, b, *, ta=False, tb=False, out_dtype=BF16, res=None, name):
    if ta:
        kdim, m = a.shape
    else:
        m, kdim = a.shape
    if tb:
        n, kb = b.shape
    else:
        kb, n = b.shape
    assert kdim == kb, (a.shape, b.shape, ta, tb)
    tm = _pick(m, (512, 256, 128))
    tn = _pick(n, (512, 256, 128))
    tk = _pick(kdim, (1024, 1408, 512, 256, 128))
    nk = kdim // tk
    dn = (((0 if ta else 1,), (1 if tb else 0,)), ((), ()))
    has_res = res is not None

    def body(*refs):
        if has_res:
            a_ref, b_ref, r_ref, o_ref, acc = refs
        else:
            a_ref, b_ref, o_ref, acc = refs
        k = pl.program_id(2)

        @pl.when(k == 0)
        def _():
            acc[...] = jnp.zeros_like(acc)

        acc[...] += lax.dot_general(a_ref[...], b_ref[...], dn, preferred_element_type=F32)

        @pl.when(k == nk - 1)
        def _():
            r = acc[...]
            if has_res:
                r = r + r_ref[...].astype(F32)
            o_ref[...] = r.astype(out_dtype)

    a_spec = (pl.BlockSpec((tk, tm), lambda i, j, k: (k, i)) if ta
              else pl.BlockSpec((tm, tk), lambda i, j, k: (i, k)))
    b_spec = (pl.BlockSpec((tn, tk), lambda i, j, k: (j, k)) if tb
              else pl.BlockSpec((tk, tn), lambda i, j, k: (k, j)))
    o_spec = pl.BlockSpec((tm, tn), lambda i, j, k: (i, j))
    in_specs = [a_spec, b_spec] + ([o_spec] if has_res else [])
    args = (a, b) + ((res,) if has_res else ())
    return pl.pallas_call(
        body, name=name, grid=(m // tm, n // tn, nk),
        out_shape=jax.ShapeDtypeStruct((m, n), out_dtype),
        in_specs=in_specs, out_specs=o_spec,
        scratch_shapes=[pltpu.VMEM((tm, tn), F32)],
        compiler_params=pltpu.CompilerParams(dimension_semantics=("parallel", "parallel", "arbitrary")),
    )(*args)


def _row_tile(s):
    return _pick(s, (256, 128))


def _rms_fwd(h, g, name):
    s, d = h.shape
    ts = _row_tile(s)

    def body(h_ref, g_ref, o_ref):
        x = h_ref[...]
        r = lax.rsqrt(jnp.mean(x * x, axis=-1, keepdims=True) + EPS)
        o_ref[...] = ((x * r) * g_ref[...]).astype(BF16)

    return pl.pallas_call(
        body, name=name, grid=(s // ts,),
        out_shape=jax.ShapeDtypeStruct((s, d), BF16),
        in_specs=[pl.BlockSpec((ts, d), lambda i: (i, 0)), pl.BlockSpec((1, d), lambda i: (0, 0))],
        out_specs=pl.BlockSpec((ts, d), lambda i: (i, 0)),
    )(h, g)


def _rms_bwd(h, g, dhns, dres, name, want_bf16):
    s, d = h.shape
    ts = _row_tile(s)
    n_in = len(dhns)

    def body(*refs):
        h_ref, g_ref, r_ref = refs[:3]
        dy_refs = refs[3:3 + n_in]
        outs = refs[3 + n_in:]
        dh_ref, dg_ref = outs[0], outs[-1]
        i = pl.program_id(0)
        x = h_ref[...]
        dy = dy_refs[0][...].astype(F32)
        for extra in dy_refs[1:]:
            dy = dy + extra[...].astype(F32)
        r = lax.rsqrt(jnp.mean(x * x, axis=-1, keepdims=True) + EPS)
        xr = x * r
        u = dy * g_ref[...]
        dx = r * (u - xr * jnp.mean(xr * u, axis=-1, keepdims=True))
        dh = r_ref[...] + dx
        dh_ref[...] = dh
        if want_bf16:
            outs[1][...] = dh.astype(BF16)

        @pl.when(i == 0)
        def _():
            dg_ref[...] = jnp.zeros_like(dg_ref)

        dg_ref[...] += jnp.sum(dy * xr, axis=0, keepdims=True)

    row = pl.BlockSpec((ts, d), lambda i: (i, 0))
    vec = pl.BlockSpec((1, d), lambda i: (0, 0))
    out_shape = [jax.ShapeDtypeStruct((s, d), F32)]
    out_specs = [row]
    if want_bf16:
        out_shape.append(jax.ShapeDtypeStruct((s, d), BF16))
        out_specs.append(row)
    out_shape.append(jax.ShapeDtypeStruct((1, d), F32))
    out_specs.append(vec)
    return pl.pallas_call(
        body, name=name, grid=(s // ts,),
        out_shape=tuple(out_shape),
        in_specs=[row, vec, row] + [row] * n_in, out_specs=tuple(out_specs),
        compiler_params=pltpu.CompilerParams(dimension_semantics=("arbitrary",)),
    )(h, g, dres, *dhns)


def _sigmoid_parts(z):
    e = jnp.exp(-jnp.abs(z))
    r = 1.0 / (1.0 + e)
    er = e * r
    pos = z >= 0
    return jnp.where(pos, r, er), jnp.where(pos, er, r)


def _swiglu_fwd(ab, name):
    s, two_f = ab.shape
    f = two_f // 2
    ts = _row_tile(s)

    def body(ab_ref, o_ref):
        a = ab_ref[:, :f].astype(F32)
        b = ab_ref[:, f:].astype(F32)
        sg, _ = _sigmoid_parts(a)
        o_ref[...] = ((a * sg) * b).astype(BF16)

    return pl.pallas_call(
        body, name=name, grid=(s // ts,),
        out_shape=jax.ShapeDtypeStruct((s, f), BF16),
        in_specs=[pl.BlockSpec((ts, two_f), lambda i: (i, 0))],
        out_specs=pl.BlockSpec((ts, f), lambda i: (i, 0)),
    )(ab)


def _swiglu_bwd(ab, du, name):
    s, two_f = ab.shape
    f = two_f // 2
    ts = _row_tile(s)

    def body(ab_ref, du_ref, o_ref):
        a = ab_ref[:, :f].astype(F32)
        b = ab_ref[:, f:].astype(F32)
        g = du_ref[...].astype(F32)
        sg, sgm = _sigmoid_parts(a)
        silu = a * sg
        o_ref[:, :f] = (g * b * (sg + silu * sgm)).astype(BF16)
        o_ref[:, f:] = (g * silu).astype(BF16)

    return pl.pallas_call(
        body, name=name, grid=(s // ts,),
        out_shape=jax.ShapeDtypeStruct((s, two_f), BF16),
        in_specs=[pl.BlockSpec((ts, two_f), lambda i: (i, 0)), pl.BlockSpec((ts, f), lambda i: (i, 0))],
        out_specs=pl.BlockSpec((ts, two_f), lambda i: (i, 0)),
    )(ab, du)


def _ple_fwd(h, gl, pp, name):
    s, d = h.shape
    ts = _row_tile(s)

    def body(h_ref, gl_ref, pp_ref, o_ref):
        sg, _ = _sigmoid_parts(gl_ref[...].astype(F32))
        o_ref[...] = h_ref[...] + sg * pp_ref[...].astype(F32)

    row = pl.BlockSpec((ts, d), lambda i: (i, 0))
    return pl.pallas_call(
        body, name=name, grid=(s // ts,),
        out_shape=jax.ShapeDtypeStruct((s, d), F32),
        in_specs=[row, row, row], out_specs=row,
    )(h, gl, pp)


def _ple_bwd(dh, gl, pp, name):
    s, d = dh.shape
    ts = _row_tile(s)

    def body(dh_ref, gl_ref, pp_ref, dgl_ref, dpp_ref):
        g = dh_ref[...]
        sg, sgm = _sigmoid_parts(gl_ref[...].astype(F32))
        dpp_ref[...] = (g * sg).astype(BF16)
        dgl_ref[...] = (g * pp_ref[...].astype(F32) * (sg * sgm)).astype(BF16)

    row = pl.BlockSpec((ts, d), lambda i: (i, 0))
    shp = jax.ShapeDtypeStruct((s, d), BF16)
    return pl.pallas_call(
        body, name=name, grid=(s // ts,),
        out_shape=(shp, shp), in_specs=[row, row, row], out_specs=(row, row),
    )(dh, gl, pp)


def _final_norm_loss(h, g, target):
    s, d = h.shape
    ts = _row_tile(s)

    def body(h_ref, g_ref, t_ref, loss_ref, dh_ref, dg_ref):
        i = pl.program_id(0)
        x = h_ref[...]
        gain = g_ref[...]
        r = lax.rsqrt(jnp.mean(x * x, axis=-1, keepdims=True) + EPS)
        xr = x * r
        err = xr * gain - t_ref[...]
        dy = err * (1.0 / d)
        u = dy * gain
        dh_ref[...] = r * (u - xr * jnp.mean(xr * u, axis=-1, keepdims=True))

        @pl.when(i == 0)
        def _():
            dg_ref[...] = jnp.zeros_like(dg_ref)
            loss_ref[...] = jnp.zeros_like(loss_ref)

        dg_ref[...] += jnp.sum(dy * xr, axis=0, keepdims=True)
        tok = jnp.mean(err * err, axis=-1, keepdims=True)
        loss_ref[...] += 0.5 * jnp.sum(tok, axis=0, keepdims=True)

    row = pl.BlockSpec((ts, d), lambda i: (i, 0))
    vec = pl.BlockSpec((1, d), lambda i: (0, 0))
    return pl.pallas_call(
        body, name="final_norm_loss", grid=(s // ts,),
        out_shape=(jax.ShapeDtypeStruct((1, LANES), F32), jax.ShapeDtypeStruct((s, d), F32),
                   jax.ShapeDtypeStruct((1, d), F32)),
        in_specs=[row, vec, row],
        out_specs=(pl.BlockSpec((1, LANES), lambda i: (0, 0)), row, vec),
        compiler_params=pltpu.CompilerParams(dimension_semantics=("arbitrary",)),
    )(h, g, target)


def _rope(xin, cos, sin_signed, sign, name):
    s, w = xin.shape
    ts = _row_tile(s)

    def body(x_ref, c_ref, s_ref, o_ref):
        x = x_ref[...].astype(F32)
        lane = lax.broadcasted_iota(jnp.int32, x.shape, 1)
        low = (lane & (HEAD_DIM - 1)) < (HEAD_DIM // 2)
        swapped = jnp.where(low, pltpu.roll(x, LANES - HEAD_DIM // 2, 1), pltpu.roll(x, HEAD_DIM // 2, 1))
        o_ref[...] = (x * c_ref[...] + sign * (swapped * s_ref[...])).astype(BF16)

    blk = pl.BlockSpec((ts, LANES), lambda i, j: (i, j))
    tab = pl.BlockSpec((ts, LANES), lambda i, j: (i, 0))
    return pl.pallas_call(
        body, name=name, grid=(s // ts, w // LANES),
        out_shape=jax.ShapeDtypeStruct((s, w), BF16),
        in_specs=[blk, tab, tab], out_specs=blk,
    )(xin, cos, sin_signed)


def _fgate_fwd(flog, bias):
    s, w = flog.shape

    def body(x_ref, b_ref, o_ref):
        rowi = lax.broadcasted_iota(jnp.int32, (8, w), 0)
        b = b_ref[...]

        def step(g, carry):
            sl = pl.ds(pl.multiple_of(g * 8, 8), 8)
            x = x_ref[sl, :] + b
            lf = jnp.minimum(x, 0.0) - jnp.log1p(jnp.exp(-jnp.abs(x)))
            for sh in (1, 2, 4):
                lf = lf + jnp.where(rowi >= sh, pltpu.roll(lf, sh, 0), 0.0)
            out = lf + carry
            o_ref[sl, :] = out
            return jnp.broadcast_to(out[7:8, :], (8, w))

        lax.fori_loop(0, s // 8, step, jnp.zeros((8, w), F32))

    vm = pl.BlockSpec(memory_space=pltpu.VMEM)
    return pl.pallas_call(
        body, name="fgate_fwd", out_shape=jax.ShapeDtypeStruct((s, w), F32),
        in_specs=[vm, vm], out_specs=vm,
    )(flog, bias)


def _fgate_bwd(gcum_k, gcum_q, flog, bias):
    s, w = flog.shape

    def body(g_ref, g2_ref, x_ref, b_ref, o_ref, db_ref):
        rowi = lax.broadcasted_iota(jnp.int32, (8, w), 0)
        lane = lax.broadcasted_iota(jnp.int32, (8, w), 1)
        b = b_ref[...]

        def step(t, carry):
            run, dbsum = carry
            g = s // 8 - 1 - t
            sl = pl.ds(pl.multiple_of(g * 8, 8), 8)
            c = g_ref[sl, :] + g2_ref[sl, :]
            for sh in (1, 2, 4):
                c = c + jnp.where(rowi < 8 - sh, pltpu.roll(c, 8 - sh, 0), 0.0)
            c = c + run
            _, sgm = _sigmoid_parts(x_ref[sl, :] + b)
            dl = jnp.where(lane < N_FOX, c * sgm, 0.0)
            o_ref[sl, :] = dl.astype(BF16)
            return jnp.broadcast_to(c[0:1, :], (8, w)), dbsum + dl

        _, dbsum = lax.fori_loop(0, s // 8, step, (jnp.zeros((8, w), F32), jnp.zeros((8, w), F32)))
        db_ref[...] = jnp.sum(dbsum, axis=0, keepdims=True)

    vm = pl.BlockSpec(memory_space=pltpu.VMEM)
    return pl.pallas_call(
        body, name="fgate_bwd",
        out_shape=(jax.ShapeDtypeStruct((s, w), BF16), jax.ShapeDtypeStruct((1, w), F32)),
        in_specs=[vm, vm, vm, vm], out_specs=(vm, vm),
    )(gcum_k, gcum_q, flog, bias)


_DN_NT = (((1,), (1,)), ((), ()))
_DN_TN = (((0,), (0,)), ((), ()))


def _head_mask(shape, hh):
    lane = lax.broadcasted_iota(jnp.int32, shape, 1)
    return (lane >= HEAD_DIM * hh) & (lane < HEAD_DIM * (hh + 1))


def _fox_fwd(proj, ccol, crow):
    s = proj.shape[0]
    blk = min(ATTN_BLOCK, s)
    nq = s // blk
    npair = N_FOX // 2

    def body(q_ref, k_ref, v_ref, cc_ref, cr_ref, o_ref, a_ref):
        p_, i = pl.program_id(0), pl.program_id(1)
        q2 = q_ref[...].astype(F32) * SCALE
        row = lax.broadcasted_iota(jnp.int32, (blk, blk), 0)
        col = lax.broadcasted_iota(jnp.int32, (blk, blk), 1)
        outs = []
        for hh in range(2):
            qh = jnp.where(_head_mask(q2.shape, hh), q2, 0.0).astype(BF16)
            ct = cc_ref[hh][:, 0:1]

            def tile(j, carry, masked, qh=qh, ct=ct, hh=hh):
                m, l, acc = carry
                sl = pl.ds(pl.multiple_of(j * blk, blk), blk)
                kb, vb = k_ref[sl, :], v_ref[sl, :]
                sc = lax.dot_general(qh, kb, _DN_NT, preferred_element_type=F32)
                sc = sc + (ct - cr_ref[2 * p_ + hh, j])
                if masked:
                    sc = jnp.where(col <= row, sc, NEG_INF)
                m_new = jnp.maximum(m, jnp.max(sc, axis=1, keepdims=True))
                alpha = jnp.exp(m - m_new)
                pm = jnp.exp(sc - m_new)
                l = alpha * l + jnp.sum(pm, axis=1, keepdims=True)
                acc = alpha * acc + jnp.dot(pm.astype(BF16), vb, preferred_element_type=F32)
                return m_new, l, acc

            init = (jnp.full((blk, 1), NEG_INF, F32), jnp.zeros((blk, 1), F32), jnp.zeros((blk, LANES), F32))
            carry = tile(i, init, True)
            m, l, acc = lax.fori_loop(0, i, lambda t, c, tile=tile: tile(i - 1 - t, c, False), carry)
            outs.append(acc / l)
            a_ref[hh] = jnp.broadcast_to(ct - (m + jnp.log(l)), (blk, LANES))
        o_ref[...] = jnp.where(_head_mask(outs[0].shape, 0), outs[0], outs[1]).astype(BF16)

    seq = lambda base: pl.BlockSpec((s, LANES), lambda p, i: (0, base + p))
    return pl.pallas_call(
        body, name="fox_fwd", grid=(npair, nq),
        out_shape=(jax.ShapeDtypeStruct((s, FOX_W), BF16), jax.ShapeDtypeStruct((N_FOX, s, LANES), F32)),
        in_specs=[pl.BlockSpec((blk, LANES), lambda p, i: (i, p)), seq(npair), seq(2 * npair),
                  pl.BlockSpec((2, blk, LANES), lambda p, i: (p, i, 0)),
                  pl.BlockSpec((N_FOX, nq, 1, blk), lambda p, i: (0, 0, 0, 0))],
        out_specs=(pl.BlockSpec((blk, LANES), lambda p, i: (i, p)),
                   pl.BlockSpec((2, blk, LANES), lambda p, i: (p, i, 0))),
        compiler_params=pltpu.CompilerParams(dimension_semantics=("parallel", "arbitrary")),
    )(proj, proj, proj, ccol, crow)


def _fox_bwd(proj, do, o, acol, crow):
    s = proj.shape[0]
    blk = min(ATTN_BLOCK, s)
    nq = s // blk
    npair = N_FOX // 2

    def body(q_ref, k_ref, v_ref, do_ref, o_ref, a_ref, cr_ref, dq_ref, dk_ref, dv_ref, gc_ref, gr_ref, dk_acc, dv_acc):
        p_, i = pl.program_id(0), pl.program_id(1)

        @pl.when(i == 0)
        def _():
            dk_acc[...] = jnp.zeros_like(dk_acc)
            dv_acc[...] = jnp.zeros_like(dv_acc)
            gc_ref[...] = jnp.zeros_like(gc_ref)

        q2 = q_ref[...].astype(F32) * SCALE
        do2 = do_ref[...]
        prod = do2.astype(F32) * o_ref[...].astype(F32)
        row = lax.broadcasted_iota(jnp.int32, (blk, blk), 0)
        col = lax.broadcasted_iota(jnp.int32, (blk, blk), 1)
        dqs = []
        for hh in range(2):
            hm = _head_mask(q2.shape, hh)
            qh = jnp.where(hm, q2, 0.0).astype(BF16)
            doh = jnp.where(hm, do2, jnp.zeros_like(do2))
            delta = jnp.sum(jnp.where(hm, prod, 0.0), axis=1, keepdims=True)
            at = a_ref[hh][:, 0:1]

            def tile(j, carry, masked, qh=qh, doh=doh, delta=delta, at=at, hh=hh):
                dq, rs = carry
                sl = pl.ds(pl.multiple_of(j * blk, blk), blk)
                kb, vb = k_ref[sl, :], v_ref[sl, :]
                sc = lax.dot_general(qh, kb, _DN_NT, preferred_element_type=F32)
                sc = sc + (at - cr_ref[2 * p_ + hh, j])
                if masked:
                    sc = jnp.where(col <= row, sc, NEG_INF)
                pm = jnp.exp(sc)
                dp = lax.dot_general(doh, vb, _DN_NT, preferred_element_type=F32)
                ds = pm * (dp - delta)
                dsb = ds.astype(BF16)
                dk_acc[sl, :] += lax.dot_general(dsb, qh, _DN_TN, preferred_element_type=F32)
                dv_acc[sl, :] += lax.dot_general(pm.astype(BF16), doh, _DN_TN, preferred_element_type=F32)
                gc_ref[hh, j] += -jnp.sum(ds, axis=0, keepdims=True)
                return dq + jnp.dot(dsb, kb, preferred_element_type=F32), rs + jnp.sum(ds, axis=1, keepdims=True)

            carry = tile(i, (jnp.zeros((blk, LANES), F32), jnp.zeros((blk, 1), F32)), True)
            dq, rs = lax.fori_loop(0, i, lambda t, c, tile=tile: tile(i - 1 - t, c, False), carry)
            dqs.append(dq)
            gr_ref[hh] = jnp.broadcast_to(rs, (blk, LANES))
        dq_ref[...] = (jnp.where(_head_mask(dqs[0].shape, 0), dqs[0], dqs[1]) * SCALE).astype(BF16)

        @pl.when(i == nq - 1)
        def _():
            dk_ref[...] = dk_acc[...].astype(BF16)
            dv_ref[...] = dv_acc[...].astype(BF16)

    seq = lambda base: pl.BlockSpec((s, LANES), lambda p, i: (0, base + p))
    qblk = lambda base: pl.BlockSpec((blk, LANES), lambda p, i: (i, base + p))
    rep = pl.BlockSpec((2, blk, LANES), lambda p, i: (p, i, 0))
    half = jax.ShapeDtypeStruct((s, FOX_W), BF16)
    return pl.pallas_call(
        body, name="fox_bwd", grid=(npair, nq),
        out_shape=(half, half, half, jax.ShapeDtypeStruct((N_FOX, nq, 1, blk), F32),
                   jax.ShapeDtypeStruct((N_FOX, s, LANES), F32)),
        in_specs=[qblk(0), seq(npair), seq(2 * npair), qblk(0), qblk(0), rep,
                  pl.BlockSpec((N_FOX, nq, 1, blk), lambda p, i: (0, 0, 0, 0))],
        out_specs=(qblk(0), seq(0), seq(0), pl.BlockSpec((2, nq, 1, blk), lambda p, i: (p, 0, 0, 0)), rep),
        scratch_shapes=[pltpu.VMEM((s, LANES), F32), pltpu.VMEM((s, LANES), F32)],
        compiler_params=pltpu.CompilerParams(dimension_semantics=("parallel", "arbitrary")),
    )(proj, proj, proj, do, o, acol, crow)


def _sb_logs(z):
    neg = -(jnp.maximum(z, 0.0) + jnp.log1p(jnp.exp(-jnp.abs(z))))
    return neg, z + neg


def _split_dot(x, tri):
    hi = x.astype(BF16)
    lo = (x - hi.astype(F32)).astype(BF16)
    return jnp.dot(hi, tri, preferred_element_type=F32) + jnp.dot(lo, tri, preferred_element_type=F32)


def _sb_fwd(proj):
    s = proj.shape[0]
    blk = min(ATTN_BLOCK, s)
    nq = s // blk
    npair = N_SB // 2
    base = 3 * (N_FOX // 2)

    def body(q_ref, k_ref, v_ref, o_ref, r_ref):
        i = pl.program_id(1)
        q2 = q_ref[...].astype(F32) * SCALE
        row = lax.broadcasted_iota(jnp.int32, (blk, blk), 0)
        col = lax.broadcasted_iota(jnp.int32, (blk, blk), 1)
        strict = col < row
        tri = jnp.where(row > col, 1.0, 0.0).astype(BF16)
        lane = lax.broadcasted_iota(jnp.int32, (blk, LANES), 1)
        outs = []
        for hh in range(2):
            qh = jnp.where(_head_mask(q2.shape, hh), q2, 0.0).astype(BF16)

            def tile(j, carry, masked, qh=qh):
                rsum, acc, rbuf = carry
                sl = pl.ds(pl.multiple_of(j * blk, blk), blk)
                kb, vb = k_ref[sl, :], v_ref[sl, :]
                z = lax.dot_general(qh, kb, _DN_NT, preferred_element_type=F32)
                l1m, lb = _sb_logs(z)
                if masked:
                    l1m = jnp.where(strict, l1m, 0.0)
                sx = _split_dot(l1m, tri)
                a = jnp.exp(lb + sx + rsum)
                if masked:
                    a = jnp.where(strict, a, 0.0)
                acc = acc + jnp.dot(a.astype(BF16), vb, preferred_element_type=F32)
                rbuf = jnp.where(lane == j, rsum, rbuf)
                return rsum + jnp.sum(l1m, axis=1, keepdims=True), acc, rbuf

            init = (jnp.zeros((blk, 1), F32), jnp.zeros((blk, LANES), F32), jnp.zeros((blk, LANES), F32))
            carry = tile(i, init, True)
            _, acc, rbuf = lax.fori_loop(0, i, lambda t, c, tile=tile: tile(i - 1 - t, c, False), carry)
            outs.append(acc)
            r_ref[hh] = rbuf
        o_ref[...] = jnp.where(_head_mask(outs[0].shape, 0), outs[0], outs[1]).astype(BF16)

    seq = lambda b: pl.BlockSpec((s, LANES), lambda p, i: (0, b + p))
    return pl.pallas_call(
        body, name="sb_fwd", grid=(npair, nq),
        out_shape=(jax.ShapeDtypeStruct((s, SB_W), BF16), jax.ShapeDtypeStruct((N_SB, s, LANES), F32)),
        in_specs=[pl.BlockSpec((blk, LANES), lambda p, i: (i, base + p)), seq(base + npair), seq(base + 2 * npair)],
        out_specs=(pl.BlockSpec((blk, LANES), lambda p, i: (i, p)),
                   pl.BlockSpec((2, blk, LANES), lambda p, i: (p, i, 0))),
        compiler_params=pltpu.CompilerParams(dimension_semantics=("parallel", "arbitrary")),
    )(proj, proj, proj)


def _sb_bwd(proj, do, rsave):
    s = proj.shape[0]
    blk = min(ATTN_BLOCK, s)
    nq = s // blk
    npair = N_SB // 2
    base = 3 * (N_FOX // 2)

    def body(q_ref, k_ref, v_ref, do_ref, r_ref, dq_ref, dk_ref, dv_ref, dk_acc, dv_acc):
        i = pl.program_id(1)

        @pl.when(i == 0)
        def _():
            dk_acc[...] = jnp.zeros_like(dk_acc)
            dv_acc[...] = jnp.zeros_like(dv_acc)

        q2 = q_ref[...].astype(F32) * SCALE
        do2 = do_ref[...]
        row = lax.broadcasted_iota(jnp.int32, (blk, blk), 0)
        col = lax.broadcasted_iota(jnp.int32, (blk, blk), 1)
        strict = col < row
        tri_suffix = jnp.where(row > col, 1.0, 0.0).astype(BF16)
        tri_prefix = jnp.where(row < col, 1.0, 0.0).astype(BF16)
        lane = lax.broadcasted_iota(jnp.int32, (blk, LANES), 1)
        dqs = []
        for hh in range(2):
            hm = _head_mask(q2.shape, hh)
            qh = jnp.where(hm, q2, 0.0).astype(BF16)
            doh = jnp.where(hm, do2, jnp.zeros_like(do2))
            rbuf = r_ref[hh]

            def tile(j, carry, masked, qh=qh, doh=doh, rbuf=rbuf):
                pre, dq = carry
                sl = pl.ds(pl.multiple_of(j * blk, blk), blk)
                kb, vb = k_ref[sl, :], v_ref[sl, :]
                z = lax.dot_general(qh, kb, _DN_NT, preferred_element_type=F32)
                l1m, lb = _sb_logs(z)
                beta, one_m_beta = _sigmoid_parts(z)
                if masked:
                    l1m = jnp.where(strict, l1m, 0.0)
                sx = _split_dot(l1m, tri_suffix)
                rj = jnp.sum(jnp.where(lane == j, rbuf, 0.0), axis=1, keepdims=True)
                a = jnp.exp(lb + sx + rj)
                if masked:
                    a = jnp.where(strict, a, 0.0)
                da = lax.dot_general(doh, vb, _DN_NT, preferred_element_type=F32)
                g = a * da
                px = _split_dot(g, tri_prefix) + pre
                dz = g * one_m_beta - beta * px
                if masked:
                    dz = jnp.where(strict, dz, 0.0)
                dzb = dz.astype(BF16)
                dk_acc[sl, :] += lax.dot_general(dzb, qh, _DN_TN, preferred_element_type=F32)
                dv_acc[sl, :] += lax.dot_general(a.astype(BF16), doh, _DN_TN, preferred_element_type=F32)
                return pre + jnp.sum(g, axis=1, keepdims=True), dq + jnp.dot(dzb, kb, preferred_element_type=F32)

            carry = (jnp.zeros((blk, 1), F32), jnp.zeros((blk, LANES), F32))
            carry = lax.fori_loop(0, i, lambda t, c, tile=tile: tile(t, c, False), carry)
            _, dq = tile(i, carry, True)
            dqs.append(dq)
        dq_ref[...] = (jnp.where(_head_mask(dqs[0].shape, 0), dqs[0], dqs[1]) * SCALE).astype(BF16)

        @pl.when(i == nq - 1)
        def _():
            dk_ref[...] = dk_acc[...].astype(BF16)
            dv_ref[...] = dv_acc[...].astype(BF16)

    seq = lambda b: pl.BlockSpec((s, LANES), lambda p, i: (0, b + p))
    qblk = lambda b: pl.BlockSpec((blk, LANES), lambda p, i: (i, b + p))
    half = jax.ShapeDtypeStruct((s, SB_W), BF16)
    return pl.pallas_call(
        body, name="sb_bwd", grid=(npair, nq),
        out_shape=(half, half, half),
        in_specs=[qblk(base), seq(base + npair), seq(base + 2 * npair), qblk(npair),
                  pl.BlockSpec((2, blk, LANES), lambda p, i: (p, i, 0))],
        out_specs=(qblk(0), seq(0), seq(0)),
        scratch_shapes=[pltpu.VMEM((s, LANES), F32), pltpu.VMEM((s, LANES), F32)],
        compiler_params=pltpu.CompilerParams(dimension_semantics=("parallel", "arbitrary")),
    )(proj, proj, proj, do, rsave)


def _swa_scores(q, kp, kc, first):
    w = q.shape[0]
    row = lax.broadcasted_iota(jnp.int32, (w, w), 0)
    col = lax.broadcasted_iota(jnp.int32, (w, w), 1)
    sp = lax.dot_general(q, kp, _DN_NT, preferred_element_type=F32)
    sc = lax.dot_general(q, kc, _DN_NT, preferred_element_type=F32)
    sp = jnp.where((col > row) & jnp.logical_not(first), sp, NEG_INF)
    sc = jnp.where(col <= row, sc, NEG_INF)
    return sp, sc


def _scaled(q_ref_val):
    return (q_ref_val.astype(F32) * SCALE).astype(BF16)


def _swa_fwd(q, k, v, sinks):
    nh, s, hd = q.shape
    w = WINDOW
    nb = s // w

    def body(q_ref, kp_ref, kc_ref, vp_ref, vc_ref, s_ref, o_ref, lse_ref):
        i = pl.program_id(1)
        sink = s_ref[0][:, 0:1]
        sp, sc = _swa_scores(_scaled(q_ref[0]), kp_ref[0], kc_ref[0], i == 0)
        m = jnp.maximum(jnp.maximum(jnp.max(sp, axis=1, keepdims=True), jnp.max(sc, axis=1, keepdims=True)), sink)
        ep, ec = jnp.exp(sp - m), jnp.exp(sc - m)
        l = jnp.sum(ep, axis=1, keepdims=True) + jnp.sum(ec, axis=1, keepdims=True) + jnp.exp(sink - m)
        acc = (jnp.dot(ep.astype(BF16), vp_ref[0], preferred_element_type=F32)
               + jnp.dot(ec.astype(BF16), vc_ref[0], preferred_element_type=F32))
        o_ref[0] = (acc / l).astype(BF16)
        lse_ref[0] = jnp.broadcast_to(m + jnp.log(l), (w, LANES))

    qs = pl.BlockSpec((1, w, hd), lambda h, i: (h, i, 0))
    prev = pl.BlockSpec((1, w, hd), lambda h, i: (h // GROUP, jnp.maximum(i - 1, 0), 0))
    cur = pl.BlockSpec((1, w, hd), lambda h, i: (h // GROUP, i, 0))
    return pl.pallas_call(
        body, name="swa_fwd", grid=(nh, nb),
        out_shape=(jax.ShapeDtypeStruct((nh, s, hd), BF16), jax.ShapeDtypeStruct((nh, s, LANES), F32)),
        in_specs=[qs, prev, cur, prev, cur, pl.BlockSpec((1, 1, LANES), lambda h, i: (h, 0, 0))],
        out_specs=(qs, pl.BlockSpec((1, w, LANES), lambda h, i: (h, i, 0))),
    )(q, k, k, v, v, sinks)


def _swa_bwd_dq(q, k, v, sinks, do, o, lse):
    nh, s, hd = q.shape
    w = WINDOW
    nb = s // w

    def body(q_ref, kp_ref, kc_ref, vp_ref, vc_ref, s_ref, do_ref, o_ref, lse_ref, dq_ref, dsink_ref):
        i = pl.program_id(1)
        sink = s_ref[0][:, 0:1]
        lse = lse_ref[0][:, 0:1]
        kp, kc = kp_ref[0], kc_ref[0]
        sp, sc = _swa_scores(_scaled(q_ref[0]), kp, kc, i == 0)
        pp, pc = jnp.exp(sp - lse), jnp.exp(sc - lse)
        dov = do_ref[0]
        delta = jnp.sum(dov.astype(F32) * o_ref[0].astype(F32), axis=1, keepdims=True)
        dsp = pp * (lax.dot_general(dov, vp_ref[0], _DN_NT, preferred_element_type=F32) - delta)
        dsc = pc * (lax.dot_general(dov, vc_ref[0], _DN_NT, preferred_element_type=F32) - delta)
        dq = (jnp.dot(dsp.astype(BF16), kp, preferred_element_type=F32)
              + jnp.dot(dsc.astype(BF16), kc, preferred_element_type=F32))
        dq_ref[0] = (dq * SCALE).astype(BF16)

        @pl.when(i == 0)
        def _():
            dsink_ref[...] = jnp.zeros_like(dsink_ref)

        part = jnp.sum(-jnp.exp(sink - lse) * delta, axis=0, keepdims=True)
        dsink_ref[0] += jnp.broadcast_to(part, (1, LANES))

    qs = pl.BlockSpec((1, w, hd), lambda h, i: (h, i, 0))
    prev = pl.BlockSpec((1, w, hd), lambda h, i: (h // GROUP, jnp.maximum(i - 1, 0), 0))
    cur = pl.BlockSpec((1, w, hd), lambda h, i: (h // GROUP, i, 0))
    vec = pl.BlockSpec((1, 1, LANES), lambda h, i: (h, 0, 0))
    return pl.pallas_call(
        body, name="swa_bwd_dq", grid=(nh, nb),
        out_shape=(jax.ShapeDtypeStruct((nh, s, hd), BF16), jax.ShapeDtypeStruct((nh, 1, LANES), F32)),
        in_specs=[qs, prev, cur, prev, cur, vec, qs, qs, pl.BlockSpec((1, w, LANES), lambda h, i: (h, i, 0))],
        out_specs=(qs, vec),
        compiler_params=pltpu.CompilerParams(dimension_semantics=("parallel", "arbitrary")),
    )(q, k, k, v, v, sinks, do, o, lse)


def _swa_bwd_dkv(q, k, v, do, o, lse):
    nh, s, hd = q.shape
    nkv = k.shape[0]
    w = WINDOW
    nb = s // w

    def body(k_ref, v_ref, qa_ref, doa_ref, oa_ref, la_ref, qb_ref, dob_ref, ob_ref, lb_ref, dk_ref, dv_ref):
        j = pl.program_id(1)
        kk, vv = k_ref[0], v_ref[0]
        row = lax.broadcasted_iota(jnp.int32, (w, w), 0)
        col = lax.broadcasted_iota(jnp.int32, (w, w), 1)
        has_next = j + 1 < nb
        dk = jnp.zeros((w, hd), F32)
        dv = jnp.zeros((w, hd), F32)
        for r in range(GROUP):
            for q_ref, do_ref, o_ref, l_ref, valid in (
                    (qa_ref, doa_ref, oa_ref, la_ref, col <= row),
                    (qb_ref, dob_ref, ob_ref, lb_ref, (col > row) & has_next)):
                qs = _scaled(q_ref[r])
                dov = do_ref[r]
                sc = lax.dot_general(qs, kk, _DN_NT, preferred_element_type=F32)
                sc = jnp.where(valid, sc, NEG_INF)
                pm = jnp.exp(sc - l_ref[r][:, 0:1])
                delta = jnp.sum(dov.astype(F32) * o_ref[r].astype(F32), axis=1, keepdims=True)
                ds = pm * (lax.dot_general(dov, vv, _DN_NT, preferred_element_type=F32) - delta)
                dk = dk + lax.dot_general(ds.astype(BF16), qs, _DN_TN, preferred_element_type=F32)
                dv = dv + lax.dot_general(pm.astype(BF16), dov, _DN_TN, preferred_element_type=F32)
        dk_ref[0] = dk.astype(BF16)
        dv_ref[0] = dv.astype(BF16)

    kv = pl.BlockSpec((1, w, hd), lambda g, j: (g, j, 0))
    same = pl.BlockSpec((GROUP, w, hd), lambda g, j: (g, j, 0))
    nxt = pl.BlockSpec((GROUP, w, hd), lambda g, j: (g, jnp.minimum(j + 1, nb - 1), 0))
    lsame = pl.BlockSpec((GROUP, w, LANES), lambda g, j: (g, j, 0))
    lnxt = pl.BlockSpec((GROUP, w, LANES), lambda g, j: (g, jnp.minimum(j + 1, nb - 1), 0))
    shp = jax.ShapeDtypeStruct((nkv, s, hd), BF16)
    return pl.pallas_call(
        body, name="swa_bwd_dkv", grid=(nkv, nb),
        out_shape=(shp, shp),
        in_specs=[kv, kv, same, same, same, lsame, nxt, nxt, nxt, lnxt],
        out_specs=(kv, kv),
    )(k, v, q, do, o, lse, q, do, o, lse)


def _heads_major(xm, nheads):
    s = xm.shape[0]
    return xm.reshape(s, nheads, HEAD_DIM).transpose(1, 0, 2)


def _heads_minor(xh):
    nheads, s, _ = xh.shape
    return xh.transpose(1, 0, 2).reshape(s, nheads * HEAD_DIM)


def _ffn_ple_fwd(h1, p_l, g_ffn, g_ple, w_gu, w_down, w_pg, w_pp, tag):
    hn2 = _rms_fwd(h1, g_ffn, f"rms_ffn_{tag}")
    ab = _mm(hn2, w_gu, name=f"mm_gate_up_{tag}")
    u = _swiglu_fwd(ab, f"swiglu_{tag}")
    h2 = _mm(u, w_down, out_dtype=F32, res=h1, name=f"mm_down_{tag}")
    hn3 = _rms_fwd(h2, g_ple, f"rms_ple_{tag}")
    gl = _mm(hn3, w_pg, name=f"mm_ple_gate_{tag}")
    pp = _mm(p_l, w_pp, name=f"mm_ple_proj_{tag}")
    h3 = _ple_fwd(h2, gl, pp, f"ple_{tag}")
    return h3, dict(h1=h1, hn2=hn2, ab=ab, u=u, h2=h2, hn3=hn3, gl=gl, pp=pp)


def _ffn_ple_bwd(dh3, sv, p_l, g_ffn, g_ple, w_gu, w_down, w_pg, tag):
    dgl, dpp = _ple_bwd(dh3, sv["gl"], sv["pp"], f"ple_bwd_{tag}")
    d_wpp = _mm(p_l, dpp, ta=True, out_dtype=F32, name=f"mm_dw_ple_proj_{tag}")
    d_wpg = _mm(sv["hn3"], dgl, ta=True, out_dtype=F32, name=f"mm_dw_ple_gate_{tag}")
    dhn3 = _mm(dgl, w_pg, tb=True, out_dtype=F32, name=f"mm_dx_ple_gate_{tag}")
    dh2, dh2b, dg_ple = _rms_bwd(sv["h2"], g_ple, [dhn3], dh3, f"rms_ple_bwd_{tag}", True)
    du = _mm(dh2b, w_down, tb=True, name=f"mm_dx_down_{tag}")
    d_wdown = _mm(sv["u"], dh2b, ta=True, out_dtype=F32, name=f"mm_dw_down_{tag}")
    dab = _swiglu_bwd(sv["ab"], du, f"swiglu_bwd_{tag}")
    d_wgu = _mm(sv["hn2"], dab, ta=True, out_dtype=F32, name=f"mm_dw_gate_up_{tag}")
    dhn2 = _mm(dab, w_gu, tb=True, out_dtype=F32, name=f"mm_dx_gate_up_{tag}")
    dh1, dh1b, dg_ffn = _rms_bwd(sv["h1"], g_ffn, [dhn2], dh2, f"rms_ffn_bwd_{tag}", True)
    return dh1, dh1b, dict(d_wpp=d_wpp, d_wpg=d_wpg, d_wdown=d_wdown, d_wgu=d_wgu, dg_ple=dg_ple, dg_ffn=dg_ffn)


def _row_form(cum, blk):
    s = cum.shape[0]
    return cum[:, :N_FOX].T.reshape(N_FOX, s // blk, 1, blk)


def _col_form(cum):
    s = cum.shape[0]
    return jnp.broadcast_to(cum[:, :N_FOX].T[:, :, None], (N_FOX, s, LANES))


def kernel(x, p, positions, norm_mix, norm_ffn, norm_ple, norm_final, ev_w_in, ev_b_f, ev_w_out, od_w_in, od_sinks, od_w_out, ffn_w_gate, ffn_w_up, ffn_w_down, ple_w_proj, ple_w_gate, loss_target, m_norm_mix, m_norm_ffn, m_norm_ple, m_norm_final, m_ev_w_in, m_ev_b_f, m_ev_w_out, m_od_w_in, m_od_sinks, m_od_w_out, m_ffn_w_gate, m_ffn_w_up, m_ffn_w_down, m_ple_w_proj, m_ple_w_gate, v_norm_mix, v_norm_ffn, v_norm_ple, v_norm_final, v_ev_w_in, v_ev_b_f, v_ev_w_out, v_od_w_in, v_od_sinks, v_od_w_out, v_ffn_w_gate, v_ffn_w_up, v_ffn_w_down, v_ple_w_proj, v_ple_w_gate):
    s = x.shape[1]
    blk = min(ATTN_BLOCK, s)
    big_w = [ev_w_in, ev_w_out, od_w_in, od_w_out, ffn_w_gate, ffn_w_up, ffn_w_down, ple_w_proj, ple_w_gate]
    big_m = [m_ev_w_in, m_ev_w_out, m_od_w_in, m_od_w_out, m_ffn_w_gate, m_ffn_w_up, m_ffn_w_down, m_ple_w_proj, m_ple_w_gate]
    big_v = [v_ev_w_in, v_ev_w_out, v_od_w_in, v_od_w_out, v_ffn_w_gate, v_ffn_w_up, v_ffn_w_down, v_ple_w_proj, v_ple_w_gate]

    gathered = _all_gather_weights(_pack_shards(big_w, BF16))
    wfull = _unpack_gathered(gathered)
    w_qkv = wfull["ev_w_in"][0][:, :QKV_W]
    w_f = jnp.pad(wfull["ev_w_in"][0][:, QKV_W:], ((0, 0), (0, LANES - N_FOX)))
    w_eo = wfull["ev_w_out"][0]
    w_oi = wfull["od_w_in"][0]
    w_oo = wfull["od_w_out"][0]
    w_gu = [jnp.concatenate([wfull["ffn_w_gate"][l], wfull["ffn_w_up"][l]], axis=1) for l in range(2)]
    w_down = [wfull["ffn_w_down"][l] for l in range(2)]
    w_pp = [wfull["ple_w_proj"][l] for l in range(2)]
    w_pg = [wfull["ple_w_gate"][l] for l in range(2)]

    h0 = x[0]
    target = loss_target[0]
    p_b = [p[l, 0].astype(BF16) for l in range(2)]
    g_mix = [norm_mix[l][None, :] for l in range(2)]
    g_ffn = [norm_ffn[l][None, :] for l in range(2)]
    g_ple = [norm_ple[l][None, :] for l in range(2)]
    b_f = jnp.pad(ev_b_f, ((0, 0), (0, LANES - N_FOX)))

    half = HEAD_DIM // 2
    inv = ROPE_THETA ** (-jnp.arange(half, dtype=F32) / half)
    ang = positions[0].astype(F32)[:, None] * inv
    cos_t = jnp.tile(jnp.cos(ang), (1, 4))
    sin_t = jnp.tile(jnp.concatenate([-jnp.sin(ang), jnp.sin(ang)], axis=1), (1, 2))

    hn1 = _rms_fwd(h0, g_mix[0], "rms_mix_0")
    proj0 = _mm(hn1, w_qkv, name="mm_in_0")
    flog = _mm(hn1, w_f, out_dtype=F32, name="mm_fgate_0")
    cum = _fgate_fwd(flog, b_f)
    crow = _row_form(cum, blk)
    o_fox, acol = _fox_fwd(proj0, _col_form(cum), crow)
    o_sb, rsave = _sb_fwd(proj0)
    o0 = jnp.concatenate([o_fox, o_sb], axis=1)
    h1 = _mm(o0, w_eo, out_dtype=F32, res=h0, name="mm_out_0")
    h3, sv0 = _ffn_ple_fwd(h1, p_b[0], g_ffn[0], g_ple[0], w_gu[0], w_down[0], w_pg[0], w_pp[0], "0")

    hn1b = _rms_fwd(h3, g_mix[1], "rms_mix_1")
    proj1 = _mm(hn1b, w_oi, name="mm_in_1")
    nq_w, nk_w = N_Q * HEAD_DIM, N_KV * HEAD_DIM
    qk_r = _rope(proj1[:, :nq_w + nk_w], cos_t, sin_t, 1.0, "rope_fwd")
    q_h = _heads_major(qk_r[:, :nq_w], N_Q)
    k_h = _heads_major(qk_r[:, nq_w:], N_KV)
    v_h = _heads_major(proj1[:, nq_w + nk_w:], N_KV)
    sinks_rep = jnp.broadcast_to(od_sinks[0][:, None, None], (N_Q, 1, LANES))
    o_h, lse1 = _swa_fwd(q_h, k_h, v_h, sinks_rep)
    o1 = _heads_minor(o_h)
    h4 = _mm(o1, w_oo, out_dtype=F32, res=h3, name="mm_out_1")
    h6, sv1 = _ffn_ple_fwd(h4, p_b[1], g_ffn[1], g_ple[1], w_gu[1], w_down[1], w_pg[1], w_pp[1], "1")

    loss_part, dh6, dg_final = _final_norm_loss(h6, norm_final[None, :], target)

    dh4, dh4b, gr1 = _ffn_ple_bwd(dh6, sv1, p_b[1], g_ffn[1], g_ple[1], w_gu[1], w_down[1], w_pg[1], "1")
    do1 = _mm(dh4b, w_oo, tb=True, name="mm_dx_out_1")
    d_woo = _mm(o1, dh4b, ta=True, out_dtype=F32, name="mm_dw_out_1")
    do_h = _heads_major(do1, N_Q)
    dq_h, dsink = _swa_bwd_dq(q_h, k_h, v_h, sinks_rep, do_h, o_h, lse1)
    dk_h, dv_h = _swa_bwd_dkv(q_h, k_h, v_h, do_h, o_h, lse1)
    dqk = _rope(jnp.concatenate([_heads_minor(dq_h), _heads_minor(dk_h)], axis=1), cos_t, sin_t, -1.0, "rope_bwd")
    dproj1 = jnp.concatenate([dqk, _heads_minor(dv_h)], axis=1)
    d_woi = _mm(hn1b, dproj1, ta=True, out_dtype=F32, name="mm_dw_in_1")
    dhn1b = _mm(dproj1, w_oi, tb=True, out_dtype=F32, name="mm_dx_in_1")
    dh3, dg_mix1 = _rms_bwd(h3, g_mix[1], [dhn1b], dh4, "rms_mix_bwd_1", False)

    dh1, dh1b, gr0 = _ffn_ple_bwd(dh3, sv0, p_b[0], g_ffn[0], g_ple[0], w_gu[0], w_down[0], w_pg[0], "0")
    do0 = _mm(dh1b, w_eo, tb=True, name="mm_dx_out_0")
    d_weo = _mm(o0, dh1b, ta=True, out_dtype=F32, name="mm_dw_out_0")
    dq_f, dk_f, dv_f, gc, gr = _fox_bwd(proj0, do0, o0, acol, crow)
    dq_s, dk_s, dv_s = _sb_bwd(proj0, do0, rsave)
    dproj0 = jnp.concatenate([dq_f, dk_f, dv_f, dq_s, dk_s, dv_s], axis=1)
    gcum_k = jnp.pad(gc.reshape(N_FOX, s).T, ((0, 0), (0, LANES - N_FOX)))
    gcum_q = jnp.pad(gr[:, :, 0].T, ((0, 0), (0, LANES - N_FOX)))
    dflog, db_f = _fgate_bwd(gcum_k, gcum_q, flog, b_f)
    d_wqkv = _mm(hn1, dproj0, ta=True, out_dtype=F32, name="mm_dw_in_0")
    d_wf = _mm(hn1, dflog, ta=True, out_dtype=F32, name="mm_dw_fgate_0")
    dhn1 = _mm(dproj0, w_qkv, tb=True, out_dtype=F32, name="mm_dx_in_0")
    dhn1f = _mm(dflog, w_f, tb=True, out_dtype=F32, name="mm_dx_fgate_0")
    grad_x, dg_mix0 = _rms_bwd(h0, g_mix[0], [dhn1, dhn1f], dh1, "rms_mix_bwd_0", False)

    grs = (gr0, gr1)
    full_grads = {
        "ev_w_in": jnp.concatenate([d_wqkv, d_wf[:, :N_FOX]], axis=1)[None],
        "ev_w_out": d_weo[None],
        "od_w_in": d_woi[None],
        "od_w_out": d_woo[None],
        "ffn_w_gate": jnp.stack([g["d_wgu"][:, :D_FF] for g in grs]),
        "ffn_w_up": jnp.stack([g["d_wgu"][:, D_FF:] for g in grs]),
        "ffn_w_down": jnp.stack([g["d_wdown"] for g in grs]),
        "ple_w_proj": jnp.stack([g["d_wpp"] for g in grs]),
        "ple_w_gate": jnp.stack([g["d_wpg"] for g in grs]),
    }
    gpacked = _pack_full_grads(full_grads)
    core = lax.axis_index("c").astype(jnp.int32).reshape(1)
    from_sibling = _rs_sibling_exchange(gpacked)
    chip_part = _rs_chip_sum(core, gpacked, from_sibling)
    chip_recv = _rs_chip_exchange(chip_part)
    g_flat, d_flat, m_flat, v_flat = _rs_sum_adamw(
        chip_recv, _pack_shards(big_w, F32), _pack_shards(big_m, F32), _pack_shards(big_v, F32))
    big_g, big_d, big_nm, big_nv = (_unpack_shards(f) for f in (g_flat, d_flat, m_flat, v_flat))

    def small_pack(nmix, nffn, nple, nfin, bf, sk, extra):
        last = jnp.concatenate([bf.reshape(-1), sk.reshape(-1), extra.reshape(-1)])
        last = jnp.pad(last, (0, D_MODEL - last.shape[0]))
        return jnp.concatenate([nmix, nffn, nple, nfin.reshape(1, -1), last[None, :]], axis=0)

    small_g = small_pack(jnp.concatenate([dg_mix0, dg_mix1]), jnp.concatenate([gr0["dg_ffn"], gr1["dg_ffn"]]),
                         jnp.concatenate([gr0["dg_ple"], gr1["dg_ple"]]), dg_final,
                         db_f[0, :N_FOX], dsink[:, 0, 0], loss_part[0, :1])
    zero1 = jnp.zeros((1,), F32)
    small_w = small_pack(norm_mix, norm_ffn, norm_ple, norm_final, ev_b_f, od_sinks, zero1)
    small_m = small_pack(m_norm_mix, m_norm_ffn, m_norm_ple, m_norm_final, m_ev_b_f, m_od_sinks, zero1)
    small_v = small_pack(v_norm_mix, v_norm_ffn, v_norm_ple, v_norm_final, v_ev_b_f, v_od_sinks, zero1)
    sg, sd, sm, sv_ = _small_allreduce_adamw(small_g, small_w, small_m, small_v)

    def small_unpack(t):
        return [t[0:2], t[2:4], t[4:6], t[6], t[7, :N_FOX][None, :], t[7, N_FOX:N_FOX + N_Q][None, :]]

    loss = sg[7, N_FOX + N_Q]

    def ordered(small, big):
        nm, nf, npl, nfin, bf, sk = small_unpack(small)
        ev_in, ev_out, od_in, od_out, fg, fu, fd, pproj, pgate = big
        return [nm, nf, npl, nfin, ev_in, bf, ev_out, od_in, sk, od_out, fg, fu, fd, pproj, pgate]

    return (loss, grad_x[None], *ordered(sg, big_g), *ordered(sd, big_d),
            *ordered(sm, big_nm), *ordered(sv_, big_nv))
```

```python
import functools

import jax
import jax.numpy as jnp
from jax import lax
from jax.experimental import pallas as pl
from jax.experimental.pallas import tpu as pltpu

F32 = jnp.float32
BF16 = jnp.bfloat16

D_MODEL = 1024
HEAD_DIM = 64
N_FOX = 8
N_SB = 8
FOX_W = N_FOX * HEAD_DIM
SB_W = N_SB * HEAD_DIM
QKV_W = 3 * FOX_W + 3 * SB_W
EVEN_IN = QKV_W + N_FOX
N_Q = 16
N_KV = 4
GROUP = N_Q // N_KV
ODD_IN = N_Q * HEAD_DIM + 2 * N_KV * HEAD_DIM
WINDOW = 128
ROPE_THETA = 10000.0
D_FF = 2816
PLE_DIM = 256
EPS = 1e-6
NEG_INF = -1e30
SCALE = HEAD_DIM ** -0.5

ADAM_LR = 0.001
ADAM_B1 = 0.9
ADAM_B2 = 0.999
ADAM_EPS = 1e-08
ADAM_WD = 0.01
ADAM_STEP = 10

N_DEV = 8
LANES = 128
FLAT_COLS = 1024
FLAT_ROW_ALIGN = 256
ATTN_BLOCK = 256

MESH = pl.DeviceIdType.MESH

_WSPEC = (
    ("ev_w_in", (1, 1024, 385), 2),
    ("ev_w_out", (1, 128, 1024), 1),
    ("od_w_in", (1, 1024, 192), 2),
    ("od_w_out", (1, 128, 1024), 1),
    ("ffn_w_gate", (2, 1024, 352), 2),
    ("ffn_w_up", (2, 1024, 352), 2),
    ("ffn_w_down", (2, 352, 1024), 1),
    ("ple_w_proj", (2, 256, 128), 2),
    ("ple_w_gate", (2, 128, 1024), 1),
)


def _size(shape):
    n = 1
    for s in shape:
        n *= s
    return n


_FLAT_N = sum(_size(s) for _, s, _ in _WSPEC)
_FLAT_ROWS = -(-_FLAT_N // (FLAT_COLS * FLAT_ROW_ALIGN)) * FLAT_ROW_ALIGN


def _pick(n, prefs):
    for t in prefs:
        if n % t == 0:
            return t
    return n


def _pack_shards(shards, dtype):
    flat = jnp.concatenate([s.reshape(-1).astype(dtype) for s in shards])
    flat = jnp.pad(flat, (0, _FLAT_ROWS * FLAT_COLS - _FLAT_N))
    return flat.reshape(_FLAT_ROWS, FLAT_COLS)


def _unpack_shards(flat):
    flat = flat.reshape(-1)
    out, off = [], 0
    for _, shape, _ in _WSPEC:
        n = _size(shape)
        out.append(flat[off:off + n].reshape(shape))
        off += n
    return out


def _unpack_gathered(g):
    g = g.reshape(N_DEV, -1)
    out, off = {}, 0
    for name, shape, axis in _WSPEC:
        n = _size(shape)
        blk = g[:, off:off + n].reshape((N_DEV,) + shape)
        off += n
        l, r, c = shape
        if axis == 2:
            out[name] = blk.transpose(1, 2, 0, 3).reshape(l, r, N_DEV * c)
        else:
            out[name] = blk.transpose(1, 0, 2, 3).reshape(l, N_DEV * r, c)
    return out


def _pack_full_grads(grads):
    parts = []
    for name, shape, axis in _WSPEC:
        l, r, c = shape
        gfull = grads[name]
        if axis == 2:
            blk = gfull.reshape(l, r, N_DEV, c).transpose(2, 0, 1, 3)
        else:
            blk = gfull.reshape(l, N_DEV, r, c).transpose(1, 0, 2, 3)
        parts.append(blk.reshape(N_DEV, -1))
    flat = jnp.concatenate(parts, axis=1)
    flat = jnp.pad(flat, ((0, 0), (0, _FLAT_ROWS * FLAT_COLS - _FLAT_N)))
    return flat.reshape(N_DEV, _FLAT_ROWS, FLAT_COLS)


_ANY = pl.BlockSpec(memory_space=pl.ANY)


def _all_gather_weights(shard):
    rows, cols = shard.shape

    def body(x_ref, out_ref, send_sems, recv_sems, local_sem):
        x, y, c = lax.axis_index("x"), lax.axis_index("y"), lax.axis_index("c")
        me, sibling = (x, y, c), (x, y, 1 - c)
        chips = [(1 - x, y), (x, 1 - y), (1 - x, 1 - y)]

        def slot(px, py, pc):
            return out_ref.at[4 * px + 2 * py + pc]

        def copy(k, block, to, src=None):
            return pltpu.make_async_remote_copy(
                src_ref=slot(*block) if src is None else src, dst_ref=slot(*block),
                send_sem=send_sems.at[k], recv_sem=recv_sems.at[k],
                device_id=to, device_id_type=MESH)

        mine = pltpu.make_async_copy(x_ref, slot(*me), local_sem)
        mine.start()
        first = [copy(0, me, sibling, src=x_ref)]
        first += [copy(1 + j, me, (*chip, c), src=x_ref) for j, chip in enumerate(chips)]
        for cp in first:
            cp.start()
        passed = [copy(4 + j, (*chip, c), sibling) for j, chip in enumerate(chips)]
        for j, chip in enumerate(chips):
            copy(1 + j, (*chip, c), me).wait_recv()
            passed[j].start()
        copy(0, sibling, me).wait_recv()
        for j, chip in enumerate(chips):
            copy(4 + j, (*chip, 1 - c), me).wait_recv()
        for cp in first + passed:
            cp.wait_send()
        mine.wait()

    return pl.pallas_call(
        body, name="ag_weights",
        out_shape=jax.ShapeDtypeStruct((N_DEV, rows, cols), shard.dtype),
        in_specs=[_ANY], out_specs=_ANY,
        scratch_shapes=[pltpu.SemaphoreType.DMA((7,)), pltpu.SemaphoreType.DMA((7,)), pltpu.SemaphoreType.DMA(())],
    )(shard)


def _rs_sibling_exchange(gp):
    _, rows, cols = gp.shape

    def body(g_ref, out_ref, send_sems, recv_sems):
        x, y, c = lax.axis_index("x"), lax.axis_index("y"), lax.axis_index("c")
        sibling = (x, y, 1 - c)
        copies = []
        for k in range(4):
            copies.append(pltpu.make_async_remote_copy(
                src_ref=g_ref.at[2 * k + (1 - c)], dst_ref=out_ref.at[k],
                send_sem=send_sems.at[k], recv_sem=recv_sems.at[k],
                device_id=sibling, device_id_type=MESH))
        for cp in copies:
            cp.start()
        for cp in copies:
            cp.wait_recv()
        for cp in copies:
            cp.wait_send()

    return pl.pallas_call(
        body, name="rs_sibling_exchange",
        out_shape=jax.ShapeDtypeStruct((4, rows, cols), gp.dtype),
        in_specs=[_ANY], out_specs=_ANY,
        scratch_shapes=[pltpu.SemaphoreType.DMA((4,)), pltpu.SemaphoreType.DMA((4,))],
    )(gp)


def _rs_chip_sum(core, gp, recv):
    _, rows, cols = gp.shape
    tr = FLAT_ROW_ALIGN

    def body(core_ref, a_ref, b_ref, o_ref):
        o_ref[...] = (a_ref[...] + b_ref[...]).astype(BF16)

    return pl.pallas_call(
        body, name="rs_chip_sum",
        out_shape=jax.ShapeDtypeStruct((4, rows, cols), BF16),
        grid_spec=pltpu.PrefetchScalarGridSpec(
            num_scalar_prefetch=1, grid=(4, rows // tr),
            in_specs=[pl.BlockSpec((1, tr, cols), lambda k, r, cr: (2 * k + cr[0], r, 0)),
                      pl.BlockSpec((1, tr, cols), lambda k, r, cr: (k, r, 0))],
            out_specs=pl.BlockSpec((1, tr, cols), lambda k, r, cr: (k, r, 0))),
    )(core, gp, recv)


def _rs_chip_exchange(part):
    _, rows, cols = part.shape

    def body(p_ref, out_ref, send_sems, recv_sems, local_sem):
        x, y, c = lax.axis_index("x"), lax.axis_index("y"), lax.axis_index("c")
        my_chip = 2 * x + y
        mine = pltpu.make_async_copy(p_ref.at[my_chip], out_ref.at[my_chip], local_sem)
        mine.start()
        copies = []
        for j, (px, py) in enumerate([(1 - x, y), (x, 1 - y), (1 - x, 1 - y)]):
            copies.append(pltpu.make_async_remote_copy(
                src_ref=p_ref.at[2 * px + py], dst_ref=out_ref.at[my_chip],
                send_sem=send_sems.at[j], recv_sem=recv_sems.at[j],
                device_id=(px, py, c), device_id_type=MESH))
        for cp in copies:
            cp.start()
        for cp in copies:
            cp.wait_recv()
        for cp in copies:
            cp.wait_send()
        mine.wait()

    return pl.pallas_call(
        body, name="rs_chip_exchange",
        out_shape=jax.ShapeDtypeStruct((4, rows, cols), part.dtype),
        in_specs=[_ANY], out_specs=_ANY,
        scratch_shapes=[pltpu.SemaphoreType.DMA((3,)), pltpu.SemaphoreType.DMA((3,)), pltpu.SemaphoreType.DMA(())],
    )(part)


def _adamw(w, g, m, v):
    m = ADAM_B1 * m + (1.0 - ADAM_B1) * g
    v = ADAM_B2 * v + (1.0 - ADAM_B2) * (g * g)
    m_hat = m / (1.0 - ADAM_B1 ** ADAM_STEP)
    v_hat = v / (1.0 - ADAM_B2 ** ADAM_STEP)
    delta = -ADAM_LR * (m_hat / (jnp.sqrt(v_hat) + ADAM_EPS) + ADAM_WD * w)
    return delta, m, v


def _rs_sum_adamw(recv, w, m, v):
    _, rows, cols = recv.shape
    tr = FLAT_ROW_ALIGN

    def body(r_ref, w_ref, m_ref, v_ref, g_out, d_out, m_out, v_out):
        g = r_ref[0].astype(F32)
        for k in range(1, 4):
            g = g + r_ref[k].astype(F32)
        delta, m_new, v_new = _adamw(w_ref[...], g, m_ref[...], v_ref[...])
        g_out[...] = g
        d_out[...] = delta
        m_out[...] = m_new
        v_out[...] = v_new

    flat = pl.BlockSpec((tr, cols), lambda r: (r, 0))
    shp = jax.ShapeDtypeStruct((rows, cols), F32)
    return pl.pallas_call(
        body, name="rs_sum_adamw", grid=(rows // tr,),
        out_shape=(shp, shp, shp, shp),
        in_specs=[pl.BlockSpec((4, tr, cols), lambda r: (0, r, 0)), flat, flat, flat],
        out_specs=(flat, flat, flat, flat),
    )(recv, w, m, v)


def _small_allreduce_adamw(vec, w, m, v):
    rows, cols = vec.shape

    def body(x_ref, w_ref, m_ref, v_ref, g_out, d_out, m_out, v_out, gather, send_sems, recv_sems):
        x, y, c = lax.axis_index("x"), lax.axis_index("y"), lax.axis_index("c")
        me = 4 * x + 2 * y + c
        copies = []
        for d in range(1, N_DEV):
            dx, dy, dc = (d >> 2) & 1, (d >> 1) & 1, d & 1
            peer = (x ^ dx if dx else x, y ^ dy if dy else y, c ^ dc if dc else c)
            copies.append(pltpu.make_async_remote_copy(
                src_ref=x_ref, dst_ref=gather.at[me],
                send_sem=send_sems.at[d - 1], recv_sem=recv_sems.at[d - 1],
                device_id=peer, device_id_type=MESH))
        for cp in copies:
            cp.start()
        gather[me] = x_ref[...]
        for cp in copies:
            cp.wait_recv()
        for cp in copies:
            cp.wait_send()
        g = gather[0]
        for k in range(1, N_DEV):
            g = g + gather[k]
        delta, m_new, v_new = _adamw(w_ref[...], g, m_ref[...], v_ref[...])
        g_out[...] = g
        d_out[...] = delta
        m_out[...] = m_new
        v_out[...] = v_new

    vm = pl.BlockSpec(memory_space=pltpu.VMEM)
    shp = jax.ShapeDtypeStruct((rows, cols), F32)
    return pl.pallas_call(
        body, name="small_allreduce_adamw",
        out_shape=(shp, shp, shp, shp),
        in_specs=[vm, vm, vm, vm], out_specs=(vm, vm, vm, vm),
        scratch_shapes=[pltpu.VMEM((N_DEV, rows, cols), F32),
                        pltpu.SemaphoreType.DMA((N_DEV - 1,)), pltpu.SemaphoreType.DMA((N_DEV - 1,))],
    )(vec, w, m, v)


def _mm(a, b, *, ta=False, tb=False, out_dtype=BF16, res=None, name):
    if ta:
        kdim, m = a.shape
    else:
        m, kdim = a.shape
    if tb:
        n, kb = b.shape
    else:
        kb, n = b.shape
    assert kdim == kb, (a.shape, b.shape, ta, tb)
    tm = _pick(m, (1024, 512, 256, 128))
    tn = _pick(n, (1024, 1408, 768, 512, 256, 128))
    tk = _pick(kdim, (1024, 1408, 512, 256, 128))
    nk = kdim // tk
    dn = (((0 if ta else 1,), (1 if tb else 0,)), ((), ()))
    has_res = res is not None
    in_place = nk > 1 and out_dtype == F32
    use_acc = nk > 1 and not in_place

    def body(*refs):
        a_ref, b_ref = refs[:2]
        r_ref = refs[2] if has_res else None
        o_ref = refs[3] if has_res else refs[2]
        part = lax.dot_general(a_ref[...], b_ref[...], dn, preferred_element_type=F32)
        if nk == 1:
            if has_res:
                part = part + r_ref[...].astype(F32)
            o_ref[...] = part.astype(out_dtype)
            return
        k = pl.program_id(2)
        acc = o_ref if in_place else refs[-1]

        @pl.when(k == 0)
        def _():
            acc[...] = part + r_ref[...].astype(F32) if has_res else part

        @pl.when(k > 0)
        def _():
            acc[...] += part

        if use_acc:
            @pl.when(k == nk - 1)
            def _():
                o_ref[...] = acc[...].astype(out_dtype)

    a_spec = (pl.BlockSpec((tk, tm), lambda i, j, k: (k, i)) if ta
              else pl.BlockSpec((tm, tk), lambda i, j, k: (i, k)))
    b_spec = (pl.BlockSpec((tn, tk), lambda i, j, k: (j, k)) if tb
              else pl.BlockSpec((tk, tn), lambda i, j, k: (k, j)))
    o_spec = pl.BlockSpec((tm, tn), lambda i, j, k: (i, j))
    in_specs = [a_spec, b_spec] + ([o_spec] if has_res else [])
    args = (a, b) + ((res,) if has_res else ())
    return pl.pallas_call(
        body, name=name, grid=(m // tm, n // tn, nk),
        out_shape=jax.ShapeDtypeStruct((m, n), out_dtype),
        in_specs=in_specs, out_specs=o_spec,
        scratch_shapes=[pltpu.VMEM((tm, tn), F32)] if use_acc else [],
        compiler_params=pltpu.CompilerParams(dimension_semantics=("parallel", "parallel", "arbitrary")),
    )(*args)


def _row_tile(s):
    return _pick(s, (256, 128))


def _rms_fwd(h, g, name):
    s, d = h.shape
    ts = _row_tile(s)

    def body(h_ref, g_ref, o_ref):
        x = h_ref[...]
        r = lax.rsqrt(jnp.mean(x * x, axis=-1, keepdims=True) + EPS)
        o_ref[...] = ((x * r) * g_ref[...]).astype(BF16)

    return pl.pallas_call(
        body, name=name, grid=(s // ts,),
        out_shape=jax.ShapeDtypeStruct((s, d), BF16),
        in_specs=[pl.BlockSpec((ts, d), lambda i: (i, 0)), pl.BlockSpec((1, d), lambda i: (0, 0))],
        out_specs=pl.BlockSpec((ts, d), lambda i: (i, 0)),
    )(h, g)


def _rms_bwd(h, g, dhns, dres, name, want_bf16):
    s, d = h.shape
    ts = _row_tile(s)
    n_in = len(dhns)

    def body(*refs):
        h_ref, g_ref, r_ref = refs[:3]
        dy_refs = refs[3:3 + n_in]
        outs = refs[3 + n_in:]
        dh_ref, dg_ref = outs[0], outs[-1]
        i = pl.program_id(0)
        x = h_ref[...]
        dy = dy_refs[0][...].astype(F32)
        for extra in dy_refs[1:]:
            dy = dy + extra[...].astype(F32)
        r = lax.rsqrt(jnp.mean(x * x, axis=-1, keepdims=True) + EPS)
        xr = x * r
        u = dy * g_ref[...]
        dx = r * (u - xr * jnp.mean(xr * u, axis=-1, keepdims=True))
        dh = r_ref[...] + dx
        dh_ref[...] = dh
        if want_bf16:
            outs[1][...] = dh.astype(BF16)

        @pl.when(i == 0)
        def _():
            dg_ref[...] = jnp.zeros_like(dg_ref)

        dg_ref[...] += jnp.sum(dy * xr, axis=0, keepdims=True)

    row = pl.BlockSpec((ts, d), lambda i: (i, 0))
    vec = pl.BlockSpec((1, d), lambda i: (0, 0))
    out_shape = [jax.ShapeDtypeStruct((s, d), F32)]
    out_specs = [row]
    if want_bf16:
        out_shape.append(jax.ShapeDtypeStruct((s, d), BF16))
        out_specs.append(row)
    out_shape.append(jax.ShapeDtypeStruct((1, d), F32))
    out_specs.append(vec)
    return pl.pallas_call(
        body, name=name, grid=(s // ts,),
        out_shape=tuple(out_shape),
        in_specs=[row, vec, row] + [row] * n_in, out_specs=tuple(out_specs),
        compiler_params=pltpu.CompilerParams(dimension_semantics=("arbitrary",)),
    )(h, g, dres, *dhns)


def _sigmoid_parts(z):
    e = jnp.exp(-jnp.abs(z))
    r = 1.0 / (1.0 + e)
    er = e * r
    pos = z >= 0
    return jnp.where(pos, r, er), jnp.where(pos, er, r)


def _swiglu_fwd(ab, name):
    s, two_f = ab.shape
    f = two_f // 2
    ts = _row_tile(s)

    def body(ab_ref, o_ref):
        a = ab_ref[:, :f].astype(F32)
        b = ab_ref[:, f:].astype(F32)
        sg, _ = _sigmoid_parts(a)
        o_ref[...] = ((a * sg) * b).astype(BF16)

    return pl.pallas_call(
        body, name=name, grid=(s // ts,),
        out_shape=jax.ShapeDtypeStruct((s, f), BF16),
        in_specs=[pl.BlockSpec((ts, two_f), lambda i: (i, 0))],
        out_specs=pl.BlockSpec((ts, f), lambda i: (i, 0)),
    )(ab)


def _swiglu_bwd(ab, du, name):
    s, two_f = ab.shape
    f = two_f // 2
    ts = _row_tile(s)

    def body(ab_ref, du_ref, o_ref):
        a = ab_ref[:, :f].astype(F32)
        b = ab_ref[:, f:].astype(F32)
        g = du_ref[...].astype(F32)
        sg, sgm = _sigmoid_parts(a)
        silu = a * sg
        o_ref[:, :f] = (g * b * (sg + silu * sgm)).astype(BF16)
        o_ref[:, f:] = (g * silu).astype(BF16)

    return pl.pallas_call(
        body, name=name, grid=(s // ts,),
        out_shape=jax.ShapeDtypeStruct((s, two_f), BF16),
        in_specs=[pl.BlockSpec((ts, two_f), lambda i: (i, 0)), pl.BlockSpec((ts, f), lambda i: (i, 0))],
        out_specs=pl.BlockSpec((ts, two_f), lambda i: (i, 0)),
    )(ab, du)


def _ple_fwd(h, gl, pp, name):
    s, d = h.shape
    ts = _row_tile(s)

    def body(h_ref, gl_ref, pp_ref, o_ref):
        sg, _ = _sigmoid_parts(gl_ref[...].astype(F32))
        o_ref[...] = h_ref[...] + sg * pp_ref[...].astype(F32)

    row = pl.BlockSpec((ts, d), lambda i: (i, 0))
    return pl.pallas_call(
        body, name=name, grid=(s // ts,),
        out_shape=jax.ShapeDtypeStruct((s, d), F32),
        in_specs=[row, row, row], out_specs=row,
    )(h, gl, pp)


def _ple_bwd(dh, gl, pp, name):
    s, d = dh.shape
    ts = _row_tile(s)

    def body(dh_ref, gl_ref, pp_ref, dgl_ref, dpp_ref):
        g = dh_ref[...]
        sg, sgm = _sigmoid_parts(gl_ref[...].astype(F32))
        dpp_ref[...] = (g * sg).astype(BF16)
        dgl_ref[...] = (g * pp_ref[...].astype(F32) * (sg * sgm)).astype(BF16)

    row = pl.BlockSpec((ts, d), lambda i: (i, 0))
    shp = jax.ShapeDtypeStruct((s, d), BF16)
    return pl.pallas_call(
        body, name=name, grid=(s // ts,),
        out_shape=(shp, shp), in_specs=[row, row, row], out_specs=(row, row),
    )(dh, gl, pp)


def _final_norm_loss(h, g, target):
    s, d = h.shape
    ts = _row_tile(s)

    def body(h_ref, g_ref, t_ref, loss_ref, dh_ref, dg_ref):
        i = pl.program_id(0)
        x = h_ref[...]
        gain = g_ref[...]
        r = lax.rsqrt(jnp.mean(x * x, axis=-1, keepdims=True) + EPS)
        xr = x * r
        err = xr * gain - t_ref[...]
        dy = err * (1.0 / d)
        u = dy * gain
        dh_ref[...] = r * (u - xr * jnp.mean(xr * u, axis=-1, keepdims=True))

        @pl.when(i == 0)
        def _():
            dg_ref[...] = jnp.zeros_like(dg_ref)
            loss_ref[...] = jnp.zeros_like(loss_ref)

        dg_ref[...] += jnp.sum(dy * xr, axis=0, keepdims=True)
        tok = jnp.mean(err * err, axis=-1, keepdims=True)
        loss_ref[...] += 0.5 * jnp.sum(tok, axis=0, keepdims=True)

    row = pl.BlockSpec((ts, d), lambda i: (i, 0))
    vec = pl.BlockSpec((1, d), lambda i: (0, 0))
    return pl.pallas_call(
        body, name="final_norm_loss", grid=(s // ts,),
        out_shape=(jax.ShapeDtypeStruct((1, LANES), F32), jax.ShapeDtypeStruct((s, d), F32),
                   jax.ShapeDtypeStruct((1, d), F32)),
        in_specs=[row, vec, row],
        out_specs=(pl.BlockSpec((1, LANES), lambda i: (0, 0)), row, vec),
        compiler_params=pltpu.CompilerParams(dimension_semantics=("arbitrary",)),
    )(h, g, target)


def _rope(xin, cos, sin_signed, sign, name):
    s, w = xin.shape
    ts = _row_tile(s)

    def body(x_ref, c_ref, s_ref, o_ref):
        x = x_ref[...].astype(F32)
        lane = lax.broadcasted_iota(jnp.int32, x.shape, 1)
        low = (lane & (HEAD_DIM - 1)) < (HEAD_DIM // 2)
        swapped = jnp.where(low, pltpu.roll(x, LANES - HEAD_DIM // 2, 1), pltpu.roll(x, HEAD_DIM // 2, 1))
        o_ref[...] = (x * c_ref[...] + sign * (swapped * s_ref[...])).astype(BF16)

    blk = pl.BlockSpec((ts, LANES), lambda i, j: (i, j))
    tab = pl.BlockSpec((ts, LANES), lambda i, j: (i, 0))
    return pl.pallas_call(
        body, name=name, grid=(s // ts, w // LANES),
        out_shape=jax.ShapeDtypeStruct((s, w), BF16),
        in_specs=[blk, tab, tab], out_specs=blk,
    )(xin, cos, sin_signed)


def _fgate_fwd(flog, bias):
    s, w = flog.shape

    def body(x_ref, b_ref, o_ref):
        rowi = lax.broadcasted_iota(jnp.int32, (8, w), 0)
        b = b_ref[...]

        def step(g, carry):
            sl = pl.ds(pl.multiple_of(g * 8, 8), 8)
            x = x_ref[sl, :] + b
            lf = jnp.minimum(x, 0.0) - jnp.log1p(jnp.exp(-jnp.abs(x)))
            for sh in (1, 2, 4):
                lf = lf + jnp.where(rowi >= sh, pltpu.roll(lf, sh, 0), 0.0)
            out = lf + carry
            o_ref[sl, :] = out
            return jnp.broadcast_to(out[7:8, :], (8, w))

        lax.fori_loop(0, s // 8, step, jnp.zeros((8, w), F32))

    vm = pl.BlockSpec(memory_space=pltpu.VMEM)
    return pl.pallas_call(
        body, name="fgate_fwd", out_shape=jax.ShapeDtypeStruct((s, w), F32),
        in_specs=[vm, vm], out_specs=vm,
    )(flog, bias)


def _fgate_bwd(gcum_k, gcum_q, flog, bias):
    s, w = flog.shape

    def body(g_ref, g2_ref, x_ref, b_ref, o_ref, db_ref):
        rowi = lax.broadcasted_iota(jnp.int32, (8, w), 0)
        lane = lax.broadcasted_iota(jnp.int32, (8, w), 1)
        b = b_ref[...]

        def step(t, carry):
            run, dbsum = carry
            g = s // 8 - 1 - t
            sl = pl.ds(pl.multiple_of(g * 8, 8), 8)
            c = g_ref[sl, :] + g2_ref[sl, :]
            for sh in (1, 2, 4):
                c = c + jnp.where(rowi < 8 - sh, pltpu.roll(c, 8 - sh, 0), 0.0)
            c = c + run
            _, sgm = _sigmoid_parts(x_ref[sl, :] + b)
            dl = jnp.where(lane < N_FOX, c * sgm, 0.0)
            o_ref[sl, :] = dl.astype(BF16)
            return jnp.broadcast_to(c[0:1, :], (8, w)), dbsum + dl

        _, dbsum = lax.fori_loop(0, s // 8, step, (jnp.zeros((8, w), F32), jnp.zeros((8, w), F32)))
        db_ref[...] = jnp.sum(dbsum, axis=0, keepdims=True)

    vm = pl.BlockSpec(memory_space=pltpu.VMEM)
    return pl.pallas_call(
        body, name="fgate_bwd",
        out_shape=(jax.ShapeDtypeStruct((s, w), BF16), jax.ShapeDtypeStruct((1, w), F32)),
        in_specs=[vm, vm, vm, vm], out_specs=(vm, vm),
    )(gcum_k, gcum_q, flog, bias)


_DN_NT = (((1,), (1,)), ((), ()))
_DN_TN = (((0,), (0,)), ((), ()))


def _head_mask(shape, hh):
    lane = lax.broadcasted_iota(jnp.int32, shape, 1)
    return (lane >= HEAD_DIM * hh) & (lane < HEAD_DIM * (hh + 1))


SKIP_BELOW = -110.0


def _sweep_left(i, carry, tile, go_on):
    def flag(j, c):
        return jnp.logical_and(j >= 0, go_on(jnp.maximum(j, 0), c)).astype(jnp.int32)

    def body(st):
        j, _, c = st
        c = tile(j, c)
        return j - 1, flag(j - 1, c), c

    return lax.while_loop(lambda st: st[1] > 0, body, (i - 1, flag(i - 1, carry), carry))[2]


def _key_norm_max(k_ref, kn_ref):
    k2 = k_ref[...].astype(F32)
    sq = k2 * k2
    for hh in range(2):
        n2 = jnp.sum(jnp.where(_head_mask(sq.shape, hh), sq, 0.0), axis=1, keepdims=True)
        kn_ref[hh] = jnp.broadcast_to(jnp.sqrt(jnp.max(n2, axis=0, keepdims=True)), kn_ref.shape[1:])


def _fox_fwd(proj, ccol, crow):
    s = proj.shape[0]
    blk = min(ATTN_BLOCK, s)
    nq = s // blk
    npair = N_FOX // 2

    def body(q_ref, k_ref, v_ref, cc_ref, cr_ref, o_ref, a_ref, kn_ref):
        p_, i = pl.program_id(0), pl.program_id(1)

        @pl.when(i == 0)
        def _():
            _key_norm_max(k_ref, kn_ref)

        q2 = q_ref[...].astype(F32) * SCALE
        row = lax.broadcasted_iota(jnp.int32, (blk, blk), 0)
        col = lax.broadcasted_iota(jnp.int32, (blk, blk), 1)
        outs = []
        for hh in range(2):
            hm = _head_mask(q2.shape, hh)
            qh = jnp.where(hm, q2, 0.0).astype(BF16)
            ct = cc_ref[hh][:, 0:1]
            qk_max = jnp.sqrt(jnp.sum(jnp.where(hm, q2 * q2, 0.0), axis=1, keepdims=True)) * kn_ref[hh][0:1, 0:1]

            def go_on(j, carry, ct=ct, qk_max=qk_max, hh=hh):
                bias_max = ct - jnp.min(cr_ref[2 * p_ + hh, j], axis=1, keepdims=True)
                return jnp.max(qk_max + bias_max - carry[0]) > SKIP_BELOW

            def tile(j, carry, masked, qh=qh, ct=ct, hh=hh):
                m, l, acc = carry
                sl = pl.ds(pl.multiple_of(j * blk, blk), blk)
                kb, vb = k_ref[sl, :], v_ref[sl, :]
                sc = lax.dot_general(qh, kb, _DN_NT, preferred_element_type=F32)
                sc = sc + (ct - cr_ref[2 * p_ + hh, j])
                if masked:
                    sc = jnp.where(col <= row, sc, NEG_INF)
                m_new = jnp.maximum(m, jnp.max(sc, axis=1, keepdims=True))
                alpha = jnp.exp(m - m_new)
                pm = jnp.exp(sc - m_new)
                l = alpha * l + jnp.sum(pm, axis=1, keepdims=True)
                acc = alpha * acc + jnp.dot(pm.astype(BF16), vb, preferred_element_type=F32)
                return m_new, l, acc

            init = (jnp.full((blk, 1), NEG_INF, F32), jnp.zeros((blk, 1), F32), jnp.zeros((blk, LANES), F32))
            carry = tile(i, init, True)
            m, l, acc = _sweep_left(i, carry, lambda j, c, tile=tile: tile(j, c, False), go_on)
            outs.append(acc / l)
            a_ref[hh] = jnp.broadcast_to(ct - (m + jnp.log(l)), (blk, LANES))
        o_ref[...] = jnp.where(_head_mask(outs[0].shape, 0), outs[0], outs[1]).astype(BF16)

    seq = lambda base: pl.BlockSpec((s, LANES), lambda p, i: (0, base + p))
    return pl.pallas_call(
        body, name="fox_fwd", grid=(npair, nq),
        scratch_shapes=[pltpu.VMEM((2, 8, LANES), F32)],
        out_shape=(jax.ShapeDtypeStruct((s, FOX_W), BF16), jax.ShapeDtypeStruct((N_FOX, s, LANES), F32)),
        in_specs=[pl.BlockSpec((blk, LANES), lambda p, i: (i, p)), seq(npair), seq(2 * npair),
                  pl.BlockSpec((2, blk, LANES), lambda p, i: (p, i, 0)),
                  pl.BlockSpec((N_FOX, nq, 1, blk), lambda p, i: (0, 0, 0, 0))],
        out_specs=(pl.BlockSpec((blk, LANES), lambda p, i: (i, p)),
                   pl.BlockSpec((2, blk, LANES), lambda p, i: (p, i, 0))),
        compiler_params=pltpu.CompilerParams(dimension_semantics=("parallel", "arbitrary")),
    )(proj, proj, proj, ccol, crow)


def _fox_bwd(proj, do, o, acol, crow):
    s = proj.shape[0]
    blk = min(ATTN_BLOCK, s)
    nq = s // blk
    npair = N_FOX // 2

    def body(q_ref, k_ref, v_ref, do_ref, o_ref, a_ref, cr_ref, dq_ref, dk_ref, dv_ref, gc_ref, gr_ref,
             dk_acc, dv_acc, kn_ref):
        p_, i = pl.program_id(0), pl.program_id(1)

        @pl.when(i == 0)
        def _():
            dk_acc[...] = jnp.zeros_like(dk_acc)
            dv_acc[...] = jnp.zeros_like(dv_acc)
            gc_ref[...] = jnp.zeros_like(gc_ref)
            _key_norm_max(k_ref, kn_ref)

        q2 = q_ref[...].astype(F32) * SCALE
        do2 = do_ref[...]
        prod = do2.astype(F32) * o_ref[...].astype(F32)
        row = lax.broadcasted_iota(jnp.int32, (blk, blk), 0)
        col = lax.broadcasted_iota(jnp.int32, (blk, blk), 1)
        dqs = []
        for hh in range(2):
            hm = _head_mask(q2.shape, hh)
            qh = jnp.where(hm, q2, 0.0).astype(BF16)
            doh = jnp.where(hm, do2, jnp.zeros_like(do2))
            delta = jnp.sum(jnp.where(hm, prod, 0.0), axis=1, keepdims=True)
            at = a_ref[hh][:, 0:1]
            qk_max = jnp.sqrt(jnp.sum(jnp.where(hm, q2 * q2, 0.0), axis=1, keepdims=True)) * kn_ref[hh][0:1, 0:1]

            def go_on(j, carry, at=at, qk_max=qk_max, hh=hh):
                bias_max = at - jnp.min(cr_ref[2 * p_ + hh, j], axis=1, keepdims=True)
                return jnp.max(qk_max + bias_max) > SKIP_BELOW

            def tile(j, carry, masked, qh=qh, doh=doh, delta=delta, at=at, hh=hh):
                dq, rs = carry
                sl = pl.ds(pl.multiple_of(j * blk, blk), blk)
                kb, vb = k_ref[sl, :], v_ref[sl, :]
                sc = lax.dot_general(qh, kb, _DN_NT, preferred_element_type=F32)
                sc = sc + (at - cr_ref[2 * p_ + hh, j])
                if masked:
                    sc = jnp.where(col <= row, sc, NEG_INF)
                pm = jnp.exp(sc)
                dp = lax.dot_general(doh, vb, _DN_NT, preferred_element_type=F32)
                ds = pm * (dp - delta)
                dsb = ds.astype(BF16)
                dk_acc[sl, :] += lax.dot_general(dsb, qh, _DN_TN, preferred_element_type=F32)
                dv_acc[sl, :] += lax.dot_general(pm.astype(BF16), doh, _DN_TN, preferred_element_type=F32)
                gc_ref[hh, j] += -jnp.sum(ds, axis=0, keepdims=True)
                return dq + jnp.dot(dsb, kb, preferred_element_type=F32), rs + jnp.sum(ds, axis=1, keepdims=True)

            carry = tile(i, (jnp.zeros((blk, LANES), F32), jnp.zeros((blk, 1), F32)), True)
            dq, rs = _sweep_left(i, carry, lambda j, c, tile=tile: tile(j, c, False), go_on)
            dqs.append(dq)
            gr_ref[hh] = jnp.broadcast_to(rs, (blk, LANES))
        dq_ref[...] = (jnp.where(_head_mask(dqs[0].shape, 0), dqs[0], dqs[1]) * SCALE).astype(BF16)

        @pl.when(i == nq - 1)
        def _():
            dk_ref[...] = dk_acc[...].astype(BF16)
            dv_ref[...] = dv_acc[...].astype(BF16)

    seq = lambda base: pl.BlockSpec((s, LANES), lambda p, i: (0, base + p))
    qblk = lambda base: pl.BlockSpec((blk, LANES), lambda p, i: (i, base + p))
    rep = pl.BlockSpec((2, blk, LANES), lambda p, i: (p, i, 0))
    half = jax.ShapeDtypeStruct((s, FOX_W), BF16)
    return pl.pallas_call(
        body, name="fox_bwd", grid=(npair, nq),
        out_shape=(half, half, half, jax.ShapeDtypeStruct((N_FOX, nq, 1, blk), F32),
                   jax.ShapeDtypeStruct((N_FOX, s, LANES), F32)),
        in_specs=[qblk(0), seq(npair), seq(2 * npair), qblk(0), qblk(0), rep,
                  pl.BlockSpec((N_FOX, nq, 1, blk), lambda p, i: (0, 0, 0, 0))],
        out_specs=(qblk(0), seq(0), seq(0), pl.BlockSpec((2, nq, 1, blk), lambda p, i: (p, 0, 0, 0)), rep),
        scratch_shapes=[pltpu.VMEM((s, LANES), F32), pltpu.VMEM((s, LANES), F32), pltpu.VMEM((2, 8, LANES), F32)],
        compiler_params=pltpu.CompilerParams(dimension_semantics=("parallel", "arbitrary")),
    )(proj, proj, proj, do, o, acol, crow)


def _sb_logs(z):
    neg = -(jnp.maximum(z, 0.0) + jnp.log1p(jnp.exp(-jnp.abs(z))))
    return neg, z + neg


def _split_dot(x, tri):
    hi = x.astype(BF16)
    lo = (x - hi.astype(F32)).astype(BF16)
    return jnp.dot(hi, tri, preferred_element_type=F32) + jnp.dot(lo, tri, preferred_element_type=F32)


def _sb_fwd(proj):
    s = proj.shape[0]
    blk = min(ATTN_BLOCK, s)
    nq = s // blk
    npair = N_SB // 2
    base = 3 * (N_FOX // 2)

    def body(q_ref, k_ref, v_ref, o_ref, r_ref):
        i = pl.program_id(1)
        q2 = q_ref[...].astype(F32) * SCALE
        row = lax.broadcasted_iota(jnp.int32, (blk, blk), 0)
        col = lax.broadcasted_iota(jnp.int32, (blk, blk), 1)
        strict = col < row
        tri = jnp.where(row > col, 1.0, 0.0).astype(BF16)
        lane = lax.broadcasted_iota(jnp.int32, (blk, LANES), 1)
        outs = []
        for hh in range(2):
            qh = jnp.where(_head_mask(q2.shape, hh), q2, 0.0).astype(BF16)

            def tile(j, carry, masked, qh=qh):
                rsum, acc, rbuf = carry
                sl = pl.ds(pl.multiple_of(j * blk, blk), blk)
                kb, vb = k_ref[sl, :], v_ref[sl, :]
                z = lax.dot_general(qh, kb, _DN_NT, preferred_element_type=F32)
                l1m, lb = _sb_logs(z)
                if masked:
                    l1m = jnp.where(strict, l1m, 0.0)
                sx = _split_dot(l1m, tri)
                a = jnp.exp(lb + sx + rsum)
                if masked:
                    a = jnp.where(strict, a, 0.0)
                acc = acc + jnp.dot(a.astype(BF16), vb, preferred_element_type=F32)
                rbuf = jnp.where(lane == j, rsum, rbuf)
                return rsum + jnp.sum(l1m, axis=1, keepdims=True), acc, rbuf

            init = (jnp.zeros((blk, 1), F32), jnp.zeros((blk, LANES), F32), jnp.full((blk, LANES), NEG_INF, F32))
            carry = tile(i, init, True)
            _, acc, rbuf = _sweep_left(i, carry, lambda j, c, tile=tile: tile(j, c, False),
                                       lambda j, c: jnp.max(c[0]) > SKIP_BELOW)
            outs.append(acc)
            r_ref[hh] = rbuf
        o_ref[...] = jnp.where(_head_mask(outs[0].shape, 0), outs[0], outs[1]).astype(BF16)

    seq = lambda b: pl.BlockSpec((s, LANES), lambda p, i: (0, b + p))
    return pl.pallas_call(
        body, name="sb_fwd", grid=(npair, nq),
        out_shape=(jax.ShapeDtypeStruct((s, SB_W), BF16), jax.ShapeDtypeStruct((N_SB, s, LANES), F32)),
        in_specs=[pl.BlockSpec((blk, LANES), lambda p, i: (i, base + p)), seq(base + npair), seq(base + 2 * npair)],
        out_specs=(pl.BlockSpec((blk, LANES), lambda p, i: (i, p)),
                   pl.BlockSpec((2, blk, LANES), lambda p, i: (p, i, 0))),
        compiler_params=pltpu.CompilerParams(dimension_semantics=("parallel", "arbitrary")),
    )(proj, proj, proj)


def _sb_bwd(proj, do, rsave):
    s = proj.shape[0]
    blk = min(ATTN_BLOCK, s)
    nq = s // blk
    npair = N_SB // 2
    base = 3 * (N_FOX // 2)

    def body(q_ref, k_ref, v_ref, do_ref, r_ref, dq_ref, dk_ref, dv_ref, dk_acc, dv_acc):
        i = pl.program_id(1)

        @pl.when(i == 0)
        def _():
            dk_acc[...] = jnp.zeros_like(dk_acc)
            dv_acc[...] = jnp.zeros_like(dv_acc)

        q2 = q_ref[...].astype(F32) * SCALE
        do2 = do_ref[...]
        row = lax.broadcasted_iota(jnp.int32, (blk, blk), 0)
        col = lax.broadcasted_iota(jnp.int32, (blk, blk), 1)
        strict = col < row
        tri_suffix = jnp.where(row > col, 1.0, 0.0).astype(BF16)
        tri_prefix = jnp.where(row < col, 1.0, 0.0).astype(BF16)
        lane = lax.broadcasted_iota(jnp.int32, (blk, LANES), 1)
        dqs = []
        for hh in range(2):
            hm = _head_mask(q2.shape, hh)
            qh = jnp.where(hm, q2, 0.0).astype(BF16)
            doh = jnp.where(hm, do2, jnp.zeros_like(do2))
            rbuf = r_ref[hh]

            def tile(j, carry, masked, qh=qh, doh=doh, rbuf=rbuf):
                pre, dq = carry
                sl = pl.ds(pl.multiple_of(j * blk, blk), blk)
                kb, vb = k_ref[sl, :], v_ref[sl, :]
                z = lax.dot_general(qh, kb, _DN_NT, preferred_element_type=F32)
                l1m, lb = _sb_logs(z)
                beta, one_m_beta = _sigmoid_parts(z)
                if masked:
                    l1m = jnp.where(strict, l1m, 0.0)
                sx = _split_dot(l1m, tri_suffix)
                rj = jnp.sum(jnp.where(lane == j, rbuf, 0.0), axis=1, keepdims=True)
                a = jnp.exp(lb + sx + rj)
                if masked:
                    a = jnp.where(strict, a, 0.0)
                da = lax.dot_general(doh, vb, _DN_NT, preferred_element_type=F32)
                g = a * da
                px = _split_dot(g, tri_prefix) + pre
                dz = g * one_m_beta - beta * px
                if masked:
                    dz = jnp.where(strict, dz, 0.0)
                dzb = dz.astype(BF16)
                dk_acc[sl, :] += lax.dot_general(dzb, qh, _DN_TN, preferred_element_type=F32)
                dv_acc[sl, :] += lax.dot_general(a.astype(BF16), doh, _DN_TN, preferred_element_type=F32)
                return pre + jnp.sum(g, axis=1, keepdims=True), dq + jnp.dot(dzb, kb, preferred_element_type=F32)

            reach = jnp.max(rbuf, axis=0, keepdims=True)
            dead = (reach <= SKIP_BELOW) & (lane[0:1, :] <= i)
            first = jnp.sum(jnp.where(dead, 1.0, 0.0)).astype(jnp.int32)
            carry = (jnp.zeros((blk, 1), F32), jnp.zeros((blk, LANES), F32))
            carry = lax.fori_loop(first, i, lambda t, c, tile=tile: tile(t, c, False), carry)
            _, dq = tile(i, carry, True)
            dqs.append(dq)
        dq_ref[...] = (jnp.where(_head_mask(dqs[0].shape, 0), dqs[0], dqs[1]) * SCALE).astype(BF16)

        @pl.when(i == nq - 1)
        def _():
            dk_ref[...] = dk_acc[...].astype(BF16)
            dv_ref[...] = dv_acc[...].astype(BF16)

    seq = lambda b: pl.BlockSpec((s, LANES), lambda p, i: (0, b + p))
    qblk = lambda b: pl.BlockSpec((blk, LANES), lambda p, i: (i, b + p))
    half = jax.ShapeDtypeStruct((s, SB_W), BF16)
    return pl.pallas_call(
        body, name="sb_bwd", grid=(npair, nq),
        out_shape=(half, half, half),
        in_specs=[qblk(base), seq(base + npair), seq(base + 2 * npair), qblk(npair),
                  pl.BlockSpec((2, blk, LANES), lambda p, i: (p, i, 0))],
        out_specs=(qblk(0), seq(0), seq(0)),
        scratch_shapes=[pltpu.VMEM((s, LANES), F32), pltpu.VMEM((s, LANES), F32)],
        compiler_params=pltpu.CompilerParams(dimension_semantics=("parallel", "arbitrary")),
    )(proj, proj, proj, do, rsave)


def _swa_scores(q, kp, kc, first):
    w = q.shape[0]
    row = lax.broadcasted_iota(jnp.int32, (w, w), 0)
    col = lax.broadcasted_iota(jnp.int32, (w, w), 1)
    sp = lax.dot_general(q, kp, _DN_NT, preferred_element_type=F32)
    sc = lax.dot_general(q, kc, _DN_NT, preferred_element_type=F32)
    sp = jnp.where((col > row) & jnp.logical_not(first), sp, NEG_INF)
    sc = jnp.where(col <= row, sc, NEG_INF)
    return sp, sc


def _scaled(q_ref_val):
    return (q_ref_val.astype(F32) * SCALE).astype(BF16)


def _swa_fwd(q, k, v, sinks):
    nh, s, hd = q.shape
    w = WINDOW
    nb = s // w

    def body(q_ref, kp_ref, kc_ref, vp_ref, vc_ref, s_ref, o_ref, lse_ref):
        i = pl.program_id(1)
        kp, kc, vp, vc = kp_ref[0], kc_ref[0], vp_ref[0], vc_ref[0]
        for r in range(GROUP):
            sink = s_ref[r][:, 0:1]
            sp, sc = _swa_scores(_scaled(q_ref[r]), kp, kc, i == 0)
            m = jnp.maximum(jnp.maximum(jnp.max(sp, axis=1, keepdims=True), jnp.max(sc, axis=1, keepdims=True)), sink)
            ep, ec = jnp.exp(sp - m), jnp.exp(sc - m)
            l = jnp.sum(ep, axis=1, keepdims=True) + jnp.sum(ec, axis=1, keepdims=True) + jnp.exp(sink - m)
            acc = (jnp.dot(ep.astype(BF16), vp, preferred_element_type=F32)
                   + jnp.dot(ec.astype(BF16), vc, preferred_element_type=F32))
            o_ref[r] = (acc / l).astype(BF16)
            lse_ref[r] = jnp.broadcast_to(m + jnp.log(l), (w, LANES))

    qs = pl.BlockSpec((GROUP, w, hd), lambda g, i: (g, i, 0))
    prev = pl.BlockSpec((1, w, hd), lambda g, i: (g, jnp.maximum(i - 1, 0), 0))
    cur = pl.BlockSpec((1, w, hd), lambda g, i: (g, i, 0))
    return pl.pallas_call(
        body, name="swa_fwd", grid=(nh // GROUP, nb),
        out_shape=(jax.ShapeDtypeStruct((nh, s, hd), BF16), jax.ShapeDtypeStruct((nh, s, LANES), F32)),
        in_specs=[qs, prev, cur, prev, cur, pl.BlockSpec((GROUP, 1, LANES), lambda g, i: (g, 0, 0))],
        out_specs=(qs, pl.BlockSpec((GROUP, w, LANES), lambda g, i: (g, i, 0))),
    )(q, k, k, v, v, sinks)


def _swa_bwd_dq(q, k, v, sinks, do, o, lse):
    nh, s, hd = q.shape
    w = WINDOW
    nb = s // w

    def body(q_ref, kp_ref, kc_ref, vp_ref, vc_ref, s_ref, do_ref, o_ref, lse_ref, dq_ref, dsink_ref):
        i = pl.program_id(1)

        @pl.when(i == 0)
        def _():
            dsink_ref[...] = jnp.zeros_like(dsink_ref)

        kp, kc, vp, vc = kp_ref[0], kc_ref[0], vp_ref[0], vc_ref[0]
        for r in range(GROUP):
            sink = s_ref[r][:, 0:1]
            lse = lse_ref[r][:, 0:1]
            sp, sc = _swa_scores(_scaled(q_ref[r]), kp, kc, i == 0)
            pp, pc = jnp.exp(sp - lse), jnp.exp(sc - lse)
            dov = do_ref[r]
            delta = jnp.sum(dov.astype(F32) * o_ref[r].astype(F32), axis=1, keepdims=True)
            dsp = pp * (lax.dot_general(dov, vp, _DN_NT, preferred_element_type=F32) - delta)
            dsc = pc * (lax.dot_general(dov, vc, _DN_NT, preferred_element_type=F32) - delta)
            dq = (jnp.dot(dsp.astype(BF16), kp, preferred_element_type=F32)
                  + jnp.dot(dsc.astype(BF16), kc, preferred_element_type=F32))
            dq_ref[r] = (dq * SCALE).astype(BF16)
            part = jnp.sum(-jnp.exp(sink - lse) * delta, axis=0, keepdims=True)
            dsink_ref[r] += jnp.broadcast_to(part, (1, LANES))

    qs = pl.BlockSpec((GROUP, w, hd), lambda g, i: (g, i, 0))
    prev = pl.BlockSpec((1, w, hd), lambda g, i: (g, jnp.maximum(i - 1, 0), 0))
    cur = pl.BlockSpec((1, w, hd), lambda g, i: (g, i, 0))
    vec = pl.BlockSpec((GROUP, 1, LANES), lambda g, i: (g, 0, 0))
    return pl.pallas_call(
        body, name="swa_bwd_dq", grid=(nh // GROUP, nb),
        out_shape=(jax.ShapeDtypeStruct((nh, s, hd), BF16), jax.ShapeDtypeStruct((nh, 1, LANES), F32)),
        in_specs=[qs, prev, cur, prev, cur, vec, qs, qs, pl.BlockSpec((GROUP, w, LANES), lambda g, i: (g, i, 0))],
        out_specs=(qs, vec),
        compiler_params=pltpu.CompilerParams(dimension_semantics=("parallel", "arbitrary")),
    )(q, k, k, v, v, sinks, do, o, lse)


def _swa_bwd_dkv(q, k, v, do, o, lse):
    nh, s, hd = q.shape
    nkv = k.shape[0]
    w = WINDOW
    nb = s // w

    def body(k_ref, v_ref, qa_ref, doa_ref, oa_ref, la_ref, qb_ref, dob_ref, ob_ref, lb_ref, dk_ref, dv_ref):
        j = pl.program_id(1)
        kk, vv = k_ref[0], v_ref[0]
        row = lax.broadcasted_iota(jnp.int32, (w, w), 0)
        col = lax.broadcasted_iota(jnp.int32, (w, w), 1)
        has_next = j + 1 < nb
        dk = jnp.zeros((w, hd), F32)
        dv = jnp.zeros((w, hd), F32)
        for r in range(GROUP):
            for q_ref, do_ref, o_ref, l_ref, valid in (
                    (qa_ref, doa_ref, oa_ref, la_ref, col <= row),
                    (qb_ref, dob_ref, ob_ref, lb_ref, (col > row) & has_next)):
                qs = _scaled(q_ref[r])
                dov = do_ref[r]
                sc = lax.dot_general(qs, kk, _DN_NT, preferred_element_type=F32)
                sc = jnp.where(valid, sc, NEG_INF)
                pm = jnp.exp(sc - l_ref[r][:, 0:1])
                delta = jnp.sum(dov.astype(F32) * o_ref[r].astype(F32), axis=1, keepdims=True)
                ds = pm * (lax.dot_general(dov, vv, _DN_NT, preferred_element_type=F32) - delta)
                dk = dk + lax.dot_general(ds.astype(BF16), qs, _DN_TN, preferred_element_type=F32)
                dv = dv + lax.dot_general(pm.astype(BF16), dov, _DN_TN, preferred_element_type=F32)
        dk_ref[0] = dk.astype(BF16)
        dv_ref[0] = dv.astype(BF16)

    kv = pl.BlockSpec((1, w, hd), lambda g, j: (g, j, 0))
    same = pl.BlockSpec((GROUP, w, hd), lambda g, j: (g, j, 0))
    nxt = pl.BlockSpec((GROUP, w, hd), lambda g, j: (g, jnp.minimum(j + 1, nb - 1), 0))
    lsame = pl.BlockSpec((GROUP, w, LANES), lambda g, j: (g, j, 0))
    lnxt = pl.BlockSpec((GROUP, w, LANES), lambda g, j: (g, jnp.minimum(j + 1, nb - 1), 0))
    shp = jax.ShapeDtypeStruct((nkv, s, hd), BF16)
    return pl.pallas_call(
        body, name="swa_bwd_dkv", grid=(nkv, nb),
        out_shape=(shp, shp),
        in_specs=[kv, kv, same, same, same, lsame, nxt, nxt, nxt, lnxt],
        out_specs=(kv, kv),
    )(k, v, q, do, o, lse, q, do, o, lse)


def _heads_major(xm, nheads):
    s = xm.shape[0]
    return xm.reshape(s, nheads, HEAD_DIM).transpose(1, 0, 2)


def _heads_minor(xh):
    nheads, s, _ = xh.shape
    return xh.transpose(1, 0, 2).reshape(s, nheads * HEAD_DIM)


def _ffn_ple_fwd(h1, p_l, g_ffn, g_ple, w_gu, w_down, w_pg, w_pp, tag):
    hn2 = _rms_fwd(h1, g_ffn, f"rms_ffn_{tag}")
    ab = _mm(hn2, w_gu, name=f"mm_gate_up_{tag}")
    u = _swiglu_fwd(ab, f"swiglu_{tag}")
    h2 = _mm(u, w_down, out_dtype=F32, res=h1, name=f"mm_down_{tag}")
    hn3 = _rms_fwd(h2, g_ple, f"rms_ple_{tag}")
    gl = _mm(hn3, w_pg, name=f"mm_ple_gate_{tag}")
    pp = _mm(p_l, w_pp, name=f"mm_ple_proj_{tag}")
    h3 = _ple_fwd(h2, gl, pp, f"ple_{tag}")
    return h3, dict(h1=h1, hn2=hn2, ab=ab, u=u, h2=h2, hn3=hn3, gl=gl, pp=pp)


def _ffn_ple_bwd(dh3, sv, p_l, g_ffn, g_ple, w_gu, w_down, w_pg, tag):
    dgl, dpp = _ple_bwd(dh3, sv["gl"], sv["pp"], f"ple_bwd_{tag}")
    d_wpp = _mm(p_l, dpp, ta=True, out_dtype=F32, name=f"mm_dw_ple_proj_{tag}")
    d_wpg = _mm(sv["hn3"], dgl, ta=True, out_dtype=F32, name=f"mm_dw_ple_gate_{tag}")
    dhn3 = _mm(dgl, w_pg, tb=True, out_dtype=F32, name=f"mm_dx_ple_gate_{tag}")
    dh2, dh2b, dg_ple = _rms_bwd(sv["h2"], g_ple, [dhn3], dh3, f"rms_ple_bwd_{tag}", True)
    du = _mm(dh2b, w_down, tb=True, name=f"mm_dx_down_{tag}")
    d_wdown = _mm(sv["u"], dh2b, ta=True, out_dtype=F32, name=f"mm_dw_down_{tag}")
    dab = _swiglu_bwd(sv["ab"], du, f"swiglu_bwd_{tag}")
    d_wgu = _mm(sv["hn2"], dab, ta=True, out_dtype=F32, name=f"mm_dw_gate_up_{tag}")
    dhn2 = _mm(dab, w_gu, tb=True, out_dtype=F32, name=f"mm_dx_gate_up_{tag}")
    dh1, dh1b, dg_ffn = _rms_bwd(sv["h1"], g_ffn, [dhn2], dh2, f"rms_ffn_bwd_{tag}", True)
    return dh1, dh1b, dict(d_wpp=d_wpp, d_wpg=d_wpg, d_wdown=d_wdown, d_wgu=d_wgu, dg_ple=dg_ple, dg_ffn=dg_ffn)


def _row_form(cum, blk):
    s = cum.shape[0]
    return cum[:, :N_FOX].T.reshape(N_FOX, s // blk, 1, blk)


def _col_form(cum):
    s = cum.shape[0]
    return jnp.broadcast_to(cum[:, :N_FOX].T[:, :, None], (N_FOX, s, LANES))


def kernel(x, p, positions, norm_mix, norm_ffn, norm_ple, norm_final, ev_w_in, ev_b_f, ev_w_out, od_w_in, od_sinks, od_w_out, ffn_w_gate, ffn_w_up, ffn_w_down, ple_w_proj, ple_w_gate, loss_target, m_norm_mix, m_norm_ffn, m_norm_ple, m_norm_final, m_ev_w_in, m_ev_b_f, m_ev_w_out, m_od_w_in, m_od_sinks, m_od_w_out, m_ffn_w_gate, m_ffn_w_up, m_ffn_w_down, m_ple_w_proj, m_ple_w_gate, v_norm_mix, v_norm_ffn, v_norm_ple, v_norm_final, v_ev_w_in, v_ev_b_f, v_ev_w_out, v_od_w_in, v_od_sinks, v_od_w_out, v_ffn_w_gate, v_ffn_w_up, v_ffn_w_down, v_ple_w_proj, v_ple_w_gate):
    s = x.shape[1]
    blk = min(ATTN_BLOCK, s)
    big_w = [ev_w_in, ev_w_out, od_w_in, od_w_out, ffn_w_gate, ffn_w_up, ffn_w_down, ple_w_proj, ple_w_gate]
    big_m = [m_ev_w_in, m_ev_w_out, m_od_w_in, m_od_w_out, m_ffn_w_gate, m_ffn_w_up, m_ffn_w_down, m_ple_w_proj, m_ple_w_gate]
    big_v = [v_ev_w_in, v_ev_w_out, v_od_w_in, v_od_w_out, v_ffn_w_gate, v_ffn_w_up, v_ffn_w_down, v_ple_w_proj, v_ple_w_gate]

    gathered = _all_gather_weights(_pack_shards(big_w, BF16))
    wfull = _unpack_gathered(gathered)
    w_qkv = wfull["ev_w_in"][0][:, :QKV_W]
    w_f = jnp.pad(wfull["ev_w_in"][0][:, QKV_W:], ((0, 0), (0, LANES - N_FOX)))
    w_eo = wfull["ev_w_out"][0]
    w_oi = wfull["od_w_in"][0]
    w_oo = wfull["od_w_out"][0]
    w_gu = [jnp.concatenate([wfull["ffn_w_gate"][l], wfull["ffn_w_up"][l]], axis=1) for l in range(2)]
    w_down = [wfull["ffn_w_down"][l] for l in range(2)]
    w_pp = [wfull["ple_w_proj"][l] for l in range(2)]
    w_pg = [wfull["ple_w_gate"][l] for l in range(2)]

    h0 = x[0]
    target = loss_target[0]
    p_b = [p[l, 0].astype(BF16) for l in range(2)]
    g_mix = [norm_mix[l][None, :] for l in range(2)]
    g_ffn = [norm_ffn[l][None, :] for l in range(2)]
    g_ple = [norm_ple[l][None, :] for l in range(2)]
    b_f = jnp.pad(ev_b_f, ((0, 0), (0, LANES - N_FOX)))

    half = HEAD_DIM // 2
    inv = ROPE_THETA ** (-jnp.arange(half, dtype=F32) / half)
    ang = positions[0].astype(F32)[:, None] * inv
    cos_t = jnp.tile(jnp.cos(ang), (1, 4))
    sin_t = jnp.tile(jnp.concatenate([-jnp.sin(ang), jnp.sin(ang)], axis=1), (1, 2))

    hn1 = _rms_fwd(h0, g_mix[0], "rms_mix_0")
    proj0 = _mm(hn1, w_qkv, name="mm_in_0")
    flog = _mm(hn1, w_f, out_dtype=F32, name="mm_fgate_0")
    cum = _fgate_fwd(flog, b_f)
    crow = _row_form(cum, blk)
    o_fox, acol = _fox_fwd(proj0, _col_form(cum), crow)
    o_sb, rsave = _sb_fwd(proj0)
    o0 = jnp.concatenate([o_fox, o_sb], axis=1)
    h1 = _mm(o0, w_eo, out_dtype=F32, res=h0, name="mm_out_0")
    h3, sv0 = _ffn_ple_fwd(h1, p_b[0], g_ffn[0], g_ple[0], w_gu[0], w_down[0], w_pg[0], w_pp[0], "0")

    hn1b = _rms_fwd(h3, g_mix[1], "rms_mix_1")
    proj1 = _mm(hn1b, w_oi, name="mm_in_1")
    nq_w, nk_w = N_Q * HEAD_DIM, N_KV * HEAD_DIM
    qk_r = _rope(proj1[:, :nq_w + nk_w], cos_t, sin_t, 1.0, "rope_fwd")
    q_h = _heads_major(qk_r[:, :nq_w], N_Q)
    k_h = _heads_major(qk_r[:, nq_w:], N_KV)
    v_h = _heads_major(proj1[:, nq_w + nk_w:], N_KV)
    sinks_rep = jnp.broadcast_to(od_sinks[0][:, None, None], (N_Q, 1, LANES))
    o_h, lse1 = _swa_fwd(q_h, k_h, v_h, sinks_rep)
    o1 = _heads_minor(o_h)
    h4 = _mm(o1, w_oo, out_dtype=F32, res=h3, name="mm_out_1")
    h6, sv1 = _ffn_ple_fwd(h4, p_b[1], g_ffn[1], g_ple[1], w_gu[1], w_down[1], w_pg[1], w_pp[1], "1")

    loss_part, dh6, dg_final = _final_norm_loss(h6, norm_final[None, :], target)

    dh4, dh4b, gr1 = _ffn_ple_bwd(dh6, sv1, p_b[1], g_ffn[1], g_ple[1], w_gu[1], w_down[1], w_pg[1], "1")
    do1 = _mm(dh4b, w_oo, tb=True, name="mm_dx_out_1")
    d_woo = _mm(o1, dh4b, ta=True, out_dtype=F32, name="mm_dw_out_1")
    do_h = _heads_major(do1, N_Q)
    dq_h, dsink = _swa_bwd_dq(q_h, k_h, v_h, sinks_rep, do_h, o_h, lse1)
    dk_h, dv_h = _swa_bwd_dkv(q_h, k_h, v_h, do_h, o_h, lse1)
    dqk = _rope(jnp.concatenate([_heads_minor(dq_h), _heads_minor(dk_h)], axis=1), cos_t, sin_t, -1.0, "rope_bwd")
    dproj1 = jnp.concatenate([dqk, _heads_minor(dv_h)], axis=1)
    d_woi = _mm(hn1b, dproj1, ta=True, out_dtype=F32, name="mm_dw_in_1")
    dhn1b = _mm(dproj1, w_oi, tb=True, out_dtype=F32, name="mm_dx_in_1")
    dh3, dg_mix1 = _rms_bwd(h3, g_mix[1], [dhn1b], dh4, "rms_mix_bwd_1", False)

    dh1, dh1b, gr0 = _ffn_ple_bwd(dh3, sv0, p_b[0], g_ffn[0], g_ple[0], w_gu[0], w_down[0], w_pg[0], "0")
    do0 = _mm(dh1b, w_eo, tb=True, name="mm_dx_out_0")
    d_weo = _mm(o0, dh1b, ta=True, out_dtype=F32, name="mm_dw_out_0")
    dq_f, dk_f, dv_f, gc, gr = _fox_bwd(proj0, do0, o0, acol, crow)
    dq_s, dk_s, dv_s = _sb_bwd(proj0, do0, rsave)
    dproj0 = jnp.concatenate([dq_f, dk_f, dv_f, dq_s, dk_s, dv_s], axis=1)
    gcum_k = jnp.pad(gc.reshape(N_FOX, s).T, ((0, 0), (0, LANES - N_FOX)))
    gcum_q = jnp.pad(gr[:, :, 0].T, ((0, 0), (0, LANES - N_FOX)))
    dflog, db_f = _fgate_bwd(gcum_k, gcum_q, flog, b_f)
    d_wqkv = _mm(hn1, dproj0, ta=True, out_dtype=F32, name="mm_dw_in_0")
    d_wf = _mm(hn1, dflog, ta=True, out_dtype=F32, name="mm_dw_fgate_0")
    dhn1 = _mm(dproj0, w_qkv, tb=True, out_dtype=F32, name="mm_dx_in_0")
    dhn1f = _mm(dflog, w_f, tb=True, out_dtype=F32, name="mm_dx_fgate_0")
    grad_x, dg_mix0 = _rms_bwd(h0, g_mix[0], [dhn1, dhn1f], dh1, "rms_mix_bwd_0", False)

    grs = (gr0, gr1)
    full_grads = {
        "ev_w_in": jnp.concatenate([d_wqkv, d_wf[:, :N_FOX]], axis=1)[None],
        "ev_w_out": d_weo[None],
        "od_w_in": d_woi[None],
        "od_w_out": d_woo[None],
        "ffn_w_gate": jnp.stack([g["d_wgu"][:, :D_FF] for g in grs]),
        "ffn_w_up": jnp.stack([g["d_wgu"][:, D_FF:] for g in grs]),
        "ffn_w_down": jnp.stack([g["d_wdown"] for g in grs]),
        "ple_w_proj": jnp.stack([g["d_wpp"] for g in grs]),
        "ple_w_gate": jnp.stack([g["d_wpg"] for g in grs]),
    }
    gpacked = _pack_full_grads(full_grads)
    core = lax.axis_index("c").astype(jnp.int32).reshape(1)
    from_sibling = _rs_sibling_exchange(gpacked)
    chip_part = _rs_chip_sum(core, gpacked, from_sibling)
    chip_recv = _rs_chip_exchange(chip_part)
    g_flat, d_flat, m_flat, v_flat = _rs_sum_adamw(
        chip_recv, _pack_shards(big_w, F32), _pack_shards(big_m, F32), _pack_shards(big_v, F32))
    big_g, big_d, big_nm, big_nv = (_unpack_shards(f) for f in (g_flat, d_flat, m_flat, v_flat))

    def small_pack(nmix, nffn, nple, nfin, bf, sk, extra):
        last = jnp.concatenate([bf.reshape(-1), sk.reshape(-1), extra.reshape(-1)])
        last = jnp.pad(last, (0, D_MODEL - last.shape[0]))
        return jnp.concatenate([nmix, nffn, nple, nfin.reshape(1, -1), last[None, :]], axis=0)

    small_g = small_pack(jnp.concatenate([dg_mix0, dg_mix1]), jnp.concatenate([gr0["dg_ffn"], gr1["dg_ffn"]]),
                         jnp.concatenate([gr0["dg_ple"], gr1["dg_ple"]]), dg_final,
                         db_f[0, :N_FOX], dsink[:, 0, 0], loss_part[0, :1])
    zero1 = jnp.zeros((1,), F32)
    small_w = small_pack(norm_mix, norm_ffn, norm_ple, norm_final, ev_b_f, od_sinks, zero1)
    small_m = small_pack(m_norm_mix, m_norm_ffn, m_norm_ple, m_norm_final, m_ev_b_f, m_od_sinks, zero1)
    small_v = small_pack(v_norm_mix, v_norm_ffn, v_norm_ple, v_norm_final, v_ev_b_f, v_od_sinks, zero1)
    sg, sd, sm, sv_ = _small_allreduce_adamw(small_g, small_w, small_m, small_v)

    def small_unpack(t):
        return [t[0:2], t[2:4], t[4:6], t[6], t[7, :N_FOX][None, :], t[7, N_FOX:N_FOX + N_Q][None, :]]

    loss = sg[7, N_FOX + N_Q]

    def ordered(small, big):
        nm, nf, npl, nfin, bf, sk = small_unpack(small)
        ev_in, ev_out, od_in, od_out, fg, fu, fd, pproj, pgate = big
        return [nm, nf, npl, nfin, ev_in, bf, ev_out, od_in, sk, od_out, fg, fu, fd, pproj, pgate]

    return (loss, grad_x[None], *ordered(sg, big_g), *ordered(sd, big_d),
            *ordered(sm, big_nm), *ordered(sv_, big_nv))
```

```python
import functools

import jax
import jax.numpy as jnp
from jax import lax
from jax.experimental import pallas as pl
from jax.experimental.pallas import tpu as pltpu

F32 = jnp.float32
BF16 = jnp.bfloat16

D_MODEL = 1024
HEAD_DIM = 64
N_FOX = 8
N_SB = 8
FOX_W = N_FOX * HEAD_DIM
SB_W = N_SB * HEAD_DIM
QKV_W = 3 * FOX_W + 3 * SB_W
EVEN_IN = QKV_W + N_FOX
N_Q = 16
N_KV = 4
GROUP = N_Q // N_KV
ODD_IN = N_Q * HEAD_DIM + 2 * N_KV * HEAD_DIM
WINDOW = 128
ROPE_THETA = 10000.0
D_FF = 2816
PLE_DIM = 256
EPS = 1e-6
NEG_INF = -1e30
SCALE = HEAD_DIM ** -0.5

ADAM_LR = 0.001
ADAM_B1 = 0.9
ADAM_B2 = 0.999
ADAM_EPS = 1e-08
ADAM_WD = 0.01
ADAM_STEP = 10

N_DEV = 8
LANES = 128
ROW_TILE = 256
ATTN_BLOCK = 256
HID_SHARD = D_FF // N_DEV
HID_PAD = 384
D_FF_PAD = N_DEV * HID_PAD

MESH = pl.DeviceIdType.MESH


def _pick(n, prefs):
    for t in prefs:
        if n % t == 0:
            return t
    return n


def _pad_to(a, axis, size):
    pad = [(0, 0)] * a.ndim
    pad[axis] = (0, size - a.shape[axis])
    return jnp.pad(a, pad)


def _group_shards(ev_in, ev_out, od_in, od_out, gate, up, down, pproj, pgate, dtype):
    rows = jnp.concatenate([ev_out[0], od_out[0], _pad_to(down[0], 0, HID_PAD), _pad_to(down[1], 0, HID_PAD),
                            pgate[0], pgate[1]], axis=0)
    gu = jnp.concatenate([jnp.concatenate([_pad_to(gate[l], 1, HID_PAD), _pad_to(up[l], 1, HID_PAD)], axis=1)
                          for l in range(2)], axis=0)
    groups = [rows, gu, ev_in[0], od_in[0], pproj.reshape(-1, pproj.shape[-1])]
    return [g.astype(dtype) for g in groups]


def _ungroup_shards(groups):
    rows, gu, ev_in, od_in, pp = groups
    d = D_MODEL
    down = jnp.stack([rows[256 + HID_PAD * l: 256 + HID_PAD * l + HID_SHARD] for l in range(2)])
    pgate = jnp.stack([rows[1024 + 128 * l: 1152 + 128 * l] for l in range(2)])
    gate = jnp.stack([gu[d * l: d * (l + 1), :HID_SHARD] for l in range(2)])
    up = jnp.stack([gu[d * l: d * (l + 1), HID_PAD:HID_PAD + HID_SHARD] for l in range(2)])
    return [ev_in[None], rows[None, 0:128], od_in[None], rows[None, 128:256], gate, up, down,
            pp.reshape(2, PLE_DIM, -1), pgate]


def _by_owner_cols(full):
    r, c8 = full.shape
    return full.reshape(r, N_DEV, c8 // N_DEV).transpose(1, 0, 2)


def _from_owner_cols(g):
    n, r, c = g.shape
    return g.transpose(1, 0, 2).reshape(r, n * c)


_ANY = pl.BlockSpec(memory_space=pl.ANY)


def _all_gather_weights(shards):
    n = len(shards)

    def body(*refs):
        x_refs, out_refs = refs[:n], refs[n:2 * n]
        send_sems, recv_sems, local_sems = refs[2 * n:]
        x, y, c = lax.axis_index("x"), lax.axis_index("y"), lax.axis_index("c")
        me, sibling = (x, y, c), (x, y, 1 - c)
        chips = [(1 - x, y), (x, 1 - y), (1 - x, 1 - y)]

        def copy(a, k, block, to, from_input=False):
            px, py, pc = block
            slot = out_refs[a].at[4 * px + 2 * py + pc]
            return pltpu.make_async_remote_copy(
                src_ref=x_refs[a] if from_input else slot, dst_ref=slot,
                send_sem=send_sems.at[7 * a + k], recv_sem=recv_sems.at[7 * a + k],
                device_id=to, device_id_type=MESH)

        mine = [pltpu.make_async_copy(x_refs[a], out_refs[a].at[4 * x + 2 * y + c], local_sems.at[a]) for a in range(n)]
        for cp in mine:
            cp.start()
        first = []
        for a in range(n):
            first.append(copy(a, 0, me, sibling, True))
            first += [copy(a, 1 + j, me, (*chip, c), True) for j, chip in enumerate(chips)]
        for cp in first:
            cp.start()
        passed = []
        for j, chip in enumerate(chips):
            for a in range(n):
                copy(a, 1 + j, (*chip, c), me).wait_recv()
                passed.append(copy(a, 4 + j, (*chip, c), sibling))
                passed[-1].start()
        for a in range(n):
            copy(a, 0, sibling, me).wait_recv()
            for j, chip in enumerate(chips):
                copy(a, 4 + j, (*chip, 1 - c), me).wait_recv()
        for cp in first + passed:
            cp.wait_send()
        for cp in mine:
            cp.wait()

    return pl.pallas_call(
        body, name="ag_weights",
        out_shape=tuple(jax.ShapeDtypeStruct((N_DEV,) + s.shape, s.dtype) for s in shards),
        in_specs=[_ANY] * n, out_specs=tuple([_ANY] * n),
        scratch_shapes=[pltpu.SemaphoreType.DMA((7 * n,)), pltpu.SemaphoreType.DMA((7 * n,)),
                        pltpu.SemaphoreType.DMA((n,))],
    )(*shards)


def _rs_sibling_exchange(gps):
    n = len(gps)

    def body(*refs):
        g_refs, out_refs = refs[:n], refs[n:2 * n]
        send_sems, recv_sems = refs[2 * n:]
        x, y, c = lax.axis_index("x"), lax.axis_index("y"), lax.axis_index("c")
        copies = []
        for a in range(n):
            for k in range(4):
                copies.append(pltpu.make_async_remote_copy(
                    src_ref=g_refs[a].at[2 * k + (1 - c)], dst_ref=out_refs[a].at[k],
                    send_sem=send_sems.at[4 * a + k], recv_sem=recv_sems.at[4 * a + k],
                    device_id=(x, y, 1 - c), device_id_type=MESH))
        for cp in copies:
            cp.start()
        for cp in copies:
            cp.wait_recv()
        for cp in copies:
            cp.wait_send()

    return pl.pallas_call(
        body, name="rs_sibling_exchange",
        out_shape=tuple(jax.ShapeDtypeStruct((4,) + g.shape[1:], g.dtype) for g in gps),
        in_specs=[_ANY] * n, out_specs=tuple([_ANY] * n),
        scratch_shapes=[pltpu.SemaphoreType.DMA((4 * n,)), pltpu.SemaphoreType.DMA((4 * n,))],
    )(*gps)


def _rs_chip_sum(core, gp, recv, name):
    _, rows, cols = gp.shape
    tr = ROW_TILE

    def body(core_ref, a_ref, b_ref, o_ref):
        o_ref[...] = (a_ref[...] + b_ref[...]).astype(BF16)

    return pl.pallas_call(
        body, name=name,
        out_shape=jax.ShapeDtypeStruct((4, rows, cols), BF16),
        grid_spec=pltpu.PrefetchScalarGridSpec(
            num_scalar_prefetch=1, grid=(4, rows // tr),
            in_specs=[pl.BlockSpec((1, tr, cols), lambda k, r, cr: (2 * k + cr[0], r, 0)),
                      pl.BlockSpec((1, tr, cols), lambda k, r, cr: (k, r, 0))],
            out_specs=pl.BlockSpec((1, tr, cols), lambda k, r, cr: (k, r, 0))),
    )(core, gp, recv)


def _rs_chip_exchange(parts):
    n = len(parts)

    def body(*refs):
        p_refs, out_refs = refs[:n], refs[n:2 * n]
        send_sems, recv_sems, local_sems = refs[2 * n:]
        x, y, c = lax.axis_index("x"), lax.axis_index("y"), lax.axis_index("c")
        my_chip = 2 * x + y
        mine = [pltpu.make_async_copy(p_refs[a].at[my_chip], out_refs[a].at[my_chip], local_sems.at[a])
                for a in range(n)]
        for cp in mine:
            cp.start()
        copies = []
        for a in range(n):
            for j, (px, py) in enumerate([(1 - x, y), (x, 1 - y), (1 - x, 1 - y)]):
                copies.append(pltpu.make_async_remote_copy(
                    src_ref=p_refs[a].at[2 * px + py], dst_ref=out_refs[a].at[my_chip],
                    send_sem=send_sems.at[3 * a + j], recv_sem=recv_sems.at[3 * a + j],
                    device_id=(px, py, c), device_id_type=MESH))
        for cp in copies:
            cp.start()
        for cp in copies:
            cp.wait_recv()
        for cp in copies:
            cp.wait_send()
        for cp in mine:
            cp.wait()

    return pl.pallas_call(
        body, name="rs_chip_exchange",
        out_shape=tuple(jax.ShapeDtypeStruct(p_.shape, p_.dtype) for p_ in parts),
        in_specs=[_ANY] * n, out_specs=tuple([_ANY] * n),
        scratch_shapes=[pltpu.SemaphoreType.DMA((3 * n,)), pltpu.SemaphoreType.DMA((3 * n,)),
                        pltpu.SemaphoreType.DMA((n,))],
    )(*parts)


def _adamw(w, g, m, v):
    m = ADAM_B1 * m + (1.0 - ADAM_B1) * g
    v = ADAM_B2 * v + (1.0 - ADAM_B2) * (g * g)
    m_hat = m / (1.0 - ADAM_B1 ** ADAM_STEP)
    v_hat = v / (1.0 - ADAM_B2 ** ADAM_STEP)
    delta = -ADAM_LR * (m_hat / (jnp.sqrt(v_hat) + ADAM_EPS) + ADAM_WD * w)
    return delta, m, v


def _rs_sum_adamw(recv, w, m, v, name):
    _, rows, cols = recv.shape
    tr = ROW_TILE

    def body(r_ref, w_ref, m_ref, v_ref, g_out, d_out, m_out, v_out):
        g = r_ref[0].astype(F32)
        for k in range(1, 4):
            g = g + r_ref[k].astype(F32)
        delta, m_new, v_new = _adamw(w_ref[...], g, m_ref[...], v_ref[...])
        g_out[...] = g
        d_out[...] = delta
        m_out[...] = m_new
        v_out[...] = v_new

    flat = pl.BlockSpec((tr, cols), lambda r: (r, 0))
    shp = jax.ShapeDtypeStruct((rows, cols), F32)
    return pl.pallas_call(
        body, name=name, grid=(rows // tr,),
        out_shape=(shp, shp, shp, shp),
        in_specs=[pl.BlockSpec((4, tr, cols), lambda r: (0, r, 0)), flat, flat, flat],
        out_specs=(flat, flat, flat, flat),
    )(recv, w, m, v)


def _small_allreduce_adamw(vec, w, m, v):
    rows, cols = vec.shape

    def body(x_ref, w_ref, m_ref, v_ref, g_out, d_out, m_out, v_out, gather, send_sems, recv_sems):
        x, y, c = lax.axis_index("x"), lax.axis_index("y"), lax.axis_index("c")
        me = 4 * x + 2 * y + c
        copies = []
        for d in range(1, N_DEV):
            dx, dy, dc = (d >> 2) & 1, (d >> 1) & 1, d & 1
            peer = (x ^ dx if dx else x, y ^ dy if dy else y, c ^ dc if dc else c)
            copies.append(pltpu.make_async_remote_copy(
                src_ref=x_ref, dst_ref=gather.at[me],
                send_sem=send_sems.at[d - 1], recv_sem=recv_sems.at[d - 1],
                device_id=peer, device_id_type=MESH))
        for cp in copies:
            cp.start()
        gather[me] = x_ref[...]
        for cp in copies:
            cp.wait_recv()
        for cp in copies:
            cp.wait_send()
        g = gather[0]
        for k in range(1, N_DEV):
            g = g + gather[k]
        delta, m_new, v_new = _adamw(w_ref[...], g, m_ref[...], v_ref[...])
        g_out[...] = g
        d_out[...] = delta
        m_out[...] = m_new
        v_out[...] = v_new

    vm = pl.BlockSpec(memory_space=pltpu.VMEM)
    shp = jax.ShapeDtypeStruct((rows, cols), F32)
    return pl.pallas_call(
        body, name="small_allreduce_adamw",
        out_shape=(shp, shp, shp, shp),
        in_specs=[vm, vm, vm, vm], out_specs=(vm, vm, vm, vm),
        scratch_shapes=[pltpu.VMEM((N_DEV, rows, cols), F32),
                        pltpu.SemaphoreType.DMA((N_DEV - 1,)), pltpu.SemaphoreType.DMA((N_DEV - 1,))],
    )(vec, w, m, v)


def _mm(a, b, *, ta=False, tb=False, out_dtype=BF16, res=None, name, group=None, group_width=None):
    if ta:
        kdim, m = a.shape
    else:
        m, kdim = a.shape
    if group == "n":
        assert not ta and not tb and res is None
        ng, kb, tn = b.shape
        n = ng * tn
    elif group == "k":
        assert tb and not ta and res is None
        ng, n, tk = b.shape
        kb = ng * tk
    elif tb:
        n, kb = b.shape
    else:
        kb, n = b.shape
    assert kdim == kb, (a.shape, b.shape, ta, tb, group)
    tm = _pick(m, (1024, 512, 256, 128))
    if group == "out":
        assert ta and not tb and res is None
        tn = group_width
    elif group != "n":
        tn = _pick(n, (1024, 1408, 768, 512, 256, 128))
    if group != "k":
        tk = _pick(kdim, (1024, 1408, 512, 256, 128))
    nk = kdim // tk
    dn = (((0 if ta else 1,), (1 if tb else 0,)), ((), ()))
    has_res = res is not None
    in_place = nk > 1 and out_dtype == F32
    use_acc = nk > 1 and not in_place

    def body(*refs):
        a_ref, b_ref = refs[:2]
        r_ref = refs[2] if has_res else None
        o_ref = refs[3] if has_res else refs[2]
        part = lax.dot_general(a_ref[...], b_ref[...], dn, preferred_element_type=F32)
        if nk == 1:
            if has_res:
                part = part + r_ref[...].astype(F32)
            o_ref[...] = part.astype(out_dtype)
            return
        k = pl.program_id(2)
        acc = o_ref if in_place else refs[-1]

        @pl.when(k == 0)
        def _():
            acc[...] = part + r_ref[...].astype(F32) if has_res else part

        @pl.when(k > 0)
        def _():
            acc[...] += part

        if use_acc:
            @pl.when(k == nk - 1)
            def _():
                o_ref[...] = acc[...].astype(out_dtype)

    a_spec = (pl.BlockSpec((tk, tm), lambda i, j, k: (k, i)) if ta
              else pl.BlockSpec((tm, tk), lambda i, j, k: (i, k)))
    if group == "n":
        b_spec = pl.BlockSpec((None, tk, tn), lambda i, j, k: (j, k, 0))
    elif group == "k":
        b_spec = pl.BlockSpec((None, tn, tk), lambda i, j, k: (k, j, 0))
    elif tb:
        b_spec = pl.BlockSpec((tn, tk), lambda i, j, k: (j, k))
    else:
        b_spec = pl.BlockSpec((tk, tn), lambda i, j, k: (k, j))
    if group == "out":
        o_spec = pl.BlockSpec((None, tm, tn), lambda i, j, k: (j, i, 0))
        out_shape = jax.ShapeDtypeStruct((n // tn, m, tn), out_dtype)
    else:
        o_spec = pl.BlockSpec((tm, tn), lambda i, j, k: (i, j))
        out_shape = jax.ShapeDtypeStruct((m, n), out_dtype)
    in_specs = [a_spec, b_spec] + ([o_spec] if has_res else [])
    args = (a, b) + ((res,) if has_res else ())
    return pl.pallas_call(
        body, name=name, grid=(m // tm, n // tn, nk),
        out_shape=out_shape,
        in_specs=in_specs, out_specs=o_spec,
        scratch_shapes=[pltpu.VMEM((tm, tn), F32)] if use_acc else [],
        compiler_params=pltpu.CompilerParams(dimension_semantics=("parallel", "parallel", "arbitrary")),
    )(*args)


def _row_tile(s):
    return _pick(s, (256, 128))


def _rms_fwd(h, g, name):
    s, d = h.shape
    ts = _row_tile(s)

    def body(h_ref, g_ref, o_ref):
        x = h_ref[...]
        r = lax.rsqrt(jnp.mean(x * x, axis=-1, keepdims=True) + EPS)
        o_ref[...] = ((x * r) * g_ref[...]).astype(BF16)

    return pl.pallas_call(
        body, name=name, grid=(s // ts,),
        out_shape=jax.ShapeDtypeStruct((s, d), BF16),
        in_specs=[pl.BlockSpec((ts, d), lambda i: (i, 0)), pl.BlockSpec((1, d), lambda i: (0, 0))],
        out_specs=pl.BlockSpec((ts, d), lambda i: (i, 0)),
    )(h, g)


def _rms_bwd(h, g, dhns, dres, name, want_bf16):
    s, d = h.shape
    ts = _row_tile(s)
    n_in = len(dhns)

    def body(*refs):
        h_ref, g_ref, r_ref = refs[:3]
        dy_refs = refs[3:3 + n_in]
        outs = refs[3 + n_in:]
        dh_ref, dg_ref = outs[0], outs[-1]
        i = pl.program_id(0)
        x = h_ref[...]
        dy = dy_refs[0][...].astype(F32)
        for extra in dy_refs[1:]:
            dy = dy + extra[...].astype(F32)
        r = lax.rsqrt(jnp.mean(x * x, axis=-1, keepdims=True) + EPS)
        xr = x * r
        u = dy * g_ref[...]
        dx = r * (u - xr * jnp.mean(xr * u, axis=-1, keepdims=True))
        dh = r_ref[...] + dx
        dh_ref[...] = dh
        if want_bf16:
            outs[1][...] = dh.astype(BF16)

        @pl.when(i == 0)
        def _():
            dg_ref[...] = jnp.zeros_like(dg_ref)

        dg_ref[...] += jnp.sum(dy * xr, axis=0, keepdims=True)

    row = pl.BlockSpec((ts, d), lambda i: (i, 0))
    vec = pl.BlockSpec((1, d), lambda i: (0, 0))
    out_shape = [jax.ShapeDtypeStruct((s, d), F32)]
    out_specs = [row]
    if want_bf16:
        out_shape.append(jax.ShapeDtypeStruct((s, d), BF16))
        out_specs.append(row)
    out_shape.append(jax.ShapeDtypeStruct((1, d), F32))
    out_specs.append(vec)
    return pl.pallas_call(
        body, name=name, grid=(s // ts,),
        out_shape=tuple(out_shape),
        in_specs=[row, vec, row] + [row] * n_in, out_specs=tuple(out_specs),
        compiler_params=pltpu.CompilerParams(dimension_semantics=("arbitrary",)),
    )(h, g, dres, *dhns)


def _sigmoid_parts(z):
    e = jnp.exp(-jnp.abs(z))
    r = 1.0 / (1.0 + e)
    er = e * r
    pos = z >= 0
    return jnp.where(pos, r, er), jnp.where(pos, er, r)


def _swiglu_fwd(ab, name):
    s = ab.shape[0]
    f = HID_PAD
    ts = _pick(s, (512, 256, 128))

    def body(ab_ref, o_ref):
        a = ab_ref[:, :f].astype(F32)
        b = ab_ref[:, f:].astype(F32)
        sg, _ = _sigmoid_parts(a)
        o_ref[...] = ((a * sg) * b).astype(BF16)

    return pl.pallas_call(
        body, name=name, grid=(s // ts, N_DEV),
        out_shape=jax.ShapeDtypeStruct((s, D_FF_PAD), BF16),
        in_specs=[pl.BlockSpec((ts, 2 * f), lambda i, d: (i, d))],
        out_specs=pl.BlockSpec((ts, f), lambda i, d: (i, d)),
    )(ab)


def _swiglu_bwd(ab, du, name):
    s = ab.shape[0]
    f = HID_PAD
    ts = _pick(s, (512, 256, 128))

    def body(ab_ref, du_ref, o_ref):
        a = ab_ref[:, :f].astype(F32)
        b = ab_ref[:, f:].astype(F32)
        g = du_ref[...].astype(F32)
        sg, sgm = _sigmoid_parts(a)
        silu = a * sg
        o_ref[:, :f] = (g * b * (sg + silu * sgm)).astype(BF16)
        o_ref[:, f:] = (g * silu).astype(BF16)

    return pl.pallas_call(
        body, name=name, grid=(s // ts, N_DEV),
        out_shape=jax.ShapeDtypeStruct(ab.shape, BF16),
        in_specs=[pl.BlockSpec((ts, 2 * f), lambda i, d: (i, d)), pl.BlockSpec((ts, f), lambda i, d: (i, d))],
        out_specs=pl.BlockSpec((ts, 2 * f), lambda i, d: (i, d)),
    )(ab, du)


def _ple_fwd(h, gl, pp, name):
    s, d = h.shape
    ts = _row_tile(s)

    def body(h_ref, gl_ref, pp_ref, o_ref):
        sg, _ = _sigmoid_parts(gl_ref[...].astype(F32))
        o_ref[...] = h_ref[...] + sg * pp_ref[...].astype(F32)

    row = pl.BlockSpec((ts, d), lambda i: (i, 0))
    return pl.pallas_call(
        body, name=name, grid=(s // ts,),
        out_shape=jax.ShapeDtypeStruct((s, d), F32),
        in_specs=[row, row, row], out_specs=row,
    )(h, gl, pp)


def _ple_bwd(dh, gl, pp, name):
    s, d = dh.shape
    ts = _row_tile(s)

    def body(dh_ref, gl_ref, pp_ref, dgl_ref, dpp_ref):
        g = dh_ref[...]
        sg, sgm = _sigmoid_parts(gl_ref[...].astype(F32))
        dpp_ref[...] = (g * sg).astype(BF16)
        dgl_ref[...] = (g * pp_ref[...].astype(F32) * (sg * sgm)).astype(BF16)

    row = pl.BlockSpec((ts, d), lambda i: (i, 0))
    shp = jax.ShapeDtypeStruct((s, d), BF16)
    return pl.pallas_call(
        body, name=name, grid=(s // ts,),
        out_shape=(shp, shp), in_specs=[row, row, row], out_specs=(row, row),
    )(dh, gl, pp)


def _final_norm_loss(h, g, target):
    s, d = h.shape
    ts = _row_tile(s)

    def body(h_ref, g_ref, t_ref, loss_ref, dh_ref, dg_ref):
        i = pl.program_id(0)
        x = h_ref[...]
        gain = g_ref[...]
        r = lax.rsqrt(jnp.mean(x * x, axis=-1, keepdims=True) + EPS)
        xr = x * r
        err = xr * gain - t_ref[...]
        dy = err * (1.0 / d)
        u = dy * gain
        dh_ref[...] = r * (u - xr * jnp.mean(xr * u, axis=-1, keepdims=True))

        @pl.when(i == 0)
        def _():
            dg_ref[...] = jnp.zeros_like(dg_ref)
            loss_ref[...] = jnp.zeros_like(loss_ref)

        dg_ref[...] += jnp.sum(dy * xr, axis=0, keepdims=True)
        tok = jnp.mean(err * err, axis=-1, keepdims=True)
        loss_ref[...] += 0.5 * jnp.sum(tok, axis=0, keepdims=True)

    row = pl.BlockSpec((ts, d), lambda i: (i, 0))
    vec = pl.BlockSpec((1, d), lambda i: (0, 0))
    return pl.pallas_call(
        body, name="final_norm_loss", grid=(s // ts,),
        out_shape=(jax.ShapeDtypeStruct((1, LANES), F32), jax.ShapeDtypeStruct((s, d), F32),
                   jax.ShapeDtypeStruct((1, d), F32)),
        in_specs=[row, vec, row],
        out_specs=(pl.BlockSpec((1, LANES), lambda i: (0, 0)), row, vec),
        compiler_params=pltpu.CompilerParams(dimension_semantics=("arbitrary",)),
    )(h, g, target)


def _rope(xin, cos, sin_signed, sign, name):
    s, w = xin.shape
    ts = _pick(s, (512, 256, 128))

    def body(x_ref, c_ref, s_ref, o_ref):
        cos_b, sin_b = c_ref[...], s_ref[...]
        lane = lax.broadcasted_iota(jnp.int32, cos_b.shape, 1)
        low = (lane & (HEAD_DIM - 1)) < (HEAD_DIM // 2)
        for j in range(w // LANES):
            cols = slice(j * LANES, (j + 1) * LANES)
            x = x_ref[:, cols].astype(F32)
            swapped = jnp.where(low, pltpu.roll(x, LANES - HEAD_DIM // 2, 1), pltpu.roll(x, HEAD_DIM // 2, 1))
            o_ref[:, cols] = (x * cos_b + sign * (swapped * sin_b)).astype(BF16)

    blk = pl.BlockSpec((ts, w), lambda i: (i, 0))
    tab = pl.BlockSpec((ts, LANES), lambda i: (i, 0))
    return pl.pallas_call(
        body, name=name, grid=(s // ts,),
        out_shape=jax.ShapeDtypeStruct((s, w), BF16),
        in_specs=[blk, tab, tab], out_specs=blk,
    )(xin, cos, sin_signed)


def _fgate_fwd(flog, bias):
    s, w = flog.shape

    def body(x_ref, b_ref, o_ref):
        rowi = lax.broadcasted_iota(jnp.int32, (8, w), 0)
        b = b_ref[...]

        def step(g, carry):
            sl = pl.ds(pl.multiple_of(g * 8, 8), 8)
            x = x_ref[sl, :] + b
            lf = jnp.minimum(x, 0.0) - jnp.log1p(jnp.exp(-jnp.abs(x)))
            for sh in (1, 2, 4):
                lf = lf + jnp.where(rowi >= sh, pltpu.roll(lf, sh, 0), 0.0)
            out = lf + carry
            o_ref[sl, :] = out
            return jnp.broadcast_to(out[7:8, :], (8, w))

        lax.fori_loop(0, s // 8, step, jnp.zeros((8, w), F32))

    vm = pl.BlockSpec(memory_space=pltpu.VMEM)
    return pl.pallas_call(
        body, name="fgate_fwd", out_shape=jax.ShapeDtypeStruct((s, w), F32),
        in_specs=[vm, vm], out_specs=vm,
    )(flog, bias)


def _fgate_bwd(gcum, flog, bias):
    s, w = flog.shape

    def body(g_ref, x_ref, b_ref, o_ref, db_ref):
        rowi = lax.broadcasted_iota(jnp.int32, (8, w), 0)
        lane = lax.broadcasted_iota(jnp.int32, (8, w), 1)
        b = b_ref[...]

        def step(t, carry):
            run, dbsum = carry
            g = s // 8 - 1 - t
            sl = pl.ds(pl.multiple_of(g * 8, 8), 8)
            c = g_ref[sl, :]
            for sh in (1, 2, 4):
                c = c + jnp.where(rowi < 8 - sh, pltpu.roll(c, 8 - sh, 0), 0.0)
            c = c + run
            _, sgm = _sigmoid_parts(x_ref[sl, :] + b)
            dl = jnp.where(lane < N_FOX, c * sgm, 0.0)
            o_ref[sl, :] = dl.astype(BF16)
            return jnp.broadcast_to(c[0:1, :], (8, w)), dbsum + dl

        _, dbsum = lax.fori_loop(0, s // 8, step, (jnp.zeros((8, w), F32), jnp.zeros((8, w), F32)))
        db_ref[...] = jnp.sum(dbsum, axis=0, keepdims=True)

    vm = pl.BlockSpec(memory_space=pltpu.VMEM)
    return pl.pallas_call(
        body, name="fgate_bwd",
        out_shape=(jax.ShapeDtypeStruct((s, w), BF16), jax.ShapeDtypeStruct((1, w), F32)),
        in_specs=[vm, vm, vm], out_specs=(vm, vm),
    )(gcum, flog, bias)


_DN_NT = (((1,), (1,)), ((), ()))
_DN_TN = (((0,), (0,)), ((), ()))


def _head_mask(shape, hh):
    lane = lax.broadcasted_iota(jnp.int32, shape, 1)
    return (lane >= HEAD_DIM * hh) & (lane < HEAD_DIM * (hh + 1))


SKIP_BELOW = -110.0


def _sweep_left(i, carry, tile, go_on):
    def flag(j, c):
        return jnp.logical_and(j >= 0, go_on(jnp.maximum(j, 0), c)).astype(jnp.int32)

    def body(st):
        j, _, c = st
        c = tile(j, c)
        return j - 1, flag(j - 1, c), c

    return lax.while_loop(lambda st: st[1] > 0, body, (i - 1, flag(i - 1, carry), carry))[2]


def _key_norm_max(k_ref, kn_ref):
    k2 = k_ref[...].astype(F32)
    sq = k2 * k2
    for hh in range(2):
        n2 = jnp.sum(jnp.where(_head_mask(sq.shape, hh), sq, 0.0), axis=1, keepdims=True)
        kn_ref[hh] = jnp.broadcast_to(jnp.sqrt(jnp.max(n2, axis=0, keepdims=True)), kn_ref.shape[1:])


def _fox_fwd(proj, ccol, crow):
    s = proj.shape[0]
    blk = min(ATTN_BLOCK, s)
    nq = s // blk
    npair = N_FOX // 2

    def body(q_ref, k_ref, v_ref, cc_ref, cr_ref, o_ref, a_ref, kn_ref):
        p_, i = pl.program_id(0), pl.program_id(1)

        @pl.when(i == 0)
        def _():
            _key_norm_max(k_ref, kn_ref)

        q2 = q_ref[...].astype(F32) * SCALE
        row = lax.broadcasted_iota(jnp.int32, (blk, blk), 0)
        col = lax.broadcasted_iota(jnp.int32, (blk, blk), 1)
        outs = []
        for hh in range(2):
            hm = _head_mask(q2.shape, hh)
            qh = jnp.where(hm, q2, 0.0).astype(BF16)
            ct = cc_ref[hh][:, 0:1]
            qk_max = jnp.sqrt(jnp.sum(jnp.where(hm, q2 * q2, 0.0), axis=1, keepdims=True)) * kn_ref[hh][0:1, 0:1]

            def go_on(j, carry, ct=ct, qk_max=qk_max, hh=hh):
                bias_max = ct - jnp.min(cr_ref[2 * p_ + hh, j], axis=1, keepdims=True)
                return jnp.max(qk_max + bias_max - carry[0]) > SKIP_BELOW

            def tile(j, carry, masked, qh=qh, ct=ct, hh=hh):
                m, l, acc = carry
                sl = pl.ds(pl.multiple_of(j * blk, blk), blk)
                kb, vb = k_ref[sl, :], v_ref[sl, :]
                sc = lax.dot_general(qh, kb, _DN_NT, preferred_element_type=F32)
                sc = sc + (ct - cr_ref[2 * p_ + hh, j])
                if masked:
                    sc = jnp.where(col <= row, sc, NEG_INF)
                m_new = jnp.maximum(m, jnp.max(sc, axis=1, keepdims=True))
                alpha = jnp.exp(m - m_new)
                pm = jnp.exp(sc - m_new)
                l = alpha * l + jnp.sum(pm, axis=1, keepdims=True)
                acc = alpha * acc + jnp.dot(pm.astype(BF16), vb, preferred_element_type=F32)
                return m_new, l, acc

            init = (jnp.full((blk, 1), NEG_INF, F32), jnp.zeros((blk, 1), F32), jnp.zeros((blk, LANES), F32))
            carry = tile(i, init, True)
            m, l, acc = _sweep_left(i, carry, lambda j, c, tile=tile: tile(j, c, False), go_on)
            outs.append(acc / l)
            a_ref[hh] = jnp.broadcast_to(ct - (m + jnp.log(l)), (blk, LANES))
        o_ref[...] = jnp.where(_head_mask(outs[0].shape, 0), outs[0], outs[1]).astype(BF16)

    seq = lambda base: pl.BlockSpec((s, LANES), lambda p, i: (0, base + p))
    return pl.pallas_call(
        body, name="fox_fwd", grid=(npair, nq),
        scratch_shapes=[pltpu.VMEM((2, 8, LANES), F32)],
        out_shape=(jax.ShapeDtypeStruct((s, FOX_W), BF16), jax.ShapeDtypeStruct((N_FOX, s, LANES), F32)),
        in_specs=[pl.BlockSpec((blk, LANES), lambda p, i: (i, p)), seq(npair), seq(2 * npair),
                  pl.BlockSpec((2, blk, LANES), lambda p, i: (p, i, 0)),
                  pl.BlockSpec((N_FOX, nq, 1, blk), lambda p, i: (0, 0, 0, 0))],
        out_specs=(pl.BlockSpec((blk, LANES), lambda p, i: (i, p)),
                   pl.BlockSpec((2, blk, LANES), lambda p, i: (p, i, 0))),
        compiler_params=pltpu.CompilerParams(dimension_semantics=("parallel", "arbitrary")),
    )(proj, proj, proj, ccol, crow)


def _fox_bwd(proj, do, o, acol, crow):
    s = proj.shape[0]
    blk = min(ATTN_BLOCK, s)
    nq = s // blk
    npair = N_FOX // 2

    def body(q_ref, k_ref, v_ref, do_ref, o_ref, a_ref, cr_ref, dq_ref, dk_ref, dv_ref, gc_ref,
             dk_acc, dv_acc, kn_ref):
        p_, i = pl.program_id(0), pl.program_id(1)

        @pl.when(i == 0)
        def _():
            dk_acc[...] = jnp.zeros_like(dk_acc)
            dv_acc[...] = jnp.zeros_like(dv_acc)
            gc_ref[...] = jnp.zeros_like(gc_ref)
            _key_norm_max(k_ref, kn_ref)

        q2 = q_ref[...].astype(F32) * SCALE
        do2 = do_ref[...]
        prod = do2.astype(F32) * o_ref[...].astype(F32)
        row = lax.broadcasted_iota(jnp.int32, (blk, blk), 0)
        col = lax.broadcasted_iota(jnp.int32, (blk, blk), 1)
        dqs = []
        for hh in range(2):
            hm = _head_mask(q2.shape, hh)
            qh = jnp.where(hm, q2, 0.0).astype(BF16)
            doh = jnp.where(hm, do2, jnp.zeros_like(do2))
            delta = jnp.sum(jnp.where(hm, prod, 0.0), axis=1, keepdims=True)
            at = a_ref[hh][:, 0:1]
            qk_max = jnp.sqrt(jnp.sum(jnp.where(hm, q2 * q2, 0.0), axis=1, keepdims=True)) * kn_ref[hh][0:1, 0:1]

            def go_on(j, carry, at=at, qk_max=qk_max, hh=hh):
                bias_max = at - jnp.min(cr_ref[2 * p_ + hh, j], axis=1, keepdims=True)
                return jnp.max(qk_max + bias_max) > SKIP_BELOW

            def tile(j, carry, masked, qh=qh, doh=doh, delta=delta, at=at, hh=hh):
                dq, rs = carry
                sl = pl.ds(pl.multiple_of(j * blk, blk), blk)
                kb, vb = k_ref[sl, :], v_ref[sl, :]
                sc = lax.dot_general(qh, kb, _DN_NT, preferred_element_type=F32)
                sc = sc + (at - cr_ref[2 * p_ + hh, j])
                if masked:
                    sc = jnp.where(col <= row, sc, NEG_INF)
                pm = jnp.exp(sc)
                dp = lax.dot_general(doh, vb, _DN_NT, preferred_element_type=F32)
                ds = pm * (dp - delta)
                dsb = ds.astype(BF16)
                dk_acc[sl, :] += lax.dot_general(dsb, qh, _DN_TN, preferred_element_type=F32)
                dv_acc[sl, :] += lax.dot_general(pm.astype(BF16), doh, _DN_TN, preferred_element_type=F32)
                gc_ref[hh, j] += -jnp.sum(ds, axis=0, keepdims=True)
                return dq + jnp.dot(dsb, kb, preferred_element_type=F32), rs + jnp.sum(ds, axis=1, keepdims=True)

            carry = tile(i, (jnp.zeros((blk, LANES), F32), jnp.zeros((blk, 1), F32)), True)
            dq, rs = _sweep_left(i, carry, lambda j, c, tile=tile: tile(j, c, False), go_on)
            dqs.append(dq)
            gc_ref[hh, i] += jnp.transpose(jnp.broadcast_to(rs, (blk, LANES)))[0:1, :]
        dq_ref[...] = (jnp.where(_head_mask(dqs[0].shape, 0), dqs[0], dqs[1]) * SCALE).astype(BF16)

        @pl.when(i == nq - 1)
        def _():
            dk_ref[...] = dk_acc[...].astype(BF16)
            dv_ref[...] = dv_acc[...].astype(BF16)

    seq = lambda base: pl.BlockSpec((s, LANES), lambda p, i: (0, base + p))
    qblk = lambda base: pl.BlockSpec((blk, LANES), lambda p, i: (i, base + p))
    rep = pl.BlockSpec((2, blk, LANES), lambda p, i: (p, i, 0))
    half = jax.ShapeDtypeStruct((s, FOX_W), BF16)
    return pl.pallas_call(
        body, name="fox_bwd", grid=(npair, nq),
        out_shape=(half, half, half, jax.ShapeDtypeStruct((N_FOX, nq, 1, blk), F32)),
        in_specs=[qblk(0), seq(npair), seq(2 * npair), qblk(0), qblk(0), rep,
                  pl.BlockSpec((N_FOX, nq, 1, blk), lambda p, i: (0, 0, 0, 0))],
        out_specs=(qblk(0), seq(0), seq(0), pl.BlockSpec((2, nq, 1, blk), lambda p, i: (p, 0, 0, 0))),
        scratch_shapes=[pltpu.VMEM((s, LANES), F32), pltpu.VMEM((s, LANES), F32), pltpu.VMEM((2, 8, LANES), F32)],
        compiler_params=pltpu.CompilerParams(dimension_semantics=("parallel", "arbitrary")),
    )(proj, proj, proj, do, o, acol, crow)


def _sb_logs(z):
    neg = -(jnp.maximum(z, 0.0) + jnp.log1p(jnp.exp(-jnp.abs(z))))
    return neg, z + neg


def _split_dot(x, tri):
    hi = x.astype(BF16)
    lo = (x - hi.astype(F32)).astype(BF16)
    return jnp.dot(hi, tri, preferred_element_type=F32) + jnp.dot(lo, tri, preferred_element_type=F32)


def _sb_fwd(proj):
    s = proj.shape[0]
    blk = min(ATTN_BLOCK, s)
    nq = s // blk
    npair = N_SB // 2
    base = 3 * (N_FOX // 2)

    def body(q_ref, k_ref, v_ref, o_ref, r_ref):
        i = pl.program_id(1)
        q2 = q_ref[...].astype(F32) * SCALE
        row = lax.broadcasted_iota(jnp.int32, (blk, blk), 0)
        col = lax.broadcasted_iota(jnp.int32, (blk, blk), 1)
        strict = col < row
        tri = jnp.where(row > col, 1.0, 0.0).astype(BF16)
        lane = lax.broadcasted_iota(jnp.int32, (blk, LANES), 1)
        outs = []
        for hh in range(2):
            qh = jnp.where(_head_mask(q2.shape, hh), q2, 0.0).astype(BF16)

            def tile(j, carry, masked, qh=qh):
                rsum, acc, rbuf = carry
                sl = pl.ds(pl.multiple_of(j * blk, blk), blk)
                kb, vb = k_ref[sl, :], v_ref[sl, :]
                z = lax.dot_general(qh, kb, _DN_NT, preferred_element_type=F32)
                l1m, lb = _sb_logs(z)
                if masked:
                    l1m = jnp.where(strict, l1m, 0.0)
                sx = _split_dot(l1m, tri)
                a = jnp.exp(lb + sx + rsum)
                if masked:
                    a = jnp.where(strict, a, 0.0)
                acc = acc + jnp.dot(a.astype(BF16), vb, preferred_element_type=F32)
                rbuf = jnp.where(lane == j, rsum, rbuf)
                return rsum + jnp.sum(l1m, axis=1, keepdims=True), acc, rbuf

            init = (jnp.zeros((blk, 1), F32), jnp.zeros((blk, LANES), F32), jnp.full((blk, LANES), NEG_INF, F32))
            carry = tile(i, init, True)
            _, acc, rbuf = _sweep_left(i, carry, lambda j, c, tile=tile: tile(j, c, False),
                                       lambda j, c: jnp.max(c[0]) > SKIP_BELOW)
            outs.append(acc)
            r_ref[hh] = rbuf
        o_ref[...] = jnp.where(_head_mask(outs[0].shape, 0), outs[0], outs[1]).astype(BF16)

    seq = lambda b: pl.BlockSpec((s, LANES), lambda p, i: (0, b + p))
    return pl.pallas_call(
        body, name="sb_fwd", grid=(npair, nq),
        out_shape=(jax.ShapeDtypeStruct((s, SB_W), BF16), jax.ShapeDtypeStruct((N_SB, s, LANES), F32)),
        in_specs=[pl.BlockSpec((blk, LANES), lambda p, i: (i, base + p)), seq(base + npair), seq(base + 2 * npair)],
        out_specs=(pl.BlockSpec((blk, LANES), lambda p, i: (i, p)),
                   pl.BlockSpec((2, blk, LANES), lambda p, i: (p, i, 0))),
        compiler_params=pltpu.CompilerParams(dimension_semantics=("parallel", "arbitrary")),
    )(proj, proj, proj)


def _sb_bwd(proj, do, rsave):
    s = proj.shape[0]
    blk = min(ATTN_BLOCK, s)
    nq = s // blk
    npair = N_SB // 2
    base = 3 * (N_FOX // 2)

    def body(q_ref, k_ref, v_ref, do_ref, r_ref, dq_ref, dk_ref, dv_ref, dk_acc, dv_acc):
        i = pl.program_id(1)

        @pl.when(i == 0)
        def _():
            dk_acc[...] = jnp.zeros_like(dk_acc)
            dv_acc[...] = jnp.zeros_like(dv_acc)

        q2 = q_ref[...].astype(F32) * SCALE
        do2 = do_ref[...]
        row = lax.broadcasted_iota(jnp.int32, (blk, blk), 0)
        col = lax.broadcasted_iota(jnp.int32, (blk, blk), 1)
        strict = col < row
        tri_suffix = jnp.where(row > col, 1.0, 0.0).astype(BF16)
        tri_prefix = jnp.where(row < col, 1.0, 0.0).astype(BF16)
        lane = lax.broadcasted_iota(jnp.int32, (blk, LANES), 1)
        dqs = []
        for hh in range(2):
            hm = _head_mask(q2.shape, hh)
            qh = jnp.where(hm, q2, 0.0).astype(BF16)
            doh = jnp.where(hm, do2, jnp.zeros_like(do2))
            rbuf = r_ref[hh]

            def tile(j, carry, masked, qh=qh, doh=doh, rbuf=rbuf):
                pre, dq = carry
                sl = pl.ds(pl.multiple_of(j * blk, blk), blk)
                kb, vb = k_ref[sl, :], v_ref[sl, :]
                z = lax.dot_general(qh, kb, _DN_NT, preferred_element_type=F32)
                l1m, lb = _sb_logs(z)
                beta, one_m_beta = _sigmoid_parts(z)
                if masked:
                    l1m = jnp.where(strict, l1m, 0.0)
                sx = _split_dot(l1m, tri_suffix)
                rj = jnp.sum(jnp.where(lane == j, rbuf, 0.0), axis=1, keepdims=True)
                a = jnp.exp(lb + sx + rj)
                if masked:
                    a = jnp.where(strict, a, 0.0)
                da = lax.dot_general(doh, vb, _DN_NT, preferred_element_type=F32)
                g = a * da
                px = _split_dot(g, tri_prefix) + pre
                dz = g * one_m_beta - beta * px
                if masked:
                    dz = jnp.where(strict, dz, 0.0)
                dzb = dz.astype(BF16)
                dk_acc[sl, :] += lax.dot_general(dzb, qh, _DN_TN, preferred_element_type=F32)
                dv_acc[sl, :] += lax.dot_general(a.astype(BF16), doh, _DN_TN, preferred_element_type=F32)
                return pre + jnp.sum(g, axis=1, keepdims=True), dq + jnp.dot(dzb, kb, preferred_element_type=F32)

            reach = jnp.max(rbuf, axis=0, keepdims=True)
            dead = (reach <= SKIP_BELOW) & (lane[0:1, :] <= i)
            first = jnp.sum(jnp.where(dead, 1.0, 0.0)).astype(jnp.int32)
            carry = (jnp.zeros((blk, 1), F32), jnp.zeros((blk, LANES), F32))
            carry = lax.fori_loop(first, i, lambda t, c, tile=tile: tile(t, c, False), carry)
            _, dq = tile(i, carry, True)
            dqs.append(dq)
        dq_ref[...] = (jnp.where(_head_mask(dqs[0].shape, 0), dqs[0], dqs[1]) * SCALE).astype(BF16)

        @pl.when(i == nq - 1)
        def _():
            dk_ref[...] = dk_acc[...].astype(BF16)
            dv_ref[...] = dv_acc[...].astype(BF16)

    seq = lambda b: pl.BlockSpec((s, LANES), lambda p, i: (0, b + p))
    qblk = lambda b: pl.BlockSpec((blk, LANES), lambda p, i: (i, b + p))
    half = jax.ShapeDtypeStruct((s, SB_W), BF16)
    return pl.pallas_call(
        body, name="sb_bwd", grid=(npair, nq),
        out_shape=(half, half, half),
        in_specs=[qblk(base), seq(base + npair), seq(base + 2 * npair), qblk(npair),
                  pl.BlockSpec((2, blk, LANES), lambda p, i: (p, i, 0))],
        out_specs=(qblk(0), seq(0), seq(0)),
        scratch_shapes=[pltpu.VMEM((s, LANES), F32), pltpu.VMEM((s, LANES), F32)],
        compiler_params=pltpu.CompilerParams(dimension_semantics=("parallel", "arbitrary")),
    )(proj, proj, proj, do, rsave)


def _swa_scores(q, kp, kc, first):
    w = q.shape[0]
    row = lax.broadcasted_iota(jnp.int32, (w, w), 0)
    col = lax.broadcasted_iota(jnp.int32, (w, w), 1)
    sp = lax.dot_general(q, kp, _DN_NT, preferred_element_type=F32)
    sc = lax.dot_general(q, kc, _DN_NT, preferred_element_type=F32)
    sp = jnp.where((col > row) & jnp.logical_not(first), sp, NEG_INF)
    sc = jnp.where(col <= row, sc, NEG_INF)
    return sp, sc


def _scaled(q_ref_val):
    return (q_ref_val.astype(F32) * SCALE).astype(BF16)


def _swa_fwd(q, k, v, sinks):
    nh, s, hd = q.shape
    w = WINDOW
    nb = s // w

    def body(q_ref, kp_ref, kc_ref, vp_ref, vc_ref, s_ref, o_ref, lse_ref):
        i = pl.program_id(1)
        kp, kc, vp, vc = kp_ref[0], kc_ref[0], vp_ref[0], vc_ref[0]
        for r in range(GROUP):
            sink = s_ref[r][:, 0:1]
            sp, sc = _swa_scores(_scaled(q_ref[r]), kp, kc, i == 0)
            m = jnp.maximum(jnp.maximum(jnp.max(sp, axis=1, keepdims=True), jnp.max(sc, axis=1, keepdims=True)), sink)
            ep, ec = jnp.exp(sp - m), jnp.exp(sc - m)
            l = jnp.sum(ep, axis=1, keepdims=True) + jnp.sum(ec, axis=1, keepdims=True) + jnp.exp(sink - m)
            acc = (jnp.dot(ep.astype(BF16), vp, preferred_element_type=F32)
                   + jnp.dot(ec.astype(BF16), vc, preferred_element_type=F32))
            o_ref[r] = (acc / l).astype(BF16)
            lse_ref[r] = jnp.broadcast_to(m + jnp.log(l), (w, LANES))

    qs = pl.BlockSpec((GROUP, w, hd), lambda g, i: (g, i, 0))
    prev = pl.BlockSpec((1, w, hd), lambda g, i: (g, jnp.maximum(i - 1, 0), 0))
    cur = pl.BlockSpec((1, w, hd), lambda g, i: (g, i, 0))
    return pl.pallas_call(
        body, name="swa_fwd", grid=(nh // GROUP, nb),
        out_shape=(jax.ShapeDtypeStruct((nh, s, hd), BF16), jax.ShapeDtypeStruct((nh, s, LANES), F32)),
        in_specs=[qs, prev, cur, prev, cur, pl.BlockSpec((GROUP, 1, LANES), lambda g, i: (g, 0, 0))],
        out_specs=(qs, pl.BlockSpec((GROUP, w, LANES), lambda g, i: (g, i, 0))),
    )(q, k, k, v, v, sinks)


def _swa_bwd_dq(q, k, v, sinks, do, o, lse):
    nh, s, hd = q.shape
    w = WINDOW
    nb = s // w

    def body(q_ref, kp_ref, kc_ref, vp_ref, vc_ref, s_ref, do_ref, o_ref, lse_ref, dq_ref, dsink_ref):
        i = pl.program_id(1)

        @pl.when(i == 0)
        def _():
            dsink_ref[...] = jnp.zeros_like(dsink_ref)

        kp, kc, vp, vc = kp_ref[0], kc_ref[0], vp_ref[0], vc_ref[0]
        for r in range(GROUP):
            sink = s_ref[r][:, 0:1]
            lse = lse_ref[r][:, 0:1]
            sp, sc = _swa_scores(_scaled(q_ref[r]), kp, kc, i == 0)
            pp, pc = jnp.exp(sp - lse), jnp.exp(sc - lse)
            dov = do_ref[r]
            delta = jnp.sum(dov.astype(F32) * o_ref[r].astype(F32), axis=1, keepdims=True)
            dsp = pp * (lax.dot_general(dov, vp, _DN_NT, preferred_element_type=F32) - delta)
            dsc = pc * (lax.dot_general(dov, vc, _DN_NT, preferred_element_type=F32) - delta)
            dq = (jnp.dot(dsp.astype(BF16), kp, preferred_element_type=F32)
                  + jnp.dot(dsc.astype(BF16), kc, preferred_element_type=F32))
            dq_ref[r] = (dq * SCALE).astype(BF16)
            part = jnp.sum(-jnp.exp(sink - lse) * delta, axis=0, keepdims=True)
            dsink_ref[r] += jnp.broadcast_to(part, (1, LANES))

    qs = pl.BlockSpec((GROUP, w, hd), lambda g, i: (g, i, 0))
    prev = pl.BlockSpec((1, w, hd), lambda g, i: (g, jnp.maximum(i - 1, 0), 0))
    cur = pl.BlockSpec((1, w, hd), lambda g, i: (g, i, 0))
    vec = pl.BlockSpec((GROUP, 1, LANES), lambda g, i: (g, 0, 0))
    return pl.pallas_call(
        body, name="swa_bwd_dq", grid=(nh // GROUP, nb),
        out_shape=(jax.ShapeDtypeStruct((nh, s, hd), BF16), jax.ShapeDtypeStruct((nh, 1, LANES), F32)),
        in_specs=[qs, prev, cur, prev, cur, vec, qs, qs, pl.BlockSpec((GROUP, w, LANES), lambda g, i: (g, i, 0))],
        out_specs=(qs, vec),
        compiler_params=pltpu.CompilerParams(dimension_semantics=("parallel", "arbitrary")),
    )(q, k, k, v, v, sinks, do, o, lse)


def _swa_bwd_dkv(q, k, v, do, o, lse):
    nh, s, hd = q.shape
    nkv = k.shape[0]
    w = WINDOW
    nb = s // w

    def body(k_ref, v_ref, qa_ref, doa_ref, oa_ref, la_ref, qb_ref, dob_ref, ob_ref, lb_ref, dk_ref, dv_ref):
        j = pl.program_id(1)
        kk, vv = k_ref[0], v_ref[0]
        row = lax.broadcasted_iota(jnp.int32, (w, w), 0)
        col = lax.broadcasted_iota(jnp.int32, (w, w), 1)
        has_next = j + 1 < nb
        dk = jnp.zeros((w, hd), F32)
        dv = jnp.zeros((w, hd), F32)
        for r in range(GROUP):
            for q_ref, do_ref, o_ref, l_ref, valid in (
                    (qa_ref, doa_ref, oa_ref, la_ref, col <= row),
                    (qb_ref, dob_ref, ob_ref, lb_ref, (col > row) & has_next)):
                qs = _scaled(q_ref[r])
                dov = do_ref[r]
                sc = lax.dot_general(qs, kk, _DN_NT, preferred_element_type=F32)
                sc = jnp.where(valid, sc, NEG_INF)
                pm = jnp.exp(sc - l_ref[r][:, 0:1])
                delta = jnp.sum(dov.astype(F32) * o_ref[r].astype(F32), axis=1, keepdims=True)
                ds = pm * (lax.dot_general(dov, vv, _DN_NT, preferred_element_type=F32) - delta)
                dk = dk + lax.dot_general(ds.astype(BF16), qs, _DN_TN, preferred_element_type=F32)
                dv = dv + lax.dot_general(pm.astype(BF16), dov, _DN_TN, preferred_element_type=F32)
        dk_ref[0] = dk.astype(BF16)
        dv_ref[0] = dv.astype(BF16)

    kv = pl.BlockSpec((1, w, hd), lambda g, j: (g, j, 0))
    same = pl.BlockSpec((GROUP, w, hd), lambda g, j: (g, j, 0))
    nxt = pl.BlockSpec((GROUP, w, hd), lambda g, j: (g, jnp.minimum(j + 1, nb - 1), 0))
    lsame = pl.BlockSpec((GROUP, w, LANES), lambda g, j: (g, j, 0))
    lnxt = pl.BlockSpec((GROUP, w, LANES), lambda g, j: (g, jnp.minimum(j + 1, nb - 1), 0))
    shp = jax.ShapeDtypeStruct((nkv, s, hd), BF16)
    return pl.pallas_call(
        body, name="swa_bwd_dkv", grid=(nkv, nb),
        out_shape=(shp, shp),
        in_specs=[kv, kv, same, same, same, lsame, nxt, nxt, nxt, lnxt],
        out_specs=(kv, kv),
    )(k, v, q, do, o, lse, q, do, o, lse)


def _heads_major(xm, nheads):
    s = xm.shape[0]
    return xm.reshape(s, nheads, HEAD_DIM).transpose(1, 0, 2)


def _heads_minor(xh):
    nheads, s, _ = xh.shape
    return xh.transpose(1, 0, 2).reshape(s, nheads * HEAD_DIM)


def _ffn_ple_fwd(h1, p_l, g_ffn, g_ple, w_gu, w_down, w_pg, w_pp, tag):
    hn2 = _rms_fwd(h1, g_ffn, f"rms_ffn_{tag}")
    ab = _mm(hn2, w_gu, group="n", name=f"mm_gate_up_{tag}")
    u = _swiglu_fwd(ab, f"swiglu_{tag}")
    h2 = _mm(u, w_down, out_dtype=F32, res=h1, name=f"mm_down_{tag}")
    hn3 = _rms_fwd(h2, g_ple, f"rms_ple_{tag}")
    gl = _mm(hn3, w_pg, name=f"mm_ple_gate_{tag}")
    pp = _mm(p_l, w_pp, name=f"mm_ple_proj_{tag}")
    h3 = _ple_fwd(h2, gl, pp, f"ple_{tag}")
    return h3, dict(h1=h1, hn2=hn2, ab=ab, u=u, h2=h2, hn3=hn3, gl=gl, pp=pp)


def _ffn_ple_bwd(dh3, sv, p_l, g_ffn, g_ple, w_gu, w_down, w_pg, tag):
    dgl, dpp = _ple_bwd(dh3, sv["gl"], sv["pp"], f"ple_bwd_{tag}")
    d_wpp = _mm(p_l, dpp, ta=True, out_dtype=F32, name=f"mm_dw_ple_proj_{tag}")
    d_wpg = _mm(sv["hn3"], dgl, ta=True, out_dtype=F32, name=f"mm_dw_ple_gate_{tag}")
    dhn3 = _mm(dgl, w_pg, tb=True, out_dtype=F32, name=f"mm_dx_ple_gate_{tag}")
    dh2, dh2b, dg_ple = _rms_bwd(sv["h2"], g_ple, [dhn3], dh3, f"rms_ple_bwd_{tag}", True)
    du = _mm(dh2b, w_down, tb=True, name=f"mm_dx_down_{tag}")
    d_wdown = _mm(sv["u"], dh2b, ta=True, out_dtype=F32, name=f"mm_dw_down_{tag}")
    dab = _swiglu_bwd(sv["ab"], du, f"swiglu_bwd_{tag}")
    d_wgu = _mm(sv["hn2"], dab, ta=True, out_dtype=F32, group="out", group_width=2 * HID_PAD,
                name=f"mm_dw_gate_up_{tag}")
    dhn2 = _mm(dab, w_gu, tb=True, out_dtype=F32, group="k", name=f"mm_dx_gate_up_{tag}")
    dh1, dh1b, dg_ffn = _rms_bwd(sv["h1"], g_ffn, [dhn2], dh2, f"rms_ffn_bwd_{tag}", True)
    return dh1, dh1b, dict(d_wpp=d_wpp, d_wpg=d_wpg, d_wdown=d_wdown, d_wgu=d_wgu, dg_ple=dg_ple, dg_ffn=dg_ffn)


def _row_form(cum, blk):
    s = cum.shape[0]
    return cum[:, :N_FOX].T.reshape(N_FOX, s // blk, 1, blk)


def _col_form(cum):
    s = cum.shape[0]
    return jnp.broadcast_to(cum[:, :N_FOX].T[:, :, None], (N_FOX, s, LANES))


def kernel(x, p, positions, norm_mix, norm_ffn, norm_ple, norm_final, ev_w_in, ev_b_f, ev_w_out, od_w_in, od_sinks, od_w_out, ffn_w_gate, ffn_w_up, ffn_w_down, ple_w_proj, ple_w_gate, loss_target, m_norm_mix, m_norm_ffn, m_norm_ple, m_norm_final, m_ev_w_in, m_ev_b_f, m_ev_w_out, m_od_w_in, m_od_sinks, m_od_w_out, m_ffn_w_gate, m_ffn_w_up, m_ffn_w_down, m_ple_w_proj, m_ple_w_gate, v_norm_mix, v_norm_ffn, v_norm_ple, v_norm_final, v_ev_w_in, v_ev_b_f, v_ev_w_out, v_od_w_in, v_od_sinks, v_od_w_out, v_ffn_w_gate, v_ffn_w_up, v_ffn_w_down, v_ple_w_proj, v_ple_w_gate):
    s = x.shape[1]
    blk = min(ATTN_BLOCK, s)
    big_w = [ev_w_in, ev_w_out, od_w_in, od_w_out, ffn_w_gate, ffn_w_up, ffn_w_down, ple_w_proj, ple_w_gate]
    big_m = [m_ev_w_in, m_ev_w_out, m_od_w_in, m_od_w_out, m_ffn_w_gate, m_ffn_w_up, m_ffn_w_down, m_ple_w_proj, m_ple_w_gate]
    big_v = [v_ev_w_in, v_ev_w_out, v_od_w_in, v_od_w_out, v_ffn_w_gate, v_ffn_w_up, v_ffn_w_down, v_ple_w_proj, v_ple_w_gate]

    g_rows, g_gu, g_evin, g_odin, g_pp = _all_gather_weights(_group_shards(*big_w, BF16))
    d = D_MODEL
    w_in0 = _from_owner_cols(g_evin)
    w_qkv = w_in0[:, :QKV_W]
    w_f = jnp.pad(w_in0[:, QKV_W:], ((0, 0), (0, LANES - N_FOX)))
    w_oi = _from_owner_cols(g_odin)
    w_eo = g_rows[:, 0:128].reshape(d, d)
    w_oo = g_rows[:, 128:256].reshape(d, d)
    w_down = [g_rows[:, 256 + HID_PAD * l: 256 + HID_PAD * (l + 1)].reshape(D_FF_PAD, d) for l in range(2)]
    w_pg = [g_rows[:, 1024 + 128 * l: 1152 + 128 * l].reshape(d, d) for l in range(2)]
    w_gu = [g_gu[:, d * l: d * (l + 1)] for l in range(2)]
    w_pp = [_from_owner_cols(g_pp[:, PLE_DIM * l: PLE_DIM * (l + 1)]) for l in range(2)]

    h0 = x[0]
    target = loss_target[0]
    p_b = [p[l, 0].astype(BF16) for l in range(2)]
    g_mix = [norm_mix[l][None, :] for l in range(2)]
    g_ffn = [norm_ffn[l][None, :] for l in range(2)]
    g_ple = [norm_ple[l][None, :] for l in range(2)]
    b_f = jnp.pad(ev_b_f, ((0, 0), (0, LANES - N_FOX)))

    half = HEAD_DIM // 2
    inv = ROPE_THETA ** (-jnp.arange(half, dtype=F32) / half)
    ang = positions[0].astype(F32)[:, None] * inv
    cos_t = jnp.tile(jnp.cos(ang), (1, 4))
    sin_t = jnp.tile(jnp.concatenate([-jnp.sin(ang), jnp.sin(ang)], axis=1), (1, 2))

    hn1 = _rms_fwd(h0, g_mix[0], "rms_mix_0")
    proj0 = _mm(hn1, w_qkv, name="mm_in_0")
    flog = _mm(hn1, w_f, out_dtype=F32, name="mm_fgate_0")
    cum = _fgate_fwd(flog, b_f)
    crow = _row_form(cum, blk)
    o_fox, acol = _fox_fwd(proj0, _col_form(cum), crow)
    o_sb, rsave = _sb_fwd(proj0)
    o0 = jnp.concatenate([o_fox, o_sb], axis=1)
    h1 = _mm(o0, w_eo, out_dtype=F32, res=h0, name="mm_out_0")
    h3, sv0 = _ffn_ple_fwd(h1, p_b[0], g_ffn[0], g_ple[0], w_gu[0], w_down[0], w_pg[0], w_pp[0], "0")

    hn1b = _rms_fwd(h3, g_mix[1], "rms_mix_1")
    proj1 = _mm(hn1b, w_oi, name="mm_in_1")
    nq_w, nk_w = N_Q * HEAD_DIM, N_KV * HEAD_DIM
    qk_r = _rope(proj1[:, :nq_w + nk_w], cos_t, sin_t, 1.0, "rope_fwd")
    q_h = _heads_major(qk_r[:, :nq_w], N_Q)
    k_h = _heads_major(qk_r[:, nq_w:], N_KV)
    v_h = _heads_major(proj1[:, nq_w + nk_w:], N_KV)
    sinks_rep = jnp.broadcast_to(od_sinks[0][:, None, None], (N_Q, 1, LANES))
    o_h, lse1 = _swa_fwd(q_h, k_h, v_h, sinks_rep)
    o1 = _heads_minor(o_h)
    h4 = _mm(o1, w_oo, out_dtype=F32, res=h3, name="mm_out_1")
    h6, sv1 = _ffn_ple_fwd(h4, p_b[1], g_ffn[1], g_ple[1], w_gu[1], w_down[1], w_pg[1], w_pp[1], "1")

    loss_part, dh6, dg_final = _final_norm_loss(h6, norm_final[None, :], target)

    dh4, dh4b, gr1 = _ffn_ple_bwd(dh6, sv1, p_b[1], g_ffn[1], g_ple[1], w_gu[1], w_down[1], w_pg[1], "1")
    do1 = _mm(dh4b, w_oo, tb=True, name="mm_dx_out_1")
    d_woo = _mm(o1, dh4b, ta=True, out_dtype=F32, name="mm_dw_out_1")
    do_h = _heads_major(do1, N_Q)
    dq_h, dsink = _swa_bwd_dq(q_h, k_h, v_h, sinks_rep, do_h, o_h, lse1)
    dk_h, dv_h = _swa_bwd_dkv(q_h, k_h, v_h, do_h, o_h, lse1)
    dqk = _rope(jnp.concatenate([_heads_minor(dq_h), _heads_minor(dk_h)], axis=1), cos_t, sin_t, -1.0, "rope_bwd")
    dproj1 = jnp.concatenate([dqk, _heads_minor(dv_h)], axis=1)
    d_woi = _mm(hn1b, dproj1, ta=True, out_dtype=F32, name="mm_dw_in_1")
    dhn1b = _mm(dproj1, w_oi, tb=True, out_dtype=F32, name="mm_dx_in_1")
    dh3, dg_mix1 = _rms_bwd(h3, g_mix[1], [dhn1b], dh4, "rms_mix_bwd_1", False)

    dh1, dh1b, gr0 = _ffn_ple_bwd(dh3, sv0, p_b[0], g_ffn[0], g_ple[0], w_gu[0], w_down[0], w_pg[0], "0")
    do0 = _mm(dh1b, w_eo, tb=True, name="mm_dx_out_0")
    d_weo = _mm(o0, dh1b, ta=True, out_dtype=F32, name="mm_dw_out_0")
    dq_f, dk_f, dv_f, gc = _fox_bwd(proj0, do0, o0, acol, crow)
    dq_s, dk_s, dv_s = _sb_bwd(proj0, do0, rsave)
    dproj0 = jnp.concatenate([dq_f, dk_f, dv_f, dq_s, dk_s, dv_s], axis=1)
    gcum = jnp.pad(gc.reshape(N_FOX, s).T, ((0, 0), (0, LANES - N_FOX)))
    dflog, db_f = _fgate_bwd(gcum, flog, b_f)
    d_wqkv = _mm(hn1, dproj0, ta=True, out_dtype=F32, name="mm_dw_in_0")
    d_wf = _mm(hn1, dflog, ta=True, out_dtype=F32, name="mm_dw_fgate_0")
    dhn1 = _mm(dproj0, w_qkv, tb=True, out_dtype=F32, name="mm_dx_in_0")
    dhn1f = _mm(dflog, w_f, tb=True, out_dtype=F32, name="mm_dx_fgate_0")
    grad_x, dg_mix0 = _rms_bwd(h0, g_mix[0], [dhn1, dhn1f], dh1, "rms_mix_bwd_0", False)

    grs = (gr0, gr1)
    by_rows = lambda g, r: g.reshape(N_DEV, r, d)
    grad_groups = [
        jnp.concatenate([by_rows(d_weo, 128), by_rows(d_woo, 128), by_rows(gr0["d_wdown"], HID_PAD),
                         by_rows(gr1["d_wdown"], HID_PAD), by_rows(gr0["d_wpg"], 128), by_rows(gr1["d_wpg"], 128)],
                        axis=1),
        jnp.concatenate([gr0["d_wgu"], gr1["d_wgu"]], axis=1),
        _by_owner_cols(jnp.concatenate([d_wqkv, d_wf[:, :N_FOX]], axis=1)),
        _by_owner_cols(d_woi),
        jnp.concatenate([_by_owner_cols(gr0["d_wpp"]), _by_owner_cols(gr1["d_wpp"])], axis=1),
    ]
    core = lax.axis_index("c").astype(jnp.int32).reshape(1)
    from_sibling = _rs_sibling_exchange(grad_groups)
    tags = ("rows", "gu", "ev_in", "od_in", "pp")
    chip_part = [_rs_chip_sum(core, g, r, f"rs_chip_sum_{t}") for g, r, t in zip(grad_groups, from_sibling, tags)]
    chip_recv = _rs_chip_exchange(chip_part)
    w_grp, m_grp, v_grp = (_group_shards(*ws, F32) for ws in (big_w, big_m, big_v))
    updated = [_rs_sum_adamw(r, w_, m_, v_, f"rs_sum_adamw_{t}")
               for r, w_, m_, v_, t in zip(chip_recv, w_grp, m_grp, v_grp, tags)]
    big_g, big_d, big_nm, big_nv = (_ungroup_shards([u[k] for u in updated]) for k in range(4))

    def small_pack(nmix, nffn, nple, nfin, bf, sk, extra):
        last = jnp.concatenate([bf.reshape(-1), sk.reshape(-1), extra.reshape(-1)])
        last = jnp.pad(last, (0, D_MODEL - last.shape[0]))
        return jnp.concatenate([nmix, nffn, nple, nfin.reshape(1, -1), last[None, :]], axis=0)

    small_g = small_pack(jnp.concatenate([dg_mix0, dg_mix1]), jnp.concatenate([gr0["dg_ffn"], gr1["dg_ffn"]]),
                         jnp.concatenate([gr0["dg_ple"], gr1["dg_ple"]]), dg_final,
                         db_f[0, :N_FOX], dsink[:, 0, 0], loss_part[0, :1])
    zero1 = jnp.zeros((1,), F32)
    small_w = small_pack(norm_mix, norm_ffn, norm_ple, norm_final, ev_b_f, od_sinks, zero1)
    small_m = small_pack(m_norm_mix, m_norm_ffn, m_norm_ple, m_norm_final, m_ev_b_f, m_od_sinks, zero1)
    small_v = small_pack(v_norm_mix, v_norm_ffn, v_norm_ple, v_norm_final, v_ev_b_f, v_od_sinks, zero1)
    sg, sd, sm, sv_ = _small_allreduce_adamw(small_g, small_w, small_m, small_v)

    def small_unpack(t):
        return [t[0:2], t[2:4], t[4:6], t[6], t[7, :N_FOX][None, :], t[7, N_FOX:N_FOX + N_Q][None, :]]

    loss = sg[7, N_FOX + N_Q]

    def ordered(small, big):
        nm, nf, npl, nfin, bf, sk = small_unpack(small)
        ev_in, ev_out, od_in, od_out, fg, fu, fd, pproj, pgate = big
        return [nm, nf, npl, nfin, ev_in, bf, ev_out, od_in, sk, od_out, fg, fu, fd, pproj, pgate]

    return (loss, grad_x[None], *ordered(sg, big_g), *ordered(sd, big_d),
            *ordered(sm, big_nm), *ordered(sv_, big_nv))
```

```python
import functools

import jax
import jax.numpy as jnp
from jax import lax
from jax.experimental import pallas as pl
from jax.experimental.pallas import tpu as pltpu

F32 = jnp.float32
BF16 = jnp.bfloat16

D_MODEL = 1024
HEAD_DIM = 64
N_FOX = 8
N_SB = 8
FOX_W = N_FOX * HEAD_DIM
SB_W = N_SB * HEAD_DIM
QKV_W = 3 * FOX_W + 3 * SB_W
EVEN_IN = QKV_W + N_FOX
N_Q = 16
N_KV = 4
ODD_IN = N_Q * HEAD_DIM + 2 * N_KV * HEAD_DIM
WINDOW = 128
ROPE_THETA = 10000.0
D_FF = 2816
PLE_DIM = 256
EPS = 1e-6
NEG_INF = -1e30
SCALE = HEAD_DIM ** -0.5

ADAM_LR = 0.001
ADAM_B1 = 0.9
ADAM_B2 = 0.999
ADAM_EPS = 1e-08
ADAM_WD = 0.01
ADAM_STEP = 10

N_DEV = 8
LANES = 128
ROW_TILE = 256
ATTN_BLOCK = 256
HID_SHARD = D_FF // N_DEV
HID_PAD = 384
D_FF_PAD = N_DEV * HID_PAD

MESH = pl.DeviceIdType.MESH


def _pick(n, prefs):
    for t in prefs:
        if n % t == 0:
            return t
    return n


def _pad_to(a, axis, size):
    pad = [(0, 0)] * a.ndim
    pad[axis] = (0, size - a.shape[axis])
    return jnp.pad(a, pad)


def _group_shards(ev_in, ev_out, od_in, od_out, gate, up, down, pproj, pgate, dtype):
    rows = jnp.concatenate([ev_out[0], od_out[0], _pad_to(down[0], 0, HID_PAD), _pad_to(down[1], 0, HID_PAD),
                            pgate[0], pgate[1]], axis=0)
    gu = jnp.concatenate([jnp.concatenate([_pad_to(gate[l], 1, HID_PAD), _pad_to(up[l], 1, HID_PAD)], axis=1)
                          for l in range(2)], axis=0)
    groups = [rows, gu, ev_in[0], od_in[0], pproj.reshape(-1, pproj.shape[-1])]
    return [g.astype(dtype) for g in groups]


def _ungroup_shards(groups):
    rows, gu, ev_in, od_in, pp = groups
    d = D_MODEL
    down = jnp.stack([rows[256 + HID_PAD * l: 256 + HID_PAD * l + HID_SHARD] for l in range(2)])
    pgate = jnp.stack([rows[1024 + 128 * l: 1152 + 128 * l] for l in range(2)])
    gate = jnp.stack([gu[d * l: d * (l + 1), :HID_SHARD] for l in range(2)])
    up = jnp.stack([gu[d * l: d * (l + 1), HID_PAD:HID_PAD + HID_SHARD] for l in range(2)])
    return [ev_in[None], rows[None, 0:128], od_in[None], rows[None, 128:256], gate, up, down,
            pp.reshape(2, PLE_DIM, -1), pgate]


def _by_owner_cols(full):
    r, c8 = full.shape
    return full.reshape(r, N_DEV, c8 // N_DEV).transpose(1, 0, 2)


def _from_owner_cols(g):
    n, r, c = g.shape
    return g.transpose(1, 0, 2).reshape(r, n * c)


_ANY = pl.BlockSpec(memory_space=pl.ANY)


def _all_gather_steps(x_refs, out_refs, send_sems, recv_sems, local_sems):
    n = len(x_refs)
    x, y, c = lax.axis_index("x"), lax.axis_index("y"), lax.axis_index("c")
    me, sibling = (x, y, c), (x, y, 1 - c)
    chips = [(1 - x, y), (x, 1 - y), (1 - x, 1 - y)]

    def copy(a, k, block, to, from_input=False):
        px, py, pc = block
        slot = out_refs[a].at[4 * px + 2 * py + pc]
        return pltpu.make_async_remote_copy(
            src_ref=x_refs[a] if from_input else slot, dst_ref=slot,
            send_sem=send_sems.at[7 * a + k], recv_sem=recv_sems.at[7 * a + k],
            device_id=to, device_id_type=MESH)

    def mine():
        return [pltpu.make_async_copy(x_refs[a], out_refs[a].at[4 * x + 2 * y + c], local_sems.at[a]) for a in range(n)]

    def first():
        out = []
        for a in range(n):
            out.append(copy(a, 0, me, sibling, True))
            out += [copy(a, 1 + j, me, (*chip, c), True) for j, chip in enumerate(chips)]
        return out

    def issue():
        for cp in mine() + first():
            cp.start()

    def complete():
        passed = []
        for j, chip in enumerate(chips):
            for a in range(n):
                copy(a, 1 + j, (*chip, c), me).wait_recv()
                passed.append(copy(a, 4 + j, (*chip, c), sibling))
                passed[-1].start()
        for a in range(n):
            copy(a, 0, sibling, me).wait_recv()
            for j, chip in enumerate(chips):
                copy(a, 4 + j, (*chip, 1 - c), me).wait_recv()
        for cp in first() + passed:
            cp.wait_send()
        for cp in mine():
            cp.wait()

    return issue, complete


def _all_gather_scratch(n):
    return [pltpu.SemaphoreType.DMA((7 * n,)), pltpu.SemaphoreType.DMA((7 * n,)), pltpu.SemaphoreType.DMA((n,))]


def _gathered_shapes(shards):
    return tuple(jax.ShapeDtypeStruct((N_DEV,) + s.shape, s.dtype) for s in shards)


def _all_gather_weights(shards):
    n = len(shards)

    def body(*refs):
        issue, complete = _all_gather_steps(refs[:n], refs[n:2 * n], *refs[2 * n:])
        issue()
        complete()

    return pl.pallas_call(
        body, name="ag_weights", out_shape=_gathered_shapes(shards),
        in_specs=[_ANY] * n, out_specs=tuple([_ANY] * n), scratch_shapes=_all_gather_scratch(n),
    )(*shards)


def _rs_sibling_exchange(gps):
    n = len(gps)

    def body(*refs):
        g_refs, out_refs = refs[:n], refs[n:2 * n]
        send_sems, recv_sems = refs[2 * n:]
        x, y, c = lax.axis_index("x"), lax.axis_index("y"), lax.axis_index("c")
        copies = []
        for a in range(n):
            for k in range(4):
                copies.append(pltpu.make_async_remote_copy(
                    src_ref=g_refs[a].at[2 * k + (1 - c)], dst_ref=out_refs[a].at[k],
                    send_sem=send_sems.at[4 * a + k], recv_sem=recv_sems.at[4 * a + k],
                    device_id=(x, y, 1 - c), device_id_type=MESH))
        for cp in copies:
            cp.start()
        for cp in copies:
            cp.wait_recv()
        for cp in copies:
            cp.wait_send()

    return pl.pallas_call(
        body, name="rs_sibling_exchange",
        out_shape=tuple(jax.ShapeDtypeStruct((4,) + g.shape[1:], g.dtype) for g in gps),
        in_specs=[_ANY] * n, out_specs=tuple([_ANY] * n),
        scratch_shapes=[pltpu.SemaphoreType.DMA((4 * n,)), pltpu.SemaphoreType.DMA((4 * n,))],
    )(*gps)


def _rs_chip_sum(core, gp, recv, name):
    _, rows, cols = gp.shape
    tr = ROW_TILE

    def body(core_ref, a_ref, b_ref, o_ref):
        o_ref[...] = (a_ref[...] + b_ref[...]).astype(BF16)

    return pl.pallas_call(
        body, name=name,
        out_shape=jax.ShapeDtypeStruct((4, rows, cols), BF16),
        grid_spec=pltpu.PrefetchScalarGridSpec(
            num_scalar_prefetch=1, grid=(4, rows // tr),
            in_specs=[pl.BlockSpec((1, tr, cols), lambda k, r, cr: (2 * k + cr[0], r, 0)),
                      pl.BlockSpec((1, tr, cols), lambda k, r, cr: (k, r, 0))],
            out_specs=pl.BlockSpec((1, tr, cols), lambda k, r, cr: (k, r, 0))),
    )(core, gp, recv)


def _rs_chip_exchange(parts):
    n = len(parts)

    def body(*refs):
        p_refs, out_refs = refs[:n], refs[n:2 * n]
        send_sems, recv_sems, local_sems = refs[2 * n:]
        x, y, c = lax.axis_index("x"), lax.axis_index("y"), lax.axis_index("c")
        my_chip = 2 * x + y
        mine = [pltpu.make_async_copy(p_refs[a].at[my_chip], out_refs[a].at[my_chip], local_sems.at[a])
                for a in range(n)]
        for cp in mine:
            cp.start()
        copies = []
        for a in range(n):
            for j, (px, py) in enumerate([(1 - x, y), (x, 1 - y), (1 - x, 1 - y)]):
                copies.append(pltpu.make_async_remote_copy(
                    src_ref=p_refs[a].at[2 * px + py], dst_ref=out_refs[a].at[my_chip],
                    send_sem=send_sems.at[3 * a + j], recv_sem=recv_sems.at[3 * a + j],
                    device_id=(px, py, c), device_id_type=MESH))
        for cp in copies:
            cp.start()
        for cp in copies:
            cp.wait_recv()
        for cp in copies:
            cp.wait_send()
        for cp in mine:
            cp.wait()

    return pl.pallas_call(
        body, name="rs_chip_exchange",
        out_shape=tuple(jax.ShapeDtypeStruct(p_.shape, p_.dtype) for p_ in parts),
        in_specs=[_ANY] * n, out_specs=tuple([_ANY] * n),
        scratch_shapes=[pltpu.SemaphoreType.DMA((3 * n,)), pltpu.SemaphoreType.DMA((3 * n,)),
                        pltpu.SemaphoreType.DMA((n,))],
    )(*parts)


def _adamw(w, g, m, v):
    m = ADAM_B1 * m + (1.0 - ADAM_B1) * g
    v = ADAM_B2 * v + (1.0 - ADAM_B2) * (g * g)
    m_hat = m / (1.0 - ADAM_B1 ** ADAM_STEP)
    v_hat = v / (1.0 - ADAM_B2 ** ADAM_STEP)
    delta = -ADAM_LR * (m_hat / (jnp.sqrt(v_hat) + ADAM_EPS) + ADAM_WD * w)
    return delta, m, v


def _rs_sum_adamw(recv, w, m, v, name):
    _, rows, cols = recv.shape
    tr = ROW_TILE

    def body(r_ref, w_ref, m_ref, v_ref, g_out, d_out, m_out, v_out):
        g = r_ref[0].astype(F32)
        for k in range(1, 4):
            g = g + r_ref[k].astype(F32)
        delta, m_new, v_new = _adamw(w_ref[...], g, m_ref[...], v_ref[...])
        g_out[...] = g
        d_out[...] = delta
        m_out[...] = m_new
        v_out[...] = v_new

    flat = pl.BlockSpec((tr, cols), lambda r: (r, 0))
    shp = jax.ShapeDtypeStruct((rows, cols), F32)
    return pl.pallas_call(
        body, name=name, grid=(rows // tr,),
        out_shape=(shp, shp, shp, shp),
        in_specs=[pl.BlockSpec((4, tr, cols), lambda r: (0, r, 0)), flat, flat, flat],
        out_specs=(flat, flat, flat, flat),
    )(recv, w, m, v)


def _small_allreduce_adamw(vec, w, m, v):
    rows, cols = vec.shape

    def body(x_ref, w_ref, m_ref, v_ref, g_out, d_out, m_out, v_out, gather, send_sems, recv_sems):
        x, y, c = lax.axis_index("x"), lax.axis_index("y"), lax.axis_index("c")
        me = 4 * x + 2 * y + c
        copies = []
        for d in range(1, N_DEV):
            dx, dy, dc = (d >> 2) & 1, (d >> 1) & 1, d & 1
            peer = (x ^ dx if dx else x, y ^ dy if dy else y, c ^ dc if dc else c)
            copies.append(pltpu.make_async_remote_copy(
                src_ref=x_ref, dst_ref=gather.at[me],
                send_sem=send_sems.at[d - 1], recv_sem=recv_sems.at[d - 1],
                device_id=peer, device_id_type=MESH))
        for cp in copies:
            cp.start()
        gather[me] = x_ref[...]
        for cp in copies:
            cp.wait_recv()
        for cp in copies:
            cp.wait_send()
        g = gather[0]
        for k in range(1, N_DEV):
            g = g + gather[k]
        delta, m_new, v_new = _adamw(w_ref[...], g, m_ref[...], v_ref[...])
        g_out[...] = g
        d_out[...] = delta
        m_out[...] = m_new
        v_out[...] = v_new

    vm = pl.BlockSpec(memory_space=pltpu.VMEM)
    shp = jax.ShapeDtypeStruct((rows, cols), F32)
    return pl.pallas_call(
        body, name="small_allreduce_adamw",
        out_shape=(shp, shp, shp, shp),
        in_specs=[vm, vm, vm, vm], out_specs=(vm, vm, vm, vm),
        scratch_shapes=[pltpu.VMEM((N_DEV, rows, cols), F32),
                        pltpu.SemaphoreType.DMA((N_DEV - 1,)), pltpu.SemaphoreType.DMA((N_DEV - 1,))],
    )(vec, w, m, v)


def _mm(a, b, *, ta=False, tb=False, out_dtype=BF16, res=None, name, group=None, group_width=None):
    if ta:
        kdim, m = a.shape
    else:
        m, kdim = a.shape
    if group == "n":
        assert not ta and not tb and res is None
        ng, kb, tn = b.shape
        n = ng * tn
    elif group == "k":
        assert tb and not ta and res is None
        ng, n, tk = b.shape
        kb = ng * tk
    elif tb:
        n, kb = b.shape
    else:
        kb, n = b.shape
    assert kdim == kb, (a.shape, b.shape, ta, tb, group)
    tm = _pick(m, (1024, 512, 256, 128))
    if group == "out":
        assert ta and not tb and res is None
        tn = group_width
    elif group != "n":
        tn = _pick(n, (1024, 1408, 768, 512, 256, 128))
    if group != "k":
        tk = _pick(kdim, (1024, 1408, 512, 256, 128))
    nk = kdim // tk
    dn = (((0 if ta else 1,), (1 if tb else 0,)), ((), ()))
    has_res = res is not None
    in_place = nk > 1 and out_dtype == F32
    use_acc = nk > 1 and not in_place

    def body(*refs):
        a_ref, b_ref = refs[:2]
        r_ref = refs[2] if has_res else None
        o_ref = refs[3] if has_res else refs[2]
        part = lax.dot_general(a_ref[...], b_ref[...], dn, preferred_element_type=F32)
        if nk == 1:
            if has_res:
                part = part + r_ref[...].astype(F32)
            o_ref[...] = part.astype(out_dtype)
            return
        k = pl.program_id(2)
        acc = o_ref if in_place else refs[-1]

        @pl.when(k == 0)
        def _():
            acc[...] = part + r_ref[...].astype(F32) if has_res else part

        @pl.when(k > 0)
        def _():
            acc[...] += part

        if use_acc:
            @pl.when(k == nk - 1)
            def _():
                o_ref[...] = acc[...].astype(out_dtype)

    a_spec = (pl.BlockSpec((tk, tm), lambda i, j, k: (k, i)) if ta
              else pl.BlockSpec((tm, tk), lambda i, j, k: (i, k)))
    if group == "n":
        b_spec = pl.BlockSpec((None, tk, tn), lambda i, j, k: (j, k, 0))
    elif group == "k":
        b_spec = pl.BlockSpec((None, tn, tk), lambda i, j, k: (k, j, 0))
    elif tb:
        b_spec = pl.BlockSpec((tn, tk), lambda i, j, k: (j, k))
    else:
        b_spec = pl.BlockSpec((tk, tn), lambda i, j, k: (k, j))
    if group == "out":
        o_spec = pl.BlockSpec((None, tm, tn), lambda i, j, k: (j, i, 0))
        out_shape = jax.ShapeDtypeStruct((n // tn, m, tn), out_dtype)
    else:
        o_spec = pl.BlockSpec((tm, tn), lambda i, j, k: (i, j))
        out_shape = jax.ShapeDtypeStruct((m, n), out_dtype)
    in_specs = [a_spec, b_spec] + ([o_spec] if has_res else [])
    args = (a, b) + ((res,) if has_res else ())
    return pl.pallas_call(
        body, name=name, grid=(m // tm, n // tn, nk),
        out_shape=out_shape,
        in_specs=in_specs, out_specs=o_spec,
        scratch_shapes=[pltpu.VMEM((tm, tn), F32)] if use_acc else [],
        compiler_params=pltpu.CompilerParams(dimension_semantics=("parallel", "parallel", "arbitrary")),
    )(*args)


def _row_tile(s):
    return _pick(s, (256, 128))


def _rms_fwd(h, g, name):
    s, d = h.shape
    ts = _row_tile(s)

    def body(h_ref, g_ref, o_ref):
        x = h_ref[...]
        r = lax.rsqrt(jnp.mean(x * x, axis=-1, keepdims=True) + EPS)
        o_ref[...] = ((x * r) * g_ref[...]).astype(BF16)

    return pl.pallas_call(
        body, name=name, grid=(s // ts,),
        out_shape=jax.ShapeDtypeStruct((s, d), BF16),
        in_specs=[pl.BlockSpec((ts, d), lambda i: (i, 0)), pl.BlockSpec((1, d), lambda i: (0, 0))],
        out_specs=pl.BlockSpec((ts, d), lambda i: (i, 0)),
    )(h, g)


def _rms_bwd(h, g, dhns, dres, name, want_bf16):
    s, d = h.shape
    ts = _row_tile(s)
    n_in = len(dhns)

    def body(*refs):
        h_ref, g_ref, r_ref = refs[:3]
        dy_refs = refs[3:3 + n_in]
        outs = refs[3 + n_in:]
        dh_ref, dg_ref = outs[0], outs[-1]
        i = pl.program_id(0)
        x = h_ref[...]
        dy = dy_refs[0][...].astype(F32)
        for extra in dy_refs[1:]:
            dy = dy + extra[...].astype(F32)
        r = lax.rsqrt(jnp.mean(x * x, axis=-1, keepdims=True) + EPS)
        xr = x * r
        u = dy * g_ref[...]
        dx = r * (u - xr * jnp.mean(xr * u, axis=-1, keepdims=True))
        dh = r_ref[...] + dx
        dh_ref[...] = dh
        if want_bf16:
            outs[1][...] = dh.astype(BF16)

        @pl.when(i == 0)
        def _():
            dg_ref[...] = jnp.zeros_like(dg_ref)

        dg_ref[...] += jnp.sum(dy * xr, axis=0, keepdims=True)

    row = pl.BlockSpec((ts, d), lambda i: (i, 0))
    vec = pl.BlockSpec((1, d), lambda i: (0, 0))
    out_shape = [jax.ShapeDtypeStruct((s, d), F32)]
    out_specs = [row]
    if want_bf16:
        out_shape.append(jax.ShapeDtypeStruct((s, d), BF16))
        out_specs.append(row)
    out_shape.append(jax.ShapeDtypeStruct((1, d), F32))
    out_specs.append(vec)
    return pl.pallas_call(
        body, name=name, grid=(s // ts,),
        out_shape=tuple(out_shape),
        in_specs=[row, vec, row] + [row] * n_in, out_specs=tuple(out_specs),
        compiler_params=pltpu.CompilerParams(dimension_semantics=("arbitrary",)),
    )(h, g, dres, *dhns)


def _sigmoid_parts(z):
    e = jnp.exp(-jnp.abs(z))
    r = 1.0 / (1.0 + e)
    er = e * r
    pos = z >= 0
    return jnp.where(pos, r, er), jnp.where(pos, er, r)


def _gate_up_swiglu(hn, w_gu, name):
    s, d = hn.shape
    f = HID_PAD
    tm = _pick(s, (1024, 512, 256, 128))

    def body(x_ref, w_ref, ab_ref, u_ref):
        ab = jnp.dot(x_ref[...], w_ref[...], preferred_element_type=F32).astype(BF16)
        ab_ref[...] = ab
        a = ab[:, :f].astype(F32)
        b = ab[:, f:].astype(F32)
        sg, _ = _sigmoid_parts(a)
        u_ref[...] = ((a * sg) * b).astype(BF16)

    return pl.pallas_call(
        body, name=name, grid=(s // tm, N_DEV),
        out_shape=(jax.ShapeDtypeStruct((s, 2 * D_FF_PAD), BF16), jax.ShapeDtypeStruct((s, D_FF_PAD), BF16)),
        in_specs=[pl.BlockSpec((tm, d), lambda i, j: (i, 0)), pl.BlockSpec((None, d, 2 * f), lambda i, j: (j, 0, 0))],
        out_specs=(pl.BlockSpec((tm, 2 * f), lambda i, j: (i, j)), pl.BlockSpec((tm, f), lambda i, j: (i, j))),
    )(hn, w_gu)


def _down_t_swiglu_bwd(dh, w_down, ab, name):
    s, d = dh.shape
    f = HID_PAD
    tm = _pick(s, (1024, 512, 256, 128))

    def body(x_ref, w_ref, ab_ref, o_ref):
        du = lax.dot_general(x_ref[...], w_ref[...], _DN_NT, preferred_element_type=F32)
        g = du.astype(BF16).astype(F32)
        a = ab_ref[:, :f].astype(F32)
        b = ab_ref[:, f:].astype(F32)
        sg, sgm = _sigmoid_parts(a)
        silu = a * sg
        o_ref[:, :f] = (g * b * (sg + silu * sgm)).astype(BF16)
        o_ref[:, f:] = (g * silu).astype(BF16)

    return pl.pallas_call(
        body, name=name, grid=(s // tm, N_DEV),
        out_shape=jax.ShapeDtypeStruct(ab.shape, BF16),
        in_specs=[pl.BlockSpec((tm, d), lambda i, j: (i, 0)), pl.BlockSpec((f, d), lambda i, j: (j, 0)),
                  pl.BlockSpec((tm, 2 * f), lambda i, j: (i, j))],
        out_specs=pl.BlockSpec((tm, 2 * f), lambda i, j: (i, j)),
    )(dh, w_down, ab)


def _ple_fwd(h, gl, pp, name):
    s, d = h.shape
    ts = _row_tile(s)

    def body(h_ref, gl_ref, pp_ref, o_ref):
        sg, _ = _sigmoid_parts(gl_ref[...].astype(F32))
        o_ref[...] = h_ref[...] + sg * pp_ref[...].astype(F32)

    row = pl.BlockSpec((ts, d), lambda i: (i, 0))
    return pl.pallas_call(
        body, name=name, grid=(s // ts,),
        out_shape=jax.ShapeDtypeStruct((s, d), F32),
        in_specs=[row, row, row], out_specs=row,
    )(h, gl, pp)


def _ple_bwd(dh, gl, pp, name):
    s, d = dh.shape
    ts = _row_tile(s)

    def body(dh_ref, gl_ref, pp_ref, dgl_ref, dpp_ref):
        g = dh_ref[...]
        sg, sgm = _sigmoid_parts(gl_ref[...].astype(F32))
        dpp_ref[...] = (g * sg).astype(BF16)
        dgl_ref[...] = (g * pp_ref[...].astype(F32) * (sg * sgm)).astype(BF16)

    row = pl.BlockSpec((ts, d), lambda i: (i, 0))
    shp = jax.ShapeDtypeStruct((s, d), BF16)
    return pl.pallas_call(
        body, name=name, grid=(s // ts,),
        out_shape=(shp, shp), in_specs=[row, row, row], out_specs=(row, row),
    )(dh, gl, pp)


def _final_norm_loss(h, g, target):
    s, d = h.shape
    ts = _row_tile(s)

    def body(h_ref, g_ref, t_ref, loss_ref, dh_ref, dg_ref):
        i = pl.program_id(0)
        x = h_ref[...]
        gain = g_ref[...]
        r = lax.rsqrt(jnp.mean(x * x, axis=-1, keepdims=True) + EPS)
        xr = x * r
        err = xr * gain - t_ref[...]
        dy = err * (1.0 / d)
        u = dy * gain
        dh_ref[...] = r * (u - xr * jnp.mean(xr * u, axis=-1, keepdims=True))

        @pl.when(i == 0)
        def _():
            dg_ref[...] = jnp.zeros_like(dg_ref)
            loss_ref[...] = jnp.zeros_like(loss_ref)

        dg_ref[...] += jnp.sum(dy * xr, axis=0, keepdims=True)
        tok = jnp.mean(err * err, axis=-1, keepdims=True)
        loss_ref[...] += 0.5 * jnp.sum(tok, axis=0, keepdims=True)

    row = pl.BlockSpec((ts, d), lambda i: (i, 0))
    vec = pl.BlockSpec((1, d), lambda i: (0, 0))
    return pl.pallas_call(
        body, name="final_norm_loss", grid=(s // ts,),
        out_shape=(jax.ShapeDtypeStruct((1, LANES), F32), jax.ShapeDtypeStruct((s, d), F32),
                   jax.ShapeDtypeStruct((1, d), F32)),
        in_specs=[row, vec, row],
        out_specs=(pl.BlockSpec((1, LANES), lambda i: (0, 0)), row, vec),
        compiler_params=pltpu.CompilerParams(dimension_semantics=("arbitrary",)),
    )(h, g, target)


def _rope(xin, w, cos, sin_signed, sign, name):
    s = xin.shape[0]
    ts = _pick(s, (512, 256, 128))

    def body(x_ref, c_ref, s_ref, o_ref):
        cos_b, sin_b = c_ref[...], s_ref[...]
        lane = lax.broadcasted_iota(jnp.int32, cos_b.shape, 1)
        low = (lane & (HEAD_DIM - 1)) < (HEAD_DIM // 2)
        for j in range(w // LANES):
            cols = slice(j * LANES, (j + 1) * LANES)
            x = x_ref[:, cols].astype(F32)
            swapped = jnp.where(low, pltpu.roll(x, LANES - HEAD_DIM // 2, 1), pltpu.roll(x, HEAD_DIM // 2, 1))
            o_ref[:, cols] = (x * cos_b + sign * (swapped * sin_b)).astype(BF16)

    blk = pl.BlockSpec((ts, w), lambda i: (i, 0))
    tab = pl.BlockSpec((ts, LANES), lambda i: (i, 0))
    return pl.pallas_call(
        body, name=name, grid=(s // ts,),
        out_shape=jax.ShapeDtypeStruct((s, w), BF16),
        in_specs=[blk, tab, tab], out_specs=blk,
    )(xin, cos, sin_signed)


def _fgate_fwd(flog, bias):
    s, w = flog.shape

    def body(x_ref, b_ref, o_ref):
        rowi = lax.broadcasted_iota(jnp.int32, (8, w), 0)
        b = b_ref[...]

        def step(g, carry):
            sl = pl.ds(pl.multiple_of(g * 8, 8), 8)
            x = x_ref[sl, :] + b
            lf = jnp.minimum(x, 0.0) - jnp.log1p(jnp.exp(-jnp.abs(x)))
            for sh in (1, 2, 4):
                lf = lf + jnp.where(rowi >= sh, pltpu.roll(lf, sh, 0), 0.0)
            out = lf + carry
            o_ref[sl, :] = out
            return jnp.broadcast_to(out[7:8, :], (8, w))

        lax.fori_loop(0, s // 8, step, jnp.zeros((8, w), F32))

    vm = pl.BlockSpec(memory_space=pltpu.VMEM)
    return pl.pallas_call(
        body, name="fgate_fwd", out_shape=jax.ShapeDtypeStruct((s, w), F32),
        in_specs=[vm, vm], out_specs=vm,
    )(flog, bias)


def _fgate_bwd(gcum, flog, bias):
    s, w = flog.shape

    def body(g_ref, x_ref, b_ref, o_ref, db_ref):
        rowi = lax.broadcasted_iota(jnp.int32, (8, w), 0)
        lane = lax.broadcasted_iota(jnp.int32, (8, w), 1)
        b = b_ref[...]

        def step(t, carry):
            run, dbsum = carry
            g = s // 8 - 1 - t
            sl = pl.ds(pl.multiple_of(g * 8, 8), 8)
            c = g_ref[sl, :]
            for sh in (1, 2, 4):
                c = c + jnp.where(rowi < 8 - sh, pltpu.roll(c, 8 - sh, 0), 0.0)
            c = c + run
            _, sgm = _sigmoid_parts(x_ref[sl, :] + b)
            dl = jnp.where(lane < N_FOX, c * sgm, 0.0)
            o_ref[sl, :] = dl.astype(BF16)
            return jnp.broadcast_to(c[0:1, :], (8, w)), dbsum + dl

        _, dbsum = lax.fori_loop(0, s // 8, step, (jnp.zeros((8, w), F32), jnp.zeros((8, w), F32)))
        db_ref[...] = jnp.sum(dbsum, axis=0, keepdims=True)

    vm = pl.BlockSpec(memory_space=pltpu.VMEM)
    return pl.pallas_call(
        body, name="fgate_bwd",
        out_shape=(jax.ShapeDtypeStruct((s, w), BF16), jax.ShapeDtypeStruct((1, w), F32)),
        in_specs=[vm, vm, vm], out_specs=(vm, vm),
    )(gcum, flog, bias)


_DN_NT = (((1,), (1,)), ((), ()))
_DN_TN = (((0,), (0,)), ((), ()))


def _head_mask(shape, hh):
    lane = lax.broadcasted_iota(jnp.int32, shape, 1)
    return (lane >= HEAD_DIM * hh) & (lane < HEAD_DIM * (hh + 1))


SKIP_BELOW = -110.0


def _sweep_left(i, carry, tile, go_on):
    def flag(j, c):
        return jnp.logical_and(j >= 0, go_on(jnp.maximum(j, 0), c)).astype(jnp.int32)

    def body(st):
        j, _, c = st
        c = tile(j, c)
        return j - 1, flag(j - 1, c), c

    return lax.while_loop(lambda st: st[1] > 0, body, (i - 1, flag(i - 1, carry), carry))[2]


def _key_norm_max(k_ref, kn_ref):
    k2 = k_ref[...].astype(F32)
    sq = k2 * k2
    for hh in range(2):
        n2 = jnp.sum(jnp.where(_head_mask(sq.shape, hh), sq, 0.0), axis=1, keepdims=True)
        kn_ref[hh] = jnp.broadcast_to(jnp.sqrt(jnp.max(n2, axis=0, keepdims=True)), kn_ref.shape[1:])


def _fox_fwd(proj, ccol, crow, gather):
    s = proj.shape[0]
    blk = min(ATTN_BLOCK, s)
    nq = s // blk
    npair = N_FOX // 2
    ng = len(gather)

    def body(*refs):
        q_ref, k_ref, v_ref, cc_ref, cr_ref = refs[:5]
        x_refs = refs[5:5 + ng]
        o_ref, a_ref = refs[5 + ng:7 + ng]
        g_refs = refs[7 + ng:7 + 2 * ng]
        kn_ref = refs[7 + 2 * ng]
        p_, i = pl.program_id(0), pl.program_id(1)
        issue, complete = _all_gather_steps(x_refs, g_refs, *refs[8 + 2 * ng:])

        @pl.when((p_ == 0) & (i == 0))
        def _():
            issue()

        @pl.when(i == 0)
        def _():
            _key_norm_max(k_ref, kn_ref)

        q2 = q_ref[...].astype(F32) * SCALE
        row = lax.broadcasted_iota(jnp.int32, (blk, blk), 0)
        col = lax.broadcasted_iota(jnp.int32, (blk, blk), 1)
        outs = []
        for hh in range(2):
            hm = _head_mask(q2.shape, hh)
            qh = jnp.where(hm, q2, 0.0).astype(BF16)
            ct = cc_ref[hh][:, 0:1]
            qk_max = jnp.sqrt(jnp.sum(jnp.where(hm, q2 * q2, 0.0), axis=1, keepdims=True)) * kn_ref[hh][0:1, 0:1]

            def go_on(j, carry, ct=ct, qk_max=qk_max, hh=hh):
                bias_max = ct - jnp.min(cr_ref[2 * p_ + hh, j], axis=1, keepdims=True)
                return jnp.max(qk_max + bias_max - carry[0]) > SKIP_BELOW

            def tile(j, carry, masked, qh=qh, ct=ct, hh=hh):
                m, l, acc = carry
                sl = pl.ds(pl.multiple_of(j * blk, blk), blk)
                kb, vb = k_ref[sl, :], v_ref[sl, :]
                sc = lax.dot_general(qh, kb, _DN_NT, preferred_element_type=F32)
                sc = sc + (ct - cr_ref[2 * p_ + hh, j])
                if masked:
                    sc = jnp.where(col <= row, sc, NEG_INF)
                m_new = jnp.maximum(m, jnp.max(sc, axis=1, keepdims=True))
                alpha = jnp.exp(m - m_new)
                pm = jnp.exp(sc - m_new)
                l = alpha * l + jnp.sum(pm, axis=1, keepdims=True)
                acc = alpha * acc + jnp.dot(pm.astype(BF16), vb, preferred_element_type=F32)
                return m_new, l, acc

            init = (jnp.full((blk, 1), NEG_INF, F32), jnp.zeros((blk, 1), F32), jnp.zeros((blk, LANES), F32))
            carry = tile(i, init, True)
            m, l, acc = _sweep_left(i, carry, lambda j, c, tile=tile: tile(j, c, False), go_on)
            outs.append(acc / l)
            a_ref[hh] = jnp.broadcast_to(ct - (m + jnp.log(l)), (blk, LANES))
        o_ref[...] = jnp.where(_head_mask(outs[0].shape, 0), outs[0], outs[1]).astype(BF16)

        @pl.when((p_ == npair - 1) & (i == nq - 1))
        def _():
            complete()

    seq = lambda base: pl.BlockSpec((s, LANES), lambda p, i: (0, base + p))
    res = pl.pallas_call(
        body, name="fox_fwd_ag", grid=(npair, nq),
        scratch_shapes=[pltpu.VMEM((2, 8, LANES), F32)] + _all_gather_scratch(ng),
        out_shape=(jax.ShapeDtypeStruct((s, FOX_W), BF16), jax.ShapeDtypeStruct((N_FOX, s, LANES), F32))
        + _gathered_shapes(gather),
        in_specs=[pl.BlockSpec((blk, LANES), lambda p, i: (i, p)), seq(npair), seq(2 * npair),
                  pl.BlockSpec((2, blk, LANES), lambda p, i: (p, i, 0)),
                  pl.BlockSpec((N_FOX, nq, 1, blk), lambda p, i: (0, 0, 0, 0))] + [_ANY] * ng,
        out_specs=(pl.BlockSpec((blk, LANES), lambda p, i: (i, p)),
                   pl.BlockSpec((2, blk, LANES), lambda p, i: (p, i, 0))) + tuple([_ANY] * ng),
        compiler_params=pltpu.CompilerParams(dimension_semantics=("arbitrary", "arbitrary")),
    )(proj, proj, proj, ccol, crow, *gather)
    return res[0], res[1], list(res[2:])


def _fox_bwd(proj, do, o, acol, crow):
    s = proj.shape[0]
    blk = min(ATTN_BLOCK, s)
    nq = s // blk
    npair = N_FOX // 2

    def body(q_ref, k_ref, v_ref, do_ref, o_ref, a_ref, cr_ref, dq_ref, dk_ref, dv_ref, gc_ref,
             dk_acc, dv_acc, kn_ref):
        p_, i = pl.program_id(0), pl.program_id(1)

        @pl.when(i == 0)
        def _():
            dk_acc[...] = jnp.zeros_like(dk_acc)
            dv_acc[...] = jnp.zeros_like(dv_acc)
            gc_ref[...] = jnp.zeros_like(gc_ref)
            _key_norm_max(k_ref, kn_ref)

        q2 = q_ref[...].astype(F32) * SCALE
        do2 = do_ref[...]
        prod = do2.astype(F32) * o_ref[...].astype(F32)
        row = lax.broadcasted_iota(jnp.int32, (blk, blk), 0)
        col = lax.broadcasted_iota(jnp.int32, (blk, blk), 1)
        dqs = []
        for hh in range(2):
            hm = _head_mask(q2.shape, hh)
            qh = jnp.where(hm, q2, 0.0).astype(BF16)
            doh = jnp.where(hm, do2, jnp.zeros_like(do2))
            delta = jnp.sum(jnp.where(hm, prod, 0.0), axis=1, keepdims=True)
            at = a_ref[hh][:, 0:1]
            qk_max = jnp.sqrt(jnp.sum(jnp.where(hm, q2 * q2, 0.0), axis=1, keepdims=True)) * kn_ref[hh][0:1, 0:1]

            def go_on(j, carry, at=at, qk_max=qk_max, hh=hh):
                bias_max = at - jnp.min(cr_ref[2 * p_ + hh, j], axis=1, keepdims=True)
                return jnp.max(qk_max + bias_max) > SKIP_BELOW

            def tile(j, carry, masked, qh=qh, doh=doh, delta=delta, at=at, hh=hh):
                dq, rs = carry
                sl = pl.ds(pl.multiple_of(j * blk, blk), blk)
                kb, vb = k_ref[sl, :], v_ref[sl, :]
                sc = lax.dot_general(qh, kb, _DN_NT, preferred_element_type=F32)
                sc = sc + (at - cr_ref[2 * p_ + hh, j])
                if masked:
                    sc = jnp.where(col <= row, sc, NEG_INF)
                pm = jnp.exp(sc)
                dp = lax.dot_general(doh, vb, _DN_NT, preferred_element_type=F32)
                ds = pm * (dp - delta)
                dsb = ds.astype(BF16)
                dk_acc[sl, :] += lax.dot_general(dsb, qh, _DN_TN, preferred_element_type=F32)
                dv_acc[sl, :] += lax.dot_general(pm.astype(BF16), doh, _DN_TN, preferred_element_type=F32)
                gc_ref[hh, j] += -jnp.sum(ds, axis=0, keepdims=True)
                return dq + jnp.dot(dsb, kb, preferred_element_type=F32), rs + jnp.sum(ds, axis=1, keepdims=True)

            carry = tile(i, (jnp.zeros((blk, LANES), F32), jnp.zeros((blk, 1), F32)), True)
            dq, rs = _sweep_left(i, carry, lambda j, c, tile=tile: tile(j, c, False), go_on)
            dqs.append(dq)
            gc_ref[hh, i] += jnp.transpose(jnp.broadcast_to(rs, (blk, LANES)))[0:1, :]
        dq_ref[...] = (jnp.where(_head_mask(dqs[0].shape, 0), dqs[0], dqs[1]) * SCALE).astype(BF16)

        @pl.when(i == nq - 1)
        def _():
            dk_ref[...] = dk_acc[...].astype(BF16)
            dv_ref[...] = dv_acc[...].astype(BF16)

    seq = lambda base: pl.BlockSpec((s, LANES), lambda p, i: (0, base + p))
    qblk = lambda base: pl.BlockSpec((blk, LANES), lambda p, i: (i, base + p))
    rep = pl.BlockSpec((2, blk, LANES), lambda p, i: (p, i, 0))
    half = jax.ShapeDtypeStruct((s, FOX_W), BF16)
    return pl.pallas_call(
        body, name="fox_bwd", grid=(npair, nq),
        out_shape=(half, half, half, jax.ShapeDtypeStruct((N_FOX, nq, 1, blk), F32)),
        in_specs=[qblk(0), seq(npair), seq(2 * npair), qblk(0), qblk(0), rep,
                  pl.BlockSpec((N_FOX, nq, 1, blk), lambda p, i: (0, 0, 0, 0))],
        out_specs=(qblk(0), seq(0), seq(0), pl.BlockSpec((2, nq, 1, blk), lambda p, i: (p, 0, 0, 0))),
        scratch_shapes=[pltpu.VMEM((s, LANES), F32), pltpu.VMEM((s, LANES), F32), pltpu.VMEM((2, 8, LANES), F32)],
        compiler_params=pltpu.CompilerParams(dimension_semantics=("parallel", "arbitrary")),
    )(proj, proj, proj, do, o, acol, crow)


def _sb_logs(z):
    neg = -(jnp.maximum(z, 0.0) + jnp.log1p(jnp.exp(-jnp.abs(z))))
    return neg, z + neg


def _split_dot(x, tri):
    hi = x.astype(BF16)
    lo = (x - hi.astype(F32)).astype(BF16)
    return jnp.dot(hi, tri, preferred_element_type=F32) + jnp.dot(lo, tri, preferred_element_type=F32)


def _sb_fwd(proj):
    s = proj.shape[0]
    blk = min(ATTN_BLOCK, s)
    nq = s // blk
    npair = N_SB // 2
    base = 3 * (N_FOX // 2)

    def body(q_ref, k_ref, v_ref, o_ref, r_ref):
        i = pl.program_id(1)
        q2 = q_ref[...].astype(F32) * SCALE
        row = lax.broadcasted_iota(jnp.int32, (blk, blk), 0)
        col = lax.broadcasted_iota(jnp.int32, (blk, blk), 1)
        strict = col < row
        tri = jnp.where(row > col, 1.0, 0.0).astype(BF16)
        lane = lax.broadcasted_iota(jnp.int32, (blk, LANES), 1)
        outs = []
        for hh in range(2):
            qh = jnp.where(_head_mask(q2.shape, hh), q2, 0.0).astype(BF16)

            def tile(j, carry, masked, qh=qh):
                rsum, acc, rbuf = carry
                sl = pl.ds(pl.multiple_of(j * blk, blk), blk)
                kb, vb = k_ref[sl, :], v_ref[sl, :]
                z = lax.dot_general(qh, kb, _DN_NT, preferred_element_type=F32)
                l1m, lb = _sb_logs(z)
                if masked:
                    l1m = jnp.where(strict, l1m, 0.0)
                sx = _split_dot(l1m, tri)
                a = jnp.exp(lb + sx + rsum)
                if masked:
                    a = jnp.where(strict, a, 0.0)
                acc = acc + jnp.dot(a.astype(BF16), vb, preferred_element_type=F32)
                rbuf = jnp.where(lane == j, rsum, rbuf)
                return rsum + jnp.sum(l1m, axis=1, keepdims=True), acc, rbuf

            init = (jnp.zeros((blk, 1), F32), jnp.zeros((blk, LANES), F32), jnp.full((blk, LANES), NEG_INF, F32))
            carry = tile(i, init, True)
            _, acc, rbuf = _sweep_left(i, carry, lambda j, c, tile=tile: tile(j, c, False),
                                       lambda j, c: jnp.max(c[0]) > SKIP_BELOW)
            outs.append(acc)
            r_ref[hh] = rbuf
        o_ref[...] = jnp.where(_head_mask(outs[0].shape, 0), outs[0], outs[1]).astype(BF16)

    seq = lambda b: pl.BlockSpec((s, LANES), lambda p, i: (0, b + p))
    return pl.pallas_call(
        body, name="sb_fwd", grid=(npair, nq),
        out_shape=(jax.ShapeDtypeStruct((s, SB_W), BF16), jax.ShapeDtypeStruct((N_SB, s, LANES), F32)),
        in_specs=[pl.BlockSpec((blk, LANES), lambda p, i: (i, base + p)), seq(base + npair), seq(base + 2 * npair)],
        out_specs=(pl.BlockSpec((blk, LANES), lambda p, i: (i, p)),
                   pl.BlockSpec((2, blk, LANES), lambda p, i: (p, i, 0))),
        compiler_params=pltpu.CompilerParams(dimension_semantics=("parallel", "arbitrary")),
    )(proj, proj, proj)


def _sb_bwd(proj, do, rsave):
    s = proj.shape[0]
    blk = min(ATTN_BLOCK, s)
    nq = s // blk
    npair = N_SB // 2
    base = 3 * (N_FOX // 2)

    def body(q_ref, k_ref, v_ref, do_ref, r_ref, dq_ref, dk_ref, dv_ref, dk_acc, dv_acc):
        i = pl.program_id(1)

        @pl.when(i == 0)
        def _():
            dk_acc[...] = jnp.zeros_like(dk_acc)
            dv_acc[...] = jnp.zeros_like(dv_acc)

        q2 = q_ref[...].astype(F32) * SCALE
        do2 = do_ref[...]
        row = lax.broadcasted_iota(jnp.int32, (blk, blk), 0)
        col = lax.broadcasted_iota(jnp.int32, (blk, blk), 1)
        strict = col < row
        tri_suffix = jnp.where(row > col, 1.0, 0.0).astype(BF16)
        tri_prefix = jnp.where(row < col, 1.0, 0.0).astype(BF16)
        lane = lax.broadcasted_iota(jnp.int32, (blk, LANES), 1)
        dqs = []
        for hh in range(2):
            hm = _head_mask(q2.shape, hh)
            qh = jnp.where(hm, q2, 0.0).astype(BF16)
            doh = jnp.where(hm, do2, jnp.zeros_like(do2))
            rbuf = r_ref[hh]

            def tile(j, carry, masked, qh=qh, doh=doh, rbuf=rbuf):
                pre, dq = carry
                sl = pl.ds(pl.multiple_of(j * blk, blk), blk)
                kb, vb = k_ref[sl, :], v_ref[sl, :]
                z = lax.dot_general(qh, kb, _DN_NT, preferred_element_type=F32)
                l1m, lb = _sb_logs(z)
                beta, one_m_beta = _sigmoid_parts(z)
                if masked:
                    l1m = jnp.where(strict, l1m, 0.0)
                sx = _split_dot(l1m, tri_suffix)
                rj = jnp.sum(jnp.where(lane == j, rbuf, 0.0), axis=1, keepdims=True)
                a = jnp.exp(lb + sx + rj)
                if masked:
                    a = jnp.where(strict, a, 0.0)
                da = lax.dot_general(doh, vb, _DN_NT, preferred_element_type=F32)
                g = a * da
                px = _split_dot(g, tri_prefix) + pre
                dz = g * one_m_beta - beta * px
                if masked:
                    dz = jnp.where(strict, dz, 0.0)
                dzb = dz.astype(BF16)
                dk_acc[sl, :] += lax.dot_general(dzb, qh, _DN_TN, preferred_element_type=F32)
                dv_acc[sl, :] += lax.dot_general(a.astype(BF16), doh, _DN_TN, preferred_element_type=F32)
                return pre + jnp.sum(g, axis=1, keepdims=True), dq + jnp.dot(dzb, kb, preferred_element_type=F32)

            reach = jnp.max(rbuf, axis=0, keepdims=True)
            dead = (reach <= SKIP_BELOW) & (lane[0:1, :] <= i)
            first = jnp.sum(jnp.where(dead, 1.0, 0.0)).astype(jnp.int32)
            carry = (jnp.zeros((blk, 1), F32), jnp.zeros((blk, LANES), F32))
            carry = lax.fori_loop(first, i, lambda t, c, tile=tile: tile(t, c, False), carry)
            _, dq = tile(i, carry, True)
            dqs.append(dq)
        dq_ref[...] = (jnp.where(_head_mask(dqs[0].shape, 0), dqs[0], dqs[1]) * SCALE).astype(BF16)

        @pl.when(i == nq - 1)
        def _():
            dk_ref[...] = dk_acc[...].astype(BF16)
            dv_ref[...] = dv_acc[...].astype(BF16)

    seq = lambda b: pl.BlockSpec((s, LANES), lambda p, i: (0, b + p))
    qblk = lambda b: pl.BlockSpec((blk, LANES), lambda p, i: (i, b + p))
    half = jax.ShapeDtypeStruct((s, SB_W), BF16)
    return pl.pallas_call(
        body, name="sb_bwd", grid=(npair, nq),
        out_shape=(half, half, half),
        in_specs=[qblk(base), seq(base + npair), seq(base + 2 * npair), qblk(npair),
                  pl.BlockSpec((2, blk, LANES), lambda p, i: (p, i, 0))],
        out_specs=(qblk(0), seq(0), seq(0)),
        scratch_shapes=[pltpu.VMEM((s, LANES), F32), pltpu.VMEM((s, LANES), F32)],
        compiler_params=pltpu.CompilerParams(dimension_semantics=("parallel", "arbitrary")),
    )(proj, proj, proj, do, rsave)


SWA_ROWS = 256
Q_W = N_Q * HEAD_DIM
KV_W = N_KV * HEAD_DIM
KV_PAIRS = KV_W // LANES
Q_PER_KVPAIR = Q_W // KV_PAIRS


def _lane_swap(x):
    xf = x.astype(F32)
    parts = [pltpu.roll(xf[:, j * LANES:(j + 1) * LANES], HEAD_DIM, 1) for j in range(x.shape[1] // LANES)]
    return (parts[0] if len(parts) == 1 else jnp.concatenate(parts, axis=1)).astype(x.dtype)


def _swa_specs(s):
    w = WINDOW
    ts = min(SWA_ROWS, s)
    nw = ts // w
    qcols = pl.BlockSpec((ts, Q_PER_KVPAIR), lambda kp, i: (i, kp))
    kbase, vbase = Q_W // LANES, (Q_W + KV_W) // LANES
    prev = lambda base: pl.BlockSpec((w, LANES), lambda kp, i: (jnp.maximum(i * nw - 1, 0), base + kp))
    cur = lambda base: pl.BlockSpec((ts, LANES), lambda kp, i: (i, base + kp))
    vec = pl.BlockSpec((1, Q_PER_KVPAIR), lambda kp, i: (0, kp))
    return ts, nw, qcols, prev(kbase), cur(kbase), prev(vbase), cur(vbase), vec


def _swa_fwd(qk, vsrc, sink_row):
    s = qk.shape[0]
    w = WINDOW
    ts, nw, qcols, kprev, kcur, vprev, vcur, vec = _swa_specs(s)

    def body(q_ref, kp_ref, kc_ref, vp_ref, vc_ref, s_ref, o_ref, lse_ref):
        i = pl.program_id(1)
        k2 = jnp.concatenate([kp_ref[...], kc_ref[...]], axis=0)
        v2 = jnp.concatenate([vp_ref[...], vc_ref[...]], axis=0)
        ksw, vsw = _lane_swap(k2), _lane_swap(v2)
        row = lax.broadcasted_iota(jnp.int32, (w, 2 * w), 0)
        col = lax.broadcasted_iota(jnp.int32, (w, 2 * w), 1)
        band = (col > row) & (col <= row + w)
        first_half = _head_mask((w, LANES), 0)
        for u in range(nw):
            valid = band if u > 0 else band & ((col >= w) | (i > 0))
            rows, keys = slice(u * w, (u + 1) * w), slice(u * w, (u + 2) * w)
            for pr in range(Q_PER_KVPAIR // LANES):
                gh = pr // 2
                cols = slice(pr * LANES, (pr + 1) * LANES)
                q2 = q_ref[rows, cols].astype(F32) * SCALE
                outs, lses = [], []
                for hh in range(2):
                    kk = (k2 if hh == gh else ksw)[keys, :]
                    vv = (v2 if hh == gh else vsw)[keys, :]
                    qm = jnp.where(_head_mask(q2.shape, hh), q2, 0.0).astype(BF16)
                    sc = jnp.where(valid, lax.dot_general(qm, kk, _DN_NT, preferred_element_type=F32), NEG_INF)
                    sink = s_ref[:, pr * LANES + HEAD_DIM * hh: pr * LANES + HEAD_DIM * hh + 1]
                    m = jnp.maximum(jnp.max(sc, axis=1, keepdims=True), sink)
                    e = jnp.exp(sc - m)
                    l = jnp.sum(e, axis=1, keepdims=True) + jnp.exp(sink - m)
                    outs.append(jnp.dot(e.astype(BF16), vv, preferred_element_type=F32) / l)
                    lses.append(m + jnp.log(l))
                o_ref[rows, cols] = jnp.where(first_half, outs[0], outs[1]).astype(BF16)
                lse_ref[rows, cols] = jnp.where(first_half, lses[0], lses[1])

    return pl.pallas_call(
        body, name="swa_fwd", grid=(KV_PAIRS, s // ts),
        out_shape=(jax.ShapeDtypeStruct((s, Q_W), BF16), jax.ShapeDtypeStruct((s, Q_W), F32)),
        in_specs=[qcols, kprev, kcur, vprev, vcur, vec], out_specs=(qcols, qcols),
    )(qk, qk, qk, vsrc, vsrc, sink_row)


def _swa_bwd_dq(qk, vsrc, sink_row, do, o, lse):
    s = qk.shape[0]
    w = WINDOW
    ts, nw, qcols, kprev, kcur, vprev, vcur, vec = _swa_specs(s)

    def body(q_ref, kp_ref, kc_ref, vp_ref, vc_ref, s_ref, do_ref, o_ref, lse_ref, dq_ref, dsink_ref):
        i = pl.program_id(1)

        @pl.when(i == 0)
        def _():
            dsink_ref[...] = jnp.zeros_like(dsink_ref)

        k2 = jnp.concatenate([kp_ref[...], kc_ref[...]], axis=0)
        v2 = jnp.concatenate([vp_ref[...], vc_ref[...]], axis=0)
        ksw, vsw = _lane_swap(k2), _lane_swap(v2)
        row = lax.broadcasted_iota(jnp.int32, (w, 2 * w), 0)
        col = lax.broadcasted_iota(jnp.int32, (w, 2 * w), 1)
        band = (col > row) & (col <= row + w)
        first_half = _head_mask((w, LANES), 0)
        lane_all = lax.broadcasted_iota(jnp.int32, (1, Q_PER_KVPAIR), 1)
        dsink = jnp.zeros((1, Q_PER_KVPAIR), F32)
        for u in range(nw):
            valid = band if u > 0 else band & ((col >= w) | (i > 0))
            rows, keys = slice(u * w, (u + 1) * w), slice(u * w, (u + 2) * w)
            for pr in range(Q_PER_KVPAIR // LANES):
                gh = pr // 2
                cols = slice(pr * LANES, (pr + 1) * LANES)
                q2 = q_ref[rows, cols].astype(F32) * SCALE
                do2 = do_ref[rows, cols]
                prod = do2.astype(F32) * o_ref[rows, cols].astype(F32)
                lse2 = lse_ref[rows, cols]
                dqs = []
                for hh in range(2):
                    hm = _head_mask(q2.shape, hh)
                    lo = pr * LANES + HEAD_DIM * hh
                    kk = (k2 if hh == gh else ksw)[keys, :]
                    vv = (v2 if hh == gh else vsw)[keys, :]
                    qm = jnp.where(hm, q2, 0.0).astype(BF16)
                    sc = jnp.where(valid, lax.dot_general(qm, kk, _DN_NT, preferred_element_type=F32), NEG_INF)
                    lse_c = lse2[:, HEAD_DIM * hh: HEAD_DIM * hh + 1]
                    pm = jnp.exp(sc - lse_c)
                    doh = jnp.where(hm, do2, jnp.zeros_like(do2))
                    delta = jnp.sum(jnp.where(hm, prod, 0.0), axis=1, keepdims=True)
                    ds = pm * (lax.dot_general(doh, vv, _DN_NT, preferred_element_type=F32) - delta)
                    dqs.append(jnp.dot(ds.astype(BF16), kk, preferred_element_type=F32))
                    part = jnp.sum(-jnp.exp(s_ref[:, lo:lo + 1] - lse_c) * delta, axis=0, keepdims=True)
                    dsink = dsink + jnp.where((lane_all >= lo) & (lane_all < lo + HEAD_DIM), part, 0.0)
                dq_ref[rows, cols] = (jnp.where(first_half, dqs[0], dqs[1]) * SCALE).astype(BF16)
        dsink_ref[...] += dsink

    return pl.pallas_call(
        body, name="swa_bwd_dq", grid=(KV_PAIRS, s // ts),
        out_shape=(jax.ShapeDtypeStruct((s, Q_W), BF16), jax.ShapeDtypeStruct((1, Q_W), F32)),
        in_specs=[qcols, kprev, kcur, vprev, vcur, vec, qcols, qcols, qcols], out_specs=(qcols, vec),
        compiler_params=pltpu.CompilerParams(dimension_semantics=("parallel", "arbitrary")),
    )(qk, qk, qk, vsrc, vsrc, sink_row, do, o, lse)


def _swa_bwd_dkv(qk, vsrc, do, o, lse):
    s = qk.shape[0]
    w = WINDOW
    ts = min(SWA_ROWS, s)
    nw = ts // w
    nstep = s // ts
    last_window = s // w - 1

    def body(k_ref, v_ref, qc_ref, qn_ref, doc_ref, don_ref, oc_ref, on_ref, lc_ref, ln_ref, dk_ref, dv_ref):
        j = pl.program_id(1)
        cat = lambda a_ref, b_ref: jnp.concatenate([a_ref[...], b_ref[...]], axis=0)
        qcat, docat, ocat, lcat = cat(qc_ref, qn_ref), cat(doc_ref, don_ref), cat(oc_ref, on_ref), cat(lc_ref, ln_ref)
        qsw, dosw = _lane_swap(qcat), _lane_swap(docat)
        row = lax.broadcasted_iota(jnp.int32, (2 * w, w), 0)
        col = lax.broadcasted_iota(jnp.int32, (2 * w, w), 1)
        band = (col <= row) & (row < col + w)
        has_next = j + 1 < nstep
        for wi in range(nw):
            valid = band if wi < nw - 1 else band & ((row < w) | has_next)
            keys, qrows = slice(wi * w, (wi + 1) * w), slice(wi * w, (wi + 2) * w)
            kw, vw = k_ref[keys, :], v_ref[keys, :]
            dk = jnp.zeros((w, LANES), F32)
            dv = jnp.zeros((w, LANES), F32)
            for pr in range(Q_PER_KVPAIR // LANES):
                gh = pr // 2
                cols = slice(pr * LANES, (pr + 1) * LANES)
                prod = docat[qrows, cols].astype(F32) * ocat[qrows, cols].astype(F32)
                lse2 = lcat[qrows, cols]
                to_kv = _head_mask(prod.shape, gh)
                for hh in range(2):
                    q_src, do_src = (qcat, docat) if hh == gh else (qsw, dosw)
                    q_al = jnp.where(to_kv, q_src[qrows, cols].astype(F32) * SCALE, 0.0).astype(BF16)
                    do_al = jnp.where(to_kv, do_src[qrows, cols], jnp.zeros((2 * w, LANES), BF16))
                    sc = jnp.where(valid, lax.dot_general(q_al, kw, _DN_NT, preferred_element_type=F32), NEG_INF)
                    pm = jnp.exp(sc - lse2[:, HEAD_DIM * hh: HEAD_DIM * hh + 1])
                    delta = jnp.sum(jnp.where(_head_mask(prod.shape, hh), prod, 0.0), axis=1, keepdims=True)
                    ds = pm * (lax.dot_general(do_al, vw, _DN_NT, preferred_element_type=F32) - delta)
                    dk = dk + lax.dot_general(ds.astype(BF16), q_al, _DN_TN, preferred_element_type=F32)
                    dv = dv + lax.dot_general(pm.astype(BF16), do_al, _DN_TN, preferred_element_type=F32)
            dk_ref[keys, :] = dk.astype(BF16)
            dv_ref[keys, :] = dv.astype(BF16)

    kbase, vbase = Q_W // LANES, (Q_W + KV_W) // LANES
    kv = lambda base: pl.BlockSpec((ts, LANES), lambda kp, j: (j, base + kp))
    same = pl.BlockSpec((ts, Q_PER_KVPAIR), lambda kp, j: (j, kp))
    nxt = pl.BlockSpec((w, Q_PER_KVPAIR), lambda kp, j: (jnp.minimum((j + 1) * nw, last_window), kp))
    out = pl.BlockSpec((ts, LANES), lambda kp, j: (j, kp))
    shp = jax.ShapeDtypeStruct((s, KV_W), BF16)
    return pl.pallas_call(
        body, name="swa_bwd_dkv", grid=(KV_PAIRS, nstep),
        out_shape=(shp, shp),
        in_specs=[kv(kbase), kv(vbase), same, nxt, same, nxt, same, nxt, same, nxt], out_specs=(out, out),
    )(qk, vsrc, qk, qk, do, do, o, o, lse, lse)


def _ffn_ple_fwd(h1, p_l, g_ffn, g_ple, w_gu, w_down, w_pg, w_pp, tag):
    hn2 = _rms_fwd(h1, g_ffn, f"rms_ffn_{tag}")
    ab, u = _gate_up_swiglu(hn2, w_gu, f"mm_gate_up_swiglu_{tag}")
    h2 = _mm(u, w_down, out_dtype=F32, res=h1, name=f"mm_down_{tag}")
    hn3 = _rms_fwd(h2, g_ple, f"rms_ple_{tag}")
    gl = _mm(hn3, w_pg, name=f"mm_ple_gate_{tag}")
    pp = _mm(p_l, w_pp, name=f"mm_ple_proj_{tag}")
    h3 = _ple_fwd(h2, gl, pp, f"ple_{tag}")
    return h3, dict(h1=h1, hn2=hn2, ab=ab, u=u, h2=h2, hn3=hn3, gl=gl, pp=pp)


def _ffn_ple_bwd(dh3, sv, p_l, g_ffn, g_ple, w_gu, w_down, w_pg, tag):
    dgl, dpp = _ple_bwd(dh3, sv["gl"], sv["pp"], f"ple_bwd_{tag}")
    d_wpp = _mm(p_l, dpp, ta=True, out_dtype=F32, name=f"mm_dw_ple_proj_{tag}")
    d_wpg = _mm(sv["hn3"], dgl, ta=True, out_dtype=F32, name=f"mm_dw_ple_gate_{tag}")
    dhn3 = _mm(dgl, w_pg, tb=True, out_dtype=F32, name=f"mm_dx_ple_gate_{tag}")
    dh2, dh2b, dg_ple = _rms_bwd(sv["h2"], g_ple, [dhn3], dh3, f"rms_ple_bwd_{tag}", True)
    d_wdown = _mm(sv["u"], dh2b, ta=True, out_dtype=F32, name=f"mm_dw_down_{tag}")
    dab = _down_t_swiglu_bwd(dh2b, w_down, sv["ab"], f"mm_dx_down_swiglu_bwd_{tag}")
    d_wgu = _mm(sv["hn2"], dab, ta=True, out_dtype=F32, group="out", group_width=2 * HID_PAD,
                name=f"mm_dw_gate_up_{tag}")
    dhn2 = _mm(dab, w_gu, tb=True, out_dtype=F32, group="k", name=f"mm_dx_gate_up_{tag}")
    dh1, dh1b, dg_ffn = _rms_bwd(sv["h1"], g_ffn, [dhn2], dh2, f"rms_ffn_bwd_{tag}", True)
    return dh1, dh1b, dict(d_wpp=d_wpp, d_wpg=d_wpg, d_wdown=d_wdown, d_wgu=d_wgu, dg_ple=dg_ple, dg_ffn=dg_ffn)


def _row_form(cum, blk):
    s = cum.shape[0]
    return cum[:, :N_FOX].T.reshape(N_FOX, s // blk, 1, blk)


def _col_form(cum):
    s = cum.shape[0]
    return jnp.broadcast_to(cum[:, :N_FOX].T[:, :, None], (N_FOX, s, LANES))


def kernel(x, p, positions, norm_mix, norm_ffn, norm_ple, norm_final, ev_w_in, ev_b_f, ev_w_out, od_w_in, od_sinks, od_w_out, ffn_w_gate, ffn_w_up, ffn_w_down, ple_w_proj, ple_w_gate, loss_target, m_norm_mix, m_norm_ffn, m_norm_ple, m_norm_final, m_ev_w_in, m_ev_b_f, m_ev_w_out, m_od_w_in, m_od_sinks, m_od_w_out, m_ffn_w_gate, m_ffn_w_up, m_ffn_w_down, m_ple_w_proj, m_ple_w_gate, v_norm_mix, v_norm_ffn, v_norm_ple, v_norm_final, v_ev_w_in, v_ev_b_f, v_ev_w_out, v_od_w_in, v_od_sinks, v_od_w_out, v_ffn_w_gate, v_ffn_w_up, v_ffn_w_down, v_ple_w_proj, v_ple_w_gate):
    s = x.shape[1]
    blk = min(ATTN_BLOCK, s)
    big_w = [ev_w_in, ev_w_out, od_w_in, od_w_out, ffn_w_gate, ffn_w_up, ffn_w_down, ple_w_proj, ple_w_gate]
    big_m = [m_ev_w_in, m_ev_w_out, m_od_w_in, m_od_w_out, m_ffn_w_gate, m_ffn_w_up, m_ffn_w_down, m_ple_w_proj, m_ple_w_gate]
    big_v = [v_ev_w_in, v_ev_w_out, v_od_w_in, v_od_w_out, v_ffn_w_gate, v_ffn_w_up, v_ffn_w_down, v_ple_w_proj, v_ple_w_gate]

    s_rows, s_gu, s_evin, s_odin, s_pp = _group_shards(*big_w, BF16)
    (g_evin,) = _all_gather_weights([s_evin])
    d = D_MODEL
    w_in0 = _from_owner_cols(g_evin)
    w_qkv = w_in0[:, :QKV_W]
    w_f = jnp.pad(w_in0[:, QKV_W:], ((0, 0), (0, LANES - N_FOX)))

    h0 = x[0]
    target = loss_target[0]
    p_b = [p[l, 0].astype(BF16) for l in range(2)]
    g_mix = [norm_mix[l][None, :] for l in range(2)]
    g_ffn = [norm_ffn[l][None, :] for l in range(2)]
    g_ple = [norm_ple[l][None, :] for l in range(2)]
    b_f = jnp.pad(ev_b_f, ((0, 0), (0, LANES - N_FOX)))

    half = HEAD_DIM // 2
    inv = ROPE_THETA ** (-jnp.arange(half, dtype=F32) / half)
    ang = positions[0].astype(F32)[:, None] * inv
    cos_t = jnp.tile(jnp.cos(ang), (1, 4))
    sin_t = jnp.tile(jnp.concatenate([-jnp.sin(ang), jnp.sin(ang)], axis=1), (1, 2))

    hn1 = _rms_fwd(h0, g_mix[0], "rms_mix_0")
    proj0 = _mm(hn1, w_qkv, name="mm_in_0")
    flog = _mm(hn1, w_f, out_dtype=F32, name="mm_fgate_0")
    cum = _fgate_fwd(flog, b_f)
    crow = _row_form(cum, blk)
    o_fox, acol, (g_rows, g_gu, g_odin, g_pp) = _fox_fwd(proj0, _col_form(cum), crow, [s_rows, s_gu, s_odin, s_pp])
    w_oi = _from_owner_cols(g_odin)
    w_eo = g_rows[:, 0:128].reshape(d, d)
    w_oo = g_rows[:, 128:256].reshape(d, d)
    w_down = [g_rows[:, 256 + HID_PAD * l: 256 + HID_PAD * (l + 1)].reshape(D_FF_PAD, d) for l in range(2)]
    w_pg = [g_rows[:, 1024 + 128 * l: 1152 + 128 * l].reshape(d, d) for l in range(2)]
    w_gu = [g_gu[:, d * l: d * (l + 1)] for l in range(2)]
    w_pp = [_from_owner_cols(g_pp[:, PLE_DIM * l: PLE_DIM * (l + 1)]) for l in range(2)]
    o_sb, rsave = _sb_fwd(proj0)
    o0 = jnp.concatenate([o_fox, o_sb], axis=1)
    h1 = _mm(o0, w_eo, out_dtype=F32, res=h0, name="mm_out_0")
    h3, sv0 = _ffn_ple_fwd(h1, p_b[0], g_ffn[0], g_ple[0], w_gu[0], w_down[0], w_pg[0], w_pp[0], "0")

    hn1b = _rms_fwd(h3, g_mix[1], "rms_mix_1")
    proj1 = _mm(hn1b, w_oi, name="mm_in_1")
    qk_r = _rope(proj1, Q_W + KV_W, cos_t, sin_t, 1.0, "rope_fwd")
    sink_row = jnp.repeat(od_sinks[0], HEAD_DIM)[None, :]
    o1, lse1 = _swa_fwd(qk_r, proj1, sink_row)
    h4 = _mm(o1, w_oo, out_dtype=F32, res=h3, name="mm_out_1")
    h6, sv1 = _ffn_ple_fwd(h4, p_b[1], g_ffn[1], g_ple[1], w_gu[1], w_down[1], w_pg[1], w_pp[1], "1")

    loss_part, dh6, dg_final = _final_norm_loss(h6, norm_final[None, :], target)

    dh4, dh4b, gr1 = _ffn_ple_bwd(dh6, sv1, p_b[1], g_ffn[1], g_ple[1], w_gu[1], w_down[1], w_pg[1], "1")
    do1 = _mm(dh4b, w_oo, tb=True, name="mm_dx_out_1")
    d_woo = _mm(o1, dh4b, ta=True, out_dtype=F32, name="mm_dw_out_1")
    dq1, dsink_row = _swa_bwd_dq(qk_r, proj1, sink_row, do1, o1, lse1)
    dk1, dv1 = _swa_bwd_dkv(qk_r, proj1, do1, o1, lse1)
    dqk = _rope(jnp.concatenate([dq1, dk1], axis=1), Q_W + KV_W, cos_t, sin_t, -1.0, "rope_bwd")
    dproj1 = jnp.concatenate([dqk, dv1], axis=1)
    d_woi = _mm(hn1b, dproj1, ta=True, out_dtype=F32, name="mm_dw_in_1")
    dhn1b = _mm(dproj1, w_oi, tb=True, out_dtype=F32, name="mm_dx_in_1")
    dh3, dg_mix1 = _rms_bwd(h3, g_mix[1], [dhn1b], dh4, "rms_mix_bwd_1", False)

    dh1, dh1b, gr0 = _ffn_ple_bwd(dh3, sv0, p_b[0], g_ffn[0], g_ple[0], w_gu[0], w_down[0], w_pg[0], "0")
    do0 = _mm(dh1b, w_eo, tb=True, name="mm_dx_out_0")
    d_weo = _mm(o0, dh1b, ta=True, out_dtype=F32, name="mm_dw_out_0")
    dq_f, dk_f, dv_f, gc = _fox_bwd(proj0, do0, o0, acol, crow)
    dq_s, dk_s, dv_s = _sb_bwd(proj0, do0, rsave)
    dproj0 = jnp.concatenate([dq_f, dk_f, dv_f, dq_s, dk_s, dv_s], axis=1)
    gcum = jnp.pad(gc.reshape(N_FOX, s).T, ((0, 0), (0, LANES - N_FOX)))
    dflog, db_f = _fgate_bwd(gcum, flog, b_f)
    d_wqkv = _mm(hn1, dproj0, ta=True, out_dtype=F32, name="mm_dw_in_0")
    d_wf = _mm(hn1, dflog, ta=True, out_dtype=F32, name="mm_dw_fgate_0")
    dhn1 = _mm(dproj0, w_qkv, tb=True, out_dtype=F32, name="mm_dx_in_0")
    dhn1f = _mm(dflog, w_f, tb=True, out_dtype=F32, name="mm_dx_fgate_0")
    grad_x, dg_mix0 = _rms_bwd(h0, g_mix[0], [dhn1, dhn1f], dh1, "rms_mix_bwd_0", False)

    grs = (gr0, gr1)
    by_rows = lambda g, r: g.reshape(N_DEV, r, d)
    grad_groups = [
        jnp.concatenate([by_rows(d_weo, 128), by_rows(d_woo, 128), by_rows(gr0["d_wdown"], HID_PAD),
                         by_rows(gr1["d_wdown"], HID_PAD), by_rows(gr0["d_wpg"], 128), by_rows(gr1["d_wpg"], 128)],
                        axis=1),
        jnp.concatenate([gr0["d_wgu"], gr1["d_wgu"]], axis=1),
        _by_owner_cols(jnp.concatenate([d_wqkv, d_wf[:, :N_FOX]], axis=1)),
        _by_owner_cols(d_woi),
        jnp.concatenate([_by_owner_cols(gr0["d_wpp"]), _by_owner_cols(gr1["d_wpp"])], axis=1),
    ]
    core = lax.axis_index("c").astype(jnp.int32).reshape(1)
    from_sibling = _rs_sibling_exchange(grad_groups)
    tags = ("rows", "gu", "ev_in", "od_in", "pp")
    chip_part = [_rs_chip_sum(core, g, r, f"rs_chip_sum_{t}") for g, r, t in zip(grad_groups, from_sibling, tags)]
    chip_recv = _rs_chip_exchange(chip_part)
    w_grp, m_grp, v_grp = (_group_shards(*ws, F32) for ws in (big_w, big_m, big_v))
    updated = [_rs_sum_adamw(r, w_, m_, v_, f"rs_sum_adamw_{t}")
               for r, w_, m_, v_, t in zip(chip_recv, w_grp, m_grp, v_grp, tags)]
    big_g, big_d, big_nm, big_nv = (_ungroup_shards([u[k] for u in updated]) for k in range(4))

    def small_pack(nmix, nffn, nple, nfin, bf, sk, extra):
        last = jnp.concatenate([bf.reshape(-1), sk.reshape(-1), extra.reshape(-1)])
        last = jnp.pad(last, (0, D_MODEL - last.shape[0]))
        return jnp.concatenate([nmix, nffn, nple, nfin.reshape(1, -1), last[None, :]], axis=0)

    small_g = small_pack(jnp.concatenate([dg_mix0, dg_mix1]), jnp.concatenate([gr0["dg_ffn"], gr1["dg_ffn"]]),
                         jnp.concatenate([gr0["dg_ple"], gr1["dg_ple"]]), dg_final,
                         db_f[0, :N_FOX], dsink_row[0, ::HEAD_DIM], loss_part[0, :1])
    zero1 = jnp.zeros((1,), F32)
    small_w = small_pack(norm_mix, norm_ffn, norm_ple, norm_final, ev_b_f, od_sinks, zero1)
    small_m = small_pack(m_norm_mix, m_norm_ffn, m_norm_ple, m_norm_final, m_ev_b_f, m_od_sinks, zero1)
    small_v = small_pack(v_norm_mix, v_norm_ffn, v_norm_ple, v_norm_final, v_ev_b_f, v_od_sinks, zero1)
    sg, sd, sm, sv_ = _small_allreduce_adamw(small_g, small_w, small_m, small_v)

    def small_unpack(t):
        return [t[0:2], t[2:4], t[4:6], t[6], t[7, :N_FOX][None, :], t[7, N_FOX:N_FOX + N_Q][None, :]]

    loss = sg[7, N_FOX + N_Q]

    def ordered(small, big):
        nm, nf, npl, nfin, bf, sk = small_unpack(small)
        ev_in, ev_out, od_in, od_out, fg, fu, fd, pproj, pgate = big
        return [nm, nf, npl, nfin, ev_in, bf, ev_out, od_in, sk, od_out, fg, fu, fd, pproj, pgate]

    return (loss, grad_x[None], *ordered(sg, big_g), *ordered(sd, big_d),
            *ordered(sm, big_nm), *ordered(sv_, big_nv))
```

```python
import functools

import jax
import jax.numpy as jnp
from jax import lax
from jax.experimental import pallas as pl
from jax.experimental.pallas import tpu as pltpu

F32 = jnp.float32
BF16 = jnp.bfloat16

D_MODEL = 1024
HEAD_DIM = 64
N_FOX = 8
N_SB = 8
FOX_W = N_FOX * HEAD_DIM
SB_W = N_SB * HEAD_DIM
QKV_W = 3 * FOX_W + 3 * SB_W
EVEN_IN = QKV_W + N_FOX
N_Q = 16
N_KV = 4
ODD_IN = N_Q * HEAD_DIM + 2 * N_KV * HEAD_DIM
WINDOW = 128
ROPE_THETA = 10000.0
D_FF = 2816
PLE_DIM = 256
EPS = 1e-6
NEG_INF = -1e30
SCALE = HEAD_DIM ** -0.5

ADAM_LR = 0.001
ADAM_B1 = 0.9
ADAM_B2 = 0.999
ADAM_EPS = 1e-08
ADAM_WD = 0.01
ADAM_STEP = 10

N_DEV = 8
LANES = 128
ROW_TILE = 256
ATTN_BLOCK = 256
HID_SHARD = D_FF // N_DEV
HID_PAD = 384
D_FF_PAD = N_DEV * HID_PAD

MESH = pl.DeviceIdType.MESH


def _pick(n, prefs):
    for t in prefs:
        if n % t == 0:
            return t
    return n


def _pad_to(a, axis, size):
    pad = [(0, 0)] * a.ndim
    pad[axis] = (0, size - a.shape[axis])
    return jnp.pad(a, pad)


def _group_shards(ev_in, ev_out, od_in, od_out, gate, up, down, pproj, pgate, dtype):
    rows = jnp.concatenate([ev_out[0], od_out[0], _pad_to(down[0], 0, HID_PAD), _pad_to(down[1], 0, HID_PAD),
                            pgate[0], pgate[1]], axis=0)
    gu = jnp.concatenate([jnp.concatenate([_pad_to(gate[l], 1, HID_PAD), _pad_to(up[l], 1, HID_PAD)], axis=1)
                          for l in range(2)], axis=0)
    groups = [rows, gu, ev_in[0], od_in[0], pproj.reshape(-1, pproj.shape[-1])]
    return [g.astype(dtype) for g in groups]


def _ungroup_shards(groups):
    rows, gu, ev_in, od_in, pp = groups
    d = D_MODEL
    down = jnp.stack([rows[256 + HID_PAD * l: 256 + HID_PAD * l + HID_SHARD] for l in range(2)])
    pgate = jnp.stack([rows[1024 + 128 * l: 1152 + 128 * l] for l in range(2)])
    gate = jnp.stack([gu[d * l: d * (l + 1), :HID_SHARD] for l in range(2)])
    up = jnp.stack([gu[d * l: d * (l + 1), HID_PAD:HID_PAD + HID_SHARD] for l in range(2)])
    return [ev_in[None], rows[None, 0:128], od_in[None], rows[None, 128:256], gate, up, down,
            pp.reshape(2, PLE_DIM, -1), pgate]


def _by_owner_cols(full):
    r, c8 = full.shape
    return full.reshape(r, N_DEV, c8 // N_DEV).transpose(1, 0, 2)


def _from_owner_cols(g):
    n, r, c = g.shape
    return g.transpose(1, 0, 2).reshape(r, n * c)


_ANY = pl.BlockSpec(memory_space=pl.ANY)


def _all_gather_steps(x_refs, out_refs, send_sems, recv_sems, local_sems):
    n = len(x_refs)
    x, y, c = lax.axis_index("x"), lax.axis_index("y"), lax.axis_index("c")
    me, sibling = (x, y, c), (x, y, 1 - c)
    chips = [(1 - x, y), (x, 1 - y), (1 - x, 1 - y)]

    def copy(a, k, block, to, from_input=False):
        px, py, pc = block
        slot = out_refs[a].at[4 * px + 2 * py + pc]
        return pltpu.make_async_remote_copy(
            src_ref=x_refs[a] if from_input else slot, dst_ref=slot,
            send_sem=send_sems.at[7 * a + k], recv_sem=recv_sems.at[7 * a + k],
            device_id=to, device_id_type=MESH)

    def mine():
        return [pltpu.make_async_copy(x_refs[a], out_refs[a].at[4 * x + 2 * y + c], local_sems.at[a]) for a in range(n)]

    def first():
        out = []
        for a in range(n):
            out.append(copy(a, 0, me, sibling, True))
            out += [copy(a, 1 + j, me, (*chip, c), True) for j, chip in enumerate(chips)]
        return out

    def issue():
        for cp in mine() + first():
            cp.start()

    def complete():
        passed = []
        for j, chip in enumerate(chips):
            for a in range(n):
                copy(a, 1 + j, (*chip, c), me).wait_recv()
                passed.append(copy(a, 4 + j, (*chip, c), sibling))
                passed[-1].start()
        for a in range(n):
            copy(a, 0, sibling, me).wait_recv()
            for j, chip in enumerate(chips):
                copy(a, 4 + j, (*chip, 1 - c), me).wait_recv()
        for cp in first() + passed:
            cp.wait_send()
        for cp in mine():
            cp.wait()

    return issue, complete


def _all_gather_scratch(n):
    return [pltpu.SemaphoreType.DMA((7 * n,)), pltpu.SemaphoreType.DMA((7 * n,)), pltpu.SemaphoreType.DMA((n,))]


def _gathered_shapes(shards):
    return tuple(jax.ShapeDtypeStruct((N_DEV,) + s.shape, s.dtype) for s in shards)


def _all_gather_weights(shards):
    n = len(shards)

    def body(*refs):
        issue, complete = _all_gather_steps(refs[:n], refs[n:2 * n], *refs[2 * n:])
        issue()
        complete()

    return pl.pallas_call(
        body, name="ag_weights", out_shape=_gathered_shapes(shards),
        in_specs=[_ANY] * n, out_specs=tuple([_ANY] * n), scratch_shapes=_all_gather_scratch(n),
    )(*shards)


def _sibling_exchange_steps(g_refs, out_refs, send_sems, recv_sems):
    n = len(g_refs)
    x, y, c = lax.axis_index("x"), lax.axis_index("y"), lax.axis_index("c")

    def copies():
        return [pltpu.make_async_remote_copy(
            src_ref=g_refs[a].at[2 * k + (1 - c)], dst_ref=out_refs[a].at[k],
            send_sem=send_sems.at[4 * a + k], recv_sem=recv_sems.at[4 * a + k],
            device_id=(x, y, 1 - c), device_id_type=MESH) for a in range(n) for k in range(4)]

    def issue():
        for cp in copies():
            cp.start()

    def complete():
        for cp in copies():
            cp.wait_recv()
        for cp in copies():
            cp.wait_send()

    return issue, complete


def _sibling_exchange_scratch(n):
    return [pltpu.SemaphoreType.DMA((4 * n,)), pltpu.SemaphoreType.DMA((4 * n,))]


def _quarter_shapes(arrays):
    return tuple(jax.ShapeDtypeStruct((4,) + g.shape[1:], g.dtype) for g in arrays)


def _rs_sibling_exchange(gps):
    n = len(gps)

    def body(*refs):
        issue, complete = _sibling_exchange_steps(refs[:n], refs[n:2 * n], *refs[2 * n:])
        issue()
        complete()

    return pl.pallas_call(
        body, name="rs_sibling_exchange", out_shape=_quarter_shapes(gps),
        in_specs=[_ANY] * n, out_specs=tuple([_ANY] * n), scratch_shapes=_sibling_exchange_scratch(n),
    )(*gps)


def _rs_chip_sum(core, gp, recv, name):
    _, rows, cols = gp.shape
    tr = ROW_TILE

    def body(core_ref, a_ref, b_ref, o_ref):
        o_ref[...] = (a_ref[...] + b_ref[...]).astype(BF16)

    return pl.pallas_call(
        body, name=name,
        out_shape=jax.ShapeDtypeStruct((4, rows, cols), BF16),
        grid_spec=pltpu.PrefetchScalarGridSpec(
            num_scalar_prefetch=1, grid=(4, rows // tr),
            in_specs=[pl.BlockSpec((1, tr, cols), lambda k, r, cr: (2 * k + cr[0], r, 0)),
                      pl.BlockSpec((1, tr, cols), lambda k, r, cr: (k, r, 0))],
            out_specs=pl.BlockSpec((1, tr, cols), lambda k, r, cr: (k, r, 0))),
    )(core, gp, recv)


def _chip_exchange_steps(p_refs, out_refs, send_sems, recv_sems, local_sems):
    n = len(p_refs)
    x, y, c = lax.axis_index("x"), lax.axis_index("y"), lax.axis_index("c")
    my_chip = 2 * x + y

    def mine():
        return [pltpu.make_async_copy(p_refs[a].at[my_chip], out_refs[a].at[my_chip], local_sems.at[a])
                for a in range(n)]

    def copies():
        return [pltpu.make_async_remote_copy(
            src_ref=p_refs[a].at[2 * px + py], dst_ref=out_refs[a].at[my_chip],
            send_sem=send_sems.at[3 * a + j], recv_sem=recv_sems.at[3 * a + j],
            device_id=(px, py, c), device_id_type=MESH)
            for a in range(n) for j, (px, py) in enumerate([(1 - x, y), (x, 1 - y), (1 - x, 1 - y)])]

    def issue():
        for cp in mine() + copies():
            cp.start()

    def complete():
        for cp in copies():
            cp.wait_recv()
        for cp in copies():
            cp.wait_send()
        for cp in mine():
            cp.wait()

    return issue, complete


def _chip_exchange_scratch(n):
    return [pltpu.SemaphoreType.DMA((3 * n,)), pltpu.SemaphoreType.DMA((3 * n,)), pltpu.SemaphoreType.DMA((n,))]


def _same_shapes(arrays):
    return tuple(jax.ShapeDtypeStruct(a.shape, a.dtype) for a in arrays)


def _rs_chip_exchange(parts):
    n = len(parts)

    def body(*refs):
        issue, complete = _chip_exchange_steps(refs[:n], refs[n:2 * n], *refs[2 * n:])
        issue()
        complete()

    return pl.pallas_call(
        body, name="rs_chip_exchange", out_shape=_same_shapes(parts),
        in_specs=[_ANY] * n, out_specs=tuple([_ANY] * n), scratch_shapes=_chip_exchange_scratch(n),
    )(*parts)


def _adamw(w, g, m, v):
    m = ADAM_B1 * m + (1.0 - ADAM_B1) * g
    v = ADAM_B2 * v + (1.0 - ADAM_B2) * (g * g)
    m_hat = m / (1.0 - ADAM_B1 ** ADAM_STEP)
    v_hat = v / (1.0 - ADAM_B2 ** ADAM_STEP)
    delta = -ADAM_LR * (m_hat / (jnp.sqrt(v_hat) + ADAM_EPS) + ADAM_WD * w)
    return delta, m, v


def _rs_sum_adamw(recv, w, m, v, name):
    _, rows, cols = recv.shape
    tr = ROW_TILE

    def body(r_ref, w_ref, m_ref, v_ref, g_out, d_out, m_out, v_out):
        g = r_ref[0].astype(F32)
        for k in range(1, 4):
            g = g + r_ref[k].astype(F32)
        delta, m_new, v_new = _adamw(w_ref[...], g, m_ref[...], v_ref[...])
        g_out[...] = g
        d_out[...] = delta
        m_out[...] = m_new
        v_out[...] = v_new

    flat = pl.BlockSpec((tr, cols), lambda r: (r, 0))
    shp = jax.ShapeDtypeStruct((rows, cols), F32)
    return pl.pallas_call(
        body, name=name, grid=(rows // tr,),
        out_shape=(shp, shp, shp, shp),
        in_specs=[pl.BlockSpec((4, tr, cols), lambda r: (0, r, 0)), flat, flat, flat],
        out_specs=(flat, flat, flat, flat),
    )(recv, w, m, v)


def _small_allreduce_adamw(vec, w, m, v):
    rows, cols = vec.shape

    def body(x_ref, w_ref, m_ref, v_ref, g_out, d_out, m_out, v_out, gather, send_sems, recv_sems):
        x, y, c = lax.axis_index("x"), lax.axis_index("y"), lax.axis_index("c")
        me = 4 * x + 2 * y + c
        copies = []
        for d in range(1, N_DEV):
            dx, dy, dc = (d >> 2) & 1, (d >> 1) & 1, d & 1
            peer = (x ^ dx if dx else x, y ^ dy if dy else y, c ^ dc if dc else c)
            copies.append(pltpu.make_async_remote_copy(
                src_ref=x_ref, dst_ref=gather.at[me],
                send_sem=send_sems.at[d - 1], recv_sem=recv_sems.at[d - 1],
                device_id=peer, device_id_type=MESH))
        for cp in copies:
            cp.start()
        gather[me] = x_ref[...]
        for cp in copies:
            cp.wait_recv()
        for cp in copies:
            cp.wait_send()
        g = gather[0]
        for k in range(1, N_DEV):
            g = g + gather[k]
        delta, m_new, v_new = _adamw(w_ref[...], g, m_ref[...], v_ref[...])
        g_out[...] = g
        d_out[...] = delta
        m_out[...] = m_new
        v_out[...] = v_new

    vm = pl.BlockSpec(memory_space=pltpu.VMEM)
    shp = jax.ShapeDtypeStruct((rows, cols), F32)
    return pl.pallas_call(
        body, name="small_allreduce_adamw",
        out_shape=(shp, shp, shp, shp),
        in_specs=[vm, vm, vm, vm], out_specs=(vm, vm, vm, vm),
        scratch_shapes=[pltpu.VMEM((N_DEV, rows, cols), F32),
                        pltpu.SemaphoreType.DMA((N_DEV - 1,)), pltpu.SemaphoreType.DMA((N_DEV - 1,))],
    )(vec, w, m, v)


def _mm(a, b, *, ta=False, tb=False, out_dtype=BF16, res=None, name, group=None, group_width=None):
    if ta:
        kdim, m = a.shape
    else:
        m, kdim = a.shape
    if group == "n":
        assert not ta and not tb and res is None
        ng, kb, tn = b.shape
        n = ng * tn
    elif group == "k":
        assert tb and not ta and res is None
        ng, n, tk = b.shape
        kb = ng * tk
    elif tb:
        n, kb = b.shape
    else:
        kb, n = b.shape
    assert kdim == kb, (a.shape, b.shape, ta, tb, group)
    tm = _pick(m, (1024, 512, 256, 128))
    if group == "out":
        assert ta and not tb and res is None
        tn = group_width
    elif group != "n":
        tn = _pick(n, (1024, 1408, 768, 512, 256, 128))
    if group != "k":
        tk = _pick(kdim, (1024, 1408, 512, 256, 128))
    nk = kdim // tk
    dn = (((0 if ta else 1,), (1 if tb else 0,)), ((), ()))
    has_res = res is not None
    in_place = nk > 1 and out_dtype == F32
    use_acc = nk > 1 and not in_place

    def body(*refs):
        a_ref, b_ref = refs[:2]
        r_ref = refs[2] if has_res else None
        o_ref = refs[3] if has_res else refs[2]
        part = lax.dot_general(a_ref[...], b_ref[...], dn, preferred_element_type=F32)
        if nk == 1:
            if has_res:
                part = part + r_ref[...].astype(F32)
            o_ref[...] = part.astype(out_dtype)
            return
        k = pl.program_id(2)
        acc = o_ref if in_place else refs[-1]

        @pl.when(k == 0)
        def _():
            acc[...] = part + r_ref[...].astype(F32) if has_res else part

        @pl.when(k > 0)
        def _():
            acc[...] += part

        if use_acc:
            @pl.when(k == nk - 1)
            def _():
                o_ref[...] = acc[...].astype(out_dtype)

    a_spec = (pl.BlockSpec((tk, tm), lambda i, j, k: (k, i)) if ta
              else pl.BlockSpec((tm, tk), lambda i, j, k: (i, k)))
    if group == "n":
        b_spec = pl.BlockSpec((None, tk, tn), lambda i, j, k: (j, k, 0))
    elif group == "k":
        b_spec = pl.BlockSpec((None, tn, tk), lambda i, j, k: (k, j, 0))
    elif tb:
        b_spec = pl.BlockSpec((tn, tk), lambda i, j, k: (j, k))
    else:
        b_spec = pl.BlockSpec((tk, tn), lambda i, j, k: (k, j))
    if group == "out":
        o_spec = pl.BlockSpec((None, tm, tn), lambda i, j, k: (j, i, 0))
        out_shape = jax.ShapeDtypeStruct((n // tn, m, tn), out_dtype)
    else:
        o_spec = pl.BlockSpec((tm, tn), lambda i, j, k: (i, j))
        out_shape = jax.ShapeDtypeStruct((m, n), out_dtype)
    in_specs = [a_spec, b_spec] + ([o_spec] if has_res else [])
    args = (a, b) + ((res,) if has_res else ())
    return pl.pallas_call(
        body, name=name, grid=(m // tm, n // tn, nk),
        out_shape=out_shape,
        in_specs=in_specs, out_specs=o_spec,
        scratch_shapes=[pltpu.VMEM((tm, tn), F32)] if use_acc else [],
        compiler_params=pltpu.CompilerParams(dimension_semantics=("parallel", "parallel", "arbitrary")),
    )(*args)


def _row_tile(s):
    return _pick(s, (256, 128))


def _rms_fwd(h, g, name):
    s, d = h.shape
    ts = _row_tile(s)

    def body(h_ref, g_ref, o_ref):
        x = h_ref[...]
        r = lax.rsqrt(jnp.mean(x * x, axis=-1, keepdims=True) + EPS)
        o_ref[...] = ((x * r) * g_ref[...]).astype(BF16)

    return pl.pallas_call(
        body, name=name, grid=(s // ts,),
        out_shape=jax.ShapeDtypeStruct((s, d), BF16),
        in_specs=[pl.BlockSpec((ts, d), lambda i: (i, 0)), pl.BlockSpec((1, d), lambda i: (0, 0))],
        out_specs=pl.BlockSpec((ts, d), lambda i: (i, 0)),
    )(h, g)


def _rms_bwd(h, g, dhns, dres, name, want_bf16):
    s, d = h.shape
    ts = _row_tile(s)
    n_in = len(dhns)

    def body(*refs):
        h_ref, g_ref, r_ref = refs[:3]
        dy_refs = refs[3:3 + n_in]
        outs = refs[3 + n_in:]
        dh_ref, dg_ref = outs[0], outs[-1]
        i = pl.program_id(0)
        x = h_ref[...]
        dy = dy_refs[0][...].astype(F32)
        for extra in dy_refs[1:]:
            dy = dy + extra[...].astype(F32)
        r = lax.rsqrt(jnp.mean(x * x, axis=-1, keepdims=True) + EPS)
        xr = x * r
        u = dy * g_ref[...]
        dx = r * (u - xr * jnp.mean(xr * u, axis=-1, keepdims=True))
        dh = r_ref[...] + dx
        dh_ref[...] = dh
        if want_bf16:
            outs[1][...] = dh.astype(BF16)

        @pl.when(i == 0)
        def _():
            dg_ref[...] = jnp.zeros_like(dg_ref)

        dg_ref[...] += jnp.sum(dy * xr, axis=0, keepdims=True)

    row = pl.BlockSpec((ts, d), lambda i: (i, 0))
    vec = pl.BlockSpec((1, d), lambda i: (0, 0))
    out_shape = [jax.ShapeDtypeStruct((s, d), F32)]
    out_specs = [row]
    if want_bf16:
        out_shape.append(jax.ShapeDtypeStruct((s, d), BF16))
        out_specs.append(row)
    out_shape.append(jax.ShapeDtypeStruct((1, d), F32))
    out_specs.append(vec)
    return pl.pallas_call(
        body, name=name, grid=(s // ts,),
        out_shape=tuple(out_shape),
        in_specs=[row, vec, row] + [row] * n_in, out_specs=tuple(out_specs),
        compiler_params=pltpu.CompilerParams(dimension_semantics=("arbitrary",)),
    )(h, g, dres, *dhns)


def _sigmoid_parts(z):
    e = jnp.exp(-jnp.abs(z))
    r = 1.0 / (1.0 + e)
    er = e * r
    pos = z >= 0
    return jnp.where(pos, r, er), jnp.where(pos, er, r)


def _gate_up_swiglu(hn, w_gu, name):
    s, d = hn.shape
    f = HID_PAD
    tm = _pick(s, (1024, 512, 256, 128))

    def body(x_ref, w_ref, ab_ref, u_ref):
        ab = jnp.dot(x_ref[...], w_ref[...], preferred_element_type=F32).astype(BF16)
        ab_ref[...] = ab
        a = ab[:, :f].astype(F32)
        b = ab[:, f:].astype(F32)
        sg, _ = _sigmoid_parts(a)
        u_ref[...] = ((a * sg) * b).astype(BF16)

    return pl.pallas_call(
        body, name=name, grid=(s // tm, N_DEV),
        out_shape=(jax.ShapeDtypeStruct((s, 2 * D_FF_PAD), BF16), jax.ShapeDtypeStruct((s, D_FF_PAD), BF16)),
        in_specs=[pl.BlockSpec((tm, d), lambda i, j: (i, 0)), pl.BlockSpec((None, d, 2 * f), lambda i, j: (j, 0, 0))],
        out_specs=(pl.BlockSpec((tm, 2 * f), lambda i, j: (i, j)), pl.BlockSpec((tm, f), lambda i, j: (i, j))),
    )(hn, w_gu)


def _down_t_swiglu_bwd(dh, w_down, ab, name):
    s, d = dh.shape
    f = HID_PAD
    tm = _pick(s, (1024, 512, 256, 128))

    def body(x_ref, w_ref, ab_ref, o_ref):
        du = lax.dot_general(x_ref[...], w_ref[...], _DN_NT, preferred_element_type=F32)
        g = du.astype(BF16).astype(F32)
        a = ab_ref[:, :f].astype(F32)
        b = ab_ref[:, f:].astype(F32)
        sg, sgm = _sigmoid_parts(a)
        silu = a * sg
        o_ref[:, :f] = (g * b * (sg + silu * sgm)).astype(BF16)
        o_ref[:, f:] = (g * silu).astype(BF16)

    return pl.pallas_call(
        body, name=name, grid=(s // tm, N_DEV),
        out_shape=jax.ShapeDtypeStruct(ab.shape, BF16),
        in_specs=[pl.BlockSpec((tm, d), lambda i, j: (i, 0)), pl.BlockSpec((f, d), lambda i, j: (j, 0)),
                  pl.BlockSpec((tm, 2 * f), lambda i, j: (i, j))],
        out_specs=pl.BlockSpec((tm, 2 * f), lambda i, j: (i, j)),
    )(dh, w_down, ab)


def _ple_fwd(h, gl, pp, name):
    s, d = h.shape
    ts = _row_tile(s)

    def body(h_ref, gl_ref, pp_ref, o_ref):
        sg, _ = _sigmoid_parts(gl_ref[...].astype(F32))
        o_ref[...] = h_ref[...] + sg * pp_ref[...].astype(F32)

    row = pl.BlockSpec((ts, d), lambda i: (i, 0))
    return pl.pallas_call(
        body, name=name, grid=(s // ts,),
        out_shape=jax.ShapeDtypeStruct((s, d), F32),
        in_specs=[row, row, row], out_specs=row,
    )(h, gl, pp)


def _ple_bwd(dh, gl, pp, name):
    s, d = dh.shape
    ts = _row_tile(s)

    def body(dh_ref, gl_ref, pp_ref, dgl_ref, dpp_ref):
        g = dh_ref[...]
        sg, sgm = _sigmoid_parts(gl_ref[...].astype(F32))
        dpp_ref[...] = (g * sg).astype(BF16)
        dgl_ref[...] = (g * pp_ref[...].astype(F32) * (sg * sgm)).astype(BF16)

    row = pl.BlockSpec((ts, d), lambda i: (i, 0))
    shp = jax.ShapeDtypeStruct((s, d), BF16)
    return pl.pallas_call(
        body, name=name, grid=(s // ts,),
        out_shape=(shp, shp), in_specs=[row, row, row], out_specs=(row, row),
    )(dh, gl, pp)


def _final_norm_loss(h, g, target):
    s, d = h.shape
    ts = _row_tile(s)

    def body(h_ref, g_ref, t_ref, loss_ref, dh_ref, dg_ref):
        i = pl.program_id(0)
        x = h_ref[...]
        gain = g_ref[...]
        r = lax.rsqrt(jnp.mean(x * x, axis=-1, keepdims=True) + EPS)
        xr = x * r
        err = xr * gain - t_ref[...]
        dy = err * (1.0 / d)
        u = dy * gain
        dh_ref[...] = r * (u - xr * jnp.mean(xr * u, axis=-1, keepdims=True))

        @pl.when(i == 0)
        def _():
            dg_ref[...] = jnp.zeros_like(dg_ref)
            loss_ref[...] = jnp.zeros_like(loss_ref)

        dg_ref[...] += jnp.sum(dy * xr, axis=0, keepdims=True)
        tok = jnp.mean(err * err, axis=-1, keepdims=True)
        loss_ref[...] += 0.5 * jnp.sum(tok, axis=0, keepdims=True)

    row = pl.BlockSpec((ts, d), lambda i: (i, 0))
    vec = pl.BlockSpec((1, d), lambda i: (0, 0))
    return pl.pallas_call(
        body, name="final_norm_loss", grid=(s // ts,),
        out_shape=(jax.ShapeDtypeStruct((1, LANES), F32), jax.ShapeDtypeStruct((s, d), F32),
                   jax.ShapeDtypeStruct((1, d), F32)),
        in_specs=[row, vec, row],
        out_specs=(pl.BlockSpec((1, LANES), lambda i: (0, 0)), row, vec),
        compiler_params=pltpu.CompilerParams(dimension_semantics=("arbitrary",)),
    )(h, g, target)


def _rope(xin, w, cos, sin_signed, sign, name):
    s = xin.shape[0]
    ts = _pick(s, (512, 256, 128))

    def body(x_ref, c_ref, s_ref, o_ref):
        cos_b, sin_b = c_ref[...], s_ref[...]
        lane = lax.broadcasted_iota(jnp.int32, cos_b.shape, 1)
        low = (lane & (HEAD_DIM - 1)) < (HEAD_DIM // 2)
        for j in range(w // LANES):
            cols = slice(j * LANES, (j + 1) * LANES)
            x = x_ref[:, cols].astype(F32)
            swapped = jnp.where(low, pltpu.roll(x, LANES - HEAD_DIM // 2, 1), pltpu.roll(x, HEAD_DIM // 2, 1))
            o_ref[:, cols] = (x * cos_b + sign * (swapped * sin_b)).astype(BF16)

    blk = pl.BlockSpec((ts, w), lambda i: (i, 0))
    tab = pl.BlockSpec((ts, LANES), lambda i: (i, 0))
    return pl.pallas_call(
        body, name=name, grid=(s // ts,),
        out_shape=jax.ShapeDtypeStruct((s, w), BF16),
        in_specs=[blk, tab, tab], out_specs=blk,
    )(xin, cos, sin_signed)


def _fgate_fwd(flog, bias):
    s, w = flog.shape

    def body(x_ref, b_ref, o_ref):
        rowi = lax.broadcasted_iota(jnp.int32, (8, w), 0)
        b = b_ref[...]

        def step(g, carry):
            sl = pl.ds(pl.multiple_of(g * 8, 8), 8)
            x = x_ref[sl, :] + b
            lf = jnp.minimum(x, 0.0) - jnp.log1p(jnp.exp(-jnp.abs(x)))
            for sh in (1, 2, 4):
                lf = lf + jnp.where(rowi >= sh, pltpu.roll(lf, sh, 0), 0.0)
            out = lf + carry
            o_ref[sl, :] = out
            return jnp.broadcast_to(out[7:8, :], (8, w))

        lax.fori_loop(0, s // 8, step, jnp.zeros((8, w), F32))

    vm = pl.BlockSpec(memory_space=pltpu.VMEM)
    return pl.pallas_call(
        body, name="fgate_fwd", out_shape=jax.ShapeDtypeStruct((s, w), F32),
        in_specs=[vm, vm], out_specs=vm,
    )(flog, bias)


def _fgate_bwd(gcum, flog, bias):
    s, w = flog.shape

    def body(g_ref, x_ref, b_ref, o_ref, db_ref):
        rowi = lax.broadcasted_iota(jnp.int32, (8, w), 0)
        lane = lax.broadcasted_iota(jnp.int32, (8, w), 1)
        b = b_ref[...]

        def step(t, carry):
            run, dbsum = carry
            g = s // 8 - 1 - t
            sl = pl.ds(pl.multiple_of(g * 8, 8), 8)
            c = g_ref[sl, :]
            for sh in (1, 2, 4):
                c = c + jnp.where(rowi < 8 - sh, pltpu.roll(c, 8 - sh, 0), 0.0)
            c = c + run
            _, sgm = _sigmoid_parts(x_ref[sl, :] + b)
            dl = jnp.where(lane < N_FOX, c * sgm, 0.0)
            o_ref[sl, :] = dl.astype(BF16)
            return jnp.broadcast_to(c[0:1, :], (8, w)), dbsum + dl

        _, dbsum = lax.fori_loop(0, s // 8, step, (jnp.zeros((8, w), F32), jnp.zeros((8, w), F32)))
        db_ref[...] = jnp.sum(dbsum, axis=0, keepdims=True)

    vm = pl.BlockSpec(memory_space=pltpu.VMEM)
    return pl.pallas_call(
        body, name="fgate_bwd",
        out_shape=(jax.ShapeDtypeStruct((s, w), BF16), jax.ShapeDtypeStruct((1, w), F32)),
        in_specs=[vm, vm, vm], out_specs=(vm, vm),
    )(gcum, flog, bias)


_DN_NT = (((1,), (1,)), ((), ()))
_DN_TN = (((0,), (0,)), ((), ()))


def _head_mask(shape, hh):
    lane = lax.broadcasted_iota(jnp.int32, shape, 1)
    return (lane >= HEAD_DIM * hh) & (lane < HEAD_DIM * (hh + 1))


SKIP_BELOW = -110.0


def _sweep_left(i, carry, tile, go_on):
    def flag(j, c):
        return jnp.logical_and(j >= 0, go_on(jnp.maximum(j, 0), c)).astype(jnp.int32)

    def body(st):
        j, _, c = st
        c = tile(j, c)
        return j - 1, flag(j - 1, c), c

    return lax.while_loop(lambda st: st[1] > 0, body, (i - 1, flag(i - 1, carry), carry))[2]


def _key_norm_max(k_ref, kn_ref):
    k2 = k_ref[...].astype(F32)
    sq = k2 * k2
    for hh in range(2):
        n2 = jnp.sum(jnp.where(_head_mask(sq.shape, hh), sq, 0.0), axis=1, keepdims=True)
        kn_ref[hh] = jnp.broadcast_to(jnp.sqrt(jnp.max(n2, axis=0, keepdims=True)), kn_ref.shape[1:])


def _fox_fwd(proj, ccol, crow, gather):
    s = proj.shape[0]
    blk = min(ATTN_BLOCK, s)
    nq = s // blk
    npair = N_FOX // 2
    ng = len(gather)

    def body(*refs):
        q_ref, k_ref, v_ref, cc_ref, cr_ref = refs[:5]
        x_refs = refs[5:5 + ng]
        o_ref, a_ref = refs[5 + ng:7 + ng]
        g_refs = refs[7 + ng:7 + 2 * ng]
        kn_ref = refs[7 + 2 * ng]
        p_, i = pl.program_id(0), pl.program_id(1)
        issue, complete = _all_gather_steps(x_refs, g_refs, *refs[8 + 2 * ng:])

        @pl.when((p_ == 0) & (i == 0))
        def _():
            issue()

        @pl.when(i == 0)
        def _():
            _key_norm_max(k_ref, kn_ref)

        q2 = q_ref[...].astype(F32) * SCALE
        row = lax.broadcasted_iota(jnp.int32, (blk, blk), 0)
        col = lax.broadcasted_iota(jnp.int32, (blk, blk), 1)
        outs = []
        for hh in range(2):
            hm = _head_mask(q2.shape, hh)
            qh = jnp.where(hm, q2, 0.0).astype(BF16)
            ct = cc_ref[hh][:, 0:1]
            qk_max = jnp.sqrt(jnp.sum(jnp.where(hm, q2 * q2, 0.0), axis=1, keepdims=True)) * kn_ref[hh][0:1, 0:1]

            def go_on(j, carry, ct=ct, qk_max=qk_max, hh=hh):
                bias_max = ct - jnp.min(cr_ref[2 * p_ + hh, j], axis=1, keepdims=True)
                return jnp.max(qk_max + bias_max - carry[0]) > SKIP_BELOW

            def tile(j, carry, masked, qh=qh, ct=ct, hh=hh):
                m, l, acc = carry
                sl = pl.ds(pl.multiple_of(j * blk, blk), blk)
                kb, vb = k_ref[sl, :], v_ref[sl, :]
                sc = lax.dot_general(qh, kb, _DN_NT, preferred_element_type=F32)
                sc = sc + (ct - cr_ref[2 * p_ + hh, j])
                if masked:
                    sc = jnp.where(col <= row, sc, NEG_INF)
                m_new = jnp.maximum(m, jnp.max(sc, axis=1, keepdims=True))
                alpha = jnp.exp(m - m_new)
                pm = jnp.exp(sc - m_new)
                l = alpha * l + jnp.sum(pm, axis=1, keepdims=True)
                acc = alpha * acc + jnp.dot(pm.astype(BF16), vb, preferred_element_type=F32)
                return m_new, l, acc

            init = (jnp.full((blk, 1), NEG_INF, F32), jnp.zeros((blk, 1), F32), jnp.zeros((blk, LANES), F32))
            carry = tile(i, init, True)
            m, l, acc = _sweep_left(i, carry, lambda j, c, tile=tile: tile(j, c, False), go_on)
            outs.append(acc / l)
            a_ref[hh] = jnp.broadcast_to(ct - (m + jnp.log(l)), (blk, LANES))
        o_ref[...] = jnp.where(_head_mask(outs[0].shape, 0), outs[0], outs[1]).astype(BF16)

        @pl.when((p_ == npair - 1) & (i == nq - 1))
        def _():
            complete()

    seq = lambda base: pl.BlockSpec((s, LANES), lambda p, i: (0, base + p))
    res = pl.pallas_call(
        body, name="fox_fwd_ag", grid=(npair, nq),
        scratch_shapes=[pltpu.VMEM((2, 8, LANES), F32)] + _all_gather_scratch(ng),
        out_shape=(jax.ShapeDtypeStruct((s, FOX_W), BF16), jax.ShapeDtypeStruct((N_FOX, s, LANES), F32))
        + _gathered_shapes(gather),
        in_specs=[pl.BlockSpec((blk, LANES), lambda p, i: (i, p)), seq(npair), seq(2 * npair),
                  pl.BlockSpec((2, blk, LANES), lambda p, i: (p, i, 0)),
                  pl.BlockSpec((N_FOX, nq, 1, blk), lambda p, i: (0, 0, 0, 0))] + [_ANY] * ng,
        out_specs=(pl.BlockSpec((blk, LANES), lambda p, i: (i, p)),
                   pl.BlockSpec((2, blk, LANES), lambda p, i: (p, i, 0))) + tuple([_ANY] * ng),
        compiler_params=pltpu.CompilerParams(dimension_semantics=("arbitrary", "arbitrary")),
    )(proj, proj, proj, ccol, crow, *gather)
    return res[0], res[1], list(res[2:])


def _fox_bwd(proj, do, o, acol, crow, grads):
    s = proj.shape[0]
    blk = min(ATTN_BLOCK, s)
    nq = s // blk
    npair = N_FOX // 2
    ng = len(grads)

    def body(*refs):
        q_ref, k_ref, v_ref, do_ref, o_ref, a_ref, cr_ref = refs[:7]
        g_refs = refs[7:7 + ng]
        dq_ref, dk_ref, dv_ref, gc_ref = refs[7 + ng:11 + ng]
        x_refs = refs[11 + ng:11 + 2 * ng]
        dk_acc, dv_acc, kn_ref = refs[11 + 2 * ng:14 + 2 * ng]
        p_, i = pl.program_id(0), pl.program_id(1)
        issue, complete = _sibling_exchange_steps(g_refs, x_refs, *refs[14 + 2 * ng:])

        @pl.when((p_ == 0) & (i == 0))
        def _():
            issue()

        @pl.when(i == 0)
        def _():
            dk_acc[...] = jnp.zeros_like(dk_acc)
            dv_acc[...] = jnp.zeros_like(dv_acc)
            gc_ref[...] = jnp.zeros_like(gc_ref)
            _key_norm_max(k_ref, kn_ref)

        q2 = q_ref[...].astype(F32) * SCALE
        do2 = do_ref[...]
        prod = do2.astype(F32) * o_ref[...].astype(F32)
        row = lax.broadcasted_iota(jnp.int32, (blk, blk), 0)
        col = lax.broadcasted_iota(jnp.int32, (blk, blk), 1)
        dqs = []
        for hh in range(2):
            hm = _head_mask(q2.shape, hh)
            qh = jnp.where(hm, q2, 0.0).astype(BF16)
            doh = jnp.where(hm, do2, jnp.zeros_like(do2))
            delta = jnp.sum(jnp.where(hm, prod, 0.0), axis=1, keepdims=True)
            at = a_ref[hh][:, 0:1]
            qk_max = jnp.sqrt(jnp.sum(jnp.where(hm, q2 * q2, 0.0), axis=1, keepdims=True)) * kn_ref[hh][0:1, 0:1]

            def go_on(j, carry, at=at, qk_max=qk_max, hh=hh):
                bias_max = at - jnp.min(cr_ref[2 * p_ + hh, j], axis=1, keepdims=True)
                return jnp.max(qk_max + bias_max) > SKIP_BELOW

            def tile(j, carry, masked, qh=qh, doh=doh, delta=delta, at=at, hh=hh):
                dq, rs = carry
                sl = pl.ds(pl.multiple_of(j * blk, blk), blk)
                kb, vb = k_ref[sl, :], v_ref[sl, :]
                sc = lax.dot_general(qh, kb, _DN_NT, preferred_element_type=F32)
                sc = sc + (at - cr_ref[2 * p_ + hh, j])
                if masked:
                    sc = jnp.where(col <= row, sc, NEG_INF)
                pm = jnp.exp(sc)
                dp = lax.dot_general(doh, vb, _DN_NT, preferred_element_type=F32)
                ds = pm * (dp - delta)
                dsb = ds.astype(BF16)
                dk_acc[sl, :] += lax.dot_general(dsb, qh, _DN_TN, preferred_element_type=F32)
                dv_acc[sl, :] += lax.dot_general(pm.astype(BF16), doh, _DN_TN, preferred_element_type=F32)
                gc_ref[hh, j] += -jnp.sum(ds, axis=0, keepdims=True)
                return dq + jnp.dot(dsb, kb, preferred_element_type=F32), rs + jnp.sum(ds, axis=1, keepdims=True)

            carry = tile(i, (jnp.zeros((blk, LANES), F32), jnp.zeros((blk, 1), F32)), True)
            dq, rs = _sweep_left(i, carry, lambda j, c, tile=tile: tile(j, c, False), go_on)
            dqs.append(dq)
            gc_ref[hh, i] += jnp.transpose(jnp.broadcast_to(rs, (blk, LANES)))[0:1, :]
        dq_ref[...] = (jnp.where(_head_mask(dqs[0].shape, 0), dqs[0], dqs[1]) * SCALE).astype(BF16)

        @pl.when(i == nq - 1)
        def _():
            dk_ref[...] = dk_acc[...].astype(BF16)
            dv_ref[...] = dv_acc[...].astype(BF16)

        @pl.when((p_ == npair - 1) & (i == nq - 1))
        def _():
            complete()

    seq = lambda base: pl.BlockSpec((s, LANES), lambda p, i: (0, base + p))
    qblk = lambda base: pl.BlockSpec((blk, LANES), lambda p, i: (i, base + p))
    rep = pl.BlockSpec((2, blk, LANES), lambda p, i: (p, i, 0))
    half = jax.ShapeDtypeStruct((s, FOX_W), BF16)
    res = pl.pallas_call(
        body, name="fox_bwd_rs", grid=(npair, nq),
        out_shape=(half, half, half, jax.ShapeDtypeStruct((N_FOX, nq, 1, blk), F32)) + _quarter_shapes(grads),
        in_specs=[qblk(0), seq(npair), seq(2 * npair), qblk(0), qblk(0), rep,
                  pl.BlockSpec((N_FOX, nq, 1, blk), lambda p, i: (0, 0, 0, 0))] + [_ANY] * ng,
        out_specs=(qblk(0), seq(0), seq(0), pl.BlockSpec((2, nq, 1, blk), lambda p, i: (p, 0, 0, 0)))
        + tuple([_ANY] * ng),
        scratch_shapes=[pltpu.VMEM((s, LANES), F32), pltpu.VMEM((s, LANES), F32), pltpu.VMEM((2, 8, LANES), F32)]
        + _sibling_exchange_scratch(ng),
        compiler_params=pltpu.CompilerParams(dimension_semantics=("arbitrary", "arbitrary")),
    )(proj, proj, proj, do, o, acol, crow, *grads)
    return res[0], res[1], res[2], res[3], list(res[4:])


def _sb_logs(z):
    neg = -(jnp.maximum(z, 0.0) + jnp.log1p(jnp.exp(-jnp.abs(z))))
    return neg, z + neg


def _split_dot(x, tri):
    hi = x.astype(BF16)
    lo = (x - hi.astype(F32)).astype(BF16)
    return jnp.dot(hi, tri, preferred_element_type=F32) + jnp.dot(lo, tri, preferred_element_type=F32)


def _sb_fwd(proj):
    s = proj.shape[0]
    blk = min(ATTN_BLOCK, s)
    nq = s // blk
    npair = N_SB // 2
    base = 3 * (N_FOX // 2)

    def body(q_ref, k_ref, v_ref, o_ref, r_ref):
        i = pl.program_id(1)
        q2 = q_ref[...].astype(F32) * SCALE
        row = lax.broadcasted_iota(jnp.int32, (blk, blk), 0)
        col = lax.broadcasted_iota(jnp.int32, (blk, blk), 1)
        strict = col < row
        tri = jnp.where(row > col, 1.0, 0.0).astype(BF16)
        lane = lax.broadcasted_iota(jnp.int32, (blk, LANES), 1)
        outs = []
        for hh in range(2):
            qh = jnp.where(_head_mask(q2.shape, hh), q2, 0.0).astype(BF16)

            def tile(j, carry, masked, qh=qh):
                rsum, acc, rbuf = carry
                sl = pl.ds(pl.multiple_of(j * blk, blk), blk)
                kb, vb = k_ref[sl, :], v_ref[sl, :]
                z = lax.dot_general(qh, kb, _DN_NT, preferred_element_type=F32)
                l1m, lb = _sb_logs(z)
                if masked:
                    l1m = jnp.where(strict, l1m, 0.0)
                sx = _split_dot(l1m, tri)
                a = jnp.exp(lb + sx + rsum)
                if masked:
                    a = jnp.where(strict, a, 0.0)
                acc = acc + jnp.dot(a.astype(BF16), vb, preferred_element_type=F32)
                rbuf = jnp.where(lane == j, rsum, rbuf)
                return rsum + jnp.sum(l1m, axis=1, keepdims=True), acc, rbuf

            init = (jnp.zeros((blk, 1), F32), jnp.zeros((blk, LANES), F32), jnp.full((blk, LANES), NEG_INF, F32))
            carry = tile(i, init, True)
            _, acc, rbuf = _sweep_left(i, carry, lambda j, c, tile=tile: tile(j, c, False),
                                       lambda j, c: jnp.max(c[0]) > SKIP_BELOW)
            outs.append(acc)
            r_ref[hh] = rbuf
        o_ref[...] = jnp.where(_head_mask(outs[0].shape, 0), outs[0], outs[1]).astype(BF16)

    seq = lambda b: pl.BlockSpec((s, LANES), lambda p, i: (0, b + p))
    return pl.pallas_call(
        body, name="sb_fwd", grid=(npair, nq),
        out_shape=(jax.ShapeDtypeStruct((s, SB_W), BF16), jax.ShapeDtypeStruct((N_SB, s, LANES), F32)),
        in_specs=[pl.BlockSpec((blk, LANES), lambda p, i: (i, base + p)), seq(base + npair), seq(base + 2 * npair)],
        out_specs=(pl.BlockSpec((blk, LANES), lambda p, i: (i, p)),
                   pl.BlockSpec((2, blk, LANES), lambda p, i: (p, i, 0))),
        compiler_params=pltpu.CompilerParams(dimension_semantics=("parallel", "arbitrary")),
    )(proj, proj, proj)


def _sb_bwd(proj, do, rsave, parts):
    s = proj.shape[0]
    blk = min(ATTN_BLOCK, s)
    nq = s // blk
    npair = N_SB // 2
    base = 3 * (N_FOX // 2)
    ng = len(parts)

    def body(*refs):
        q_ref, k_ref, v_ref, do_ref, r_ref = refs[:5]
        p_refs = refs[5:5 + ng]
        dq_ref, dk_ref, dv_ref = refs[5 + ng:8 + ng]
        x_refs = refs[8 + ng:8 + 2 * ng]
        dk_acc, dv_acc = refs[8 + 2 * ng:10 + 2 * ng]
        p_, i = pl.program_id(0), pl.program_id(1)
        issue, complete = _chip_exchange_steps(p_refs, x_refs, *refs[10 + 2 * ng:])

        @pl.when((p_ == 0) & (i == 0))
        def _():
            issue()

        @pl.when(i == 0)
        def _():
            dk_acc[...] = jnp.zeros_like(dk_acc)
            dv_acc[...] = jnp.zeros_like(dv_acc)

        q2 = q_ref[...].astype(F32) * SCALE
        do2 = do_ref[...]
        row = lax.broadcasted_iota(jnp.int32, (blk, blk), 0)
        col = lax.broadcasted_iota(jnp.int32, (blk, blk), 1)
        strict = col < row
        tri_suffix = jnp.where(row > col, 1.0, 0.0).astype(BF16)
        tri_prefix = jnp.where(row < col, 1.0, 0.0).astype(BF16)
        lane = lax.broadcasted_iota(jnp.int32, (blk, LANES), 1)
        dqs = []
        for hh in range(2):
            hm = _head_mask(q2.shape, hh)
            qh = jnp.where(hm, q2, 0.0).astype(BF16)
            doh = jnp.where(hm, do2, jnp.zeros_like(do2))
            rbuf = r_ref[hh]

            def tile(j, carry, masked, qh=qh, doh=doh, rbuf=rbuf):
                pre, dq = carry
                sl = pl.ds(pl.multiple_of(j * blk, blk), blk)
                kb, vb = k_ref[sl, :], v_ref[sl, :]
                z = lax.dot_general(qh, kb, _DN_NT, preferred_element_type=F32)
                l1m, lb = _sb_logs(z)
                beta, one_m_beta = _sigmoid_parts(z)
                if masked:
                    l1m = jnp.where(strict, l1m, 0.0)
                sx = _split_dot(l1m, tri_suffix)
                rj = jnp.sum(jnp.where(lane == j, rbuf, 0.0), axis=1, keepdims=True)
                a = jnp.exp(lb + sx + rj)
                if masked:
                    a = jnp.where(strict, a, 0.0)
                da = lax.dot_general(doh, vb, _DN_NT, preferred_element_type=F32)
                g = a * da
                px = _split_dot(g, tri_prefix) + pre
                dz = g * one_m_beta - beta * px
                if masked:
                    dz = jnp.where(strict, dz, 0.0)
                dzb = dz.astype(BF16)
                dk_acc[sl, :] += lax.dot_general(dzb, qh, _DN_TN, preferred_element_type=F32)
                dv_acc[sl, :] += lax.dot_general(a.astype(BF16), doh, _DN_TN, preferred_element_type=F32)
                return pre + jnp.sum(g, axis=1, keepdims=True), dq + jnp.dot(dzb, kb, preferred_element_type=F32)

            reach = jnp.max(rbuf, axis=0, keepdims=True)
            dead = (reach <= SKIP_BELOW) & (lane[0:1, :] <= i)
            first = jnp.sum(jnp.where(dead, 1.0, 0.0)).astype(jnp.int32)
            carry = (jnp.zeros((blk, 1), F32), jnp.zeros((blk, LANES), F32))
            carry = lax.fori_loop(first, i, lambda t, c, tile=tile: tile(t, c, False), carry)
            _, dq = tile(i, carry, True)
            dqs.append(dq)
        dq_ref[...] = (jnp.where(_head_mask(dqs[0].shape, 0), dqs[0], dqs[1]) * SCALE).astype(BF16)

        @pl.when(i == nq - 1)
        def _():
            dk_ref[...] = dk_acc[...].astype(BF16)
            dv_ref[...] = dv_acc[...].astype(BF16)

        @pl.when((p_ == npair - 1) & (i == nq - 1))
        def _():
            complete()

    seq = lambda b: pl.BlockSpec((s, LANES), lambda p, i: (0, b + p))
    qblk = lambda b: pl.BlockSpec((blk, LANES), lambda p, i: (i, b + p))
    half = jax.ShapeDtypeStruct((s, SB_W), BF16)
    res = pl.pallas_call(
        body, name="sb_bwd_rs", grid=(npair, nq),
        out_shape=(half, half, half) + _same_shapes(parts),
        in_specs=[qblk(base), seq(base + npair), seq(base + 2 * npair), qblk(npair),
                  pl.BlockSpec((2, blk, LANES), lambda p, i: (p, i, 0))] + [_ANY] * ng,
        out_specs=(qblk(0), seq(0), seq(0)) + tuple([_ANY] * ng),
        scratch_shapes=[pltpu.VMEM((s, LANES), F32), pltpu.VMEM((s, LANES), F32)] + _chip_exchange_scratch(ng),
        compiler_params=pltpu.CompilerParams(dimension_semantics=("arbitrary", "arbitrary")),
    )(proj, proj, proj, do, rsave, *parts)
    return res[0], res[1], res[2], list(res[3:])


SWA_ROWS = 256
Q_W = N_Q * HEAD_DIM
KV_W = N_KV * HEAD_DIM
KV_PAIRS = KV_W // LANES
Q_PER_KVPAIR = Q_W // KV_PAIRS


def _lane_swap(x):
    xf = x.astype(F32)
    parts = [pltpu.roll(xf[:, j * LANES:(j + 1) * LANES], HEAD_DIM, 1) for j in range(x.shape[1] // LANES)]
    return (parts[0] if len(parts) == 1 else jnp.concatenate(parts, axis=1)).astype(x.dtype)


def _swa_specs(s):
    w = WINDOW
    ts = min(SWA_ROWS, s)
    nw = ts // w
    qcols = pl.BlockSpec((ts, Q_PER_KVPAIR), lambda kp, i: (i, kp))
    kbase, vbase = Q_W // LANES, (Q_W + KV_W) // LANES
    prev = lambda base: pl.BlockSpec((w, LANES), lambda kp, i: (jnp.maximum(i * nw - 1, 0), base + kp))
    cur = lambda base: pl.BlockSpec((ts, LANES), lambda kp, i: (i, base + kp))
    vec = pl.BlockSpec((1, Q_PER_KVPAIR), lambda kp, i: (0, kp))
    return ts, nw, qcols, prev(kbase), cur(kbase), prev(vbase), cur(vbase), vec


def _swa_fwd(qk, vsrc, sink_row):
    s = qk.shape[0]
    w = WINDOW
    ts, nw, qcols, kprev, kcur, vprev, vcur, vec = _swa_specs(s)

    def body(q_ref, kp_ref, kc_ref, vp_ref, vc_ref, s_ref, o_ref, lse_ref):
        i = pl.program_id(1)
        k2 = jnp.concatenate([kp_ref[...], kc_ref[...]], axis=0)
        v2 = jnp.concatenate([vp_ref[...], vc_ref[...]], axis=0)
        ksw, vsw = _lane_swap(k2), _lane_swap(v2)
        row = lax.broadcasted_iota(jnp.int32, (w, 2 * w), 0)
        col = lax.broadcasted_iota(jnp.int32, (w, 2 * w), 1)
        band = (col > row) & (col <= row + w)
        first_half = _head_mask((w, LANES), 0)
        for u in range(nw):
            valid = band if u > 0 else band & ((col >= w) | (i > 0))
            rows, keys = slice(u * w, (u + 1) * w), slice(u * w, (u + 2) * w)
            for pr in range(Q_PER_KVPAIR // LANES):
                gh = pr // 2
                cols = slice(pr * LANES, (pr + 1) * LANES)
                q2 = q_ref[rows, cols].astype(F32) * SCALE
                outs, lses = [], []
                for hh in range(2):
                    kk = (k2 if hh == gh else ksw)[keys, :]
                    vv = (v2 if hh == gh else vsw)[keys, :]
                    qm = jnp.where(_head_mask(q2.shape, hh), q2, 0.0).astype(BF16)
                    sc = jnp.where(valid, lax.dot_general(qm, kk, _DN_NT, preferred_element_type=F32), NEG_INF)
                    sink = s_ref[:, pr * LANES + HEAD_DIM * hh: pr * LANES + HEAD_DIM * hh + 1]
                    m = jnp.maximum(jnp.max(sc, axis=1, keepdims=True), sink)
                    e = jnp.exp(sc - m)
                    l = jnp.sum(e, axis=1, keepdims=True) + jnp.exp(sink - m)
                    outs.append(jnp.dot(e.astype(BF16), vv, preferred_element_type=F32) / l)
                    lses.append(m + jnp.log(l))
                o_ref[rows, cols] = jnp.where(first_half, outs[0], outs[1]).astype(BF16)
                lse_ref[rows, cols] = jnp.where(first_half, lses[0], lses[1])

    return pl.pallas_call(
        body, name="swa_fwd", grid=(KV_PAIRS, s // ts),
        out_shape=(jax.ShapeDtypeStruct((s, Q_W), BF16), jax.ShapeDtypeStruct((s, Q_W), F32)),
        in_specs=[qcols, kprev, kcur, vprev, vcur, vec], out_specs=(qcols, qcols),
    )(qk, qk, qk, vsrc, vsrc, sink_row)


def _swa_bwd_dq(qk, vsrc, sink_row, do, o, lse):
    s = qk.shape[0]
    w = WINDOW
    ts, nw, qcols, kprev, kcur, vprev, vcur, vec = _swa_specs(s)

    def body(q_ref, kp_ref, kc_ref, vp_ref, vc_ref, s_ref, do_ref, o_ref, lse_ref, dq_ref, dsink_ref):
        i = pl.program_id(1)

        @pl.when(i == 0)
        def _():
            dsink_ref[...] = jnp.zeros_like(dsink_ref)

        k2 = jnp.concatenate([kp_ref[...], kc_ref[...]], axis=0)
        v2 = jnp.concatenate([vp_ref[...], vc_ref[...]], axis=0)
        ksw, vsw = _lane_swap(k2), _lane_swap(v2)
        row = lax.broadcasted_iota(jnp.int32, (w, 2 * w), 0)
        col = lax.broadcasted_iota(jnp.int32, (w, 2 * w), 1)
        band = (col > row) & (col <= row + w)
        first_half = _head_mask((w, LANES), 0)
        lane_all = lax.broadcasted_iota(jnp.int32, (1, Q_PER_KVPAIR), 1)
        dsink = jnp.zeros((1, Q_PER_KVPAIR), F32)
        for u in range(nw):
            valid = band if u > 0 else band & ((col >= w) | (i > 0))
            rows, keys = slice(u * w, (u + 1) * w), slice(u * w, (u + 2) * w)
            for pr in range(Q_PER_KVPAIR // LANES):
                gh = pr // 2
                cols = slice(pr * LANES, (pr + 1) * LANES)
                q2 = q_ref[rows, cols].astype(F32) * SCALE
                do2 = do_ref[rows, cols]
                prod = do2.astype(F32) * o_ref[rows, cols].astype(F32)
                lse2 = lse_ref[rows, cols]
                dqs = []
                for hh in range(2):
                    hm = _head_mask(q2.shape, hh)
                    lo = pr * LANES + HEAD_DIM * hh
                    kk = (k2 if hh == gh else ksw)[keys, :]
                    vv = (v2 if hh == gh else vsw)[keys, :]
                    qm = jnp.where(hm, q2, 0.0).astype(BF16)
                    sc = jnp.where(valid, lax.dot_general(qm, kk, _DN_NT, preferred_element_type=F32), NEG_INF)
                    lse_c = lse2[:, HEAD_DIM * hh: HEAD_DIM * hh + 1]
                    pm = jnp.exp(sc - lse_c)
                    doh = jnp.where(hm, do2, jnp.zeros_like(do2))
                    delta = jnp.sum(jnp.where(hm, prod, 0.0), axis=1, keepdims=True)
                    ds = pm * (lax.dot_general(doh, vv, _DN_NT, preferred_element_type=F32) - delta)
                    dqs.append(jnp.dot(ds.astype(BF16), kk, preferred_element_type=F32))
                    part = jnp.sum(-jnp.exp(s_ref[:, lo:lo + 1] - lse_c) * delta, axis=0, keepdims=True)
                    dsink = dsink + jnp.where((lane_all >= lo) & (lane_all < lo + HEAD_DIM), part, 0.0)
                dq_ref[rows, cols] = (jnp.where(first_half, dqs[0], dqs[1]) * SCALE).astype(BF16)
        dsink_ref[...] += dsink

    return pl.pallas_call(
        body, name="swa_bwd_dq", grid=(KV_PAIRS, s // ts),
        out_shape=(jax.ShapeDtypeStruct((s, Q_W), BF16), jax.ShapeDtypeStruct((1, Q_W), F32)),
        in_specs=[qcols, kprev, kcur, vprev, vcur, vec, qcols, qcols, qcols], out_specs=(qcols, vec),
        compiler_params=pltpu.CompilerParams(dimension_semantics=("parallel", "arbitrary")),
    )(qk, qk, qk, vsrc, vsrc, sink_row, do, o, lse)


def _swa_bwd_dkv(qk, vsrc, do, o, lse):
    s = qk.shape[0]
    w = WINDOW
    ts = min(SWA_ROWS, s)
    nw = ts // w
    nstep = s // ts
    last_window = s // w - 1

    def body(k_ref, v_ref, qc_ref, qn_ref, doc_ref, don_ref, oc_ref, on_ref, lc_ref, ln_ref, dk_ref, dv_ref):
        j = pl.program_id(1)
        cat = lambda a_ref, b_ref: jnp.concatenate([a_ref[...], b_ref[...]], axis=0)
        qcat, docat, ocat, lcat = cat(qc_ref, qn_ref), cat(doc_ref, don_ref), cat(oc_ref, on_ref), cat(lc_ref, ln_ref)
        qsw, dosw = _lane_swap(qcat), _lane_swap(docat)
        row = lax.broadcasted_iota(jnp.int32, (2 * w, w), 0)
        col = lax.broadcasted_iota(jnp.int32, (2 * w, w), 1)
        band = (col <= row) & (row < col + w)
        has_next = j + 1 < nstep
        for wi in range(nw):
            valid = band if wi < nw - 1 else band & ((row < w) | has_next)
            keys, qrows = slice(wi * w, (wi + 1) * w), slice(wi * w, (wi + 2) * w)
            kw, vw = k_ref[keys, :], v_ref[keys, :]
            dk = jnp.zeros((w, LANES), F32)
            dv = jnp.zeros((w, LANES), F32)
            for pr in range(Q_PER_KVPAIR // LANES):
                gh = pr // 2
                cols = slice(pr * LANES, (pr + 1) * LANES)
                prod = docat[qrows, cols].astype(F32) * ocat[qrows, cols].astype(F32)
                lse2 = lcat[qrows, cols]
                to_kv = _head_mask(prod.shape, gh)
                for hh in range(2):
                    q_src, do_src = (qcat, docat) if hh == gh else (qsw, dosw)
                    q_al = jnp.where(to_kv, q_src[qrows, cols].astype(F32) * SCALE, 0.0).astype(BF16)
                    do_al = jnp.where(to_kv, do_src[qrows, cols], jnp.zeros((2 * w, LANES), BF16))
                    sc = jnp.where(valid, lax.dot_general(q_al, kw, _DN_NT, preferred_element_type=F32), NEG_INF)
                    pm = jnp.exp(sc - lse2[:, HEAD_DIM * hh: HEAD_DIM * hh + 1])
                    delta = jnp.sum(jnp.where(_head_mask(prod.shape, hh), prod, 0.0), axis=1, keepdims=True)
                    ds = pm * (lax.dot_general(do_al, vw, _DN_NT, preferred_element_type=F32) - delta)
                    dk = dk + lax.dot_general(ds.astype(BF16), q_al, _DN_TN, preferred_element_type=F32)
                    dv = dv + lax.dot_general(pm.astype(BF16), do_al, _DN_TN, preferred_element_type=F32)
            dk_ref[keys, :] = dk.astype(BF16)
            dv_ref[keys, :] = dv.astype(BF16)

    kbase, vbase = Q_W // LANES, (Q_W + KV_W) // LANES
    kv = lambda base: pl.BlockSpec((ts, LANES), lambda kp, j: (j, base + kp))
    same = pl.BlockSpec((ts, Q_PER_KVPAIR), lambda kp, j: (j, kp))
    nxt = pl.BlockSpec((w, Q_PER_KVPAIR), lambda kp, j: (jnp.minimum((j + 1) * nw, last_window), kp))
    out = pl.BlockSpec((ts, LANES), lambda kp, j: (j, kp))
    shp = jax.ShapeDtypeStruct((s, KV_W), BF16)
    return pl.pallas_call(
        body, name="swa_bwd_dkv", grid=(KV_PAIRS, nstep),
        out_shape=(shp, shp),
        in_specs=[kv(kbase), kv(vbase), same, nxt, same, nxt, same, nxt, same, nxt], out_specs=(out, out),
    )(qk, vsrc, qk, qk, do, do, o, o, lse, lse)


def _ffn_ple_fwd(h1, p_l, g_ffn, g_ple, w_gu, w_down, w_pg, w_pp, tag):
    hn2 = _rms_fwd(h1, g_ffn, f"rms_ffn_{tag}")
    ab, u = _gate_up_swiglu(hn2, w_gu, f"mm_gate_up_swiglu_{tag}")
    h2 = _mm(u, w_down, out_dtype=F32, res=h1, name=f"mm_down_{tag}")
    hn3 = _rms_fwd(h2, g_ple, f"rms_ple_{tag}")
    gl = _mm(hn3, w_pg, name=f"mm_ple_gate_{tag}")
    pp = _mm(p_l, w_pp, name=f"mm_ple_proj_{tag}")
    h3 = _ple_fwd(h2, gl, pp, f"ple_{tag}")
    return h3, dict(h1=h1, hn2=hn2, ab=ab, u=u, h2=h2, hn3=hn3, gl=gl, pp=pp)


def _ffn_ple_bwd(dh3, sv, p_l, g_ffn, g_ple, w_gu, w_down, w_pg, tag):
    dgl, dpp = _ple_bwd(dh3, sv["gl"], sv["pp"], f"ple_bwd_{tag}")
    d_wpp = _mm(p_l, dpp, ta=True, out_dtype=F32, name=f"mm_dw_ple_proj_{tag}")
    d_wpg = _mm(sv["hn3"], dgl, ta=True, out_dtype=F32, name=f"mm_dw_ple_gate_{tag}")
    dhn3 = _mm(dgl, w_pg, tb=True, out_dtype=F32, name=f"mm_dx_ple_gate_{tag}")
    dh2, dh2b, dg_ple = _rms_bwd(sv["h2"], g_ple, [dhn3], dh3, f"rms_ple_bwd_{tag}", True)
    d_wdown = _mm(sv["u"], dh2b, ta=True, out_dtype=F32, name=f"mm_dw_down_{tag}")
    dab = _down_t_swiglu_bwd(dh2b, w_down, sv["ab"], f"mm_dx_down_swiglu_bwd_{tag}")
    d_wgu = _mm(sv["hn2"], dab, ta=True, out_dtype=F32, group="out", group_width=2 * HID_PAD,
                name=f"mm_dw_gate_up_{tag}")
    dhn2 = _mm(dab, w_gu, tb=True, out_dtype=F32, group="k", name=f"mm_dx_gate_up_{tag}")
    dh1, dh1b, dg_ffn = _rms_bwd(sv["h1"], g_ffn, [dhn2], dh2, f"rms_ffn_bwd_{tag}", True)
    return dh1, dh1b, dict(d_wpp=d_wpp, d_wpg=d_wpg, d_wdown=d_wdown, d_wgu=d_wgu, dg_ple=dg_ple, dg_ffn=dg_ffn)


def _row_form(cum, blk):
    s = cum.shape[0]
    return cum[:, :N_FOX].T.reshape(N_FOX, s // blk, 1, blk)


def _col_form(cum):
    s = cum.shape[0]
    return jnp.broadcast_to(cum[:, :N_FOX].T[:, :, None], (N_FOX, s, LANES))


def kernel(x, p, positions, norm_mix, norm_ffn, norm_ple, norm_final, ev_w_in, ev_b_f, ev_w_out, od_w_in, od_sinks, od_w_out, ffn_w_gate, ffn_w_up, ffn_w_down, ple_w_proj, ple_w_gate, loss_target, m_norm_mix, m_norm_ffn, m_norm_ple, m_norm_final, m_ev_w_in, m_ev_b_f, m_ev_w_out, m_od_w_in, m_od_sinks, m_od_w_out, m_ffn_w_gate, m_ffn_w_up, m_ffn_w_down, m_ple_w_proj, m_ple_w_gate, v_norm_mix, v_norm_ffn, v_norm_ple, v_norm_final, v_ev_w_in, v_ev_b_f, v_ev_w_out, v_od_w_in, v_od_sinks, v_od_w_out, v_ffn_w_gate, v_ffn_w_up, v_ffn_w_down, v_ple_w_proj, v_ple_w_gate):
    s = x.shape[1]
    blk = min(ATTN_BLOCK, s)
    big_w = [ev_w_in, ev_w_out, od_w_in, od_w_out, ffn_w_gate, ffn_w_up, ffn_w_down, ple_w_proj, ple_w_gate]
    big_m = [m_ev_w_in, m_ev_w_out, m_od_w_in, m_od_w_out, m_ffn_w_gate, m_ffn_w_up, m_ffn_w_down, m_ple_w_proj, m_ple_w_gate]
    big_v = [v_ev_w_in, v_ev_w_out, v_od_w_in, v_od_w_out, v_ffn_w_gate, v_ffn_w_up, v_ffn_w_down, v_ple_w_proj, v_ple_w_gate]

    s_rows, s_gu, s_evin, s_odin, s_pp = _group_shards(*big_w, BF16)
    (g_evin,) = _all_gather_weights([s_evin])
    d = D_MODEL
    w_in0 = _from_owner_cols(g_evin)
    w_qkv = w_in0[:, :QKV_W]
    w_f = jnp.pad(w_in0[:, QKV_W:], ((0, 0), (0, LANES - N_FOX)))

    h0 = x[0]
    target = loss_target[0]
    p_b = [p[l, 0].astype(BF16) for l in range(2)]
    g_mix = [norm_mix[l][None, :] for l in range(2)]
    g_ffn = [norm_ffn[l][None, :] for l in range(2)]
    g_ple = [norm_ple[l][None, :] for l in range(2)]
    b_f = jnp.pad(ev_b_f, ((0, 0), (0, LANES - N_FOX)))

    half = HEAD_DIM // 2
    inv = ROPE_THETA ** (-jnp.arange(half, dtype=F32) / half)
    ang = positions[0].astype(F32)[:, None] * inv
    cos_t = jnp.tile(jnp.cos(ang), (1, 4))
    sin_t = jnp.tile(jnp.concatenate([-jnp.sin(ang), jnp.sin(ang)], axis=1), (1, 2))

    hn1 = _rms_fwd(h0, g_mix[0], "rms_mix_0")
    proj0 = _mm(hn1, w_qkv, name="mm_in_0")
    flog = _mm(hn1, w_f, out_dtype=F32, name="mm_fgate_0")
    cum = _fgate_fwd(flog, b_f)
    crow = _row_form(cum, blk)
    o_fox, acol, (g_rows, g_gu, g_odin, g_pp) = _fox_fwd(proj0, _col_form(cum), crow, [s_rows, s_gu, s_odin, s_pp])
    w_oi = _from_owner_cols(g_odin)
    w_eo = g_rows[:, 0:128].reshape(d, d)
    w_oo = g_rows[:, 128:256].reshape(d, d)
    w_down = [g_rows[:, 256 + HID_PAD * l: 256 + HID_PAD * (l + 1)].reshape(D_FF_PAD, d) for l in range(2)]
    w_pg = [g_rows[:, 1024 + 128 * l: 1152 + 128 * l].reshape(d, d) for l in range(2)]
    w_gu = [g_gu[:, d * l: d * (l + 1)] for l in range(2)]
    w_pp = [_from_owner_cols(g_pp[:, PLE_DIM * l: PLE_DIM * (l + 1)]) for l in range(2)]
    o_sb, rsave = _sb_fwd(proj0)
    o0 = jnp.concatenate([o_fox, o_sb], axis=1)
    h1 = _mm(o0, w_eo, out_dtype=F32, res=h0, name="mm_out_0")
    h3, sv0 = _ffn_ple_fwd(h1, p_b[0], g_ffn[0], g_ple[0], w_gu[0], w_down[0], w_pg[0], w_pp[0], "0")

    hn1b = _rms_fwd(h3, g_mix[1], "rms_mix_1")
    proj1 = _mm(hn1b, w_oi, name="mm_in_1")
    qk_r = _rope(proj1, Q_W + KV_W, cos_t, sin_t, 1.0, "rope_fwd")
    sink_row = jnp.repeat(od_sinks[0], HEAD_DIM)[None, :]
    o1, lse1 = _swa_fwd(qk_r, proj1, sink_row)
    h4 = _mm(o1, w_oo, out_dtype=F32, res=h3, name="mm_out_1")
    h6, sv1 = _ffn_ple_fwd(h4, p_b[1], g_ffn[1], g_ple[1], w_gu[1], w_down[1], w_pg[1], w_pp[1], "1")

    loss_part, dh6, dg_final = _final_norm_loss(h6, norm_final[None, :], target)

    dh4, dh4b, gr1 = _ffn_ple_bwd(dh6, sv1, p_b[1], g_ffn[1], g_ple[1], w_gu[1], w_down[1], w_pg[1], "1")
    do1 = _mm(dh4b, w_oo, tb=True, name="mm_dx_out_1")
    d_woo = _mm(o1, dh4b, ta=True, out_dtype=F32, name="mm_dw_out_1")
    dq1, dsink_row = _swa_bwd_dq(qk_r, proj1, sink_row, do1, o1, lse1)
    dk1, dv1 = _swa_bwd_dkv(qk_r, proj1, do1, o1, lse1)
    dqk = _rope(jnp.concatenate([dq1, dk1], axis=1), Q_W + KV_W, cos_t, sin_t, -1.0, "rope_bwd")
    dproj1 = jnp.concatenate([dqk, dv1], axis=1)
    d_woi = _mm(hn1b, dproj1, ta=True, out_dtype=F32, name="mm_dw_in_1")
    dhn1b = _mm(dproj1, w_oi, tb=True, out_dtype=F32, name="mm_dx_in_1")
    dh3, dg_mix1 = _rms_bwd(h3, g_mix[1], [dhn1b], dh4, "rms_mix_bwd_1", False)

    dh1, dh1b, gr0 = _ffn_ple_bwd(dh3, sv0, p_b[0], g_ffn[0], g_ple[0], w_gu[0], w_down[0], w_pg[0], "0")
    do0 = _mm(dh1b, w_eo, tb=True, name="mm_dx_out_0")
    d_weo = _mm(o0, dh1b, ta=True, out_dtype=F32, name="mm_dw_out_0")
    by_rows = lambda g, r: g.reshape(N_DEV, r, d)
    early = [
        jnp.concatenate([by_rows(d_weo, 128), by_rows(d_woo, 128), by_rows(gr0["d_wdown"], HID_PAD),
                         by_rows(gr1["d_wdown"], HID_PAD), by_rows(gr0["d_wpg"], 128), by_rows(gr1["d_wpg"], 128)],
                        axis=1),
        jnp.concatenate([gr0["d_wgu"], gr1["d_wgu"]], axis=1),
        _by_owner_cols(d_woi),
        jnp.concatenate([_by_owner_cols(gr0["d_wpp"]), _by_owner_cols(gr1["d_wpp"])], axis=1),
    ]
    early_tags = ("rows", "gu", "od_in", "pp")
    core = lax.axis_index("c").astype(jnp.int32).reshape(1)
    dq_f, dk_f, dv_f, gc, early_sib = _fox_bwd(proj0, do0, o0, acol, crow, early)
    early_part = [_rs_chip_sum(core, g, r, f"rs_chip_sum_{t}") for g, r, t in zip(early, early_sib, early_tags)]
    dq_s, dk_s, dv_s, early_recv = _sb_bwd(proj0, do0, rsave, early_part)
    dproj0 = jnp.concatenate([dq_f, dk_f, dv_f, dq_s, dk_s, dv_s], axis=1)
    gcum = jnp.pad(gc.reshape(N_FOX, s).T, ((0, 0), (0, LANES - N_FOX)))
    dflog, db_f = _fgate_bwd(gcum, flog, b_f)
    d_wqkv = _mm(hn1, dproj0, ta=True, out_dtype=F32, name="mm_dw_in_0")
    d_wf = _mm(hn1, dflog, ta=True, out_dtype=F32, name="mm_dw_fgate_0")
    dhn1 = _mm(dproj0, w_qkv, tb=True, out_dtype=F32, name="mm_dx_in_0")
    dhn1f = _mm(dflog, w_f, tb=True, out_dtype=F32, name="mm_dx_fgate_0")
    grad_x, dg_mix0 = _rms_bwd(h0, g_mix[0], [dhn1, dhn1f], dh1, "rms_mix_bwd_0", False)

    late = [_by_owner_cols(jnp.concatenate([d_wqkv, d_wf[:, :N_FOX]], axis=1))]
    late_sib = _rs_sibling_exchange(late)
    late_recv = _rs_chip_exchange([_rs_chip_sum(core, late[0], late_sib[0], "rs_chip_sum_ev_in")])
    tags = ("rows", "gu", "ev_in", "od_in", "pp")
    chip_recv = [early_recv[0], early_recv[1], late_recv[0], early_recv[2], early_recv[3]]
    w_grp, m_grp, v_grp = (_group_shards(*ws, F32) for ws in (big_w, big_m, big_v))
    updated = [_rs_sum_adamw(r, w_, m_, v_, f"rs_sum_adamw_{t}")
               for r, w_, m_, v_, t in zip(chip_recv, w_grp, m_grp, v_grp, tags)]
    big_g, big_d, big_nm, big_nv = (_ungroup_shards([u[k] for u in updated]) for k in range(4))

    def small_pack(nmix, nffn, nple, nfin, bf, sk, extra):
        last = jnp.concatenate([bf.reshape(-1), sk.reshape(-1), extra.reshape(-1)])
        last = jnp.pad(last, (0, D_MODEL - last.shape[0]))
        return jnp.concatenate([nmix, nffn, nple, nfin.reshape(1, -1), last[None, :]], axis=0)

    small_g = small_pack(jnp.concatenate([dg_mix0, dg_mix1]), jnp.concatenate([gr0["dg_ffn"], gr1["dg_ffn"]]),
                         jnp.concatenate([gr0["dg_ple"], gr1["dg_ple"]]), dg_final,
                         db_f[0, :N_FOX], dsink_row[0, ::HEAD_DIM], loss_part[0, :1])
    zero1 = jnp.zeros((1,), F32)
    small_w = small_pack(norm_mix, norm_ffn, norm_ple, norm_final, ev_b_f, od_sinks, zero1)
    small_m = small_pack(m_norm_mix, m_norm_ffn, m_norm_ple, m_norm_final, m_ev_b_f, m_od_sinks, zero1)
    small_v = small_pack(v_norm_mix, v_norm_ffn, v_norm_ple, v_norm_final, v_ev_b_f, v_od_sinks, zero1)
    sg, sd, sm, sv_ = _small_allreduce_adamw(small_g, small_w, small_m, small_v)

    def small_unpack(t):
        return [t[0:2], t[2:4], t[4:6], t[6], t[7, :N_FOX][None, :], t[7, N_FOX:N_FOX + N_Q][None, :]]

    loss = sg[7, N_FOX + N_Q]

    def ordered(small, big):
        nm, nf, npl, nfin, bf, sk = small_unpack(small)
        ev_in, ev_out, od_in, od_out, fg, fu, fd, pproj, pgate = big
        return [nm, nf, npl, nfin, ev_in, bf, ev_out, od_in, sk, od_out, fg, fu, fd, pproj, pgate]

    return (loss, grad_x[None], *ordered(sg, big_g), *ordered(sd, big_d),
            *ordered(sm, big_nm), *ordered(sv_, big_nv))
```

```python
import functools

import jax
import jax.numpy as jnp
from jax import lax
from jax.experimental import pallas as pl
from jax.experimental.pallas import tpu as pltpu

F32 = jnp.float32
BF16 = jnp.bfloat16

D_MODEL = 1024
HEAD_DIM = 64
N_FOX = 8
N_SB = 8
FOX_W = N_FOX * HEAD_DIM
SB_W = N_SB * HEAD_DIM
QKV_W = 3 * FOX_W + 3 * SB_W
EVEN_IN = QKV_W + N_FOX
N_Q = 16
N_KV = 4
ODD_IN = N_Q * HEAD_DIM + 2 * N_KV * HEAD_DIM
WINDOW = 128
ROPE_THETA = 10000.0
D_FF = 2816
PLE_DIM = 256
EPS = 1e-6
NEG_INF = -1e30
SCALE = HEAD_DIM ** -0.5

ADAM_LR = 0.001
ADAM_B1 = 0.9
ADAM_B2 = 0.999
ADAM_EPS = 1e-08
ADAM_WD = 0.01
ADAM_STEP = 10

N_DEV = 8
LANES = 128
ROW_TILE = 256
ATTN_BLOCK = 256
HID_SHARD = D_FF // N_DEV
HID_PAD = 384
D_FF_PAD = N_DEV * HID_PAD

MESH = pl.DeviceIdType.MESH


def _pick(n, prefs):
    for t in prefs:
        if n % t == 0:
            return t
    return n


def _pad_to(a, axis, size):
    pad = [(0, 0)] * a.ndim
    pad[axis] = (0, size - a.shape[axis])
    return jnp.pad(a, pad)


def _group_shards(ev_in, ev_out, od_in, od_out, gate, up, down, pproj, pgate, dtype):
    rows = jnp.concatenate([ev_out[0], od_out[0], _pad_to(down[0], 0, HID_PAD), _pad_to(down[1], 0, HID_PAD),
                            pgate[0], pgate[1]], axis=0)
    gu = jnp.concatenate([jnp.concatenate([_pad_to(gate[l], 1, HID_PAD), _pad_to(up[l], 1, HID_PAD)], axis=1)
                          for l in range(2)], axis=0)
    groups = [rows, gu, ev_in[0], od_in[0], pproj.reshape(-1, pproj.shape[-1])]
    return [g.astype(dtype) for g in groups]


def _ungroup_shards(groups):
    rows, gu, ev_in, od_in, pp = groups
    d = D_MODEL
    down = jnp.stack([rows[256 + HID_PAD * l: 256 + HID_PAD * l + HID_SHARD] for l in range(2)])
    pgate = jnp.stack([rows[1024 + 128 * l: 1152 + 128 * l] for l in range(2)])
    gate = jnp.stack([gu[d * l: d * (l + 1), :HID_SHARD] for l in range(2)])
    up = jnp.stack([gu[d * l: d * (l + 1), HID_PAD:HID_PAD + HID_SHARD] for l in range(2)])
    return [ev_in[None], rows[None, 0:128], od_in[None], rows[None, 128:256], gate, up, down,
            pp.reshape(2, PLE_DIM, -1), pgate]


def _by_owner_cols(full):
    r, c8 = full.shape
    return full.reshape(r, N_DEV, c8 // N_DEV).transpose(1, 0, 2)


def _from_owner_cols(g):
    n, r, c = g.shape
    return g.transpose(1, 0, 2).reshape(r, n * c)


_ANY = pl.BlockSpec(memory_space=pl.ANY)


def _all_gather_steps(x_refs, out_refs, send_sems, recv_sems, local_sems):
    n = len(x_refs)
    x, y, c = lax.axis_index("x"), lax.axis_index("y"), lax.axis_index("c")
    me, sibling = (x, y, c), (x, y, 1 - c)
    chips = [(1 - x, y), (x, 1 - y), (1 - x, 1 - y)]

    def copy(a, k, block, to, from_input=False):
        px, py, pc = block
        slot = out_refs[a].at[4 * px + 2 * py + pc]
        return pltpu.make_async_remote_copy(
            src_ref=x_refs[a] if from_input else slot, dst_ref=slot,
            send_sem=send_sems.at[7 * a + k], recv_sem=recv_sems.at[7 * a + k],
            device_id=to, device_id_type=MESH)

    def mine():
        return [pltpu.make_async_copy(x_refs[a], out_refs[a].at[4 * x + 2 * y + c], local_sems.at[a]) for a in range(n)]

    def first():
        out = []
        for a in range(n):
            out.append(copy(a, 0, me, sibling, True))
            out += [copy(a, 1 + j, me, (*chip, c), True) for j, chip in enumerate(chips)]
        return out

    def issue():
        for cp in mine() + first():
            cp.start()

    def complete():
        passed = []
        for j, chip in enumerate(chips):
            for a in range(n):
                copy(a, 1 + j, (*chip, c), me).wait_recv()
                passed.append(copy(a, 4 + j, (*chip, c), sibling))
                passed[-1].start()
        for a in range(n):
            copy(a, 0, sibling, me).wait_recv()
            for j, chip in enumerate(chips):
                copy(a, 4 + j, (*chip, 1 - c), me).wait_recv()
        for cp in first() + passed:
            cp.wait_send()
        for cp in mine():
            cp.wait()

    return issue, complete


def _all_gather_scratch(n):
    return [pltpu.SemaphoreType.DMA((7 * n,)), pltpu.SemaphoreType.DMA((7 * n,)), pltpu.SemaphoreType.DMA((n,))]


def _gathered_shapes(shards):
    return tuple(jax.ShapeDtypeStruct((N_DEV,) + s.shape, s.dtype) for s in shards)


def _all_gather_weights(shards):
    n = len(shards)

    def body(*refs):
        issue, complete = _all_gather_steps(refs[:n], refs[n:2 * n], *refs[2 * n:])
        issue()
        complete()

    return pl.pallas_call(
        body, name="ag_weights", out_shape=_gathered_shapes(shards),
        in_specs=[_ANY] * n, out_specs=tuple([_ANY] * n), scratch_shapes=_all_gather_scratch(n),
    )(*shards)


def _sibling_exchange_steps(g_refs, out_refs, send_sems, recv_sems):
    n = len(g_refs)
    x, y, c = lax.axis_index("x"), lax.axis_index("y"), lax.axis_index("c")

    def copies():
        return [pltpu.make_async_remote_copy(
            src_ref=g_refs[a].at[2 * k + (1 - c)], dst_ref=out_refs[a].at[k],
            send_sem=send_sems.at[4 * a + k], recv_sem=recv_sems.at[4 * a + k],
            device_id=(x, y, 1 - c), device_id_type=MESH) for a in range(n) for k in range(4)]

    def issue():
        for cp in copies():
            cp.start()

    def complete():
        for cp in copies():
            cp.wait_recv()
        for cp in copies():
            cp.wait_send()

    return issue, complete


def _sibling_exchange_scratch(n):
    return [pltpu.SemaphoreType.DMA((4 * n,)), pltpu.SemaphoreType.DMA((4 * n,))]


def _quarter_shapes(arrays):
    return tuple(jax.ShapeDtypeStruct((4,) + g.shape[1:], g.dtype) for g in arrays)


def _rs_sibling_exchange(gps):
    n = len(gps)

    def body(*refs):
        issue, complete = _sibling_exchange_steps(refs[:n], refs[n:2 * n], *refs[2 * n:])
        issue()
        complete()

    return pl.pallas_call(
        body, name="rs_sibling_exchange", out_shape=_quarter_shapes(gps),
        in_specs=[_ANY] * n, out_specs=tuple([_ANY] * n), scratch_shapes=_sibling_exchange_scratch(n),
    )(*gps)


def _rs_chip_sum(core, gp, recv, name):
    _, rows, cols = gp.shape
    tr = ROW_TILE

    def body(core_ref, a_ref, b_ref, o_ref):
        o_ref[...] = (a_ref[...] + b_ref[...]).astype(BF16)

    return pl.pallas_call(
        body, name=name,
        out_shape=jax.ShapeDtypeStruct((4, rows, cols), BF16),
        grid_spec=pltpu.PrefetchScalarGridSpec(
            num_scalar_prefetch=1, grid=(4, rows // tr),
            in_specs=[pl.BlockSpec((1, tr, cols), lambda k, r, cr: (2 * k + cr[0], r, 0)),
                      pl.BlockSpec((1, tr, cols), lambda k, r, cr: (k, r, 0))],
            out_specs=pl.BlockSpec((1, tr, cols), lambda k, r, cr: (k, r, 0))),
    )(core, gp, recv)


def _chip_exchange_steps(p_refs, out_refs, send_sems, recv_sems, local_sems):
    n = len(p_refs)
    x, y, c = lax.axis_index("x"), lax.axis_index("y"), lax.axis_index("c")
    my_chip = 2 * x + y

    def mine():
        return [pltpu.make_async_copy(p_refs[a].at[my_chip], out_refs[a].at[my_chip], local_sems.at[a])
                for a in range(n)]

    def copies():
        return [pltpu.make_async_remote_copy(
            src_ref=p_refs[a].at[2 * px + py], dst_ref=out_refs[a].at[my_chip],
            send_sem=send_sems.at[3 * a + j], recv_sem=recv_sems.at[3 * a + j],
            device_id=(px, py, c), device_id_type=MESH)
            for a in range(n) for j, (px, py) in enumerate([(1 - x, y), (x, 1 - y), (1 - x, 1 - y)])]

    def issue():
        for cp in mine() + copies():
            cp.start()

    def complete():
        for cp in copies():
            cp.wait_recv()
        for cp in copies():
            cp.wait_send()
        for cp in mine():
            cp.wait()

    return issue, complete


def _chip_exchange_scratch(n):
    return [pltpu.SemaphoreType.DMA((3 * n,)), pltpu.SemaphoreType.DMA((3 * n,)), pltpu.SemaphoreType.DMA((n,))]


def _same_shapes(arrays):
    return tuple(jax.ShapeDtypeStruct(a.shape, a.dtype) for a in arrays)


def _rs_chip_exchange(parts):
    n = len(parts)

    def body(*refs):
        issue, complete = _chip_exchange_steps(refs[:n], refs[n:2 * n], *refs[2 * n:])
        issue()
        complete()

    return pl.pallas_call(
        body, name="rs_chip_exchange", out_shape=_same_shapes(parts),
        in_specs=[_ANY] * n, out_specs=tuple([_ANY] * n), scratch_shapes=_chip_exchange_scratch(n),
    )(*parts)


def _adamw(w, g, m, v):
    m = ADAM_B1 * m + (1.0 - ADAM_B1) * g
    v = ADAM_B2 * v + (1.0 - ADAM_B2) * (g * g)
    m_hat = m / (1.0 - ADAM_B1 ** ADAM_STEP)
    v_hat = v / (1.0 - ADAM_B2 ** ADAM_STEP)
    delta = -ADAM_LR * (m_hat / (jnp.sqrt(v_hat) + ADAM_EPS) + ADAM_WD * w)
    return delta, m, v


def _rs_sum_adamw(recv, w, m, v, name):
    _, rows, cols = recv.shape
    tr = ROW_TILE

    def body(r_ref, w_ref, m_ref, v_ref, g_out, d_out, m_out, v_out):
        g = r_ref[0].astype(F32)
        for k in range(1, 4):
            g = g + r_ref[k].astype(F32)
        delta, m_new, v_new = _adamw(w_ref[...], g, m_ref[...], v_ref[...])
        g_out[...] = g
        d_out[...] = delta
        m_out[...] = m_new
        v_out[...] = v_new

    flat = pl.BlockSpec((tr, cols), lambda r: (r, 0))
    shp = jax.ShapeDtypeStruct((rows, cols), F32)
    return pl.pallas_call(
        body, name=name, grid=(rows // tr,),
        out_shape=(shp, shp, shp, shp),
        in_specs=[pl.BlockSpec((4, tr, cols), lambda r: (0, r, 0)), flat, flat, flat],
        out_specs=(flat, flat, flat, flat),
    )(recv, w, m, v)


def _small_allreduce_adamw(vec, w, m, v):
    rows, cols = vec.shape

    def body(x_ref, w_ref, m_ref, v_ref, g_out, d_out, m_out, v_out, gather, send_sems, recv_sems):
        x, y, c = lax.axis_index("x"), lax.axis_index("y"), lax.axis_index("c")
        me = 4 * x + 2 * y + c
        copies = []
        for d in range(1, N_DEV):
            dx, dy, dc = (d >> 2) & 1, (d >> 1) & 1, d & 1
            peer = (x ^ dx if dx else x, y ^ dy if dy else y, c ^ dc if dc else c)
            copies.append(pltpu.make_async_remote_copy(
                src_ref=x_ref, dst_ref=gather.at[me],
                send_sem=send_sems.at[d - 1], recv_sem=recv_sems.at[d - 1],
                device_id=peer, device_id_type=MESH))
        for cp in copies:
            cp.start()
        gather[me] = x_ref[...]
        for cp in copies:
            cp.wait_recv()
        for cp in copies:
            cp.wait_send()
        g = gather[0]
        for k in range(1, N_DEV):
            g = g + gather[k]
        delta, m_new, v_new = _adamw(w_ref[...], g, m_ref[...], v_ref[...])
        g_out[...] = g
        d_out[...] = delta
        m_out[...] = m_new
        v_out[...] = v_new

    vm = pl.BlockSpec(memory_space=pltpu.VMEM)
    shp = jax.ShapeDtypeStruct((rows, cols), F32)
    return pl.pallas_call(
        body, name="small_allreduce_adamw",
        out_shape=(shp, shp, shp, shp),
        in_specs=[vm, vm, vm, vm], out_specs=(vm, vm, vm, vm),
        scratch_shapes=[pltpu.VMEM((N_DEV, rows, cols), F32),
                        pltpu.SemaphoreType.DMA((N_DEV - 1,)), pltpu.SemaphoreType.DMA((N_DEV - 1,))],
    )(vec, w, m, v)


def _mm(a, b, *, ta=False, tb=False, out_dtype=BF16, res=None, name, group=None, group_width=None):
    if ta:
        kdim, m = a.shape
    else:
        m, kdim = a.shape
    if group == "n":
        assert not ta and not tb and res is None
        ng, kb, tn = b.shape
        n = ng * tn
    elif group == "k":
        assert tb and not ta and res is None
        ng, n, chunk = b.shape
        kb = ng * chunk
        per_step = 2 if ng % 2 == 0 else 1
        tk = per_step * chunk
    elif tb:
        n, kb = b.shape
    else:
        kb, n = b.shape
    assert kdim == kb, (a.shape, b.shape, ta, tb, group)
    tm = _pick(m, (1024, 512, 256, 128))
    if group == "out":
        assert ta and not tb and res is None
        tn = group_width
    elif group != "n":
        tn = _pick(n, (1024, 1408, 768, 512, 256, 128))
    if group != "k":
        tk = _pick(kdim, ((2048,) if ta else ()) + (1024, 1408, 512, 256, 128))
    nk = kdim // tk
    dn = (((0 if ta else 1,), (1 if tb else 0,)), ((), ()))
    has_res = res is not None
    in_place = nk > 1 and out_dtype == F32
    use_acc = nk > 1 and not in_place

    def body(*refs):
        a_ref, b_ref = refs[:2]
        r_ref = refs[2] if has_res else None
        o_ref = refs[3] if has_res else refs[2]
        if group == "k":
            part = sum(lax.dot_general(a_ref[:, c * chunk:(c + 1) * chunk], b_ref[c], dn, preferred_element_type=F32)
                       for c in range(per_step))
        else:
            part = lax.dot_general(a_ref[...], b_ref[...], dn, preferred_element_type=F32)
        if nk == 1:
            if has_res:
                part = part + r_ref[...].astype(F32)
            o_ref[...] = part.astype(out_dtype)
            return
        k = pl.program_id(2)
        acc = o_ref if in_place else refs[-1]

        @pl.when(k == 0)
        def _():
            acc[...] = part + r_ref[...].astype(F32) if has_res else part

        @pl.when(k > 0)
        def _():
            acc[...] += part

        if use_acc:
            @pl.when(k == nk - 1)
            def _():
                o_ref[...] = acc[...].astype(out_dtype)

    a_spec = (pl.BlockSpec((tk, tm), lambda i, j, k: (k, i)) if ta
              else pl.BlockSpec((tm, tk), lambda i, j, k: (i, k)))
    if group == "n":
        b_spec = pl.BlockSpec((None, tk, tn), lambda i, j, k: (j, k, 0))
    elif group == "k":
        b_spec = pl.BlockSpec((per_step, tn, chunk), lambda i, j, k: (k, j, 0))
    elif tb:
        b_spec = pl.BlockSpec((tn, tk), lambda i, j, k: (j, k))
    else:
        b_spec = pl.BlockSpec((tk, tn), lambda i, j, k: (k, j))
    if group == "out":
        o_spec = pl.BlockSpec((None, tm, tn), lambda i, j, k: (j, i, 0))
        out_shape = jax.ShapeDtypeStruct((n // tn, m, tn), out_dtype)
    else:
        o_spec = pl.BlockSpec((tm, tn), lambda i, j, k: (i, j))
        out_shape = jax.ShapeDtypeStruct((m, n), out_dtype)
    in_specs = [a_spec, b_spec] + ([o_spec] if has_res else [])
    args = (a, b) + ((res,) if has_res else ())
    return pl.pallas_call(
        body, name=name, grid=(m // tm, n // tn, nk),
        out_shape=out_shape,
        in_specs=in_specs, out_specs=o_spec,
        scratch_shapes=[pltpu.VMEM((tm, tn), F32)] if use_acc else [],
        compiler_params=pltpu.CompilerParams(dimension_semantics=("parallel", "parallel", "arbitrary")),
    )(*args)


def _row_tile(s):
    return _pick(s, (256, 128))


def _rms_fwd(h, g, name):
    s, d = h.shape
    ts = _row_tile(s)

    def body(h_ref, g_ref, o_ref):
        x = h_ref[...]
        r = lax.rsqrt(jnp.mean(x * x, axis=-1, keepdims=True) + EPS)
        o_ref[...] = ((x * r) * g_ref[...]).astype(BF16)

    return pl.pallas_call(
        body, name=name, grid=(s // ts,),
        out_shape=jax.ShapeDtypeStruct((s, d), BF16),
        in_specs=[pl.BlockSpec((ts, d), lambda i: (i, 0)), pl.BlockSpec((1, d), lambda i: (0, 0))],
        out_specs=pl.BlockSpec((ts, d), lambda i: (i, 0)),
    )(h, g)


def _rms_bwd(h, g, dhns, dres, name, want_bf16):
    s, d = h.shape
    ts = _row_tile(s)
    n_in = len(dhns)

    def body(*refs):
        h_ref, g_ref, r_ref = refs[:3]
        dy_refs = refs[3:3 + n_in]
        outs = refs[3 + n_in:]
        dh_ref, dg_ref = outs[0], outs[-1]
        i = pl.program_id(0)
        x = h_ref[...]
        dy = dy_refs[0][...].astype(F32)
        for extra in dy_refs[1:]:
            dy = dy + extra[...].astype(F32)
        r = lax.rsqrt(jnp.mean(x * x, axis=-1, keepdims=True) + EPS)
        xr = x * r
        u = dy * g_ref[...]
        dx = r * (u - xr * jnp.mean(xr * u, axis=-1, keepdims=True))
        dh = r_ref[...] + dx
        dh_ref[...] = dh
        if want_bf16:
            outs[1][...] = dh.astype(BF16)

        @pl.when(i == 0)
        def _():
            dg_ref[...] = jnp.zeros_like(dg_ref)

        dg_ref[...] += jnp.sum(dy * xr, axis=0, keepdims=True)

    row = pl.BlockSpec((ts, d), lambda i: (i, 0))
    vec = pl.BlockSpec((1, d), lambda i: (0, 0))
    out_shape = [jax.ShapeDtypeStruct((s, d), F32)]
    out_specs = [row]
    if want_bf16:
        out_shape.append(jax.ShapeDtypeStruct((s, d), BF16))
        out_specs.append(row)
    out_shape.append(jax.ShapeDtypeStruct((1, d), F32))
    out_specs.append(vec)
    return pl.pallas_call(
        body, name=name, grid=(s // ts,),
        out_shape=tuple(out_shape),
        in_specs=[row, vec, row] + [row] * n_in, out_specs=tuple(out_specs),
        compiler_params=pltpu.CompilerParams(dimension_semantics=("arbitrary",)),
    )(h, g, dres, *dhns)


def _sigmoid_parts(z):
    e = jnp.exp(-jnp.abs(z))
    r = 1.0 / (1.0 + e)
    er = e * r
    pos = z >= 0
    return jnp.where(pos, r, er), jnp.where(pos, er, r)


def _gate_up_swiglu(hn, w_gu, name):
    s, d = hn.shape
    f = HID_PAD
    tm = _pick(s, (1024, 512, 256, 128))

    def body(x_ref, w_ref, ab_ref, u_ref):
        ab = jnp.dot(x_ref[...], w_ref[...], preferred_element_type=F32).astype(BF16)
        ab_ref[...] = ab
        a = ab[:, :f].astype(F32)
        b = ab[:, f:].astype(F32)
        sg, _ = _sigmoid_parts(a)
        u_ref[...] = ((a * sg) * b).astype(BF16)

    return pl.pallas_call(
        body, name=name, grid=(s // tm, N_DEV),
        out_shape=(jax.ShapeDtypeStruct((s, 2 * D_FF_PAD), BF16), jax.ShapeDtypeStruct((s, D_FF_PAD), BF16)),
        in_specs=[pl.BlockSpec((tm, d), lambda i, j: (i, 0)), pl.BlockSpec((None, d, 2 * f), lambda i, j: (j, 0, 0))],
        out_specs=(pl.BlockSpec((tm, 2 * f), lambda i, j: (i, j)), pl.BlockSpec((tm, f), lambda i, j: (i, j))),
    )(hn, w_gu)


def _down_t_swiglu_bwd(dh, w_down, ab, name):
    s, d = dh.shape
    f = HID_PAD
    tm = _pick(s, (1024, 512, 256, 128))

    def body(x_ref, w_ref, ab_ref, o_ref):
        du = lax.dot_general(x_ref[...], w_ref[...], _DN_NT, preferred_element_type=F32)
        g = du.astype(BF16).astype(F32)
        a = ab_ref[:, :f].astype(F32)
        b = ab_ref[:, f:].astype(F32)
        sg, sgm = _sigmoid_parts(a)
        silu = a * sg
        o_ref[:, :f] = (g * b * (sg + silu * sgm)).astype(BF16)
        o_ref[:, f:] = (g * silu).astype(BF16)

    return pl.pallas_call(
        body, name=name, grid=(s // tm, N_DEV),
        out_shape=jax.ShapeDtypeStruct(ab.shape, BF16),
        in_specs=[pl.BlockSpec((tm, d), lambda i, j: (i, 0)), pl.BlockSpec((f, d), lambda i, j: (j, 0)),
                  pl.BlockSpec((tm, 2 * f), lambda i, j: (i, j))],
        out_specs=pl.BlockSpec((tm, 2 * f), lambda i, j: (i, j)),
    )(dh, w_down, ab)


def _ple_fwd(h, gl, pp, name):
    s, d = h.shape
    ts = _row_tile(s)

    def body(h_ref, gl_ref, pp_ref, o_ref):
        sg, _ = _sigmoid_parts(gl_ref[...].astype(F32))
        o_ref[...] = h_ref[...] + sg * pp_ref[...].astype(F32)

    row = pl.BlockSpec((ts, d), lambda i: (i, 0))
    return pl.pallas_call(
        body, name=name, grid=(s // ts,),
        out_shape=jax.ShapeDtypeStruct((s, d), F32),
        in_specs=[row, row, row], out_specs=row,
    )(h, gl, pp)


def _ple_bwd(dh, gl, pp, name):
    s, d = dh.shape
    ts = _row_tile(s)

    def body(dh_ref, gl_ref, pp_ref, dgl_ref, dpp_ref):
        g = dh_ref[...]
        sg, sgm = _sigmoid_parts(gl_ref[...].astype(F32))
        dpp_ref[...] = (g * sg).astype(BF16)
        dgl_ref[...] = (g * pp_ref[...].astype(F32) * (sg * sgm)).astype(BF16)

    row = pl.BlockSpec((ts, d), lambda i: (i, 0))
    shp = jax.ShapeDtypeStruct((s, d), BF16)
    return pl.pallas_call(
        body, name=name, grid=(s // ts,),
        out_shape=(shp, shp), in_specs=[row, row, row], out_specs=(row, row),
    )(dh, gl, pp)


def _final_norm_loss(h, g, target):
    s, d = h.shape
    ts = _row_tile(s)

    def body(h_ref, g_ref, t_ref, loss_ref, dh_ref, dg_ref):
        i = pl.program_id(0)
        x = h_ref[...]
        gain = g_ref[...]
        r = lax.rsqrt(jnp.mean(x * x, axis=-1, keepdims=True) + EPS)
        xr = x * r
        err = xr * gain - t_ref[...]
        dy = err * (1.0 / d)
        u = dy * gain
        dh_ref[...] = r * (u - xr * jnp.mean(xr * u, axis=-1, keepdims=True))

        @pl.when(i == 0)
        def _():
            dg_ref[...] = jnp.zeros_like(dg_ref)
            loss_ref[...] = jnp.zeros_like(loss_ref)

        dg_ref[...] += jnp.sum(dy * xr, axis=0, keepdims=True)
        tok = jnp.mean(err * err, axis=-1, keepdims=True)
        loss_ref[...] += 0.5 * jnp.sum(tok, axis=0, keepdims=True)

    row = pl.BlockSpec((ts, d), lambda i: (i, 0))
    vec = pl.BlockSpec((1, d), lambda i: (0, 0))
    return pl.pallas_call(
        body, name="final_norm_loss", grid=(s // ts,),
        out_shape=(jax.ShapeDtypeStruct((1, LANES), F32), jax.ShapeDtypeStruct((s, d), F32),
                   jax.ShapeDtypeStruct((1, d), F32)),
        in_specs=[row, vec, row],
        out_specs=(pl.BlockSpec((1, LANES), lambda i: (0, 0)), row, vec),
        compiler_params=pltpu.CompilerParams(dimension_semantics=("arbitrary",)),
    )(h, g, target)


def _rope(xin, w, cos, sin_signed, sign, name):
    s = xin.shape[0]
    ts = _pick(s, (512, 256, 128))

    def body(x_ref, c_ref, s_ref, o_ref):
        cos_b, sin_b = c_ref[...], s_ref[...]
        lane = lax.broadcasted_iota(jnp.int32, cos_b.shape, 1)
        low = (lane & (HEAD_DIM - 1)) < (HEAD_DIM // 2)
        for j in range(w // LANES):
            cols = slice(j * LANES, (j + 1) * LANES)
            x = x_ref[:, cols].astype(F32)
            swapped = jnp.where(low, pltpu.roll(x, LANES - HEAD_DIM // 2, 1), pltpu.roll(x, HEAD_DIM // 2, 1))
            o_ref[:, cols] = (x * cos_b + sign * (swapped * sin_b)).astype(BF16)

    blk = pl.BlockSpec((ts, w), lambda i: (i, 0))
    tab = pl.BlockSpec((ts, LANES), lambda i: (i, 0))
    return pl.pallas_call(
        body, name=name, grid=(s // ts,),
        out_shape=jax.ShapeDtypeStruct((s, w), BF16),
        in_specs=[blk, tab, tab], out_specs=blk,
    )(xin, cos, sin_signed)


def _fgate_fwd(flog, bias):
    s, w = flog.shape

    def body(x_ref, b_ref, o_ref):
        rowi = lax.broadcasted_iota(jnp.int32, (8, w), 0)
        b = b_ref[...]

        def step(g, carry):
            sl = pl.ds(pl.multiple_of(g * 8, 8), 8)
            x = x_ref[sl, :] + b
            lf = jnp.minimum(x, 0.0) - jnp.log1p(jnp.exp(-jnp.abs(x)))
            for sh in (1, 2, 4):
                lf = lf + jnp.where(rowi >= sh, pltpu.roll(lf, sh, 0), 0.0)
            out = lf + carry
            o_ref[sl, :] = out
            return jnp.broadcast_to(out[7:8, :], (8, w))

        lax.fori_loop(0, s // 8, step, jnp.zeros((8, w), F32))

    vm = pl.BlockSpec(memory_space=pltpu.VMEM)
    return pl.pallas_call(
        body, name="fgate_fwd", out_shape=jax.ShapeDtypeStruct((s, w), F32),
        in_specs=[vm, vm], out_specs=vm,
    )(flog, bias)


def _fgate_bwd(gcum, flog, bias):
    s, w = flog.shape

    def body(g_ref, x_ref, b_ref, o_ref, db_ref):
        rowi = lax.broadcasted_iota(jnp.int32, (8, w), 0)
        lane = lax.broadcasted_iota(jnp.int32, (8, w), 1)
        b = b_ref[...]

        def step(t, carry):
            run, dbsum = carry
            g = s // 8 - 1 - t
            sl = pl.ds(pl.multiple_of(g * 8, 8), 8)
            c = g_ref[sl, :]
            for sh in (1, 2, 4):
                c = c + jnp.where(rowi < 8 - sh, pltpu.roll(c, 8 - sh, 0), 0.0)
            c = c + run
            _, sgm = _sigmoid_parts(x_ref[sl, :] + b)
            dl = jnp.where(lane < N_FOX, c * sgm, 0.0)
            o_ref[sl, :] = dl.astype(BF16)
            return jnp.broadcast_to(c[0:1, :], (8, w)), dbsum + dl

        _, dbsum = lax.fori_loop(0, s // 8, step, (jnp.zeros((8, w), F32), jnp.zeros((8, w), F32)))
        db_ref[...] = jnp.sum(dbsum, axis=0, keepdims=True)

    vm = pl.BlockSpec(memory_space=pltpu.VMEM)
    return pl.pallas_call(
        body, name="fgate_bwd",
        out_shape=(jax.ShapeDtypeStruct((s, w), BF16), jax.ShapeDtypeStruct((1, w), F32)),
        in_specs=[vm, vm, vm], out_specs=(vm, vm),
    )(gcum, flog, bias)


_DN_NT = (((1,), (1,)), ((), ()))
_DN_TN = (((0,), (0,)), ((), ()))


def _head_mask(shape, hh):
    lane = lax.broadcasted_iota(jnp.int32, shape, 1)
    return (lane >= HEAD_DIM * hh) & (lane < HEAD_DIM * (hh + 1))


SKIP_BELOW = -110.0


def _sweep_left(i, carry, tile, go_on):
    def flag(j, c):
        return jnp.logical_and(j >= 0, go_on(jnp.maximum(j, 0), c)).astype(jnp.int32)

    def body(st):
        j, _, c = st
        c = tile(j, c)
        return j - 1, flag(j - 1, c), c

    return lax.while_loop(lambda st: st[1] > 0, body, (i - 1, flag(i - 1, carry), carry))[2]


def _sweep_left_pair(i, carries, tiles, go_ons):
    return _sweep_left(
        i, tuple(carries),
        lambda j, c: tuple(t(j, ch) for t, ch in zip(tiles, c)),
        lambda j, c: jnp.logical_or(go_ons[0](j, c[0]), go_ons[1](j, c[1])))


def _key_norm_max(k_ref, kn_ref):
    k2 = k_ref[...].astype(F32)
    sq = k2 * k2
    for hh in range(2):
        n2 = jnp.sum(jnp.where(_head_mask(sq.shape, hh), sq, 0.0), axis=1, keepdims=True)
        kn_ref[hh] = jnp.broadcast_to(jnp.sqrt(jnp.max(n2, axis=0, keepdims=True)), kn_ref.shape[1:])


def _fox_fwd(proj, ccol, crow, gather):
    s = proj.shape[0]
    blk = min(ATTN_BLOCK, s)
    nq = s // blk
    npair = N_FOX // 2
    ng = len(gather)

    def body(*refs):
        q_ref, k_ref, v_ref, cc_ref, cr_ref = refs[:5]
        x_refs = refs[5:5 + ng]
        o_ref, a_ref = refs[5 + ng:7 + ng]
        g_refs = refs[7 + ng:7 + 2 * ng]
        kn_ref = refs[7 + 2 * ng]
        p_, i = pl.program_id(0), pl.program_id(1)
        issue, complete = _all_gather_steps(x_refs, g_refs, *refs[8 + 2 * ng:])

        @pl.when((p_ == 0) & (i == 0))
        def _():
            issue()

        @pl.when(i == 0)
        def _():
            _key_norm_max(k_ref, kn_ref)

        q2 = q_ref[...].astype(F32) * SCALE
        row = lax.broadcasted_iota(jnp.int32, (blk, blk), 0)
        col = lax.broadcasted_iota(jnp.int32, (blk, blk), 1)
        outs, heads = [], []
        for hh in range(2):
            hm = _head_mask(q2.shape, hh)
            qh = jnp.where(hm, q2, 0.0).astype(BF16)
            ct = cc_ref[hh][:, 0:1]
            qk_max = jnp.sqrt(jnp.sum(jnp.where(hm, q2 * q2, 0.0), axis=1, keepdims=True)) * kn_ref[hh][0:1, 0:1]

            def go_on(j, carry, ct=ct, qk_max=qk_max, hh=hh):
                bias_max = ct - jnp.min(cr_ref[2 * p_ + hh, j], axis=1, keepdims=True)
                return jnp.max(qk_max + bias_max - carry[0]) > SKIP_BELOW

            def tile(j, carry, masked, qh=qh, ct=ct, hh=hh):
                m, l, acc = carry
                sl = pl.ds(pl.multiple_of(j * blk, blk), blk)
                kb, vb = k_ref[sl, :], v_ref[sl, :]
                sc = lax.dot_general(qh, kb, _DN_NT, preferred_element_type=F32)
                sc = sc + (ct - cr_ref[2 * p_ + hh, j])
                if masked:
                    sc = jnp.where(col <= row, sc, NEG_INF)
                m_new = jnp.maximum(m, jnp.max(sc, axis=1, keepdims=True))
                alpha = jnp.exp(m - m_new)
                pm = jnp.exp(sc - m_new)
                l = alpha * l + jnp.sum(pm, axis=1, keepdims=True)
                acc = alpha * acc + jnp.dot(pm.astype(BF16), vb, preferred_element_type=F32)
                return m_new, l, acc

            init = (jnp.full((blk, 1), NEG_INF, F32), jnp.zeros((blk, 1), F32), jnp.zeros((blk, LANES), F32))
            heads.append((lambda j, c, tile=tile: tile(j, c, False), go_on, tile(i, init, True), ct))
        swept = _sweep_left_pair(i, [h[2] for h in heads], [h[0] for h in heads], [h[1] for h in heads])
        for hh, (m, l, acc) in enumerate(swept):
            outs.append(acc / l)
            a_ref[hh] = jnp.broadcast_to(heads[hh][3] - (m + jnp.log(l)), (blk, LANES))
        o_ref[...] = jnp.where(_head_mask(outs[0].shape, 0), outs[0], outs[1]).astype(BF16)

        @pl.when((p_ == npair - 1) & (i == nq - 1))
        def _():
            complete()

    seq = lambda base: pl.BlockSpec((s, LANES), lambda p, i: (0, base + p))
    res = pl.pallas_call(
        body, name="fox_fwd_ag", grid=(npair, nq),
        scratch_shapes=[pltpu.VMEM((2, 8, LANES), F32)] + _all_gather_scratch(ng),
        out_shape=(jax.ShapeDtypeStruct((s, FOX_W), BF16), jax.ShapeDtypeStruct((N_FOX, s, LANES), F32))
        + _gathered_shapes(gather),
        in_specs=[pl.BlockSpec((blk, LANES), lambda p, i: (i, p)), seq(npair), seq(2 * npair),
                  pl.BlockSpec((2, blk, LANES), lambda p, i: (p, i, 0)),
                  pl.BlockSpec((N_FOX, nq, 1, blk), lambda p, i: (0, 0, 0, 0))] + [_ANY] * ng,
        out_specs=(pl.BlockSpec((blk, LANES), lambda p, i: (i, p)),
                   pl.BlockSpec((2, blk, LANES), lambda p, i: (p, i, 0))) + tuple([_ANY] * ng),
        compiler_params=pltpu.CompilerParams(dimension_semantics=("arbitrary", "arbitrary")),
    )(proj, proj, proj, ccol, crow, *gather)
    return res[0], res[1], list(res[2:])


def _fox_bwd(proj, do, o, acol, crow, grads):
    s = proj.shape[0]
    blk = min(ATTN_BLOCK, s)
    nq = s // blk
    npair = N_FOX // 2
    ng = len(grads)

    def body(*refs):
        q_ref, k_ref, v_ref, do_ref, o_ref, a_ref, cr_ref = refs[:7]
        g_refs = refs[7:7 + ng]
        dq_ref, dk_ref, dv_ref, gc_ref = refs[7 + ng:11 + ng]
        x_refs = refs[11 + ng:11 + 2 * ng]
        dk_acc, dv_acc, kn_ref = refs[11 + 2 * ng:14 + 2 * ng]
        p_, i = pl.program_id(0), pl.program_id(1)
        issue, complete = _sibling_exchange_steps(g_refs, x_refs, *refs[14 + 2 * ng:])

        @pl.when((p_ == 0) & (i == 0))
        def _():
            issue()

        @pl.when(i == 0)
        def _():
            dk_acc[...] = jnp.zeros_like(dk_acc)
            dv_acc[...] = jnp.zeros_like(dv_acc)
            gc_ref[...] = jnp.zeros_like(gc_ref)
            _key_norm_max(k_ref, kn_ref)

        q2 = q_ref[...].astype(F32) * SCALE
        do2 = do_ref[...]
        prod = do2.astype(F32) * o_ref[...].astype(F32)
        row = lax.broadcasted_iota(jnp.int32, (blk, blk), 0)
        col = lax.broadcasted_iota(jnp.int32, (blk, blk), 1)
        dqs, heads = [], []
        for hh in range(2):
            hm = _head_mask(q2.shape, hh)
            qh = jnp.where(hm, q2, 0.0).astype(BF16)
            doh = jnp.where(hm, do2, jnp.zeros_like(do2))
            delta = jnp.sum(jnp.where(hm, prod, 0.0), axis=1, keepdims=True)
            at = a_ref[hh][:, 0:1]
            qk_max = jnp.sqrt(jnp.sum(jnp.where(hm, q2 * q2, 0.0), axis=1, keepdims=True)) * kn_ref[hh][0:1, 0:1]

            def go_on(j, carry, at=at, qk_max=qk_max, hh=hh):
                bias_max = at - jnp.min(cr_ref[2 * p_ + hh, j], axis=1, keepdims=True)
                return jnp.max(qk_max + bias_max) > SKIP_BELOW

            def tile(j, carry, masked, qh=qh, doh=doh, delta=delta, at=at, hh=hh):
                dq, rs = carry
                sl = pl.ds(pl.multiple_of(j * blk, blk), blk)
                kb, vb = k_ref[sl, :], v_ref[sl, :]
                sc = lax.dot_general(qh, kb, _DN_NT, preferred_element_type=F32)
                sc = sc + (at - cr_ref[2 * p_ + hh, j])
                if masked:
                    sc = jnp.where(col <= row, sc, NEG_INF)
                pm = jnp.exp(sc)
                dp = lax.dot_general(doh, vb, _DN_NT, preferred_element_type=F32)
                ds = pm * (dp - delta)
                dsb = ds.astype(BF16)
                dk_acc[sl, :] += lax.dot_general(dsb, qh, _DN_TN, preferred_element_type=F32)
                dv_acc[sl, :] += lax.dot_general(pm.astype(BF16), doh, _DN_TN, preferred_element_type=F32)
                gc_ref[hh, j] += -jnp.sum(ds, axis=0, keepdims=True)
                return dq + jnp.dot(dsb, kb, preferred_element_type=F32), rs + jnp.sum(ds, axis=1, keepdims=True)

            carry = tile(i, (jnp.zeros((blk, LANES), F32), jnp.zeros((blk, 1), F32)), True)
            heads.append((lambda j, c, tile=tile: tile(j, c, False), go_on, carry))
        swept = _sweep_left_pair(i, [h[2] for h in heads], [h[0] for h in heads], [h[1] for h in heads])
        for hh, (dq, rs) in enumerate(swept):
            dqs.append(dq)
            gc_ref[hh, i] += jnp.transpose(jnp.broadcast_to(rs, (blk, LANES)))[0:1, :]
        dq_ref[...] = (jnp.where(_head_mask(dqs[0].shape, 0), dqs[0], dqs[1]) * SCALE).astype(BF16)

        @pl.when(i == nq - 1)
        def _():
            dk_ref[...] = dk_acc[...].astype(BF16)
            dv_ref[...] = dv_acc[...].astype(BF16)

        @pl.when((p_ == npair - 1) & (i == nq - 1))
        def _():
            complete()

    seq = lambda base: pl.BlockSpec((s, LANES), lambda p, i: (0, base + p))
    qblk = lambda base: pl.BlockSpec((blk, LANES), lambda p, i: (i, base + p))
    rep = pl.BlockSpec((2, blk, LANES), lambda p, i: (p, i, 0))
    half = jax.ShapeDtypeStruct((s, FOX_W), BF16)
    res = pl.pallas_call(
        body, name="fox_bwd_rs", grid=(npair, nq),
        out_shape=(half, half, half, jax.ShapeDtypeStruct((N_FOX, nq, 1, blk), F32)) + _quarter_shapes(grads),
        in_specs=[qblk(0), seq(npair), seq(2 * npair), qblk(0), qblk(0), rep,
                  pl.BlockSpec((N_FOX, nq, 1, blk), lambda p, i: (0, 0, 0, 0))] + [_ANY] * ng,
        out_specs=(qblk(0), seq(0), seq(0), pl.BlockSpec((2, nq, 1, blk), lambda p, i: (p, 0, 0, 0)))
        + tuple([_ANY] * ng),
        scratch_shapes=[pltpu.VMEM((s, LANES), F32), pltpu.VMEM((s, LANES), F32), pltpu.VMEM((2, 8, LANES), F32)]
        + _sibling_exchange_scratch(ng),
        compiler_params=pltpu.CompilerParams(dimension_semantics=("arbitrary", "arbitrary")),
    )(proj, proj, proj, do, o, acol, crow, *grads)
    return res[0], res[1], res[2], res[3], list(res[4:])


def _sb_logs(z):
    neg = -(jnp.maximum(z, 0.0) + jnp.log1p(jnp.exp(-jnp.abs(z))))
    return neg, z + neg


def _split_dot(x, tri):
    hi = x.astype(BF16)
    lo = (x - hi.astype(F32)).astype(BF16)
    return jnp.dot(hi, tri, preferred_element_type=F32) + jnp.dot(lo, tri, preferred_element_type=F32)


def _sb_fwd(proj):
    s = proj.shape[0]
    blk = min(ATTN_BLOCK, s)
    nq = s // blk
    npair = N_SB // 2
    base = 3 * (N_FOX // 2)

    def body(q_ref, k_ref, v_ref, o_ref, r_ref):
        i = pl.program_id(1)
        q2 = q_ref[...].astype(F32) * SCALE
        row = lax.broadcasted_iota(jnp.int32, (blk, blk), 0)
        col = lax.broadcasted_iota(jnp.int32, (blk, blk), 1)
        strict = col < row
        tri = jnp.where(row > col, 1.0, 0.0).astype(BF16)
        lane = lax.broadcasted_iota(jnp.int32, (blk, LANES), 1)
        outs, heads = [], []
        for hh in range(2):
            qh = jnp.where(_head_mask(q2.shape, hh), q2, 0.0).astype(BF16)

            def tile(j, carry, masked, qh=qh):
                rsum, acc, rbuf = carry
                sl = pl.ds(pl.multiple_of(j * blk, blk), blk)
                kb, vb = k_ref[sl, :], v_ref[sl, :]
                z = lax.dot_general(qh, kb, _DN_NT, preferred_element_type=F32)
                l1m, lb = _sb_logs(z)
                if masked:
                    l1m = jnp.where(strict, l1m, 0.0)
                sx = _split_dot(l1m, tri)
                a = jnp.exp(lb + sx + rsum)
                if masked:
                    a = jnp.where(strict, a, 0.0)
                acc = acc + jnp.dot(a.astype(BF16), vb, preferred_element_type=F32)
                rbuf = jnp.where(lane == j, rsum, rbuf)
                return rsum + jnp.sum(l1m, axis=1, keepdims=True), acc, rbuf

            init = (jnp.zeros((blk, 1), F32), jnp.zeros((blk, LANES), F32), jnp.full((blk, LANES), NEG_INF, F32))
            heads.append((lambda j, c, tile=tile: tile(j, c, False), lambda j, c: jnp.max(c[0]) > SKIP_BELOW,
                          tile(i, init, True)))
        swept = _sweep_left_pair(i, [h[2] for h in heads], [h[0] for h in heads], [h[1] for h in heads])
        for hh, (_, acc, rbuf) in enumerate(swept):
            outs.append(acc)
            r_ref[hh] = rbuf
        o_ref[...] = jnp.where(_head_mask(outs[0].shape, 0), outs[0], outs[1]).astype(BF16)

    seq = lambda b: pl.BlockSpec((s, LANES), lambda p, i: (0, b + p))
    return pl.pallas_call(
        body, name="sb_fwd", grid=(npair, nq),
        out_shape=(jax.ShapeDtypeStruct((s, SB_W), BF16), jax.ShapeDtypeStruct((N_SB, s, LANES), F32)),
        in_specs=[pl.BlockSpec((blk, LANES), lambda p, i: (i, base + p)), seq(base + npair), seq(base + 2 * npair)],
        out_specs=(pl.BlockSpec((blk, LANES), lambda p, i: (i, p)),
                   pl.BlockSpec((2, blk, LANES), lambda p, i: (p, i, 0))),
        compiler_params=pltpu.CompilerParams(dimension_semantics=("parallel", "arbitrary")),
    )(proj, proj, proj)


def _sb_bwd(proj, do, rsave, parts):
    s = proj.shape[0]
    blk = min(ATTN_BLOCK, s)
    nq = s // blk
    npair = N_SB // 2
    base = 3 * (N_FOX // 2)
    ng = len(parts)

    def body(*refs):
        q_ref, k_ref, v_ref, do_ref, r_ref = refs[:5]
        p_refs = refs[5:5 + ng]
        dq_ref, dk_ref, dv_ref = refs[5 + ng:8 + ng]
        x_refs = refs[8 + ng:8 + 2 * ng]
        dk_acc, dv_acc = refs[8 + 2 * ng:10 + 2 * ng]
        p_, i = pl.program_id(0), pl.program_id(1)
        issue, complete = _chip_exchange_steps(p_refs, x_refs, *refs[10 + 2 * ng:])

        @pl.when((p_ == 0) & (i == 0))
        def _():
            issue()

        @pl.when(i == 0)
        def _():
            dk_acc[...] = jnp.zeros_like(dk_acc)
            dv_acc[...] = jnp.zeros_like(dv_acc)

        q2 = q_ref[...].astype(F32) * SCALE
        do2 = do_ref[...]
        row = lax.broadcasted_iota(jnp.int32, (blk, blk), 0)
        col = lax.broadcasted_iota(jnp.int32, (blk, blk), 1)
        strict = col < row
        tri_suffix = jnp.where(row > col, 1.0, 0.0).astype(BF16)
        tri_prefix = jnp.where(row < col, 1.0, 0.0).astype(BF16)
        lane = lax.broadcasted_iota(jnp.int32, (blk, LANES), 1)
        heads = []
        for hh in range(2):
            hm = _head_mask(q2.shape, hh)
            qh = jnp.where(hm, q2, 0.0).astype(BF16)
            doh = jnp.where(hm, do2, jnp.zeros_like(do2))
            rbuf = r_ref[hh]

            def tile(j, carry, masked, qh=qh, doh=doh, rbuf=rbuf):
                pre, dq = carry
                sl = pl.ds(pl.multiple_of(j * blk, blk), blk)
                kb, vb = k_ref[sl, :], v_ref[sl, :]
                z = lax.dot_general(qh, kb, _DN_NT, preferred_element_type=F32)
                l1m, lb = _sb_logs(z)
                beta, one_m_beta = _sigmoid_parts(z)
                if masked:
                    l1m = jnp.where(strict, l1m, 0.0)
                sx = _split_dot(l1m, tri_suffix)
                rj = jnp.sum(jnp.where(lane == j, rbuf, 0.0), axis=1, keepdims=True)
                a = jnp.exp(lb + sx + rj)
                if masked:
                    a = jnp.where(strict, a, 0.0)
                da = lax.dot_general(doh, vb, _DN_NT, preferred_element_type=F32)
                g = a * da
                px = _split_dot(g, tri_prefix) + pre
                dz = g * one_m_beta - beta * px
                if masked:
                    dz = jnp.where(strict, dz, 0.0)
                dzb = dz.astype(BF16)
                dk_acc[sl, :] += lax.dot_general(dzb, qh, _DN_TN, preferred_element_type=F32)
                dv_acc[sl, :] += lax.dot_general(a.astype(BF16), doh, _DN_TN, preferred_element_type=F32)
                return pre + jnp.sum(g, axis=1, keepdims=True), dq + jnp.dot(dzb, kb, preferred_element_type=F32)

            reach = jnp.max(rbuf, axis=0, keepdims=True)
            dead = (reach <= SKIP_BELOW) & (lane[0:1, :] <= i)
            heads.append((tile, jnp.sum(jnp.where(dead, 1.0, 0.0)).astype(jnp.int32)))
        first = jnp.minimum(heads[0][1], heads[1][1])
        zero = (jnp.zeros((blk, 1), F32), jnp.zeros((blk, LANES), F32))
        carries = lax.fori_loop(first, i, lambda t, c: tuple(h[0](t, ch, False) for h, ch in zip(heads, c)),
                                (zero, zero))
        dqs = [h[0](i, ch, True)[1] for h, ch in zip(heads, carries)]
        dq_ref[...] = (jnp.where(_head_mask(dqs[0].shape, 0), dqs[0], dqs[1]) * SCALE).astype(BF16)

        @pl.when(i == nq - 1)
        def _():
            dk_ref[...] = dk_acc[...].astype(BF16)
            dv_ref[...] = dv_acc[...].astype(BF16)

        @pl.when((p_ == npair - 1) & (i == nq - 1))
        def _():
            complete()

    seq = lambda b: pl.BlockSpec((s, LANES), lambda p, i: (0, b + p))
    qblk = lambda b: pl.BlockSpec((blk, LANES), lambda p, i: (i, b + p))
    half = jax.ShapeDtypeStruct((s, SB_W), BF16)
    res = pl.pallas_call(
        body, name="sb_bwd_rs", grid=(npair, nq),
        out_shape=(half, half, half) + _same_shapes(parts),
        in_specs=[qblk(base), seq(base + npair), seq(base + 2 * npair), qblk(npair),
                  pl.BlockSpec((2, blk, LANES), lambda p, i: (p, i, 0))] + [_ANY] * ng,
        out_specs=(qblk(0), seq(0), seq(0)) + tuple([_ANY] * ng),
        scratch_shapes=[pltpu.VMEM((s, LANES), F32), pltpu.VMEM((s, LANES), F32)] + _chip_exchange_scratch(ng),
        compiler_params=pltpu.CompilerParams(dimension_semantics=("arbitrary", "arbitrary")),
    )(proj, proj, proj, do, rsave, *parts)
    return res[0], res[1], res[2], list(res[3:])


SWA_ROWS = 256
Q_W = N_Q * HEAD_DIM
KV_W = N_KV * HEAD_DIM
KV_PAIRS = KV_W // LANES
Q_PER_KVPAIR = Q_W // KV_PAIRS


def _lane_swap(x):
    xf = x.astype(F32)
    parts = [pltpu.roll(xf[:, j * LANES:(j + 1) * LANES], HEAD_DIM, 1) for j in range(x.shape[1] // LANES)]
    return (parts[0] if len(parts) == 1 else jnp.concatenate(parts, axis=1)).astype(x.dtype)


def _swa_specs(s):
    w = WINDOW
    ts = min(SWA_ROWS, s)
    nw = ts // w
    qcols = pl.BlockSpec((ts, Q_PER_KVPAIR), lambda kp, i: (i, kp))
    kbase, vbase = Q_W // LANES, (Q_W + KV_W) // LANES
    prev = lambda base: pl.BlockSpec((w, LANES), lambda kp, i: (jnp.maximum(i * nw - 1, 0), base + kp))
    cur = lambda base: pl.BlockSpec((ts, LANES), lambda kp, i: (i, base + kp))
    vec = pl.BlockSpec((1, Q_PER_KVPAIR), lambda kp, i: (0, kp))
    return ts, nw, qcols, prev(kbase), cur(kbase), prev(vbase), cur(vbase), vec


def _swa_fwd(qk, vsrc, sink_row):
    s = qk.shape[0]
    w = WINDOW
    ts, nw, qcols, kprev, kcur, vprev, vcur, vec = _swa_specs(s)

    def body(q_ref, kp_ref, kc_ref, vp_ref, vc_ref, s_ref, o_ref, lse_ref):
        i = pl.program_id(1)
        k2 = jnp.concatenate([kp_ref[...], kc_ref[...]], axis=0)
        v2 = jnp.concatenate([vp_ref[...], vc_ref[...]], axis=0)
        ksw, vsw = _lane_swap(k2), _lane_swap(v2)
        row = lax.broadcasted_iota(jnp.int32, (w, 2 * w), 0)
        col = lax.broadcasted_iota(jnp.int32, (w, 2 * w), 1)
        band = (col > row) & (col <= row + w)
        first_half = _head_mask((w, LANES), 0)
        for u in range(nw):
            valid = band if u > 0 else band & ((col >= w) | (i > 0))
            rows, keys = slice(u * w, (u + 1) * w), slice(u * w, (u + 2) * w)
            for pr in range(Q_PER_KVPAIR // LANES):
                gh = pr // 2
                cols = slice(pr * LANES, (pr + 1) * LANES)
                q2 = q_ref[rows, cols].astype(F32) * SCALE
                outs, lses = [], []
                for hh in range(2):
                    kk = (k2 if hh == gh else ksw)[keys, :]
                    vv = (v2 if hh == gh else vsw)[keys, :]
                    qm = jnp.where(_head_mask(q2.shape, hh), q2, 0.0).astype(BF16)
                    sc = jnp.where(valid, lax.dot_general(qm, kk, _DN_NT, preferred_element_type=F32), NEG_INF)
                    sink = s_ref[:, pr * LANES + HEAD_DIM * hh: pr * LANES + HEAD_DIM * hh + 1]
                    m = jnp.maximum(jnp.max(sc, axis=1, keepdims=True), sink)
                    e = jnp.exp(sc - m)
                    l = jnp.sum(e, axis=1, keepdims=True) + jnp.exp(sink - m)
                    outs.append(jnp.dot(e.astype(BF16), vv, preferred_element_type=F32) / l)
                    lses.append(m + jnp.log(l))
                o_ref[rows, cols] = jnp.where(first_half, outs[0], outs[1]).astype(BF16)
                lse_ref[rows, cols] = jnp.where(first_half, lses[0], lses[1])

    return pl.pallas_call(
        body, name="swa_fwd", grid=(KV_PAIRS, s // ts),
        out_shape=(jax.ShapeDtypeStruct((s, Q_W), BF16), jax.ShapeDtypeStruct((s, Q_W), F32)),
        in_specs=[qcols, kprev, kcur, vprev, vcur, vec], out_specs=(qcols, qcols),
    )(qk, qk, qk, vsrc, vsrc, sink_row)


def _swa_bwd_dq(qk, vsrc, sink_row, do, o, lse):
    s = qk.shape[0]
    w = WINDOW
    ts, nw, qcols, kprev, kcur, vprev, vcur, vec = _swa_specs(s)

    def body(q_ref, kp_ref, kc_ref, vp_ref, vc_ref, s_ref, do_ref, o_ref, lse_ref, dq_ref, dsink_ref):
        i = pl.program_id(1)

        @pl.when(i == 0)
        def _():
            dsink_ref[...] = jnp.zeros_like(dsink_ref)

        k2 = jnp.concatenate([kp_ref[...], kc_ref[...]], axis=0)
        v2 = jnp.concatenate([vp_ref[...], vc_ref[...]], axis=0)
        ksw, vsw = _lane_swap(k2), _lane_swap(v2)
        row = lax.broadcasted_iota(jnp.int32, (w, 2 * w), 0)
        col = lax.broadcasted_iota(jnp.int32, (w, 2 * w), 1)
        band = (col > row) & (col <= row + w)
        first_half = _head_mask((w, LANES), 0)
        lane_all = lax.broadcasted_iota(jnp.int32, (1, Q_PER_KVPAIR), 1)
        dsink = jnp.zeros((1, Q_PER_KVPAIR), F32)
        for u in range(nw):
            valid = band if u > 0 else band & ((col >= w) | (i > 0))
            rows, keys = slice(u * w, (u + 1) * w), slice(u * w, (u + 2) * w)
            for pr in range(Q_PER_KVPAIR // LANES):
                gh = pr // 2
                cols = slice(pr * LANES, (pr + 1) * LANES)
                q2 = q_ref[rows, cols].astype(F32) * SCALE
                do2 = do_ref[rows, cols]
                prod = do2.astype(F32) * o_ref[rows, cols].astype(F32)
                lse2 = lse_ref[rows, cols]
                dqs = []
                for hh in range(2):
                    hm = _head_mask(q2.shape, hh)
                    lo = pr * LANES + HEAD_DIM * hh
                    kk = (k2 if hh == gh else ksw)[keys, :]
                    vv = (v2 if hh == gh else vsw)[keys, :]
                    qm = jnp.where(hm, q2, 0.0).astype(BF16)
                    sc = jnp.where(valid, lax.dot_general(qm, kk, _DN_NT, preferred_element_type=F32), NEG_INF)
                    lse_c = lse2[:, HEAD_DIM * hh: HEAD_DIM * hh + 1]
                    pm = jnp.exp(sc - lse_c)
                    doh = jnp.where(hm, do2, jnp.zeros_like(do2))
                    delta = jnp.sum(jnp.where(hm, prod, 0.0), axis=1, keepdims=True)
                    ds = pm * (lax.dot_general(doh, vv, _DN_NT, preferred_element_type=F32) - delta)
                    dqs.append(jnp.dot(ds.astype(BF16), kk, preferred_element_type=F32))
                    part = jnp.sum(-jnp.exp(s_ref[:, lo:lo + 1] - lse_c) * delta, axis=0, keepdims=True)
                    dsink = dsink + jnp.where((lane_all >= lo) & (lane_all < lo + HEAD_DIM), part, 0.0)
                dq_ref[rows, cols] = (jnp.where(first_half, dqs[0], dqs[1]) * SCALE).astype(BF16)
        dsink_ref[...] += dsink

    return pl.pallas_call(
        body, name="swa_bwd_dq", grid=(KV_PAIRS, s // ts),
        out_shape=(jax.ShapeDtypeStruct((s, Q_W), BF16), jax.ShapeDtypeStruct((1, Q_W), F32)),
        in_specs=[qcols, kprev, kcur, vprev, vcur, vec, qcols, qcols, qcols], out_specs=(qcols, vec),
        compiler_params=pltpu.CompilerParams(dimension_semantics=("parallel", "arbitrary")),
    )(qk, qk, qk, vsrc, vsrc, sink_row, do, o, lse)


def _swa_bwd_dkv(qk, vsrc, do, o, lse):
    s = qk.shape[0]
    w = WINDOW
    ts = min(SWA_ROWS, s)
    nw = ts // w
    nstep = s // ts
    last_window = s // w - 1

    def body(k_ref, v_ref, qc_ref, qn_ref, doc_ref, don_ref, oc_ref, on_ref, lc_ref, ln_ref, dk_ref, dv_ref):
        j = pl.program_id(1)
        cat = lambda a_ref, b_ref: jnp.concatenate([a_ref[...], b_ref[...]], axis=0)
        qcat, docat, ocat, lcat = cat(qc_ref, qn_ref), cat(doc_ref, don_ref), cat(oc_ref, on_ref), cat(lc_ref, ln_ref)
        qsw, dosw = _lane_swap(qcat), _lane_swap(docat)
        row = lax.broadcasted_iota(jnp.int32, (2 * w, w), 0)
        col = lax.broadcasted_iota(jnp.int32, (2 * w, w), 1)
        band = (col <= row) & (row < col + w)
        has_next = j + 1 < nstep
        for wi in range(nw):
            valid = band if wi < nw - 1 else band & ((row < w) | has_next)
            keys, qrows = slice(wi * w, (wi + 1) * w), slice(wi * w, (wi + 2) * w)
            kw, vw = k_ref[keys, :], v_ref[keys, :]
            dk = jnp.zeros((w, LANES), F32)
            dv = jnp.zeros((w, LANES), F32)
            for pr in range(Q_PER_KVPAIR // LANES):
                gh = pr // 2
                cols = slice(pr * LANES, (pr + 1) * LANES)
                prod = docat[qrows, cols].astype(F32) * ocat[qrows, cols].astype(F32)
                lse2 = lcat[qrows, cols]
                to_kv = _head_mask(prod.shape, gh)
                for hh in range(2):
                    q_src, do_src = (qcat, docat) if hh == gh else (qsw, dosw)
                    q_al = jnp.where(to_kv, q_src[qrows, cols].astype(F32) * SCALE, 0.0).astype(BF16)
                    do_al = jnp.where(to_kv, do_src[qrows, cols], jnp.zeros((2 * w, LANES), BF16))
                    sc = jnp.where(valid, lax.dot_general(q_al, kw, _DN_NT, preferred_element_type=F32), NEG_INF)
                    pm = jnp.exp(sc - lse2[:, HEAD_DIM * hh: HEAD_DIM * hh + 1])
                    delta = jnp.sum(jnp.where(_head_mask(prod.shape, hh), prod, 0.0), axis=1, keepdims=True)
                    ds = pm * (lax.dot_general(do_al, vw, _DN_NT, preferred_element_type=F32) - delta)
                    dk = dk + lax.dot_general(ds.astype(BF16), q_al, _DN_TN, preferred_element_type=F32)
                    dv = dv + lax.dot_general(pm.astype(BF16), do_al, _DN_TN, preferred_element_type=F32)
            dk_ref[keys, :] = dk.astype(BF16)
            dv_ref[keys, :] = dv.astype(BF16)

    kbase, vbase = Q_W // LANES, (Q_W + KV_W) // LANES
    kv = lambda base: pl.BlockSpec((ts, LANES), lambda kp, j: (j, base + kp))
    same = pl.BlockSpec((ts, Q_PER_KVPAIR), lambda kp, j: (j, kp))
    nxt = pl.BlockSpec((w, Q_PER_KVPAIR), lambda kp, j: (jnp.minimum((j + 1) * nw, last_window), kp))
    out = pl.BlockSpec((ts, LANES), lambda kp, j: (j, kp))
    shp = jax.ShapeDtypeStruct((s, KV_W), BF16)
    return pl.pallas_call(
        body, name="swa_bwd_dkv", grid=(KV_PAIRS, nstep),
        out_shape=(shp, shp),
        in_specs=[kv(kbase), kv(vbase), same, nxt, same, nxt, same, nxt, same, nxt], out_specs=(out, out),
    )(qk, vsrc, qk, qk, do, do, o, o, lse, lse)


def _ffn_ple_fwd(h1, p_l, g_ffn, g_ple, w_gu, w_down, w_pg, w_pp, tag):
    hn2 = _rms_fwd(h1, g_ffn, f"rms_ffn_{tag}")
    ab, u = _gate_up_swiglu(hn2, w_gu, f"mm_gate_up_swiglu_{tag}")
    h2 = _mm(u, w_down, out_dtype=F32, res=h1, name=f"mm_down_{tag}")
    hn3 = _rms_fwd(h2, g_ple, f"rms_ple_{tag}")
    gl = _mm(hn3, w_pg, name=f"mm_ple_gate_{tag}")
    pp = _mm(p_l, w_pp, name=f"mm_ple_proj_{tag}")
    h3 = _ple_fwd(h2, gl, pp, f"ple_{tag}")
    return h3, dict(h1=h1, hn2=hn2, ab=ab, u=u, h2=h2, hn3=hn3, gl=gl, pp=pp)


def _ffn_ple_bwd(dh3, sv, p_l, g_ffn, g_ple, w_gu, w_down, w_pg, tag):
    dgl, dpp = _ple_bwd(dh3, sv["gl"], sv["pp"], f"ple_bwd_{tag}")
    d_wpp = _mm(p_l, dpp, ta=True, out_dtype=F32, name=f"mm_dw_ple_proj_{tag}")
    d_wpg = _mm(sv["hn3"], dgl, ta=True, out_dtype=F32, name=f"mm_dw_ple_gate_{tag}")
    dhn3 = _mm(dgl, w_pg, tb=True, out_dtype=F32, name=f"mm_dx_ple_gate_{tag}")
    dh2, dh2b, dg_ple = _rms_bwd(sv["h2"], g_ple, [dhn3], dh3, f"rms_ple_bwd_{tag}", True)
    d_wdown = _mm(sv["u"], dh2b, ta=True, out_dtype=F32, name=f"mm_dw_down_{tag}")
    dab = _down_t_swiglu_bwd(dh2b, w_down, sv["ab"], f"mm_dx_down_swiglu_bwd_{tag}")
    d_wgu = _mm(sv["hn2"], dab, ta=True, out_dtype=F32, group="out", group_width=2 * HID_PAD,
                name=f"mm_dw_gate_up_{tag}")
    dhn2 = _mm(dab, w_gu, tb=True, out_dtype=F32, group="k", name=f"mm_dx_gate_up_{tag}")
    dh1, dh1b, dg_ffn = _rms_bwd(sv["h1"], g_ffn, [dhn2], dh2, f"rms_ffn_bwd_{tag}", True)
    return dh1, dh1b, dict(d_wpp=d_wpp, d_wpg=d_wpg, d_wdown=d_wdown, d_wgu=d_wgu, dg_ple=dg_ple, dg_ffn=dg_ffn)


def _row_form(cum, blk):
    s = cum.shape[0]
    return cum[:, :N_FOX].T.reshape(N_FOX, s // blk, 1, blk)


def _col_form(cum):
    s = cum.shape[0]
    return jnp.broadcast_to(cum[:, :N_FOX].T[:, :, None], (N_FOX, s, LANES))


def kernel(x, p, positions, norm_mix, norm_ffn, norm_ple, norm_final, ev_w_in, ev_b_f, ev_w_out, od_w_in, od_sinks, od_w_out, ffn_w_gate, ffn_w_up, ffn_w_down, ple_w_proj, ple_w_gate, loss_target, m_norm_mix, m_norm_ffn, m_norm_ple, m_norm_final, m_ev_w_in, m_ev_b_f, m_ev_w_out, m_od_w_in, m_od_sinks, m_od_w_out, m_ffn_w_gate, m_ffn_w_up, m_ffn_w_down, m_ple_w_proj, m_ple_w_gate, v_norm_mix, v_norm_ffn, v_norm_ple, v_norm_final, v_ev_w_in, v_ev_b_f, v_ev_w_out, v_od_w_in, v_od_sinks, v_od_w_out, v_ffn_w_gate, v_ffn_w_up, v_ffn_w_down, v_ple_w_proj, v_ple_w_gate):
    s = x.shape[1]
    blk = min(ATTN_BLOCK, s)
    big_w = [ev_w_in, ev_w_out, od_w_in, od_w_out, ffn_w_gate, ffn_w_up, ffn_w_down, ple_w_proj, ple_w_gate]
    big_m = [m_ev_w_in, m_ev_w_out, m_od_w_in, m_od_w_out, m_ffn_w_gate, m_ffn_w_up, m_ffn_w_down, m_ple_w_proj, m_ple_w_gate]
    big_v = [v_ev_w_in, v_ev_w_out, v_od_w_in, v_od_w_out, v_ffn_w_gate, v_ffn_w_up, v_ffn_w_down, v_ple_w_proj, v_ple_w_gate]

    s_rows, s_gu, s_evin, s_odin, s_pp = _group_shards(*big_w, BF16)
    (g_evin,) = _all_gather_weights([s_evin])
    d = D_MODEL
    w_in0 = _from_owner_cols(g_evin)
    w_qkv = w_in0[:, :QKV_W]
    w_f = jnp.pad(w_in0[:, QKV_W:], ((0, 0), (0, LANES - N_FOX)))

    h0 = x[0]
    target = loss_target[0]
    p_b = [p[l, 0].astype(BF16) for l in range(2)]
    g_mix = [norm_mix[l][None, :] for l in range(2)]
    g_ffn = [norm_ffn[l][None, :] for l in range(2)]
    g_ple = [norm_ple[l][None, :] for l in range(2)]
    b_f = jnp.pad(ev_b_f, ((0, 0), (0, LANES - N_FOX)))

    half = HEAD_DIM // 2
    inv = ROPE_THETA ** (-jnp.arange(half, dtype=F32) / half)
    ang = positions[0].astype(F32)[:, None] * inv
    cos_t = jnp.tile(jnp.cos(ang), (1, 4))
    sin_t = jnp.tile(jnp.concatenate([-jnp.sin(ang), jnp.sin(ang)], axis=1), (1, 2))

    hn1 = _rms_fwd(h0, g_mix[0], "rms_mix_0")
    proj0 = _mm(hn1, w_qkv, name="mm_in_0")
    flog = _mm(hn1, w_f, out_dtype=F32, name="mm_fgate_0")
    cum = _fgate_fwd(flog, b_f)
    crow = _row_form(cum, blk)
    o_fox, acol, (g_rows, g_gu, g_odin, g_pp) = _fox_fwd(proj0, _col_form(cum), crow, [s_rows, s_gu, s_odin, s_pp])
    w_oi = _from_owner_cols(g_odin)
    w_eo = g_rows[:, 0:128].reshape(d, d)
    w_oo = g_rows[:, 128:256].reshape(d, d)
    w_down = [g_rows[:, 256 + HID_PAD * l: 256 + HID_PAD * (l + 1)].reshape(D_FF_PAD, d) for l in range(2)]
    w_pg = [g_rows[:, 1024 + 128 * l: 1152 + 128 * l].reshape(d, d) for l in range(2)]
    w_gu = [g_gu[:, d * l: d * (l + 1)] for l in range(2)]
    w_pp = [_from_owner_cols(g_pp[:, PLE_DIM * l: PLE_DIM * (l + 1)]) for l in range(2)]
    o_sb, rsave = _sb_fwd(proj0)
    o0 = jnp.concatenate([o_fox, o_sb], axis=1)
    h1 = _mm(o0, w_eo, out_dtype=F32, res=h0, name="mm_out_0")
    h3, sv0 = _ffn_ple_fwd(h1, p_b[0], g_ffn[0], g_ple[0], w_gu[0], w_down[0], w_pg[0], w_pp[0], "0")

    hn1b = _rms_fwd(h3, g_mix[1], "rms_mix_1")
    proj1 = _mm(hn1b, w_oi, name="mm_in_1")
    qk_r = _rope(proj1, Q_W + KV_W, cos_t, sin_t, 1.0, "rope_fwd")
    sink_row = jnp.repeat(od_sinks[0], HEAD_DIM)[None, :]
    o1, lse1 = _swa_fwd(qk_r, proj1, sink_row)
    h4 = _mm(o1, w_oo, out_dtype=F32, res=h3, name="mm_out_1")
    h6, sv1 = _ffn_ple_fwd(h4, p_b[1], g_ffn[1], g_ple[1], w_gu[1], w_down[1], w_pg[1], w_pp[1], "1")

    loss_part, dh6, dg_final = _final_norm_loss(h6, norm_final[None, :], target)

    dh4, dh4b, gr1 = _ffn_ple_bwd(dh6, sv1, p_b[1], g_ffn[1], g_ple[1], w_gu[1], w_down[1], w_pg[1], "1")
    do1 = _mm(dh4b, w_oo, tb=True, name="mm_dx_out_1")
    d_woo = _mm(o1, dh4b, ta=True, out_dtype=F32, name="mm_dw_out_1")
    dq1, dsink_row = _swa_bwd_dq(qk_r, proj1, sink_row, do1, o1, lse1)
    dk1, dv1 = _swa_bwd_dkv(qk_r, proj1, do1, o1, lse1)
    dqk = _rope(jnp.concatenate([dq1, dk1], axis=1), Q_W + KV_W, cos_t, sin_t, -1.0, "rope_bwd")
    dproj1 = jnp.concatenate([dqk, dv1], axis=1)
    d_woi = _mm(hn1b, dproj1, ta=True, out_dtype=F32, name="mm_dw_in_1")
    dhn1b = _mm(dproj1, w_oi, tb=True, out_dtype=F32, name="mm_dx_in_1")
    dh3, dg_mix1 = _rms_bwd(h3, g_mix[1], [dhn1b], dh4, "rms_mix_bwd_1", False)

    dh1, dh1b, gr0 = _ffn_ple_bwd(dh3, sv0, p_b[0], g_ffn[0], g_ple[0], w_gu[0], w_down[0], w_pg[0], "0")
    do0 = _mm(dh1b, w_eo, tb=True, name="mm_dx_out_0")
    d_weo = _mm(o0, dh1b, ta=True, out_dtype=F32, name="mm_dw_out_0")
    by_rows = lambda g, r: g.reshape(N_DEV, r, d)
    early = [
        jnp.concatenate([by_rows(d_weo, 128), by_rows(d_woo, 128), by_rows(gr0["d_wdown"], HID_PAD),
                         by_rows(gr1["d_wdown"], HID_PAD), by_rows(gr0["d_wpg"], 128), by_rows(gr1["d_wpg"], 128)],
                        axis=1),
        jnp.concatenate([gr0["d_wgu"], gr1["d_wgu"]], axis=1),
        _by_owner_cols(d_woi),
        jnp.concatenate([_by_owner_cols(gr0["d_wpp"]), _by_owner_cols(gr1["d_wpp"])], axis=1),
    ]
    early_tags = ("rows", "gu", "od_in", "pp")
    core = lax.axis_index("c").astype(jnp.int32).reshape(1)
    dq_f, dk_f, dv_f, gc, early_sib = _fox_bwd(proj0, do0, o0, acol, crow, early)
    early_part = [_rs_chip_sum(core, g, r, f"rs_chip_sum_{t}") for g, r, t in zip(early, early_sib, early_tags)]
    dq_s, dk_s, dv_s, early_recv = _sb_bwd(proj0, do0, rsave, early_part)
    dproj0 = jnp.concatenate([dq_f, dk_f, dv_f, dq_s, dk_s, dv_s], axis=1)
    gcum = jnp.pad(gc.reshape(N_FOX, s).T, ((0, 0), (0, LANES - N_FOX)))
    dflog, db_f = _fgate_bwd(gcum, flog, b_f)
    d_wqkv = _mm(hn1, dproj0, ta=True, out_dtype=F32, name="mm_dw_in_0")
    d_wf = _mm(hn1, dflog, ta=True, out_dtype=F32, name="mm_dw_fgate_0")
    dhn1 = _mm(dproj0, w_qkv, tb=True, out_dtype=F32, name="mm_dx_in_0")
    dhn1f = _mm(dflog, w_f, tb=True, out_dtype=F32, name="mm_dx_fgate_0")
    grad_x, dg_mix0 = _rms_bwd(h0, g_mix[0], [dhn1, dhn1f], dh1, "rms_mix_bwd_0", False)

    late = [_by_owner_cols(jnp.concatenate([d_wqkv, d_wf[:, :N_FOX]], axis=1))]
    late_sib = _rs_sibling_exchange(late)
    late_recv = _rs_chip_exchange([_rs_chip_sum(core, late[0], late_sib[0], "rs_chip_sum_ev_in")])
    tags = ("rows", "gu", "ev_in", "od_in", "pp")
    chip_recv = [early_recv[0], early_recv[1], late_recv[0], early_recv[2], early_recv[3]]
    w_grp, m_grp, v_grp = (_group_shards(*ws, F32) for ws in (big_w, big_m, big_v))
    updated = [_rs_sum_adamw(r, w_, m_, v_, f"rs_sum_adamw_{t}")
               for r, w_, m_, v_, t in zip(chip_recv, w_grp, m_grp, v_grp, tags)]
    big_g, big_d, big_nm, big_nv = (_ungroup_shards([u[k] for u in updated]) for k in range(4))

    def small_pack(nmix, nffn, nple, nfin, bf, sk, extra):
        last = jnp.concatenate([bf.reshape(-1), sk.reshape(-1), extra.reshape(-1)])
        last = jnp.pad(last, (0, D_MODEL - last.shape[0]))
        return jnp.concatenate([nmix, nffn, nple, nfin.reshape(1, -1), last[None, :]], axis=0)

    small_g = small_pack(jnp.concatenate([dg_mix0, dg_mix1]), jnp.concatenate([gr0["dg_ffn"], gr1["dg_ffn"]]),
                         jnp.concatenate([gr0["dg_ple"], gr1["dg_ple"]]), dg_final,
                         db_f[0, :N_FOX], dsink_row[0, ::HEAD_DIM], loss_part[0, :1])
    zero1 = jnp.zeros((1,), F32)
    small_w = small_pack(norm_mix, norm_ffn, norm_ple, norm_final, ev_b_f, od_sinks, zero1)
    small_m = small_pack(m_norm_mix, m_norm_ffn, m_norm_ple, m_norm_final, m_ev_b_f, m_od_sinks, zero1)
    small_v = small_pack(v_norm_mix, v_norm_ffn, v_norm_ple, v_norm_final, v_ev_b_f, v_od_sinks, zero1)
    sg, sd, sm, sv_ = _small_allreduce_adamw(small_g, small_w, small_m, small_v)

    def small_unpack(t):
        return [t[0:2], t[2:4], t[4:6], t[6], t[7, :N_FOX][None, :], t[7, N_FOX:N_FOX + N_Q][None, :]]

    loss = sg[7, N_FOX + N_Q]

    def ordered(small, big):
        nm, nf, npl, nfin, bf, sk = small_unpack(small)
        ev_in, ev_out, od_in, od_out, fg, fu, fd, pproj, pgate = big
        return [nm, nf, npl, nfin, ev_in, bf, ev_out, od_in, sk, od_out, fg, fu, fd, pproj, pgate]

    return (loss, grad_x[None], *ordered(sg, big_g), *ordered(sd, big_d),
            *ordered(sm, big_nm), *ordered(sv_, big_nv))
```

```python
import functools

import jax
import jax.numpy as jnp
from jax import lax
from jax.experimental import pallas as pl
from jax.experimental.pallas import tpu as pltpu

F32 = jnp.float32
BF16 = jnp.bfloat16

D_MODEL = 1024
HEAD_DIM = 64
N_FOX = 8
N_SB = 8
FOX_W = N_FOX * HEAD_DIM
SB_W = N_SB * HEAD_DIM
QKV_W = 3 * FOX_W + 3 * SB_W
EVEN_IN = QKV_W + N_FOX
N_Q = 16
N_KV = 4
ODD_IN = N_Q * HEAD_DIM + 2 * N_KV * HEAD_DIM
WINDOW = 128
ROPE_THETA = 10000.0
D_FF = 2816
PLE_DIM = 256
EPS = 1e-6
NEG_INF = -1e30
SCALE = HEAD_DIM ** -0.5

ADAM_LR = 0.001
ADAM_B1 = 0.9
ADAM_B2 = 0.999
ADAM_EPS = 1e-08
ADAM_WD = 0.01
ADAM_STEP = 10

N_DEV = 8
LANES = 128
ROW_TILE = 256
ATTN_BLOCK = 256
HID_SHARD = D_FF // N_DEV
HID_PAD = 384
D_FF_PAD = N_DEV * HID_PAD

MESH = pl.DeviceIdType.MESH


def _pick(n, prefs):
    for t in prefs:
        if n % t == 0:
            return t
    return n


def _pad_to(a, axis, size):
    pad = [(0, 0)] * a.ndim
    pad[axis] = (0, size - a.shape[axis])
    return jnp.pad(a, pad)


def _group_shards(ev_in, ev_out, od_in, od_out, gate, up, down, pproj, pgate, dtype):
    rows = jnp.concatenate([ev_out[0], od_out[0], _pad_to(down[0], 0, HID_PAD), _pad_to(down[1], 0, HID_PAD),
                            pgate[0], pgate[1]], axis=0)
    gu = jnp.concatenate([jnp.concatenate([_pad_to(gate[l], 1, HID_PAD), _pad_to(up[l], 1, HID_PAD)], axis=1)
                          for l in range(2)], axis=0)
    groups = [rows, gu, ev_in[0], od_in[0], pproj.reshape(-1, pproj.shape[-1])]
    return [g.astype(dtype) for g in groups]


def _ungroup_shards(groups):
    rows, gu, ev_in, od_in, pp = groups
    d = D_MODEL
    down = jnp.stack([rows[256 + HID_PAD * l: 256 + HID_PAD * l + HID_SHARD] for l in range(2)])
    pgate = jnp.stack([rows[1024 + 128 * l: 1152 + 128 * l] for l in range(2)])
    gate = jnp.stack([gu[d * l: d * (l + 1), :HID_SHARD] for l in range(2)])
    up = jnp.stack([gu[d * l: d * (l + 1), HID_PAD:HID_PAD + HID_SHARD] for l in range(2)])
    return [ev_in[None], rows[None, 0:128], od_in[None], rows[None, 128:256], gate, up, down,
            pp.reshape(2, PLE_DIM, -1), pgate]


def _by_owner_cols(full):
    r, c8 = full.shape
    return full.reshape(r, N_DEV, c8 // N_DEV).transpose(1, 0, 2)


def _from_owner_cols(g):
    n, r, c = g.shape
    return g.transpose(1, 0, 2).reshape(r, n * c)


_ANY = pl.BlockSpec(memory_space=pl.ANY)


def _all_gather_steps(x_refs, out_refs, send_sems, recv_sems, local_sems):
    n = len(x_refs)
    x, y, c = lax.axis_index("x"), lax.axis_index("y"), lax.axis_index("c")
    me, sibling = (x, y, c), (x, y, 1 - c)
    chips = [(1 - x, y), (x, 1 - y), (1 - x, 1 - y)]

    def copy(a, k, block, to, from_input=False):
        px, py, pc = block
        slot = out_refs[a].at[4 * px + 2 * py + pc]
        return pltpu.make_async_remote_copy(
            src_ref=x_refs[a] if from_input else slot, dst_ref=slot,
            send_sem=send_sems.at[7 * a + k], recv_sem=recv_sems.at[7 * a + k],
            device_id=to, device_id_type=MESH)

    def mine():
        return [pltpu.make_async_copy(x_refs[a], out_refs[a].at[4 * x + 2 * y + c], local_sems.at[a]) for a in range(n)]

    def first():
        out = []
        for a in range(n):
            out.append(copy(a, 0, me, sibling, True))
            out += [copy(a, 1 + j, me, (*chip, c), True) for j, chip in enumerate(chips)]
        return out

    def issue():
        for cp in mine() + first():
            cp.start()

    def complete():
        passed = []
        for j, chip in enumerate(chips):
            for a in range(n):
                copy(a, 1 + j, (*chip, c), me).wait_recv()
                passed.append(copy(a, 4 + j, (*chip, c), sibling))
                passed[-1].start()
        for a in range(n):
            copy(a, 0, sibling, me).wait_recv()
            for j, chip in enumerate(chips):
                copy(a, 4 + j, (*chip, 1 - c), me).wait_recv()
        for cp in first() + passed:
            cp.wait_send()
        for cp in mine():
            cp.wait()

    return issue, complete


def _all_gather_scratch(n):
    return [pltpu.SemaphoreType.DMA((7 * n,)), pltpu.SemaphoreType.DMA((7 * n,)), pltpu.SemaphoreType.DMA((n,))]


def _gathered_shapes(shards):
    return tuple(jax.ShapeDtypeStruct((N_DEV,) + s.shape, s.dtype) for s in shards)


def _all_gather_weights(shards):
    n = len(shards)

    def body(*refs):
        issue, complete = _all_gather_steps(refs[:n], refs[n:2 * n], *refs[2 * n:])
        issue()
        complete()

    return pl.pallas_call(
        body, name="ag_weights", out_shape=_gathered_shapes(shards),
        in_specs=[_ANY] * n, out_specs=tuple([_ANY] * n), scratch_shapes=_all_gather_scratch(n),
    )(*shards)


def _sibling_exchange_steps(g_refs, out_refs, send_sems, recv_sems):
    n = len(g_refs)
    x, y, c = lax.axis_index("x"), lax.axis_index("y"), lax.axis_index("c")

    def copies():
        return [pltpu.make_async_remote_copy(
            src_ref=g_refs[a].at[2 * k + (1 - c)], dst_ref=out_refs[a].at[k],
            send_sem=send_sems.at[4 * a + k], recv_sem=recv_sems.at[4 * a + k],
            device_id=(x, y, 1 - c), device_id_type=MESH) for a in range(n) for k in range(4)]

    def issue():
        for cp in copies():
            cp.start()

    def complete():
        for cp in copies():
            cp.wait_recv()
        for cp in copies():
            cp.wait_send()

    return issue, complete


def _sibling_exchange_scratch(n):
    return [pltpu.SemaphoreType.DMA((4 * n,)), pltpu.SemaphoreType.DMA((4 * n,))]


def _quarter_shapes(arrays):
    return tuple(jax.ShapeDtypeStruct((4,) + g.shape[1:], g.dtype) for g in arrays)


def _rs_sibling_exchange(gps):
    n = len(gps)

    def body(*refs):
        issue, complete = _sibling_exchange_steps(refs[:n], refs[n:2 * n], *refs[2 * n:])
        issue()
        complete()

    return pl.pallas_call(
        body, name="rs_sibling_exchange", out_shape=_quarter_shapes(gps),
        in_specs=[_ANY] * n, out_specs=tuple([_ANY] * n), scratch_shapes=_sibling_exchange_scratch(n),
    )(*gps)


def _rs_chip_sum(core, gp, recv, name):
    _, rows, cols = gp.shape
    tr = ROW_TILE

    def body(core_ref, a_ref, b_ref, o_ref):
        o_ref[...] = (a_ref[...] + b_ref[...]).astype(BF16)

    return pl.pallas_call(
        body, name=name,
        out_shape=jax.ShapeDtypeStruct((4, rows, cols), BF16),
        grid_spec=pltpu.PrefetchScalarGridSpec(
            num_scalar_prefetch=1, grid=(4, rows // tr),
            in_specs=[pl.BlockSpec((1, tr, cols), lambda k, r, cr: (2 * k + cr[0], r, 0)),
                      pl.BlockSpec((1, tr, cols), lambda k, r, cr: (k, r, 0))],
            out_specs=pl.BlockSpec((1, tr, cols), lambda k, r, cr: (k, r, 0))),
    )(core, gp, recv)


def _chip_exchange_steps(p_refs, out_refs, send_sems, recv_sems, local_sems):
    n = len(p_refs)
    x, y, c = lax.axis_index("x"), lax.axis_index("y"), lax.axis_index("c")
    my_chip = 2 * x + y

    def mine():
        return [pltpu.make_async_copy(p_refs[a].at[my_chip], out_refs[a].at[my_chip], local_sems.at[a])
                for a in range(n)]

    def copies():
        return [pltpu.make_async_remote_copy(
            src_ref=p_refs[a].at[2 * px + py], dst_ref=out_refs[a].at[my_chip],
            send_sem=send_sems.at[3 * a + j], recv_sem=recv_sems.at[3 * a + j],
            device_id=(px, py, c), device_id_type=MESH)
            for a in range(n) for j, (px, py) in enumerate([(1 - x, y), (x, 1 - y), (1 - x, 1 - y)])]

    def issue():
        for cp in mine() + copies():
            cp.start()

    def complete():
        for cp in copies():
            cp.wait_recv()
        for cp in copies():
            cp.wait_send()
        for cp in mine():
            cp.wait()

    return issue, complete


def _chip_exchange_scratch(n):
    return [pltpu.SemaphoreType.DMA((3 * n,)), pltpu.SemaphoreType.DMA((3 * n,)), pltpu.SemaphoreType.DMA((n,))]


def _same_shapes(arrays):
    return tuple(jax.ShapeDtypeStruct(a.shape, a.dtype) for a in arrays)


def _rs_chip_exchange(parts):
    n = len(parts)

    def body(*refs):
        issue, complete = _chip_exchange_steps(refs[:n], refs[n:2 * n], *refs[2 * n:])
        issue()
        complete()

    return pl.pallas_call(
        body, name="rs_chip_exchange", out_shape=_same_shapes(parts),
        in_specs=[_ANY] * n, out_specs=tuple([_ANY] * n), scratch_shapes=_chip_exchange_scratch(n),
    )(*parts)


def _adamw(w, g, m, v):
    m = ADAM_B1 * m + (1.0 - ADAM_B1) * g
    v = ADAM_B2 * v + (1.0 - ADAM_B2) * (g * g)
    m_hat = m / (1.0 - ADAM_B1 ** ADAM_STEP)
    v_hat = v / (1.0 - ADAM_B2 ** ADAM_STEP)
    delta = -ADAM_LR * (m_hat / (jnp.sqrt(v_hat) + ADAM_EPS) + ADAM_WD * w)
    return delta, m, v


def _rs_sum_adamw(recv, w, m, v, name):
    _, rows, cols = recv.shape
    tr = ROW_TILE

    def body(r_ref, w_ref, m_ref, v_ref, g_out, d_out, m_out, v_out):
        g = r_ref[0].astype(F32)
        for k in range(1, 4):
            g = g + r_ref[k].astype(F32)
        delta, m_new, v_new = _adamw(w_ref[...], g, m_ref[...], v_ref[...])
        g_out[...] = g
        d_out[...] = delta
        m_out[...] = m_new
        v_out[...] = v_new

    flat = pl.BlockSpec((tr, cols), lambda r: (r, 0))
    shp = jax.ShapeDtypeStruct((rows, cols), F32)
    return pl.pallas_call(
        body, name=name, grid=(rows // tr,),
        out_shape=(shp, shp, shp, shp),
        in_specs=[pl.BlockSpec((4, tr, cols), lambda r: (0, r, 0)), flat, flat, flat],
        out_specs=(flat, flat, flat, flat),
    )(recv, w, m, v)


def _small_allreduce_adamw(vec, w, m, v):
    rows, cols = vec.shape

    def body(x_ref, w_ref, m_ref, v_ref, g_out, d_out, m_out, v_out, gather, send_sems, recv_sems):
        x, y, c = lax.axis_index("x"), lax.axis_index("y"), lax.axis_index("c")
        me = 4 * x + 2 * y + c
        copies = []
        for d in range(1, N_DEV):
            dx, dy, dc = (d >> 2) & 1, (d >> 1) & 1, d & 1
            peer = (x ^ dx if dx else x, y ^ dy if dy else y, c ^ dc if dc else c)
            copies.append(pltpu.make_async_remote_copy(
                src_ref=x_ref, dst_ref=gather.at[me],
                send_sem=send_sems.at[d - 1], recv_sem=recv_sems.at[d - 1],
                device_id=peer, device_id_type=MESH))
        for cp in copies:
            cp.start()
        gather[me] = x_ref[...]
        for cp in copies:
            cp.wait_recv()
        for cp in copies:
            cp.wait_send()
        g = gather[0]
        for k in range(1, N_DEV):
            g = g + gather[k]
        delta, m_new, v_new = _adamw(w_ref[...], g, m_ref[...], v_ref[...])
        g_out[...] = g
        d_out[...] = delta
        m_out[...] = m_new
        v_out[...] = v_new

    vm = pl.BlockSpec(memory_space=pltpu.VMEM)
    shp = jax.ShapeDtypeStruct((rows, cols), F32)
    return pl.pallas_call(
        body, name="small_allreduce_adamw",
        out_shape=(shp, shp, shp, shp),
        in_specs=[vm, vm, vm, vm], out_specs=(vm, vm, vm, vm),
        scratch_shapes=[pltpu.VMEM((N_DEV, rows, cols), F32),
                        pltpu.SemaphoreType.DMA((N_DEV - 1,)), pltpu.SemaphoreType.DMA((N_DEV - 1,))],
    )(vec, w, m, v)


def _mm(a, b, *, ta=False, tb=False, out_dtype=BF16, res=None, name, group=None, group_width=None):
    if ta:
        kdim, m = a.shape
    else:
        m, kdim = a.shape
    if group == "n":
        assert not ta and not tb and res is None
        ng, kb, tn = b.shape
        n = ng * tn
    elif group == "k":
        assert tb and not ta and res is None
        ng, n, chunk = b.shape
        kb = ng * chunk
        per_step = 2 if ng % 2 == 0 else 1
        tk = per_step * chunk
    elif tb:
        n, kb = b.shape
    else:
        kb, n = b.shape
    assert kdim == kb, (a.shape, b.shape, ta, tb, group)
    tm = _pick(m, (1024, 512, 256, 128))
    if group == "out":
        assert ta and not tb and res is None
        tn = group_width
    elif group != "n":
        tn = _pick(n, (1024, 1408, 768, 512, 256, 128))
    if group != "k":
        tk = _pick(kdim, ((2048,) if ta else ()) + (1024, 1408, 512, 256, 128))
    nk = kdim // tk
    dn = (((0 if ta else 1,), (1 if tb else 0,)), ((), ()))
    has_res = res is not None
    in_place = nk > 1 and out_dtype == F32
    use_acc = nk > 1 and not in_place

    def body(*refs):
        a_ref, b_ref = refs[:2]
        r_ref = refs[2] if has_res else None
        o_ref = refs[3] if has_res else refs[2]
        if group == "k":
            part = sum(lax.dot_general(a_ref[:, c * chunk:(c + 1) * chunk], b_ref[c], dn, preferred_element_type=F32)
                       for c in range(per_step))
        else:
            part = lax.dot_general(a_ref[...], b_ref[...], dn, preferred_element_type=F32)
        if nk == 1:
            if has_res:
                part = part + r_ref[...].astype(F32)
            o_ref[...] = part.astype(out_dtype)
            return
        k = pl.program_id(2)
        acc = o_ref if in_place else refs[-1]

        @pl.when(k == 0)
        def _():
            acc[...] = part + r_ref[...].astype(F32) if has_res else part

        @pl.when(k > 0)
        def _():
            acc[...] += part

        if use_acc:
            @pl.when(k == nk - 1)
            def _():
                o_ref[...] = acc[...].astype(out_dtype)

    a_spec = (pl.BlockSpec((tk, tm), lambda i, j, k: (k, i)) if ta
              else pl.BlockSpec((tm, tk), lambda i, j, k: (i, k)))
    if group == "n":
        b_spec = pl.BlockSpec((None, tk, tn), lambda i, j, k: (j, k, 0))
    elif group == "k":
        b_spec = pl.BlockSpec((per_step, tn, chunk), lambda i, j, k: (k, j, 0))
    elif tb:
        b_spec = pl.BlockSpec((tn, tk), lambda i, j, k: (j, k))
    else:
        b_spec = pl.BlockSpec((tk, tn), lambda i, j, k: (k, j))
    if group == "out":
        o_spec = pl.BlockSpec((None, tm, tn), lambda i, j, k: (j, i, 0))
        out_shape = jax.ShapeDtypeStruct((n // tn, m, tn), out_dtype)
    else:
        o_spec = pl.BlockSpec((tm, tn), lambda i, j, k: (i, j))
        out_shape = jax.ShapeDtypeStruct((m, n), out_dtype)
    in_specs = [a_spec, b_spec] + ([o_spec] if has_res else [])
    args = (a, b) + ((res,) if has_res else ())
    return pl.pallas_call(
        body, name=name, grid=(m // tm, n // tn, nk),
        out_shape=out_shape,
        in_specs=in_specs, out_specs=o_spec,
        scratch_shapes=[pltpu.VMEM((tm, tn), F32)] if use_acc else [],
        compiler_params=pltpu.CompilerParams(dimension_semantics=("parallel", "parallel", "arbitrary")),
    )(*args)


def _row_tile(s):
    return _pick(s, (512, 256, 128))


def _rms_fwd(h, g, name):
    s, d = h.shape
    ts = _row_tile(s)

    def body(h_ref, g_ref, o_ref):
        x = h_ref[...]
        r = lax.rsqrt(jnp.mean(x * x, axis=-1, keepdims=True) + EPS)
        o_ref[...] = ((x * r) * g_ref[...]).astype(BF16)

    return pl.pallas_call(
        body, name=name, grid=(s // ts,),
        out_shape=jax.ShapeDtypeStruct((s, d), BF16),
        in_specs=[pl.BlockSpec((ts, d), lambda i: (i, 0)), pl.BlockSpec((1, d), lambda i: (0, 0))],
        out_specs=pl.BlockSpec((ts, d), lambda i: (i, 0)),
    )(h, g)


def _rms_bwd(h, g, dhns, dres, name, want_bf16):
    s, d = h.shape
    ts = _row_tile(s)
    n_in = len(dhns)

    def body(*refs):
        h_ref, g_ref, r_ref = refs[:3]
        dy_refs = refs[3:3 + n_in]
        outs = refs[3 + n_in:]
        dh_ref, dg_ref = outs[0], outs[-1]
        i = pl.program_id(0)
        x = h_ref[...]
        dy = dy_refs[0][...].astype(F32)
        for extra in dy_refs[1:]:
            dy = dy + extra[...].astype(F32)
        r = lax.rsqrt(jnp.mean(x * x, axis=-1, keepdims=True) + EPS)
        xr = x * r
        u = dy * g_ref[...]
        dx = r * (u - xr * jnp.mean(xr * u, axis=-1, keepdims=True))
        dh = r_ref[...] + dx
        dh_ref[...] = dh
        if want_bf16:
            outs[1][...] = dh.astype(BF16)

        @pl.when(i == 0)
        def _():
            dg_ref[...] = jnp.zeros_like(dg_ref)

        dg_ref[...] += jnp.sum(dy * xr, axis=0, keepdims=True)

    row = pl.BlockSpec((ts, d), lambda i: (i, 0))
    vec = pl.BlockSpec((1, d), lambda i: (0, 0))
    out_shape = [jax.ShapeDtypeStruct((s, d), F32)]
    out_specs = [row]
    if want_bf16:
        out_shape.append(jax.ShapeDtypeStruct((s, d), BF16))
        out_specs.append(row)
    out_shape.append(jax.ShapeDtypeStruct((1, d), F32))
    out_specs.append(vec)
    return pl.pallas_call(
        body, name=name, grid=(s // ts,),
        out_shape=tuple(out_shape),
        in_specs=[row, vec, row] + [row] * n_in, out_specs=tuple(out_specs),
        compiler_params=pltpu.CompilerParams(dimension_semantics=("arbitrary",)),
    )(h, g, dres, *dhns)


def _sigmoid_parts(z):
    e = jnp.exp(-jnp.abs(z))
    r = 1.0 / (1.0 + e)
    er = e * r
    pos = z >= 0
    return jnp.where(pos, r, er), jnp.where(pos, er, r)


def _gate_up_swiglu(hn, w_gu, name):
    s, d = hn.shape
    f = HID_PAD
    tm = _pick(s, (1024, 512, 256, 128))
    chunk_rows = min(tm, 256)

    def body(x_ref, w_ref, ab_ref, u_ref):
        for r in range(tm // chunk_rows):
            rows = slice(r * chunk_rows, (r + 1) * chunk_rows)
            ab = jnp.dot(x_ref[rows, :], w_ref[...], preferred_element_type=F32).astype(BF16)
            ab_ref[rows, :] = ab
            a = ab[:, :f].astype(F32)
            b = ab[:, f:].astype(F32)
            sg, _ = _sigmoid_parts(a)
            u_ref[rows, :] = ((a * sg) * b).astype(BF16)

    return pl.pallas_call(
        body, name=name, grid=(s // tm, N_DEV),
        out_shape=(jax.ShapeDtypeStruct((s, 2 * D_FF_PAD), BF16), jax.ShapeDtypeStruct((s, D_FF_PAD), BF16)),
        in_specs=[pl.BlockSpec((tm, d), lambda i, j: (i, 0)), pl.BlockSpec((None, d, 2 * f), lambda i, j: (j, 0, 0))],
        out_specs=(pl.BlockSpec((tm, 2 * f), lambda i, j: (i, j)), pl.BlockSpec((tm, f), lambda i, j: (i, j))),
    )(hn, w_gu)


def _down_t_swiglu_bwd(dh, w_down, ab, name):
    s, d = dh.shape
    f = HID_PAD
    tm = _pick(s, (1024, 512, 256, 128))
    chunk_rows = min(tm, 256)

    def body(x_ref, w_ref, ab_ref, o_ref):
        for r in range(tm // chunk_rows):
            rows = slice(r * chunk_rows, (r + 1) * chunk_rows)
            du = lax.dot_general(x_ref[rows, :], w_ref[...], _DN_NT, preferred_element_type=F32)
            g = du.astype(BF16).astype(F32)
            a = ab_ref[rows, :f].astype(F32)
            b = ab_ref[rows, f:].astype(F32)
            sg, sgm = _sigmoid_parts(a)
            silu = a * sg
            o_ref[rows, :f] = (g * b * (sg + silu * sgm)).astype(BF16)
            o_ref[rows, f:] = (g * silu).astype(BF16)

    return pl.pallas_call(
        body, name=name, grid=(s // tm, N_DEV),
        out_shape=jax.ShapeDtypeStruct(ab.shape, BF16),
        in_specs=[pl.BlockSpec((tm, d), lambda i, j: (i, 0)), pl.BlockSpec((f, d), lambda i, j: (j, 0)),
                  pl.BlockSpec((tm, 2 * f), lambda i, j: (i, j))],
        out_specs=pl.BlockSpec((tm, 2 * f), lambda i, j: (i, j)),
    )(dh, w_down, ab)


def _ple_fwd(h, gl, pp, name):
    s, d = h.shape
    ts = _row_tile(s)

    def body(h_ref, gl_ref, pp_ref, o_ref):
        sg, _ = _sigmoid_parts(gl_ref[...].astype(F32))
        o_ref[...] = h_ref[...] + sg * pp_ref[...].astype(F32)

    row = pl.BlockSpec((ts, d), lambda i: (i, 0))
    return pl.pallas_call(
        body, name=name, grid=(s // ts,),
        out_shape=jax.ShapeDtypeStruct((s, d), F32),
        in_specs=[row, row, row], out_specs=row,
    )(h, gl, pp)


def _ple_bwd(dh, gl, pp, name):
    s, d = dh.shape
    ts = _row_tile(s)

    def body(dh_ref, gl_ref, pp_ref, dgl_ref, dpp_ref):
        g = dh_ref[...]
        sg, sgm = _sigmoid_parts(gl_ref[...].astype(F32))
        dpp_ref[...] = (g * sg).astype(BF16)
        dgl_ref[...] = (g * pp_ref[...].astype(F32) * (sg * sgm)).astype(BF16)

    row = pl.BlockSpec((ts, d), lambda i: (i, 0))
    shp = jax.ShapeDtypeStruct((s, d), BF16)
    return pl.pallas_call(
        body, name=name, grid=(s // ts,),
        out_shape=(shp, shp), in_specs=[row, row, row], out_specs=(row, row),
    )(dh, gl, pp)


def _final_norm_loss(h, g, target):
    s, d = h.shape
    ts = _row_tile(s)

    def body(h_ref, g_ref, t_ref, loss_ref, dh_ref, dg_ref):
        i = pl.program_id(0)
        x = h_ref[...]
        gain = g_ref[...]
        r = lax.rsqrt(jnp.mean(x * x, axis=-1, keepdims=True) + EPS)
        xr = x * r
        err = xr * gain - t_ref[...]
        dy = err * (1.0 / d)
        u = dy * gain
        dh_ref[...] = r * (u - xr * jnp.mean(xr * u, axis=-1, keepdims=True))

        @pl.when(i == 0)
        def _():
            dg_ref[...] = jnp.zeros_like(dg_ref)
            loss_ref[...] = jnp.zeros_like(loss_ref)

        dg_ref[...] += jnp.sum(dy * xr, axis=0, keepdims=True)
        tok = jnp.mean(err * err, axis=-1, keepdims=True)
        loss_ref[...] += 0.5 * jnp.sum(tok, axis=0, keepdims=True)

    row = pl.BlockSpec((ts, d), lambda i: (i, 0))
    vec = pl.BlockSpec((1, d), lambda i: (0, 0))
    return pl.pallas_call(
        body, name="final_norm_loss", grid=(s // ts,),
        out_shape=(jax.ShapeDtypeStruct((1, LANES), F32), jax.ShapeDtypeStruct((s, d), F32),
                   jax.ShapeDtypeStruct((1, d), F32)),
        in_specs=[row, vec, row],
        out_specs=(pl.BlockSpec((1, LANES), lambda i: (0, 0)), row, vec),
        compiler_params=pltpu.CompilerParams(dimension_semantics=("arbitrary",)),
    )(h, g, target)


def _rope(xin, w, cos, sin_signed, sign, name):
    s = xin.shape[0]
    ts = _pick(s, (512, 256, 128))

    def body(x_ref, c_ref, s_ref, o_ref):
        cos_b, sin_b = c_ref[...], s_ref[...]
        lane = lax.broadcasted_iota(jnp.int32, cos_b.shape, 1)
        low = (lane & (HEAD_DIM - 1)) < (HEAD_DIM // 2)
        for j in range(w // LANES):
            cols = slice(j * LANES, (j + 1) * LANES)
            x = x_ref[:, cols].astype(F32)
            swapped = jnp.where(low, pltpu.roll(x, LANES - HEAD_DIM // 2, 1), pltpu.roll(x, HEAD_DIM // 2, 1))
            o_ref[:, cols] = (x * cos_b + sign * (swapped * sin_b)).astype(BF16)

    blk = pl.BlockSpec((ts, w), lambda i: (i, 0))
    tab = pl.BlockSpec((ts, LANES), lambda i: (i, 0))
    return pl.pallas_call(
        body, name=name, grid=(s // ts,),
        out_shape=jax.ShapeDtypeStruct((s, w), BF16),
        in_specs=[blk, tab, tab], out_specs=blk,
    )(xin, cos, sin_signed)


def _fgate_fwd(flog, bias):
    s, w = flog.shape

    def body(x_ref, b_ref, o_ref):
        rowi = lax.broadcasted_iota(jnp.int32, (8, w), 0)
        b = b_ref[...]

        def step(g, carry):
            sl = pl.ds(pl.multiple_of(g * 8, 8), 8)
            x = x_ref[sl, :] + b
            lf = jnp.minimum(x, 0.0) - jnp.log1p(jnp.exp(-jnp.abs(x)))
            for sh in (1, 2, 4):
                lf = lf + jnp.where(rowi >= sh, pltpu.roll(lf, sh, 0), 0.0)
            out = lf + carry
            o_ref[sl, :] = out
            return jnp.broadcast_to(out[7:8, :], (8, w))

        lax.fori_loop(0, s // 8, step, jnp.zeros((8, w), F32))

    vm = pl.BlockSpec(memory_space=pltpu.VMEM)
    return pl.pallas_call(
        body, name="fgate_fwd", out_shape=jax.ShapeDtypeStruct((s, w), F32),
        in_specs=[vm, vm], out_specs=vm,
    )(flog, bias)


def _fgate_bwd(gcum, flog, bias):
    s, w = flog.shape

    def body(g_ref, x_ref, b_ref, o_ref, db_ref):
        rowi = lax.broadcasted_iota(jnp.int32, (8, w), 0)
        lane = lax.broadcasted_iota(jnp.int32, (8, w), 1)
        b = b_ref[...]

        def step(t, carry):
            run, dbsum = carry
            g = s // 8 - 1 - t
            sl = pl.ds(pl.multiple_of(g * 8, 8), 8)
            c = g_ref[sl, :]
            for sh in (1, 2, 4):
                c = c + jnp.where(rowi < 8 - sh, pltpu.roll(c, 8 - sh, 0), 0.0)
            c = c + run
            _, sgm = _sigmoid_parts(x_ref[sl, :] + b)
            dl = jnp.where(lane < N_FOX, c * sgm, 0.0)
            o_ref[sl, :] = dl.astype(BF16)
            return jnp.broadcast_to(c[0:1, :], (8, w)), dbsum + dl

        _, dbsum = lax.fori_loop(0, s // 8, step, (jnp.zeros((8, w), F32), jnp.zeros((8, w), F32)))
        db_ref[...] = jnp.sum(dbsum, axis=0, keepdims=True)

    vm = pl.BlockSpec(memory_space=pltpu.VMEM)
    return pl.pallas_call(
        body, name="fgate_bwd",
        out_shape=(jax.ShapeDtypeStruct((s, w), BF16), jax.ShapeDtypeStruct((1, w), F32)),
        in_specs=[vm, vm, vm], out_specs=(vm, vm),
    )(gcum, flog, bias)


_DN_NT = (((1,), (1,)), ((), ()))
_DN_TN = (((0,), (0,)), ((), ()))


def _head_mask(shape, hh):
    lane = lax.broadcasted_iota(jnp.int32, shape, 1)
    return (lane >= HEAD_DIM * hh) & (lane < HEAD_DIM * (hh + 1))


SKIP_BELOW = -110.0


def _sweep_left(i, carry, tile, go_on):
    def flag(j, c):
        return jnp.logical_and(j >= 0, go_on(jnp.maximum(j, 0), c)).astype(jnp.int32)

    def body(st):
        j, _, c = st
        c = tile(j, c)
        return j - 1, flag(j - 1, c), c

    return lax.while_loop(lambda st: st[1] > 0, body, (i - 1, flag(i - 1, carry), carry))[2]


def _sweep_left_pair(i, carries, tiles, go_ons):
    return _sweep_left(
        i, tuple(carries),
        lambda j, c: tuple(t(j, ch) for t, ch in zip(tiles, c)),
        lambda j, c: jnp.logical_or(go_ons[0](j, c[0]), go_ons[1](j, c[1])))


def _key_norm_max(k_ref, kn_ref):
    k2 = k_ref[...].astype(F32)
    sq = k2 * k2
    for hh in range(2):
        n2 = jnp.sum(jnp.where(_head_mask(sq.shape, hh), sq, 0.0), axis=1, keepdims=True)
        kn_ref[hh] = jnp.broadcast_to(jnp.sqrt(jnp.max(n2, axis=0, keepdims=True)), kn_ref.shape[1:])


def _fox_fwd(proj, ccol, crow, gather):
    s = proj.shape[0]
    blk = min(ATTN_BLOCK, s)
    nq = s // blk
    npair = N_FOX // 2
    ng = len(gather)

    def body(*refs):
        q_ref, k_ref, v_ref, cc_ref, cr_ref = refs[:5]
        x_refs = refs[5:5 + ng]
        o_ref, a_ref = refs[5 + ng:7 + ng]
        g_refs = refs[7 + ng:7 + 2 * ng]
        kn_ref = refs[7 + 2 * ng]
        p_, i = pl.program_id(0), pl.program_id(1)
        issue, complete = _all_gather_steps(x_refs, g_refs, *refs[8 + 2 * ng:])

        @pl.when((p_ == 0) & (i == 0))
        def _():
            issue()

        @pl.when(i == 0)
        def _():
            _key_norm_max(k_ref, kn_ref)

        q2 = q_ref[...].astype(F32) * SCALE
        row = lax.broadcasted_iota(jnp.int32, (blk, blk), 0)
        col = lax.broadcasted_iota(jnp.int32, (blk, blk), 1)
        outs, heads = [], []
        for hh in range(2):
            hm = _head_mask(q2.shape, hh)
            qh = jnp.where(hm, q2, 0.0).astype(BF16)
            ct = cc_ref[hh][:, 0:1]
            qk_max = jnp.sqrt(jnp.sum(jnp.where(hm, q2 * q2, 0.0), axis=1, keepdims=True)) * kn_ref[hh][0:1, 0:1]

            def go_on(j, carry, ct=ct, qk_max=qk_max, hh=hh):
                bias_max = ct - jnp.min(cr_ref[2 * p_ + hh, j], axis=1, keepdims=True)
                return jnp.max(qk_max + bias_max - carry[0]) > SKIP_BELOW

            def tile(j, carry, masked, qh=qh, ct=ct, hh=hh):
                m, l, acc = carry
                sl = pl.ds(pl.multiple_of(j * blk, blk), blk)
                kb, vb = k_ref[sl, :], v_ref[sl, :]
                sc = lax.dot_general(qh, kb, _DN_NT, preferred_element_type=F32)
                sc = sc + (ct - cr_ref[2 * p_ + hh, j])
                if masked:
                    sc = jnp.where(col <= row, sc, NEG_INF)
                m_new = jnp.maximum(m, jnp.max(sc, axis=1, keepdims=True))
                alpha = jnp.exp(m - m_new)
                pm = jnp.exp(sc - m_new)
                l = alpha * l + jnp.sum(pm, axis=1, keepdims=True)
                acc = alpha * acc + jnp.dot(pm.astype(BF16), vb, preferred_element_type=F32)
                return m_new, l, acc

            init = (jnp.full((blk, 1), NEG_INF, F32), jnp.zeros((blk, 1), F32), jnp.zeros((blk, LANES), F32))
            heads.append((lambda j, c, tile=tile: tile(j, c, False), go_on, tile(i, init, True), ct))
        swept = _sweep_left_pair(i, [h[2] for h in heads], [h[0] for h in heads], [h[1] for h in heads])
        for hh, (m, l, acc) in enumerate(swept):
            outs.append(acc / l)
            a_ref[hh] = jnp.broadcast_to(heads[hh][3] - (m + jnp.log(l)), (blk, LANES))
        o_ref[...] = jnp.where(_head_mask(outs[0].shape, 0), outs[0], outs[1]).astype(BF16)

        @pl.when((p_ == npair - 1) & (i == nq - 1))
        def _():
            complete()

    seq = lambda base: pl.BlockSpec((s, LANES), lambda p, i: (0, base + p))
    res = pl.pallas_call(
        body, name="fox_fwd_ag", grid=(npair, nq),
        scratch_shapes=[pltpu.VMEM((2, 8, LANES), F32)] + _all_gather_scratch(ng),
        out_shape=(jax.ShapeDtypeStruct((s, FOX_W), BF16), jax.ShapeDtypeStruct((N_FOX, s, LANES), F32))
        + _gathered_shapes(gather),
        in_specs=[pl.BlockSpec((blk, LANES), lambda p, i: (i, p)), seq(npair), seq(2 * npair),
                  pl.BlockSpec((2, blk, LANES), lambda p, i: (p, i, 0)),
                  pl.BlockSpec((N_FOX, nq, 1, blk), lambda p, i: (0, 0, 0, 0))] + [_ANY] * ng,
        out_specs=(pl.BlockSpec((blk, LANES), lambda p, i: (i, p)),
                   pl.BlockSpec((2, blk, LANES), lambda p, i: (p, i, 0))) + tuple([_ANY] * ng),
        compiler_params=pltpu.CompilerParams(dimension_semantics=("arbitrary", "arbitrary")),
    )(proj, proj, proj, ccol, crow, *gather)
    return res[0], res[1], list(res[2:])


def _fox_bwd(proj, do, o, acol, crow, grads):
    s = proj.shape[0]
    blk = min(ATTN_BLOCK, s)
    nq = s // blk
    npair = N_FOX // 2
    ng = len(grads)

    def body(*refs):
        q_ref, k_ref, v_ref, do_ref, o_ref, a_ref, cr_ref = refs[:7]
        g_refs = refs[7:7 + ng]
        dq_ref, dk_ref, dv_ref, gc_ref = refs[7 + ng:11 + ng]
        x_refs = refs[11 + ng:11 + 2 * ng]
        dk_acc, dv_acc, kn_ref = refs[11 + 2 * ng:14 + 2 * ng]
        p_, i = pl.program_id(0), pl.program_id(1)
        issue, complete = _sibling_exchange_steps(g_refs, x_refs, *refs[14 + 2 * ng:])

        @pl.when((p_ == 0) & (i == 0))
        def _():
            issue()

        @pl.when(i == 0)
        def _():
            dk_acc[...] = jnp.zeros_like(dk_acc)
            dv_acc[...] = jnp.zeros_like(dv_acc)
            gc_ref[...] = jnp.zeros_like(gc_ref)
            _key_norm_max(k_ref, kn_ref)

        q2 = q_ref[...].astype(F32) * SCALE
        do2 = do_ref[...]
        prod = do2.astype(F32) * o_ref[...].astype(F32)
        row = lax.broadcasted_iota(jnp.int32, (blk, blk), 0)
        col = lax.broadcasted_iota(jnp.int32, (blk, blk), 1)
        dqs, heads = [], []
        for hh in range(2):
            hm = _head_mask(q2.shape, hh)
            qh = jnp.where(hm, q2, 0.0).astype(BF16)
            doh = jnp.where(hm, do2, jnp.zeros_like(do2))
            delta = jnp.sum(jnp.where(hm, prod, 0.0), axis=1, keepdims=True)
            at = a_ref[hh][:, 0:1]
            qk_max = jnp.sqrt(jnp.sum(jnp.where(hm, q2 * q2, 0.0), axis=1, keepdims=True)) * kn_ref[hh][0:1, 0:1]

            def go_on(j, carry, at=at, qk_max=qk_max, hh=hh):
                bias_max = at - jnp.min(cr_ref[2 * p_ + hh, j], axis=1, keepdims=True)
                return jnp.max(qk_max + bias_max) > SKIP_BELOW

            def tile(j, carry, masked, qh=qh, doh=doh, delta=delta, at=at, hh=hh):
                dq, rs = carry
                sl = pl.ds(pl.multiple_of(j * blk, blk), blk)
                kb, vb = k_ref[sl, :], v_ref[sl, :]
                sc = lax.dot_general(qh, kb, _DN_NT, preferred_element_type=F32)
                sc = sc + (at - cr_ref[2 * p_ + hh, j])
                if masked:
                    sc = jnp.where(col <= row, sc, NEG_INF)
                pm = jnp.exp(sc)
                dp = lax.dot_general(doh, vb, _DN_NT, preferred_element_type=F32)
                ds = pm * (dp - delta)
                dsb = ds.astype(BF16)
                dk_acc[sl, :] += lax.dot_general(dsb, qh, _DN_TN, preferred_element_type=F32)
                dv_acc[sl, :] += lax.dot_general(pm.astype(BF16), doh, _DN_TN, preferred_element_type=F32)
                gc_ref[hh, j] += -jnp.sum(ds, axis=0, keepdims=True)
                return dq + jnp.dot(dsb, kb, preferred_element_type=F32), rs + jnp.sum(ds, axis=1, keepdims=True)

            carry = tile(i, (jnp.zeros((blk, LANES), F32), jnp.zeros((blk, 1), F32)), True)
            heads.append((lambda j, c, tile=tile: tile(j, c, False), go_on, carry))
        swept = _sweep_left_pair(i, [h[2] for h in heads], [h[0] for h in heads], [h[1] for h in heads])
        for hh, (dq, rs) in enumerate(swept):
            dqs.append(dq)
            gc_ref[hh, i] += jnp.transpose(jnp.broadcast_to(rs, (blk, LANES)))[0:1, :]
        dq_ref[...] = (jnp.where(_head_mask(dqs[0].shape, 0), dqs[0], dqs[1]) * SCALE).astype(BF16)

        @pl.when(i == nq - 1)
        def _():
            dk_ref[...] = dk_acc[...].astype(BF16)
            dv_ref[...] = dv_acc[...].astype(BF16)

        @pl.when((p_ == npair - 1) & (i == nq - 1))
        def _():
            complete()

    seq = lambda base: pl.BlockSpec((s, LANES), lambda p, i: (0, base + p))
    qblk = lambda base: pl.BlockSpec((blk, LANES), lambda p, i: (i, base + p))
    rep = pl.BlockSpec((2, blk, LANES), lambda p, i: (p, i, 0))
    half = jax.ShapeDtypeStruct((s, FOX_W), BF16)
    res = pl.pallas_call(
        body, name="fox_bwd_rs", grid=(npair, nq),
        out_shape=(half, half, half, jax.ShapeDtypeStruct((N_FOX, nq, 1, blk), F32)) + _quarter_shapes(grads),
        in_specs=[qblk(0), seq(npair), seq(2 * npair), qblk(0), qblk(0), rep,
                  pl.BlockSpec((N_FOX, nq, 1, blk), lambda p, i: (0, 0, 0, 0))] + [_ANY] * ng,
        out_specs=(qblk(0), seq(0), seq(0), pl.BlockSpec((2, nq, 1, blk), lambda p, i: (p, 0, 0, 0)))
        + tuple([_ANY] * ng),
        scratch_shapes=[pltpu.VMEM((s, LANES), F32), pltpu.VMEM((s, LANES), F32), pltpu.VMEM((2, 8, LANES), F32)]
        + _sibling_exchange_scratch(ng),
        compiler_params=pltpu.CompilerParams(dimension_semantics=("arbitrary", "arbitrary")),
    )(proj, proj, proj, do, o, acol, crow, *grads)
    return res[0], res[1], res[2], res[3], list(res[4:])


def _sb_logs(z):
    neg = -(jnp.maximum(z, 0.0) + jnp.log1p(jnp.exp(-jnp.abs(z))))
    return neg, z + neg


def _split_dot(x, tri):
    hi = x.astype(BF16)
    lo = (x - hi.astype(F32)).astype(BF16)
    return jnp.dot(hi, tri, preferred_element_type=F32) + jnp.dot(lo, tri, preferred_element_type=F32)


def _sb_fwd(proj):
    s = proj.shape[0]
    blk = min(ATTN_BLOCK, s)
    nq = s // blk
    npair = N_SB // 2
    base = 3 * (N_FOX // 2)

    def body(q_ref, k_ref, v_ref, o_ref, r_ref):
        i = pl.program_id(1)
        q2 = q_ref[...].astype(F32) * SCALE
        row = lax.broadcasted_iota(jnp.int32, (blk, blk), 0)
        col = lax.broadcasted_iota(jnp.int32, (blk, blk), 1)
        strict = col < row
        tri = jnp.where(row > col, 1.0, 0.0).astype(BF16)
        lane = lax.broadcasted_iota(jnp.int32, (blk, LANES), 1)
        outs, heads = [], []
        for hh in range(2):
            qh = jnp.where(_head_mask(q2.shape, hh), q2, 0.0).astype(BF16)

            def tile(j, carry, masked, qh=qh):
                rsum, acc, rbuf = carry
                sl = pl.ds(pl.multiple_of(j * blk, blk), blk)
                kb, vb = k_ref[sl, :], v_ref[sl, :]
                z = lax.dot_general(qh, kb, _DN_NT, preferred_element_type=F32)
                l1m, lb = _sb_logs(z)
                if masked:
                    l1m = jnp.where(strict, l1m, 0.0)
                sx = _split_dot(l1m, tri)
                a = jnp.exp(lb + sx + rsum)
                if masked:
                    a = jnp.where(strict, a, 0.0)
                acc = acc + jnp.dot(a.astype(BF16), vb, preferred_element_type=F32)
                rbuf = jnp.where(lane == j, rsum, rbuf)
                return rsum + jnp.sum(l1m, axis=1, keepdims=True), acc, rbuf

            init = (jnp.zeros((blk, 1), F32), jnp.zeros((blk, LANES), F32), jnp.full((blk, LANES), NEG_INF, F32))
            heads.append((lambda j, c, tile=tile: tile(j, c, False), lambda j, c: jnp.max(c[0]) > SKIP_BELOW,
                          tile(i, init, True)))
        swept = _sweep_left_pair(i, [h[2] for h in heads], [h[0] for h in heads], [h[1] for h in heads])
        for hh, (_, acc, rbuf) in enumerate(swept):
            outs.append(acc)
            r_ref[hh] = rbuf
        o_ref[...] = jnp.where(_head_mask(outs[0].shape, 0), outs[0], outs[1]).astype(BF16)

    seq = lambda b: pl.BlockSpec((s, LANES), lambda p, i: (0, b + p))
    return pl.pallas_call(
        body, name="sb_fwd", grid=(npair, nq),
        out_shape=(jax.ShapeDtypeStruct((s, SB_W), BF16), jax.ShapeDtypeStruct((N_SB, s, LANES), F32)),
        in_specs=[pl.BlockSpec((blk, LANES), lambda p, i: (i, base + p)), seq(base + npair), seq(base + 2 * npair)],
        out_specs=(pl.BlockSpec((blk, LANES), lambda p, i: (i, p)),
                   pl.BlockSpec((2, blk, LANES), lambda p, i: (p, i, 0))),
        compiler_params=pltpu.CompilerParams(dimension_semantics=("parallel", "arbitrary")),
    )(proj, proj, proj)


def _sb_bwd(proj, do, rsave, parts):
    s = proj.shape[0]
    blk = min(ATTN_BLOCK, s)
    nq = s // blk
    npair = N_SB // 2
    base = 3 * (N_FOX // 2)
    ng = len(parts)

    def body(*refs):
        q_ref, k_ref, v_ref, do_ref, r_ref = refs[:5]
        p_refs = refs[5:5 + ng]
        dq_ref, dk_ref, dv_ref = refs[5 + ng:8 + ng]
        x_refs = refs[8 + ng:8 + 2 * ng]
        dk_acc, dv_acc = refs[8 + 2 * ng:10 + 2 * ng]
        p_, i = pl.program_id(0), pl.program_id(1)
        issue, complete = _chip_exchange_steps(p_refs, x_refs, *refs[10 + 2 * ng:])

        @pl.when((p_ == 0) & (i == 0))
        def _():
            issue()

        @pl.when(i == 0)
        def _():
            dk_acc[...] = jnp.zeros_like(dk_acc)
            dv_acc[...] = jnp.zeros_like(dv_acc)

        q2 = q_ref[...].astype(F32) * SCALE
        do2 = do_ref[...]
        row = lax.broadcasted_iota(jnp.int32, (blk, blk), 0)
        col = lax.broadcasted_iota(jnp.int32, (blk, blk), 1)
        strict = col < row
        tri_suffix = jnp.where(row > col, 1.0, 0.0).astype(BF16)
        tri_prefix = jnp.where(row < col, 1.0, 0.0).astype(BF16)
        lane = lax.broadcasted_iota(jnp.int32, (blk, LANES), 1)
        heads = []
        for hh in range(2):
            hm = _head_mask(q2.shape, hh)
            qh = jnp.where(hm, q2, 0.0).astype(BF16)
            doh = jnp.where(hm, do2, jnp.zeros_like(do2))
            rbuf = r_ref[hh]

            def tile(j, carry, masked, qh=qh, doh=doh, rbuf=rbuf):
                pre, dq = carry
                sl = pl.ds(pl.multiple_of(j * blk, blk), blk)
                kb, vb = k_ref[sl, :], v_ref[sl, :]
                z = lax.dot_general(qh, kb, _DN_NT, preferred_element_type=F32)
                l1m, lb = _sb_logs(z)
                beta, one_m_beta = _sigmoid_parts(z)
                if masked:
                    l1m = jnp.where(strict, l1m, 0.0)
                sx = _split_dot(l1m, tri_suffix)
                rj = jnp.sum(jnp.where(lane == j, rbuf, 0.0), axis=1, keepdims=True)
                a = jnp.exp(lb + sx + rj)
                if masked:
                    a = jnp.where(strict, a, 0.0)
                da = lax.dot_general(doh, vb, _DN_NT, preferred_element_type=F32)
                g = a * da
                px = _split_dot(g, tri_prefix) + pre
                dz = g * one_m_beta - beta * px
                if masked:
                    dz = jnp.where(strict, dz, 0.0)
                dzb = dz.astype(BF16)
                dk_acc[sl, :] += lax.dot_general(dzb, qh, _DN_TN, preferred_element_type=F32)
                dv_acc[sl, :] += lax.dot_general(a.astype(BF16), doh, _DN_TN, preferred_element_type=F32)
                return pre + jnp.sum(g, axis=1, keepdims=True), dq + jnp.dot(dzb, kb, preferred_element_type=F32)

            reach = jnp.max(rbuf, axis=0, keepdims=True)
            dead = (reach <= SKIP_BELOW) & (lane[0:1, :] <= i)
            heads.append((tile, jnp.sum(jnp.where(dead, 1.0, 0.0)).astype(jnp.int32)))
        first = jnp.minimum(heads[0][1], heads[1][1])
        zero = (jnp.zeros((blk, 1), F32), jnp.zeros((blk, LANES), F32))
        carries = lax.fori_loop(first, i, lambda t, c: tuple(h[0](t, ch, False) for h, ch in zip(heads, c)),
                                (zero, zero))
        dqs = [h[0](i, ch, True)[1] for h, ch in zip(heads, carries)]
        dq_ref[...] = (jnp.where(_head_mask(dqs[0].shape, 0), dqs[0], dqs[1]) * SCALE).astype(BF16)

        @pl.when(i == nq - 1)
        def _():
            dk_ref[...] = dk_acc[...].astype(BF16)
            dv_ref[...] = dv_acc[...].astype(BF16)

        @pl.when((p_ == npair - 1) & (i == nq - 1))
        def _():
            complete()

    seq = lambda b: pl.BlockSpec((s, LANES), lambda p, i: (0, b + p))
    qblk = lambda b: pl.BlockSpec((blk, LANES), lambda p, i: (i, b + p))
    half = jax.ShapeDtypeStruct((s, SB_W), BF16)
    res = pl.pallas_call(
        body, name="sb_bwd_rs", grid=(npair, nq),
        out_shape=(half, half, half) + _same_shapes(parts),
        in_specs=[qblk(base), seq(base + npair), seq(base + 2 * npair), qblk(npair),
                  pl.BlockSpec((2, blk, LANES), lambda p, i: (p, i, 0))] + [_ANY] * ng,
        out_specs=(qblk(0), seq(0), seq(0)) + tuple([_ANY] * ng),
        scratch_shapes=[pltpu.VMEM((s, LANES), F32), pltpu.VMEM((s, LANES), F32)] + _chip_exchange_scratch(ng),
        compiler_params=pltpu.CompilerParams(dimension_semantics=("arbitrary", "arbitrary")),
    )(proj, proj, proj, do, rsave, *parts)
    return res[0], res[1], res[2], list(res[3:])


SWA_ROWS = 512
Q_W = N_Q * HEAD_DIM
KV_W = N_KV * HEAD_DIM
KV_PAIRS = KV_W // LANES
Q_PER_KVPAIR = Q_W // KV_PAIRS


def _lane_swap(x):
    xf = x.astype(F32)
    parts = [pltpu.roll(xf[:, j * LANES:(j + 1) * LANES], HEAD_DIM, 1) for j in range(x.shape[1] // LANES)]
    return (parts[0] if len(parts) == 1 else jnp.concatenate(parts, axis=1)).astype(x.dtype)


def _swa_specs(s):
    w = WINDOW
    ts = min(SWA_ROWS, s)
    nw = ts // w
    qcols = pl.BlockSpec((ts, Q_PER_KVPAIR), lambda kp, i: (i, kp))
    kbase, vbase = Q_W // LANES, (Q_W + KV_W) // LANES
    prev = lambda base: pl.BlockSpec((w, LANES), lambda kp, i: (jnp.maximum(i * nw - 1, 0), base + kp))
    cur = lambda base: pl.BlockSpec((ts, LANES), lambda kp, i: (i, base + kp))
    vec = pl.BlockSpec((1, Q_PER_KVPAIR), lambda kp, i: (0, kp))
    return ts, nw, qcols, prev(kbase), cur(kbase), prev(vbase), cur(vbase), vec


def _swa_fwd(qk, vsrc, sink_row):
    s = qk.shape[0]
    w = WINDOW
    ts, nw, qcols, kprev, kcur, vprev, vcur, vec = _swa_specs(s)

    def body(q_ref, kp_ref, kc_ref, vp_ref, vc_ref, s_ref, o_ref, lse_ref):
        i = pl.program_id(1)
        k2 = jnp.concatenate([kp_ref[...], kc_ref[...]], axis=0)
        v2 = jnp.concatenate([vp_ref[...], vc_ref[...]], axis=0)
        ksw, vsw = _lane_swap(k2), _lane_swap(v2)
        row = lax.broadcasted_iota(jnp.int32, (w, 2 * w), 0)
        col = lax.broadcasted_iota(jnp.int32, (w, 2 * w), 1)
        band = (col > row) & (col <= row + w)
        first_half = _head_mask((w, LANES), 0)
        for u in range(nw):
            valid = band if u > 0 else band & ((col >= w) | (i > 0))
            rows, keys = slice(u * w, (u + 1) * w), slice(u * w, (u + 2) * w)
            for pr in range(Q_PER_KVPAIR // LANES):
                gh = pr // 2
                cols = slice(pr * LANES, (pr + 1) * LANES)
                q2 = q_ref[rows, cols].astype(F32) * SCALE
                outs, lses = [], []
                for hh in range(2):
                    kk = (k2 if hh == gh else ksw)[keys, :]
                    vv = (v2 if hh == gh else vsw)[keys, :]
                    qm = jnp.where(_head_mask(q2.shape, hh), q2, 0.0).astype(BF16)
                    sc = jnp.where(valid, lax.dot_general(qm, kk, _DN_NT, preferred_element_type=F32), NEG_INF)
                    sink = s_ref[:, pr * LANES + HEAD_DIM * hh: pr * LANES + HEAD_DIM * hh + 1]
                    m = jnp.maximum(jnp.max(sc, axis=1, keepdims=True), sink)
                    e = jnp.exp(sc - m)
                    l = jnp.sum(e, axis=1, keepdims=True) + jnp.exp(sink - m)
                    outs.append(jnp.dot(e.astype(BF16), vv, preferred_element_type=F32) / l)
                    lses.append(m + jnp.log(l))
                o_ref[rows, cols] = jnp.where(first_half, outs[0], outs[1]).astype(BF16)
                lse_ref[rows, cols] = jnp.where(first_half, lses[0], lses[1])

    return pl.pallas_call(
        body, name="swa_fwd", grid=(KV_PAIRS, s // ts),
        out_shape=(jax.ShapeDtypeStruct((s, Q_W), BF16), jax.ShapeDtypeStruct((s, Q_W), F32)),
        in_specs=[qcols, kprev, kcur, vprev, vcur, vec], out_specs=(qcols, qcols),
    )(qk, qk, qk, vsrc, vsrc, sink_row)


def _swa_bwd_dq(qk, vsrc, sink_row, do, o, lse):
    s = qk.shape[0]
    w = WINDOW
    ts, nw, qcols, kprev, kcur, vprev, vcur, vec = _swa_specs(s)

    def body(q_ref, kp_ref, kc_ref, vp_ref, vc_ref, s_ref, do_ref, o_ref, lse_ref, dq_ref, dsink_ref):
        i = pl.program_id(1)

        @pl.when(i == 0)
        def _():
            dsink_ref[...] = jnp.zeros_like(dsink_ref)

        k2 = jnp.concatenate([kp_ref[...], kc_ref[...]], axis=0)
        v2 = jnp.concatenate([vp_ref[...], vc_ref[...]], axis=0)
        ksw, vsw = _lane_swap(k2), _lane_swap(v2)
        row = lax.broadcasted_iota(jnp.int32, (w, 2 * w), 0)
        col = lax.broadcasted_iota(jnp.int32, (w, 2 * w), 1)
        band = (col > row) & (col <= row + w)
        first_half = _head_mask((w, LANES), 0)
        lane_all = lax.broadcasted_iota(jnp.int32, (1, Q_PER_KVPAIR), 1)
        dsink = jnp.zeros((1, Q_PER_KVPAIR), F32)
        for u in range(nw):
            valid = band if u > 0 else band & ((col >= w) | (i > 0))
            rows, keys = slice(u * w, (u + 1) * w), slice(u * w, (u + 2) * w)
            for pr in range(Q_PER_KVPAIR // LANES):
                gh = pr // 2
                cols = slice(pr * LANES, (pr + 1) * LANES)
                q2 = q_ref[rows, cols].astype(F32) * SCALE
                do2 = do_ref[rows, cols]
                prod = do2.astype(F32) * o_ref[rows, cols].astype(F32)
                lse2 = lse_ref[rows, cols]
                dqs = []
                for hh in range(2):
                    hm = _head_mask(q2.shape, hh)
                    lo = pr * LANES + HEAD_DIM * hh
                    kk = (k2 if hh == gh else ksw)[keys, :]
                    vv = (v2 if hh == gh else vsw)[keys, :]
                    qm = jnp.where(hm, q2, 0.0).astype(BF16)
                    sc = jnp.where(valid, lax.dot_general(qm, kk, _DN_NT, preferred_element_type=F32), NEG_INF)
                    lse_c = lse2[:, HEAD_DIM * hh: HEAD_DIM * hh + 1]
                    pm = jnp.exp(sc - lse_c)
                    doh = jnp.where(hm, do2, jnp.zeros_like(do2))
                    delta = jnp.sum(jnp.where(hm, prod, 0.0), axis=1, keepdims=True)
                    ds = pm * (lax.dot_general(doh, vv, _DN_NT, preferred_element_type=F32) - delta)
                    dqs.append(jnp.dot(ds.astype(BF16), kk, preferred_element_type=F32))
                    part = jnp.sum(-jnp.exp(s_ref[:, lo:lo + 1] - lse_c) * delta, axis=0, keepdims=True)
                    dsink = dsink + jnp.where((lane_all >= lo) & (lane_all < lo + HEAD_DIM), part, 0.0)
                dq_ref[rows, cols] = (jnp.where(first_half, dqs[0], dqs[1]) * SCALE).astype(BF16)
        dsink_ref[...] += dsink

    return pl.pallas_call(
        body, name="swa_bwd_dq", grid=(KV_PAIRS, s // ts),
        out_shape=(jax.ShapeDtypeStruct((s, Q_W), BF16), jax.ShapeDtypeStruct((1, Q_W), F32)),
        in_specs=[qcols, kprev, kcur, vprev, vcur, vec, qcols, qcols, qcols], out_specs=(qcols, vec),
        compiler_params=pltpu.CompilerParams(dimension_semantics=("parallel", "arbitrary")),
    )(qk, qk, qk, vsrc, vsrc, sink_row, do, o, lse)


def _swa_bwd_dkv(qk, vsrc, do, o, lse):
    s = qk.shape[0]
    w = WINDOW
    ts = min(SWA_ROWS, s)
    nw = ts // w
    nstep = s // ts
    last_window = s // w - 1

    def body(k_ref, v_ref, qc_ref, qn_ref, doc_ref, don_ref, oc_ref, on_ref, lc_ref, ln_ref, dk_ref, dv_ref):
        j = pl.program_id(1)
        cat = lambda a_ref, b_ref: jnp.concatenate([a_ref[...], b_ref[...]], axis=0)
        qcat, docat, ocat, lcat = cat(qc_ref, qn_ref), cat(doc_ref, don_ref), cat(oc_ref, on_ref), cat(lc_ref, ln_ref)
        qsw, dosw = _lane_swap(qcat), _lane_swap(docat)
        row = lax.broadcasted_iota(jnp.int32, (2 * w, w), 0)
        col = lax.broadcasted_iota(jnp.int32, (2 * w, w), 1)
        band = (col <= row) & (row < col + w)
        has_next = j + 1 < nstep
        for wi in range(nw):
            valid = band if wi < nw - 1 else band & ((row < w) | has_next)
            keys, qrows = slice(wi * w, (wi + 1) * w), slice(wi * w, (wi + 2) * w)
            kw, vw = k_ref[keys, :], v_ref[keys, :]
            dk = jnp.zeros((w, LANES), F32)
            dv = jnp.zeros((w, LANES), F32)
            for pr in range(Q_PER_KVPAIR // LANES):
                gh = pr // 2
                cols = slice(pr * LANES, (pr + 1) * LANES)
                prod = docat[qrows, cols].astype(F32) * ocat[qrows, cols].astype(F32)
                lse2 = lcat[qrows, cols]
                to_kv = _head_mask(prod.shape, gh)
                for hh in range(2):
                    q_src, do_src = (qcat, docat) if hh == gh else (qsw, dosw)
                    q_al = jnp.where(to_kv, q_src[qrows, cols].astype(F32) * SCALE, 0.0).astype(BF16)
                    do_al = jnp.where(to_kv, do_src[qrows, cols], jnp.zeros((2 * w, LANES), BF16))
                    sc = jnp.where(valid, lax.dot_general(q_al, kw, _DN_NT, preferred_element_type=F32), NEG_INF)
                    pm = jnp.exp(sc - lse2[:, HEAD_DIM * hh: HEAD_DIM * hh + 1])
                    delta = jnp.sum(jnp.where(_head_mask(prod.shape, hh), prod, 0.0), axis=1, keepdims=True)
                    ds = pm * (lax.dot_general(do_al, vw, _DN_NT, preferred_element_type=F32) - delta)
                    dk = dk + lax.dot_general(ds.astype(BF16), q_al, _DN_TN, preferred_element_type=F32)
                    dv = dv + lax.dot_general(pm.astype(BF16), do_al, _DN_TN, preferred_element_type=F32)
            dk_ref[keys, :] = dk.astype(BF16)
            dv_ref[keys, :] = dv.astype(BF16)

    kbase, vbase = Q_W // LANES, (Q_W + KV_W) // LANES
    kv = lambda base: pl.BlockSpec((ts, LANES), lambda kp, j: (j, base + kp))
    same = pl.BlockSpec((ts, Q_PER_KVPAIR), lambda kp, j: (j, kp))
    nxt = pl.BlockSpec((w, Q_PER_KVPAIR), lambda kp, j: (jnp.minimum((j + 1) * nw, last_window), kp))
    out = pl.BlockSpec((ts, LANES), lambda kp, j: (j, kp))
    shp = jax.ShapeDtypeStruct((s, KV_W), BF16)
    return pl.pallas_call(
        body, name="swa_bwd_dkv", grid=(KV_PAIRS, nstep),
        out_shape=(shp, shp),
        in_specs=[kv(kbase), kv(vbase), same, nxt, same, nxt, same, nxt, same, nxt], out_specs=(out, out),
    )(qk, vsrc, qk, qk, do, do, o, o, lse, lse)


def _ffn_ple_fwd(h1, p_l, g_ffn, g_ple, w_gu, w_down, w_pg, w_pp, tag):
    hn2 = _rms_fwd(h1, g_ffn, f"rms_ffn_{tag}")
    ab, u = _gate_up_swiglu(hn2, w_gu, f"mm_gate_up_swiglu_{tag}")
    h2 = _mm(u, w_down, out_dtype=F32, res=h1, name=f"mm_down_{tag}")
    hn3 = _rms_fwd(h2, g_ple, f"rms_ple_{tag}")
    gl = _mm(hn3, w_pg, name=f"mm_ple_gate_{tag}")
    pp = _mm(p_l, w_pp, name=f"mm_ple_proj_{tag}")
    h3 = _ple_fwd(h2, gl, pp, f"ple_{tag}")
    return h3, dict(h1=h1, hn2=hn2, ab=ab, u=u, h2=h2, hn3=hn3, gl=gl, pp=pp)


def _ffn_ple_bwd(dh3, sv, p_l, g_ffn, g_ple, w_gu, w_down, w_pg, tag):
    dgl, dpp = _ple_bwd(dh3, sv["gl"], sv["pp"], f"ple_bwd_{tag}")
    d_wpp = _mm(p_l, dpp, ta=True, out_dtype=F32, name=f"mm_dw_ple_proj_{tag}")
    d_wpg = _mm(sv["hn3"], dgl, ta=True, out_dtype=F32, name=f"mm_dw_ple_gate_{tag}")
    dhn3 = _mm(dgl, w_pg, tb=True, out_dtype=F32, name=f"mm_dx_ple_gate_{tag}")
    dh2, dh2b, dg_ple = _rms_bwd(sv["h2"], g_ple, [dhn3], dh3, f"rms_ple_bwd_{tag}", True)
    d_wdown = _mm(sv["u"], dh2b, ta=True, out_dtype=F32, name=f"mm_dw_down_{tag}")
    dab = _down_t_swiglu_bwd(dh2b, w_down, sv["ab"], f"mm_dx_down_swiglu_bwd_{tag}")
    d_wgu = _mm(sv["hn2"], dab, ta=True, out_dtype=F32, group="out", group_width=2 * HID_PAD,
                name=f"mm_dw_gate_up_{tag}")
    dhn2 = _mm(dab, w_gu, tb=True, out_dtype=F32, group="k", name=f"mm_dx_gate_up_{tag}")
    dh1, dh1b, dg_ffn = _rms_bwd(sv["h1"], g_ffn, [dhn2], dh2, f"rms_ffn_bwd_{tag}", True)
    return dh1, dh1b, dict(d_wpp=d_wpp, d_wpg=d_wpg, d_wdown=d_wdown, d_wgu=d_wgu, dg_ple=dg_ple, dg_ffn=dg_ffn)


def _row_form(cum, blk):
    s = cum.shape[0]
    return cum[:, :N_FOX].T.reshape(N_FOX, s // blk, 1, blk)


def _col_form(cum):
    s = cum.shape[0]
    return jnp.broadcast_to(cum[:, :N_FOX].T[:, :, None], (N_FOX, s, LANES))


def kernel(x, p, positions, norm_mix, norm_ffn, norm_ple, norm_final, ev_w_in, ev_b_f, ev_w_out, od_w_in, od_sinks, od_w_out, ffn_w_gate, ffn_w_up, ffn_w_down, ple_w_proj, ple_w_gate, loss_target, m_norm_mix, m_norm_ffn, m_norm_ple, m_norm_final, m_ev_w_in, m_ev_b_f, m_ev_w_out, m_od_w_in, m_od_sinks, m_od_w_out, m_ffn_w_gate, m_ffn_w_up, m_ffn_w_down, m_ple_w_proj, m_ple_w_gate, v_norm_mix, v_norm_ffn, v_norm_ple, v_norm_final, v_ev_w_in, v_ev_b_f, v_ev_w_out, v_od_w_in, v_od_sinks, v_od_w_out, v_ffn_w_gate, v_ffn_w_up, v_ffn_w_down, v_ple_w_proj, v_ple_w_gate):
    s = x.shape[1]
    blk = min(ATTN_BLOCK, s)
    big_w = [ev_w_in, ev_w_out, od_w_in, od_w_out, ffn_w_gate, ffn_w_up, ffn_w_down, ple_w_proj, ple_w_gate]
    big_m = [m_ev_w_in, m_ev_w_out, m_od_w_in, m_od_w_out, m_ffn_w_gate, m_ffn_w_up, m_ffn_w_down, m_ple_w_proj, m_ple_w_gate]
    big_v = [v_ev_w_in, v_ev_w_out, v_od_w_in, v_od_w_out, v_ffn_w_gate, v_ffn_w_up, v_ffn_w_down, v_ple_w_proj, v_ple_w_gate]

    s_rows, s_gu, s_evin, s_odin, s_pp = _group_shards(*big_w, BF16)
    (g_evin,) = _all_gather_weights([s_evin])
    d = D_MODEL
    w_in0 = _from_owner_cols(g_evin)
    w_qkv = w_in0[:, :QKV_W]
    w_f = jnp.pad(w_in0[:, QKV_W:], ((0, 0), (0, LANES - N_FOX)))

    h0 = x[0]
    target = loss_target[0]
    p_b = [p[l, 0].astype(BF16) for l in range(2)]
    g_mix = [norm_mix[l][None, :] for l in range(2)]
    g_ffn = [norm_ffn[l][None, :] for l in range(2)]
    g_ple = [norm_ple[l][None, :] for l in range(2)]
    b_f = jnp.pad(ev_b_f, ((0, 0), (0, LANES - N_FOX)))

    half = HEAD_DIM // 2
    inv = ROPE_THETA ** (-jnp.arange(half, dtype=F32) / half)
    ang = positions[0].astype(F32)[:, None] * inv
    cos_t = jnp.tile(jnp.cos(ang), (1, 4))
    sin_t = jnp.tile(jnp.concatenate([-jnp.sin(ang), jnp.sin(ang)], axis=1), (1, 2))

    hn1 = _rms_fwd(h0, g_mix[0], "rms_mix_0")
    proj0 = _mm(hn1, w_qkv, name="mm_in_0")
    flog = _mm(hn1, w_f, out_dtype=F32, name="mm_fgate_0")
    cum = _fgate_fwd(flog, b_f)
    crow = _row_form(cum, blk)
    o_fox, acol, (g_rows, g_gu, g_odin, g_pp) = _fox_fwd(proj0, _col_form(cum), crow, [s_rows, s_gu, s_odin, s_pp])
    w_oi = _from_owner_cols(g_odin)
    w_eo = g_rows[:, 0:128].reshape(d, d)
    w_oo = g_rows[:, 128:256].reshape(d, d)
    w_down = [g_rows[:, 256 + HID_PAD * l: 256 + HID_PAD * (l + 1)].reshape(D_FF_PAD, d) for l in range(2)]
    w_pg = [g_rows[:, 1024 + 128 * l: 1152 + 128 * l].reshape(d, d) for l in range(2)]
    w_gu = [g_gu[:, d * l: d * (l + 1)] for l in range(2)]
    w_pp = [_from_owner_cols(g_pp[:, PLE_DIM * l: PLE_DIM * (l + 1)]) for l in range(2)]
    o_sb, rsave = _sb_fwd(proj0)
    o0 = jnp.concatenate([o_fox, o_sb], axis=1)
    h1 = _mm(o0, w_eo, out_dtype=F32, res=h0, name="mm_out_0")
    h3, sv0 = _ffn_ple_fwd(h1, p_b[0], g_ffn[0], g_ple[0], w_gu[0], w_down[0], w_pg[0], w_pp[0], "0")

    hn1b = _rms_fwd(h3, g_mix[1], "rms_mix_1")
    proj1 = _mm(hn1b, w_oi, name="mm_in_1")
    qk_r = _rope(proj1, Q_W + KV_W, cos_t, sin_t, 1.0, "rope_fwd")
    sink_row = jnp.repeat(od_sinks[0], HEAD_DIM)[None, :]
    o1, lse1 = _swa_fwd(qk_r, proj1, sink_row)
    h4 = _mm(o1, w_oo, out_dtype=F32, res=h3, name="mm_out_1")
    h6, sv1 = _ffn_ple_fwd(h4, p_b[1], g_ffn[1], g_ple[1], w_gu[1], w_down[1], w_pg[1], w_pp[1], "1")

    loss_part, dh6, dg_final = _final_norm_loss(h6, norm_final[None, :], target)

    dh4, dh4b, gr1 = _ffn_ple_bwd(dh6, sv1, p_b[1], g_ffn[1], g_ple[1], w_gu[1], w_down[1], w_pg[1], "1")
    do1 = _mm(dh4b, w_oo, tb=True, name="mm_dx_out_1")
    d_woo = _mm(o1, dh4b, ta=True, out_dtype=F32, name="mm_dw_out_1")
    dq1, dsink_row = _swa_bwd_dq(qk_r, proj1, sink_row, do1, o1, lse1)
    dk1, dv1 = _swa_bwd_dkv(qk_r, proj1, do1, o1, lse1)
    dqk = _rope(jnp.concatenate([dq1, dk1], axis=1), Q_W + KV_W, cos_t, sin_t, -1.0, "rope_bwd")
    dproj1 = jnp.concatenate([dqk, dv1], axis=1)
    d_woi = _mm(hn1b, dproj1, ta=True, out_dtype=F32, name="mm_dw_in_1")
    dhn1b = _mm(dproj1, w_oi, tb=True, out_dtype=F32, name="mm_dx_in_1")
    dh3, dg_mix1 = _rms_bwd(h3, g_mix[1], [dhn1b], dh4, "rms_mix_bwd_1", False)

    dh1, dh1b, gr0 = _ffn_ple_bwd(dh3, sv0, p_b[0], g_ffn[0], g_ple[0], w_gu[0], w_down[0], w_pg[0], "0")
    do0 = _mm(dh1b, w_eo, tb=True, name="mm_dx_out_0")
    d_weo = _mm(o0, dh1b, ta=True, out_dtype=F32, name="mm_dw_out_0")
    by_rows = lambda g, r: g.reshape(N_DEV, r, d)
    early = [
        jnp.concatenate([by_rows(d_weo, 128), by_rows(d_woo, 128), by_rows(gr0["d_wdown"], HID_PAD),
                         by_rows(gr1["d_wdown"], HID_PAD), by_rows(gr0["d_wpg"], 128), by_rows(gr1["d_wpg"], 128)],
                        axis=1),
        jnp.concatenate([gr0["d_wgu"], gr1["d_wgu"]], axis=1),
        _by_owner_cols(d_woi),
        jnp.concatenate([_by_owner_cols(gr0["d_wpp"]), _by_owner_cols(gr1["d_wpp"])], axis=1),
    ]
    early_tags = ("rows", "gu", "od_in", "pp")
    core = lax.axis_index("c").astype(jnp.int32).reshape(1)
    dq_f, dk_f, dv_f, gc, early_sib = _fox_bwd(proj0, do0, o0, acol, crow, early)
    early_part = [_rs_chip_sum(core, g, r, f"rs_chip_sum_{t}") for g, r, t in zip(early, early_sib, early_tags)]
    dq_s, dk_s, dv_s, early_recv = _sb_bwd(proj0, do0, rsave, early_part)
    dproj0 = jnp.concatenate([dq_f, dk_f, dv_f, dq_s, dk_s, dv_s], axis=1)
    gcum = jnp.pad(gc.reshape(N_FOX, s).T, ((0, 0), (0, LANES - N_FOX)))
    dflog, db_f = _fgate_bwd(gcum, flog, b_f)
    d_wqkv = _mm(hn1, dproj0, ta=True, out_dtype=F32, name="mm_dw_in_0")
    d_wf = _mm(hn1, dflog, ta=True, out_dtype=F32, name="mm_dw_fgate_0")
    dhn1 = _mm(dproj0, w_qkv, tb=True, out_dtype=F32, name="mm_dx_in_0")
    dhn1f = _mm(dflog, w_f, tb=True, out_dtype=F32, name="mm_dx_fgate_0")
    grad_x, dg_mix0 = _rms_bwd(h0, g_mix[0], [dhn1, dhn1f], dh1, "rms_mix_bwd_0", False)

    late = [_by_owner_cols(jnp.concatenate([d_wqkv, d_wf[:, :N_FOX]], axis=1))]
    late_sib = _rs_sibling_exchange(late)
    late_recv = _rs_chip_exchange([_rs_chip_sum(core, late[0], late_sib[0], "rs_chip_sum_ev_in")])
    tags = ("rows", "gu", "ev_in", "od_in", "pp")
    chip_recv = [early_recv[0], early_recv[1], late_recv[0], early_recv[2], early_recv[3]]
    w_grp, m_grp, v_grp = (_group_shards(*ws, F32) for ws in (big_w, big_m, big_v))
    updated = [_rs_sum_adamw(r, w_, m_, v_, f"rs_sum_adamw_{t}")
               for r, w_, m_, v_, t in zip(chip_recv, w_grp, m_grp, v_grp, tags)]
    big_g, big_d, big_nm, big_nv = (_ungroup_shards([u[k] for u in updated]) for k in range(4))

    def small_pack(nmix, nffn, nple, nfin, bf, sk, extra):
        last = jnp.concatenate([bf.reshape(-1), sk.reshape(-1), extra.reshape(-1)])
        last = jnp.pad(last, (0, D_MODEL - last.shape[0]))
        return jnp.concatenate([nmix, nffn, nple, nfin.reshape(1, -1), last[None, :]], axis=0)

    small_g = small_pack(jnp.concatenate([dg_mix0, dg_mix1]), jnp.concatenate([gr0["dg_ffn"], gr1["dg_ffn"]]),
                         jnp.concatenate([gr0["dg_ple"], gr1["dg_ple"]]), dg_final,
                         db_f[0, :N_FOX], dsink_row[0, ::HEAD_DIM], loss_part[0, :1])
    zero1 = jnp.zeros((1,), F32)
    small_w = small_pack(norm_mix, norm_ffn, norm_ple, norm_final, ev_b_f, od_sinks, zero1)
    small_m = small_pack(m_norm_mix, m_norm_ffn, m_norm_ple, m_norm_final, m_ev_b_f, m_od_sinks, zero1)
    small_v = small_pack(v_norm_mix, v_norm_ffn, v_norm_ple, v_norm_final, v_ev_b_f, v_od_sinks, zero1)
    sg, sd, sm, sv_ = _small_allreduce_adamw(small_g, small_w, small_m, small_v)

    def small_unpack(t):
        return [t[0:2], t[2:4], t[4:6], t[6], t[7, :N_FOX][None, :], t[7, N_FOX:N_FOX + N_Q][None, :]]

    loss = sg[7, N_FOX + N_Q]

    def ordered(small, big):
        nm, nf, npl, nfin, bf, sk = small_unpack(small)
        ev_in, ev_out, od_in, od_out, fg, fu, fd, pproj, pgate = big
        return [nm, nf, npl, nfin, ev_in, bf, ev_out, od_in, sk, od_out, fg, fu, fd, pproj, pgate]

    return (loss, grad_x[None], *ordered(sg, big_g), *ordered(sd, big_d),
            *ordered(sm, big_nm), *ordered(sv_, big_nv))
```

```python
import functools

import jax
import jax.numpy as jnp
from jax import lax
from jax.experimental import pallas as pl
from jax.experimental.pallas import tpu as pltpu

F32 = jnp.float32
BF16 = jnp.bfloat16

D_MODEL = 1024
HEAD_DIM = 64
N_FOX = 8
N_SB = 8
FOX_W = N_FOX * HEAD_DIM
SB_W = N_SB * HEAD_DIM
QKV_W = 3 * FOX_W + 3 * SB_W
EVEN_IN = QKV_W + N_FOX
N_Q = 16
N_KV = 4
ODD_IN = N_Q * HEAD_DIM + 2 * N_KV * HEAD_DIM
WINDOW = 128
ROPE_THETA = 10000.0
D_FF = 2816
PLE_DIM = 256
EPS = 1e-6
NEG_INF = -1e30
SCALE = HEAD_DIM ** -0.5

ADAM_LR = 0.001
ADAM_B1 = 0.9
ADAM_B2 = 0.999
ADAM_EPS = 1e-08
ADAM_WD = 0.01
ADAM_STEP = 10

N_DEV = 8
LANES = 128
ROW_TILE = 256
ATTN_BLOCK = 256
HID_SHARD = D_FF // N_DEV
HID_PAD = 384
D_FF_PAD = N_DEV * HID_PAD

MESH = pl.DeviceIdType.MESH


def _pick(n, prefs):
    for t in prefs:
        if n % t == 0:
            return t
    return n


def _pad_to(a, axis, size):
    pad = [(0, 0)] * a.ndim
    pad[axis] = (0, size - a.shape[axis])
    return jnp.pad(a, pad)


def _group_shards(ev_in, ev_out, od_in, od_out, gate, up, down, pproj, pgate, dtype):
    rows = jnp.concatenate([ev_out[0], od_out[0], _pad_to(down[0], 0, HID_PAD), _pad_to(down[1], 0, HID_PAD),
                            pgate[0], pgate[1]], axis=0)
    gu = jnp.concatenate([jnp.concatenate([_pad_to(gate[l], 1, HID_PAD), _pad_to(up[l], 1, HID_PAD)], axis=1)
                          for l in range(2)], axis=0)
    groups = [rows, gu, ev_in[0], od_in[0], pproj.reshape(-1, pproj.shape[-1])]
    return [g.astype(dtype) for g in groups]


def _ungroup_shards(groups):
    rows, gu, ev_in, od_in, pp = groups
    d = D_MODEL
    down = jnp.stack([rows[256 + HID_PAD * l: 256 + HID_PAD * l + HID_SHARD] for l in range(2)])
    pgate = jnp.stack([rows[1024 + 128 * l: 1152 + 128 * l] for l in range(2)])
    gate = jnp.stack([gu[d * l: d * (l + 1), :HID_SHARD] for l in range(2)])
    up = jnp.stack([gu[d * l: d * (l + 1), HID_PAD:HID_PAD + HID_SHARD] for l in range(2)])
    return [ev_in[None], rows[None, 0:128], od_in[None], rows[None, 128:256], gate, up, down,
            pp.reshape(2, PLE_DIM, -1), pgate]


def _by_owner_cols(full):
    r, c8 = full.shape
    return full.reshape(r, N_DEV, c8 // N_DEV).transpose(1, 0, 2)


def _from_owner_cols(g):
    n, r, c = g.shape
    return g.transpose(1, 0, 2).reshape(r, n * c)


_ANY = pl.BlockSpec(memory_space=pl.ANY)


def _all_gather_steps(x_refs, out_refs, send_sems, recv_sems, local_sems):
    n = len(x_refs)
    x, y, c = lax.axis_index("x"), lax.axis_index("y"), lax.axis_index("c")
    me, sibling = (x, y, c), (x, y, 1 - c)
    chips = [(1 - x, y), (x, 1 - y), (1 - x, 1 - y)]

    def copy(a, k, block, to, from_input=False):
        px, py, pc = block
        slot = out_refs[a].at[4 * px + 2 * py + pc]
        return pltpu.make_async_remote_copy(
            src_ref=x_refs[a] if from_input else slot, dst_ref=slot,
            send_sem=send_sems.at[7 * a + k], recv_sem=recv_sems.at[7 * a + k],
            device_id=to, device_id_type=MESH)

    def mine():
        return [pltpu.make_async_copy(x_refs[a], out_refs[a].at[4 * x + 2 * y + c], local_sems.at[a]) for a in range(n)]

    def first():
        out = []
        for a in range(n):
            out.append(copy(a, 0, me, sibling, True))
            out += [copy(a, 1 + j, me, (*chip, c), True) for j, chip in enumerate(chips)]
        return out

    def issue():
        for cp in mine() + first():
            cp.start()

    def complete():
        passed = []
        for j, chip in enumerate(chips):
            for a in range(n):
                copy(a, 1 + j, (*chip, c), me).wait_recv()
                passed.append(copy(a, 4 + j, (*chip, c), sibling))
                passed[-1].start()
        for a in range(n):
            copy(a, 0, sibling, me).wait_recv()
            for j, chip in enumerate(chips):
                copy(a, 4 + j, (*chip, 1 - c), me).wait_recv()
        for cp in first() + passed:
            cp.wait_send()
        for cp in mine():
            cp.wait()

    return issue, complete


def _all_gather_scratch(n):
    return [pltpu.SemaphoreType.DMA((7 * n,)), pltpu.SemaphoreType.DMA((7 * n,)), pltpu.SemaphoreType.DMA((n,))]


def _gathered_shapes(shards):
    return tuple(jax.ShapeDtypeStruct((N_DEV,) + s.shape, s.dtype) for s in shards)


def _all_gather_weights(shards):
    n = len(shards)

    def body(*refs):
        issue, complete = _all_gather_steps(refs[:n], refs[n:2 * n], *refs[2 * n:])
        issue()
        complete()

    return pl.pallas_call(
        body, name="ag_weights", out_shape=_gathered_shapes(shards),
        in_specs=[_ANY] * n, out_specs=tuple([_ANY] * n), scratch_shapes=_all_gather_scratch(n),
    )(*shards)


def _sibling_exchange_steps(g_refs, out_refs, send_sems, recv_sems):
    n = len(g_refs)
    x, y, c = lax.axis_index("x"), lax.axis_index("y"), lax.axis_index("c")

    def copies():
        return [pltpu.make_async_remote_copy(
            src_ref=g_refs[a].at[2 * k + (1 - c)], dst_ref=out_refs[a].at[k],
            send_sem=send_sems.at[4 * a + k], recv_sem=recv_sems.at[4 * a + k],
            device_id=(x, y, 1 - c), device_id_type=MESH) for a in range(n) for k in range(4)]

    def issue():
        for cp in copies():
            cp.start()

    def complete():
        for cp in copies():
            cp.wait_recv()
        for cp in copies():
            cp.wait_send()

    return issue, complete


def _sibling_exchange_scratch(n):
    return [pltpu.SemaphoreType.DMA((4 * n,)), pltpu.SemaphoreType.DMA((4 * n,))]


def _quarter_shapes(arrays):
    return tuple(jax.ShapeDtypeStruct((4,) + g.shape[1:], g.dtype) for g in arrays)


def _rs_sibling_exchange(gps):
    n = len(gps)

    def body(*refs):
        issue, complete = _sibling_exchange_steps(refs[:n], refs[n:2 * n], *refs[2 * n:])
        issue()
        complete()

    return pl.pallas_call(
        body, name="rs_sibling_exchange", out_shape=_quarter_shapes(gps),
        in_specs=[_ANY] * n, out_specs=tuple([_ANY] * n), scratch_shapes=_sibling_exchange_scratch(n),
    )(*gps)


def _rs_chip_sum(core, gp, recv, name):
    _, rows, cols = gp.shape
    tr = ROW_TILE

    def body(core_ref, a_ref, b_ref, o_ref):
        o_ref[...] = (a_ref[...] + b_ref[...]).astype(BF16)

    return pl.pallas_call(
        body, name=name,
        out_shape=jax.ShapeDtypeStruct((4, rows, cols), BF16),
        grid_spec=pltpu.PrefetchScalarGridSpec(
            num_scalar_prefetch=1, grid=(4, rows // tr),
            in_specs=[pl.BlockSpec((1, tr, cols), lambda k, r, cr: (2 * k + cr[0], r, 0)),
                      pl.BlockSpec((1, tr, cols), lambda k, r, cr: (k, r, 0))],
            out_specs=pl.BlockSpec((1, tr, cols), lambda k, r, cr: (k, r, 0))),
    )(core, gp, recv)


def _chip_exchange_steps(p_refs, out_refs, send_sems, recv_sems, local_sems):
    n = len(p_refs)
    x, y, c = lax.axis_index("x"), lax.axis_index("y"), lax.axis_index("c")
    my_chip = 2 * x + y

    def mine():
        return [pltpu.make_async_copy(p_refs[a].at[my_chip], out_refs[a].at[my_chip], local_sems.at[a])
                for a in range(n)]

    def copies():
        return [pltpu.make_async_remote_copy(
            src_ref=p_refs[a].at[2 * px + py], dst_ref=out_refs[a].at[my_chip],
            send_sem=send_sems.at[3 * a + j], recv_sem=recv_sems.at[3 * a + j],
            device_id=(px, py, c), device_id_type=MESH)
            for a in range(n) for j, (px, py) in enumerate([(1 - x, y), (x, 1 - y), (1 - x, 1 - y)])]

    def issue():
        for cp in mine() + copies():
            cp.start()

    def complete():
        for cp in copies():
            cp.wait_recv()
        for cp in copies():
            cp.wait_send()
        for cp in mine():
            cp.wait()

    return issue, complete


def _chip_exchange_scratch(n):
    return [pltpu.SemaphoreType.DMA((3 * n,)), pltpu.SemaphoreType.DMA((3 * n,)), pltpu.SemaphoreType.DMA((n,))]


def _same_shapes(arrays):
    return tuple(jax.ShapeDtypeStruct(a.shape, a.dtype) for a in arrays)


def _rs_chip_exchange(parts):
    n = len(parts)

    def body(*refs):
        issue, complete = _chip_exchange_steps(refs[:n], refs[n:2 * n], *refs[2 * n:])
        issue()
        complete()

    return pl.pallas_call(
        body, name="rs_chip_exchange", out_shape=_same_shapes(parts),
        in_specs=[_ANY] * n, out_specs=tuple([_ANY] * n), scratch_shapes=_chip_exchange_scratch(n),
    )(*parts)


def _adamw(w, g, m, v):
    m = ADAM_B1 * m + (1.0 - ADAM_B1) * g
    v = ADAM_B2 * v + (1.0 - ADAM_B2) * (g * g)
    m_hat = m / (1.0 - ADAM_B1 ** ADAM_STEP)
    v_hat = v / (1.0 - ADAM_B2 ** ADAM_STEP)
    delta = -ADAM_LR * (m_hat / (jnp.sqrt(v_hat) + ADAM_EPS) + ADAM_WD * w)
    return delta, m, v


def _rs_sum_adamw(recv, w, m, v, name):
    _, rows, cols = recv.shape
    tr = ROW_TILE

    def body(r_ref, w_ref, m_ref, v_ref, g_out, d_out, m_out, v_out):
        g = r_ref[0].astype(F32)
        for k in range(1, 4):
            g = g + r_ref[k].astype(F32)
        delta, m_new, v_new = _adamw(w_ref[...], g, m_ref[...], v_ref[...])
        g_out[...] = g
        d_out[...] = delta
        m_out[...] = m_new
        v_out[...] = v_new

    flat = pl.BlockSpec((tr, cols), lambda r: (r, 0))
    shp = jax.ShapeDtypeStruct((rows, cols), F32)
    return pl.pallas_call(
        body, name=name, grid=(rows // tr,),
        out_shape=(shp, shp, shp, shp),
        in_specs=[pl.BlockSpec((4, tr, cols), lambda r: (0, r, 0)), flat, flat, flat],
        out_specs=(flat, flat, flat, flat),
    )(recv, w, m, v)


def _small_allreduce_adamw(vec, w, m, v):
    rows, cols = vec.shape

    def body(x_ref, w_ref, m_ref, v_ref, g_out, d_out, m_out, v_out, gather, send_sems, recv_sems):
        x, y, c = lax.axis_index("x"), lax.axis_index("y"), lax.axis_index("c")
        me = 4 * x + 2 * y + c
        copies = []
        for d in range(1, N_DEV):
            dx, dy, dc = (d >> 2) & 1, (d >> 1) & 1, d & 1
            peer = (x ^ dx if dx else x, y ^ dy if dy else y, c ^ dc if dc else c)
            copies.append(pltpu.make_async_remote_copy(
                src_ref=x_ref, dst_ref=gather.at[me],
                send_sem=send_sems.at[d - 1], recv_sem=recv_sems.at[d - 1],
                device_id=peer, device_id_type=MESH))
        for cp in copies:
            cp.start()
        gather[me] = x_ref[...]
        for cp in copies:
            cp.wait_recv()
        for cp in copies:
            cp.wait_send()
        g = gather[0]
        for k in range(1, N_DEV):
            g = g + gather[k]
        delta, m_new, v_new = _adamw(w_ref[...], g, m_ref[...], v_ref[...])
        g_out[...] = g
        d_out[...] = delta
        m_out[...] = m_new
        v_out[...] = v_new

    vm = pl.BlockSpec(memory_space=pltpu.VMEM)
    shp = jax.ShapeDtypeStruct((rows, cols), F32)
    return pl.pallas_call(
        body, name="small_allreduce_adamw",
        out_shape=(shp, shp, shp, shp),
        in_specs=[vm, vm, vm, vm], out_specs=(vm, vm, vm, vm),
        scratch_shapes=[pltpu.VMEM((N_DEV, rows, cols), F32),
                        pltpu.SemaphoreType.DMA((N_DEV - 1,)), pltpu.SemaphoreType.DMA((N_DEV - 1,))],
    )(vec, w, m, v)


def _mm(a, b, *, ta=False, tb=False, out_dtype=BF16, res=None, name, group=None, group_width=None):
    if ta:
        kdim, m = a.shape
    else:
        m, kdim = a.shape
    if group == "n":
        assert not ta and not tb and res is None
        ng, kb, tn = b.shape
        n = ng * tn
    elif group == "k":
        assert tb and not ta and res is None
        ng, n, chunk = b.shape
        kb = ng * chunk
        per_step = 2 if ng % 2 == 0 else 1
        tk = per_step * chunk
    elif tb:
        n, kb = b.shape
    else:
        kb, n = b.shape
    assert kdim == kb, (a.shape, b.shape, ta, tb, group)
    tm = _pick(m, (1024, 512, 256, 128))
    if group == "out":
        assert ta and not tb and res is None
        tn = group_width
    elif group != "n":
        tn = _pick(n, (1024, 1408, 768, 512, 256, 128))
    if group != "k":
        tk = _pick(kdim, ((2048,) if ta else (1536,)) + (1024, 1408, 512, 256, 128))
    nk = kdim // tk
    dn = (((0 if ta else 1,), (1 if tb else 0,)), ((), ()))
    has_res = res is not None
    in_place = nk > 1 and out_dtype == F32
    use_acc = nk > 1 and not in_place

    def body(*refs):
        a_ref, b_ref = refs[:2]
        r_ref = refs[2] if has_res else None
        o_ref = refs[3] if has_res else refs[2]
        if group == "k":
            part = sum(lax.dot_general(a_ref[:, c * chunk:(c + 1) * chunk], b_ref[c], dn, preferred_element_type=F32)
                       for c in range(per_step))
        else:
            part = lax.dot_general(a_ref[...], b_ref[...], dn, preferred_element_type=F32)
        if nk == 1:
            if has_res:
                part = part + r_ref[...].astype(F32)
            o_ref[...] = part.astype(out_dtype)
            return
        k = pl.program_id(2)
        acc = o_ref if in_place else refs[-1]

        @pl.when(k == 0)
        def _():
            acc[...] = part + r_ref[...].astype(F32) if has_res else part

        @pl.when(k > 0)
        def _():
            acc[...] += part

        if use_acc:
            @pl.when(k == nk - 1)
            def _():
                o_ref[...] = acc[...].astype(out_dtype)

    a_spec = (pl.BlockSpec((tk, tm), lambda i, j, k: (k, i)) if ta
              else pl.BlockSpec((tm, tk), lambda i, j, k: (i, k)))
    if group == "n":
        b_spec = pl.BlockSpec((None, tk, tn), lambda i, j, k: (j, k, 0))
    elif group == "k":
        b_spec = pl.BlockSpec((per_step, tn, chunk), lambda i, j, k: (k, j, 0))
    elif tb:
        b_spec = pl.BlockSpec((tn, tk), lambda i, j, k: (j, k))
    else:
        b_spec = pl.BlockSpec((tk, tn), lambda i, j, k: (k, j))
    if group == "out":
        o_spec = pl.BlockSpec((None, tm, tn), lambda i, j, k: (j, i, 0))
        out_shape = jax.ShapeDtypeStruct((n // tn, m, tn), out_dtype)
    else:
        o_spec = pl.BlockSpec((tm, tn), lambda i, j, k: (i, j))
        out_shape = jax.ShapeDtypeStruct((m, n), out_dtype)
    in_specs = [a_spec, b_spec] + ([o_spec] if has_res else [])
    args = (a, b) + ((res,) if has_res else ())
    return pl.pallas_call(
        body, name=name, grid=(m // tm, n // tn, nk),
        out_shape=out_shape,
        in_specs=in_specs, out_specs=o_spec,
        scratch_shapes=[pltpu.VMEM((tm, tn), F32)] if use_acc else [],
        compiler_params=pltpu.CompilerParams(dimension_semantics=("parallel", "parallel", "arbitrary")),
    )(*args)


def _row_tile(s):
    return _pick(s, (512, 256, 128))


def _rms_fwd(h, g, name):
    s, d = h.shape
    ts = _row_tile(s)

    def body(h_ref, g_ref, o_ref):
        x = h_ref[...]
        r = lax.rsqrt(jnp.mean(x * x, axis=-1, keepdims=True) + EPS)
        o_ref[...] = ((x * r) * g_ref[...]).astype(BF16)

    return pl.pallas_call(
        body, name=name, grid=(s // ts,),
        out_shape=jax.ShapeDtypeStruct((s, d), BF16),
        in_specs=[pl.BlockSpec((ts, d), lambda i: (i, 0)), pl.BlockSpec((1, d), lambda i: (0, 0))],
        out_specs=pl.BlockSpec((ts, d), lambda i: (i, 0)),
    )(h, g)


def _rms_bwd(h, g, dhns, dres, name, want_bf16):
    s, d = h.shape
    ts = _row_tile(s)
    n_in = len(dhns)

    def body(*refs):
        h_ref, g_ref, r_ref = refs[:3]
        dy_refs = refs[3:3 + n_in]
        outs = refs[3 + n_in:]
        dh_ref, dg_ref = outs[0], outs[-1]
        i = pl.program_id(0)
        x = h_ref[...]
        dy = dy_refs[0][...].astype(F32)
        for extra in dy_refs[1:]:
            dy = dy + extra[...].astype(F32)
        r = lax.rsqrt(jnp.mean(x * x, axis=-1, keepdims=True) + EPS)
        xr = x * r
        u = dy * g_ref[...]
        dx = r * (u - xr * jnp.mean(xr * u, axis=-1, keepdims=True))
        dh = r_ref[...] + dx
        dh_ref[...] = dh
        if want_bf16:
            outs[1][...] = dh.astype(BF16)

        @pl.when(i == 0)
        def _():
            dg_ref[...] = jnp.zeros_like(dg_ref)

        dg_ref[...] += jnp.sum(dy * xr, axis=0, keepdims=True)

    row = pl.BlockSpec((ts, d), lambda i: (i, 0))
    vec = pl.BlockSpec((1, d), lambda i: (0, 0))
    out_shape = [jax.ShapeDtypeStruct((s, d), F32)]
    out_specs = [row]
    if want_bf16:
        out_shape.append(jax.ShapeDtypeStruct((s, d), BF16))
        out_specs.append(row)
    out_shape.append(jax.ShapeDtypeStruct((1, d), F32))
    out_specs.append(vec)
    return pl.pallas_call(
        body, name=name, grid=(s // ts,),
        out_shape=tuple(out_shape),
        in_specs=[row, vec, row] + [row] * n_in, out_specs=tuple(out_specs),
        compiler_params=pltpu.CompilerParams(dimension_semantics=("arbitrary",)),
    )(h, g, dres, *dhns)


def _sigmoid_parts(z):
    e = jnp.exp(-jnp.abs(z))
    r = 1.0 / (1.0 + e)
    er = e * r
    pos = z >= 0
    return jnp.where(pos, r, er), jnp.where(pos, er, r)


def _gate_up_swiglu(hn, w_gu, name):
    s, d = hn.shape
    f = HID_PAD
    tm = _pick(s, (1024, 512, 256, 128))
    chunk_rows = min(tm, 256)

    def body(x_ref, w_ref, ab_ref, u_ref):
        for r in range(tm // chunk_rows):
            rows = slice(r * chunk_rows, (r + 1) * chunk_rows)
            ab = jnp.dot(x_ref[rows, :], w_ref[...], preferred_element_type=F32).astype(BF16)
            ab_ref[rows, :] = ab
            a = ab[:, :f].astype(F32)
            b = ab[:, f:].astype(F32)
            sg, _ = _sigmoid_parts(a)
            u_ref[rows, :] = ((a * sg) * b).astype(BF16)

    return pl.pallas_call(
        body, name=name, grid=(s // tm, N_DEV),
        out_shape=(jax.ShapeDtypeStruct((s, 2 * D_FF_PAD), BF16), jax.ShapeDtypeStruct((s, D_FF_PAD), BF16)),
        in_specs=[pl.BlockSpec((tm, d), lambda i, j: (i, 0)), pl.BlockSpec((None, d, 2 * f), lambda i, j: (j, 0, 0))],
        out_specs=(pl.BlockSpec((tm, 2 * f), lambda i, j: (i, j)), pl.BlockSpec((tm, f), lambda i, j: (i, j))),
    )(hn, w_gu)


def _down_t_swiglu_bwd(dh, w_down, ab, name):
    s, d = dh.shape
    f = HID_PAD
    tm = _pick(s, (1024, 512, 256, 128))
    chunk_rows = min(tm, 256)

    def body(x_ref, w_ref, ab_ref, o_ref):
        for r in range(tm // chunk_rows):
            rows = slice(r * chunk_rows, (r + 1) * chunk_rows)
            du = lax.dot_general(x_ref[rows, :], w_ref[...], _DN_NT, preferred_element_type=F32)
            g = du.astype(BF16).astype(F32)
            a = ab_ref[rows, :f].astype(F32)
            b = ab_ref[rows, f:].astype(F32)
            sg, sgm = _sigmoid_parts(a)
            silu = a * sg
            o_ref[rows, :f] = (g * b * (sg + silu * sgm)).astype(BF16)
            o_ref[rows, f:] = (g * silu).astype(BF16)

    return pl.pallas_call(
        body, name=name, grid=(s // tm, N_DEV),
        out_shape=jax.ShapeDtypeStruct(ab.shape, BF16),
        in_specs=[pl.BlockSpec((tm, d), lambda i, j: (i, 0)), pl.BlockSpec((f, d), lambda i, j: (j, 0)),
                  pl.BlockSpec((tm, 2 * f), lambda i, j: (i, j))],
        out_specs=pl.BlockSpec((tm, 2 * f), lambda i, j: (i, j)),
    )(dh, w_down, ab)


def _ple_fwd(h, gl, pp, name, next_gain=None):
    s, d = h.shape
    ts = _row_tile(s)
    fused = next_gain is not None

    def body(*refs):
        h_ref, gl_ref, pp_ref = refs[:3]
        sg, _ = _sigmoid_parts(gl_ref[...].astype(F32))
        x = h_ref[...] + sg * pp_ref[...].astype(F32)
        if fused:
            refs[4][...] = x
            r = lax.rsqrt(jnp.mean(x * x, axis=-1, keepdims=True) + EPS)
            refs[5][...] = ((x * r) * refs[3][...]).astype(BF16)
        else:
            refs[3][...] = x

    row = pl.BlockSpec((ts, d), lambda i: (i, 0))
    vec = pl.BlockSpec((1, d), lambda i: (0, 0))
    if not fused:
        return pl.pallas_call(
            body, name=name, grid=(s // ts,),
            out_shape=jax.ShapeDtypeStruct((s, d), F32),
            in_specs=[row, row, row], out_specs=row,
        )(h, gl, pp)
    return pl.pallas_call(
        body, name=name, grid=(s // ts,),
        out_shape=(jax.ShapeDtypeStruct((s, d), F32), jax.ShapeDtypeStruct((s, d), BF16)),
        in_specs=[row, row, row, vec], out_specs=(row, row),
    )(h, gl, pp, next_gain)


def _ple_bwd(dh, gl, pp, name):
    s, d = dh.shape
    ts = _row_tile(s)

    def body(dh_ref, gl_ref, pp_ref, dgl_ref, dpp_ref):
        g = dh_ref[...]
        sg, sgm = _sigmoid_parts(gl_ref[...].astype(F32))
        dpp_ref[...] = (g * sg).astype(BF16)
        dgl_ref[...] = (g * pp_ref[...].astype(F32) * (sg * sgm)).astype(BF16)

    row = pl.BlockSpec((ts, d), lambda i: (i, 0))
    shp = jax.ShapeDtypeStruct((s, d), BF16)
    return pl.pallas_call(
        body, name=name, grid=(s // ts,),
        out_shape=(shp, shp), in_specs=[row, row, row], out_specs=(row, row),
    )(dh, gl, pp)


def _final_norm_loss(h, g, target):
    s, d = h.shape
    ts = _row_tile(s)

    def body(h_ref, g_ref, t_ref, loss_ref, dh_ref, dg_ref):
        i = pl.program_id(0)
        x = h_ref[...]
        gain = g_ref[...]
        r = lax.rsqrt(jnp.mean(x * x, axis=-1, keepdims=True) + EPS)
        xr = x * r
        err = xr * gain - t_ref[...]
        dy = err * (1.0 / d)
        u = dy * gain
        dh_ref[...] = r * (u - xr * jnp.mean(xr * u, axis=-1, keepdims=True))

        @pl.when(i == 0)
        def _():
            dg_ref[...] = jnp.zeros_like(dg_ref)
            loss_ref[...] = jnp.zeros_like(loss_ref)

        dg_ref[...] += jnp.sum(dy * xr, axis=0, keepdims=True)
        tok = jnp.mean(err * err, axis=-1, keepdims=True)
        loss_ref[...] += 0.5 * jnp.sum(tok, axis=0, keepdims=True)

    row = pl.BlockSpec((ts, d), lambda i: (i, 0))
    vec = pl.BlockSpec((1, d), lambda i: (0, 0))
    return pl.pallas_call(
        body, name="final_norm_loss", grid=(s // ts,),
        out_shape=(jax.ShapeDtypeStruct((1, LANES), F32), jax.ShapeDtypeStruct((s, d), F32),
                   jax.ShapeDtypeStruct((1, d), F32)),
        in_specs=[row, vec, row],
        out_specs=(pl.BlockSpec((1, LANES), lambda i: (0, 0)), row, vec),
        compiler_params=pltpu.CompilerParams(dimension_semantics=("arbitrary",)),
    )(h, g, target)


def _rope(xin, w, cos, sin_signed, sign, name):
    s = xin.shape[0]
    ts = _pick(s, (512, 256, 128))

    def body(x_ref, c_ref, s_ref, o_ref):
        cos_b, sin_b = c_ref[...], s_ref[...]
        lane = lax.broadcasted_iota(jnp.int32, cos_b.shape, 1)
        low = (lane & (HEAD_DIM - 1)) < (HEAD_DIM // 2)
        for j in range(w // LANES):
            cols = slice(j * LANES, (j + 1) * LANES)
            x = x_ref[:, cols].astype(F32)
            swapped = jnp.where(low, pltpu.roll(x, LANES - HEAD_DIM // 2, 1), pltpu.roll(x, HEAD_DIM // 2, 1))
            o_ref[:, cols] = (x * cos_b + sign * (swapped * sin_b)).astype(BF16)

    blk = pl.BlockSpec((ts, w), lambda i: (i, 0))
    tab = pl.BlockSpec((ts, LANES), lambda i: (i, 0))
    return pl.pallas_call(
        body, name=name, grid=(s // ts,),
        out_shape=jax.ShapeDtypeStruct((s, w), BF16),
        in_specs=[blk, tab, tab], out_specs=blk,
    )(xin, cos, sin_signed)


def _fgate_fwd(flog, bias):
    s, w = flog.shape

    def body(x_ref, b_ref, o_ref):
        rowi = lax.broadcasted_iota(jnp.int32, (8, w), 0)
        b = b_ref[...]

        def step(g, carry):
            sl = pl.ds(pl.multiple_of(g * 8, 8), 8)
            x = x_ref[sl, :] + b
            lf = jnp.minimum(x, 0.0) - jnp.log1p(jnp.exp(-jnp.abs(x)))
            for sh in (1, 2, 4):
                lf = lf + jnp.where(rowi >= sh, pltpu.roll(lf, sh, 0), 0.0)
            out = lf + carry
            o_ref[sl, :] = out
            return jnp.broadcast_to(out[7:8, :], (8, w))

        lax.fori_loop(0, s // 8, step, jnp.zeros((8, w), F32))

    vm = pl.BlockSpec(memory_space=pltpu.VMEM)
    return pl.pallas_call(
        body, name="fgate_fwd", out_shape=jax.ShapeDtypeStruct((s, w), F32),
        in_specs=[vm, vm], out_specs=vm,
    )(flog, bias)


def _fgate_bwd(gcum, flog, bias):
    s, w = flog.shape

    def body(g_ref, x_ref, b_ref, o_ref, db_ref):
        rowi = lax.broadcasted_iota(jnp.int32, (8, w), 0)
        lane = lax.broadcasted_iota(jnp.int32, (8, w), 1)
        b = b_ref[...]

        def step(t, carry):
            run, dbsum = carry
            g = s // 8 - 1 - t
            sl = pl.ds(pl.multiple_of(g * 8, 8), 8)
            c = g_ref[sl, :]
            for sh in (1, 2, 4):
                c = c + jnp.where(rowi < 8 - sh, pltpu.roll(c, 8 - sh, 0), 0.0)
            c = c + run
            _, sgm = _sigmoid_parts(x_ref[sl, :] + b)
            dl = jnp.where(lane < N_FOX, c * sgm, 0.0)
            o_ref[sl, :] = dl.astype(BF16)
            return jnp.broadcast_to(c[0:1, :], (8, w)), dbsum + dl

        _, dbsum = lax.fori_loop(0, s // 8, step, (jnp.zeros((8, w), F32), jnp.zeros((8, w), F32)))
        db_ref[...] = jnp.sum(dbsum, axis=0, keepdims=True)

    vm = pl.BlockSpec(memory_space=pltpu.VMEM)
    return pl.pallas_call(
        body, name="fgate_bwd",
        out_shape=(jax.ShapeDtypeStruct((s, w), BF16), jax.ShapeDtypeStruct((1, w), F32)),
        in_specs=[vm, vm, vm], out_specs=(vm, vm),
    )(gcum, flog, bias)


_DN_NT = (((1,), (1,)), ((), ()))
_DN_TN = (((0,), (0,)), ((), ()))


def _head_mask(shape, hh):
    lane = lax.broadcasted_iota(jnp.int32, shape, 1)
    return (lane >= HEAD_DIM * hh) & (lane < HEAD_DIM * (hh + 1))


SKIP_BELOW = -110.0


def _sweep_left(i, carry, tile, go_on):
    def flag(j, c):
        return jnp.logical_and(j >= 0, go_on(jnp.maximum(j, 0), c)).astype(jnp.int32)

    def body(st):
        j, _, c = st
        c = tile(j, c)
        return j - 1, flag(j - 1, c), c

    return lax.while_loop(lambda st: st[1] > 0, body, (i - 1, flag(i - 1, carry), carry))[2]


def _sweep_left_pair(i, carries, tiles, go_ons):
    return _sweep_left(
        i, tuple(carries),
        lambda j, c: tuple(t(j, ch) for t, ch in zip(tiles, c)),
        lambda j, c: jnp.logical_or(go_ons[0](j, c[0]), go_ons[1](j, c[1])))


def _key_norm_max(k_ref, kn_ref):
    k2 = k_ref[...].astype(F32)
    sq = k2 * k2
    for hh in range(2):
        n2 = jnp.sum(jnp.where(_head_mask(sq.shape, hh), sq, 0.0), axis=1, keepdims=True)
        kn_ref[hh] = jnp.broadcast_to(jnp.sqrt(jnp.max(n2, axis=0, keepdims=True)), kn_ref.shape[1:])


def _fox_fwd(proj, ccol, crow, gather):
    s = proj.shape[0]
    blk = min(ATTN_BLOCK, s)
    nq = s // blk
    npair = N_FOX // 2
    ng = len(gather)

    def body(*refs):
        q_ref, k_ref, v_ref, cc_ref, cr_ref = refs[:5]
        x_refs = refs[5:5 + ng]
        o_ref, a_ref = refs[5 + ng:7 + ng]
        g_refs = refs[7 + ng:7 + 2 * ng]
        kn_ref = refs[7 + 2 * ng]
        p_, i = pl.program_id(0), pl.program_id(1)
        issue, complete = _all_gather_steps(x_refs, g_refs, *refs[8 + 2 * ng:])

        @pl.when((p_ == 0) & (i == 0))
        def _():
            issue()

        @pl.when(i == 0)
        def _():
            _key_norm_max(k_ref, kn_ref)

        q2 = q_ref[...].astype(F32) * SCALE
        row = lax.broadcasted_iota(jnp.int32, (blk, blk), 0)
        col = lax.broadcasted_iota(jnp.int32, (blk, blk), 1)
        outs, heads = [], []
        for hh in range(2):
            hm = _head_mask(q2.shape, hh)
            qh = jnp.where(hm, q2, 0.0).astype(BF16)
            ct = cc_ref[hh][:, 0:1]
            qk_max = jnp.sqrt(jnp.sum(jnp.where(hm, q2 * q2, 0.0), axis=1, keepdims=True)) * kn_ref[hh][0:1, 0:1]

            def go_on(j, carry, ct=ct, qk_max=qk_max, hh=hh):
                bias_max = ct - jnp.min(cr_ref[2 * p_ + hh, j], axis=1, keepdims=True)
                return jnp.max(qk_max + bias_max - carry[0]) > SKIP_BELOW

            def tile(j, carry, masked, qh=qh, ct=ct, hh=hh):
                m, l, acc = carry
                sl = pl.ds(pl.multiple_of(j * blk, blk), blk)
                kb, vb = k_ref[sl, :], v_ref[sl, :]
                sc = lax.dot_general(qh, kb, _DN_NT, preferred_element_type=F32)
                sc = sc + (ct - cr_ref[2 * p_ + hh, j])
                if masked:
                    sc = jnp.where(col <= row, sc, NEG_INF)
                m_new = jnp.maximum(m, jnp.max(sc, axis=1, keepdims=True))
                alpha = jnp.exp(m - m_new)
                pm = jnp.exp(sc - m_new)
                l = alpha * l + jnp.sum(pm, axis=1, keepdims=True)
                acc = alpha * acc + jnp.dot(pm.astype(BF16), vb, preferred_element_type=F32)
                return m_new, l, acc

            init = (jnp.full((blk, 1), NEG_INF, F32), jnp.zeros((blk, 1), F32), jnp.zeros((blk, LANES), F32))
            heads.append((lambda j, c, tile=tile: tile(j, c, False), go_on, tile(i, init, True), ct))
        swept = _sweep_left_pair(i, [h[2] for h in heads], [h[0] for h in heads], [h[1] for h in heads])
        for hh, (m, l, acc) in enumerate(swept):
            outs.append(acc / l)
            a_ref[hh] = jnp.broadcast_to(heads[hh][3] - (m + jnp.log(l)), (blk, LANES))
        o_ref[...] = jnp.where(_head_mask(outs[0].shape, 0), outs[0], outs[1]).astype(BF16)

        @pl.when((p_ == npair - 1) & (i == nq - 1))
        def _():
            complete()

    seq = lambda base: pl.BlockSpec((s, LANES), lambda p, i: (0, base + p))
    res = pl.pallas_call(
        body, name="fox_fwd_ag", grid=(npair, nq),
        scratch_shapes=[pltpu.VMEM((2, 8, LANES), F32)] + _all_gather_scratch(ng),
        out_shape=(jax.ShapeDtypeStruct((s, FOX_W), BF16), jax.ShapeDtypeStruct((N_FOX, s, LANES), F32))
        + _gathered_shapes(gather),
        in_specs=[pl.BlockSpec((blk, LANES), lambda p, i: (i, p)), seq(npair), seq(2 * npair),
                  pl.BlockSpec((2, blk, LANES), lambda p, i: (p, i, 0)),
                  pl.BlockSpec((N_FOX, nq, 1, blk), lambda p, i: (0, 0, 0, 0))] + [_ANY] * ng,
        out_specs=(pl.BlockSpec((blk, LANES), lambda p, i: (i, p)),
                   pl.BlockSpec((2, blk, LANES), lambda p, i: (p, i, 0))) + tuple([_ANY] * ng),
        compiler_params=pltpu.CompilerParams(dimension_semantics=("arbitrary", "arbitrary")),
    )(proj, proj, proj, ccol, crow, *gather)
    return res[0], res[1], list(res[2:])


def _fox_bwd(proj, do, o, acol, crow, grads):
    s = proj.shape[0]
    blk = min(ATTN_BLOCK, s)
    nq = s // blk
    npair = N_FOX // 2
    ng = len(grads)

    def body(*refs):
        q_ref, k_ref, v_ref, do_ref, o_ref, a_ref, cr_ref = refs[:7]
        g_refs = refs[7:7 + ng]
        dq_ref, dk_ref, dv_ref, gc_ref = refs[7 + ng:11 + ng]
        x_refs = refs[11 + ng:11 + 2 * ng]
        dk_acc, dv_acc, kn_ref = refs[11 + 2 * ng:14 + 2 * ng]
        p_, i = pl.program_id(0), pl.program_id(1)
        issue, complete = _sibling_exchange_steps(g_refs, x_refs, *refs[14 + 2 * ng:])

        @pl.when((p_ == 0) & (i == 0))
        def _():
            issue()

        @pl.when(i == 0)
        def _():
            dk_acc[...] = jnp.zeros_like(dk_acc)
            dv_acc[...] = jnp.zeros_like(dv_acc)
            gc_ref[...] = jnp.zeros_like(gc_ref)
            _key_norm_max(k_ref, kn_ref)

        q2 = q_ref[...].astype(F32) * SCALE
        do2 = do_ref[...]
        prod = do2.astype(F32) * o_ref[...].astype(F32)
        row = lax.broadcasted_iota(jnp.int32, (blk, blk), 0)
        col = lax.broadcasted_iota(jnp.int32, (blk, blk), 1)
        dqs, heads = [], []
        for hh in range(2):
            hm = _head_mask(q2.shape, hh)
            qh = jnp.where(hm, q2, 0.0).astype(BF16)
            doh = jnp.where(hm, do2, jnp.zeros_like(do2))
            delta = jnp.sum(jnp.where(hm, prod, 0.0), axis=1, keepdims=True)
            at = a_ref[hh][:, 0:1]
            qk_max = jnp.sqrt(jnp.sum(jnp.where(hm, q2 * q2, 0.0), axis=1, keepdims=True)) * kn_ref[hh][0:1, 0:1]

            def go_on(j, carry, at=at, qk_max=qk_max, hh=hh):
                bias_max = at - jnp.min(cr_ref[2 * p_ + hh, j], axis=1, keepdims=True)
                return jnp.max(qk_max + bias_max) > SKIP_BELOW

            def tile(j, carry, masked, qh=qh, doh=doh, delta=delta, at=at, hh=hh):
                dq, rs = carry
                sl = pl.ds(pl.multiple_of(j * blk, blk), blk)
                kb, vb = k_ref[sl, :], v_ref[sl, :]
                sc = lax.dot_general(qh, kb, _DN_NT, preferred_element_type=F32)
                sc = sc + (at - cr_ref[2 * p_ + hh, j])
                if masked:
                    sc = jnp.where(col <= row, sc, NEG_INF)
                pm = jnp.exp(sc)
                dp = lax.dot_general(doh, vb, _DN_NT, preferred_element_type=F32)
                ds = pm * (dp - delta)
                dsb = ds.astype(BF16)
                dk_acc[sl, :] += lax.dot_general(dsb, qh, _DN_TN, preferred_element_type=F32)
                dv_acc[sl, :] += lax.dot_general(pm.astype(BF16), doh, _DN_TN, preferred_element_type=F32)
                gc_ref[hh, j] += -jnp.sum(ds, axis=0, keepdims=True)
                return dq + jnp.dot(dsb, kb, preferred_element_type=F32), rs + jnp.sum(ds, axis=1, keepdims=True)

            carry = tile(i, (jnp.zeros((blk, LANES), F32), jnp.zeros((blk, 1), F32)), True)
            heads.append((lambda j, c, tile=tile: tile(j, c, False), go_on, carry))
        swept = _sweep_left_pair(i, [h[2] for h in heads], [h[0] for h in heads], [h[1] for h in heads])
        for hh, (dq, rs) in enumerate(swept):
            dqs.append(dq)
            gc_ref[hh, i] += jnp.transpose(jnp.broadcast_to(rs, (blk, LANES)))[0:1, :]
        dq_ref[...] = (jnp.where(_head_mask(dqs[0].shape, 0), dqs[0], dqs[1]) * SCALE).astype(BF16)

        @pl.when(i == nq - 1)
        def _():
            dk_ref[...] = dk_acc[...].astype(BF16)
            dv_ref[...] = dv_acc[...].astype(BF16)

        @pl.when((p_ == npair - 1) & (i == nq - 1))
        def _():
            complete()

    seq = lambda base: pl.BlockSpec((s, LANES), lambda p, i: (0, base + p))
    qblk = lambda base: pl.BlockSpec((blk, LANES), lambda p, i: (i, base + p))
    rep = pl.BlockSpec((2, blk, LANES), lambda p, i: (p, i, 0))
    half = jax.ShapeDtypeStruct((s, FOX_W), BF16)
    res = pl.pallas_call(
        body, name="fox_bwd_rs", grid=(npair, nq),
        out_shape=(half, half, half, jax.ShapeDtypeStruct((N_FOX, nq, 1, blk), F32)) + _quarter_shapes(grads),
        in_specs=[qblk(0), seq(npair), seq(2 * npair), qblk(0), qblk(0), rep,
                  pl.BlockSpec((N_FOX, nq, 1, blk), lambda p, i: (0, 0, 0, 0))] + [_ANY] * ng,
        out_specs=(qblk(0), seq(0), seq(0), pl.BlockSpec((2, nq, 1, blk), lambda p, i: (p, 0, 0, 0)))
        + tuple([_ANY] * ng),
        scratch_shapes=[pltpu.VMEM((s, LANES), F32), pltpu.VMEM((s, LANES), F32), pltpu.VMEM((2, 8, LANES), F32)]
        + _sibling_exchange_scratch(ng),
        compiler_params=pltpu.CompilerParams(dimension_semantics=("arbitrary", "arbitrary")),
    )(proj, proj, proj, do, o, acol, crow, *grads)
    return res[0], res[1], res[2], res[3], list(res[4:])


def _sb_logs(z):
    neg = -(jnp.maximum(z, 0.0) + jnp.log1p(jnp.exp(-jnp.abs(z))))
    return neg, z + neg


def _split_dot(x, tri):
    hi = x.astype(BF16)
    lo = (x - hi.astype(F32)).astype(BF16)
    return jnp.dot(hi, tri, preferred_element_type=F32) + jnp.dot(lo, tri, preferred_element_type=F32)


def _sb_fwd(proj):
    s = proj.shape[0]
    blk = min(ATTN_BLOCK, s)
    nq = s // blk
    npair = N_SB // 2
    base = 3 * (N_FOX // 2)

    def body(q_ref, k_ref, v_ref, o_ref, r_ref):
        i = pl.program_id(1)
        q2 = q_ref[...].astype(F32) * SCALE
        row = lax.broadcasted_iota(jnp.int32, (blk, blk), 0)
        col = lax.broadcasted_iota(jnp.int32, (blk, blk), 1)
        strict = col < row
        tri = jnp.where(row > col, 1.0, 0.0).astype(BF16)
        lane = lax.broadcasted_iota(jnp.int32, (blk, LANES), 1)
        outs, heads = [], []
        for hh in range(2):
            qh = jnp.where(_head_mask(q2.shape, hh), q2, 0.0).astype(BF16)

            def tile(j, carry, masked, qh=qh):
                rsum, acc, rbuf = carry
                sl = pl.ds(pl.multiple_of(j * blk, blk), blk)
                kb, vb = k_ref[sl, :], v_ref[sl, :]
                z = lax.dot_general(qh, kb, _DN_NT, preferred_element_type=F32)
                l1m, lb = _sb_logs(z)
                if masked:
                    l1m = jnp.where(strict, l1m, 0.0)
                sx = _split_dot(l1m, tri)
                a = jnp.exp(lb + sx + rsum)
                if masked:
                    a = jnp.where(strict, a, 0.0)
                acc = acc + jnp.dot(a.astype(BF16), vb, preferred_element_type=F32)
                rbuf = jnp.where(lane == j, rsum, rbuf)
                return rsum + jnp.sum(l1m, axis=1, keepdims=True), acc, rbuf

            init = (jnp.zeros((blk, 1), F32), jnp.zeros((blk, LANES), F32), jnp.full((blk, LANES), NEG_INF, F32))
            heads.append((lambda j, c, tile=tile: tile(j, c, False), lambda j, c: jnp.max(c[0]) > SKIP_BELOW,
                          tile(i, init, True)))
        swept = _sweep_left_pair(i, [h[2] for h in heads], [h[0] for h in heads], [h[1] for h in heads])
        for hh, (_, acc, rbuf) in enumerate(swept):
            outs.append(acc)
            r_ref[hh] = rbuf
        o_ref[...] = jnp.where(_head_mask(outs[0].shape, 0), outs[0], outs[1]).astype(BF16)

    seq = lambda b: pl.BlockSpec((s, LANES), lambda p, i: (0, b + p))
    return pl.pallas_call(
        body, name="sb_fwd", grid=(npair, nq),
        out_shape=(jax.ShapeDtypeStruct((s, SB_W), BF16), jax.ShapeDtypeStruct((N_SB, s, LANES), F32)),
        in_specs=[pl.BlockSpec((blk, LANES), lambda p, i: (i, base + p)), seq(base + npair), seq(base + 2 * npair)],
        out_specs=(pl.BlockSpec((blk, LANES), lambda p, i: (i, p)),
                   pl.BlockSpec((2, blk, LANES), lambda p, i: (p, i, 0))),
        compiler_params=pltpu.CompilerParams(dimension_semantics=("parallel", "arbitrary")),
    )(proj, proj, proj)


def _sb_bwd(proj, do, rsave, parts):
    s = proj.shape[0]
    blk = min(ATTN_BLOCK, s)
    nq = s // blk
    npair = N_SB // 2
    base = 3 * (N_FOX // 2)
    ng = len(parts)

    def body(*refs):
        q_ref, k_ref, v_ref, do_ref, r_ref = refs[:5]
        p_refs = refs[5:5 + ng]
        dq_ref, dk_ref, dv_ref = refs[5 + ng:8 + ng]
        x_refs = refs[8 + ng:8 + 2 * ng]
        dk_acc, dv_acc = refs[8 + 2 * ng:10 + 2 * ng]
        p_, i = pl.program_id(0), pl.program_id(1)
        issue, complete = _chip_exchange_steps(p_refs, x_refs, *refs[10 + 2 * ng:])

        @pl.when((p_ == 0) & (i == 0))
        def _():
            issue()

        @pl.when(i == 0)
        def _():
            dk_acc[...] = jnp.zeros_like(dk_acc)
            dv_acc[...] = jnp.zeros_like(dv_acc)

        q2 = q_ref[...].astype(F32) * SCALE
        do2 = do_ref[...]
        row = lax.broadcasted_iota(jnp.int32, (blk, blk), 0)
        col = lax.broadcasted_iota(jnp.int32, (blk, blk), 1)
        strict = col < row
        tri_suffix = jnp.where(row > col, 1.0, 0.0).astype(BF16)
        tri_prefix = jnp.where(row < col, 1.0, 0.0).astype(BF16)
        lane = lax.broadcasted_iota(jnp.int32, (blk, LANES), 1)
        heads = []
        for hh in range(2):
            hm = _head_mask(q2.shape, hh)
            qh = jnp.where(hm, q2, 0.0).astype(BF16)
            doh = jnp.where(hm, do2, jnp.zeros_like(do2))
            rbuf = r_ref[hh]

            def tile(j, carry, masked, qh=qh, doh=doh, rbuf=rbuf):
                pre, dq = carry
                sl = pl.ds(pl.multiple_of(j * blk, blk), blk)
                kb, vb = k_ref[sl, :], v_ref[sl, :]
                z = lax.dot_general(qh, kb, _DN_NT, preferred_element_type=F32)
                l1m, lb = _sb_logs(z)
                beta, one_m_beta = _sigmoid_parts(z)
                if masked:
                    l1m = jnp.where(strict, l1m, 0.0)
                sx = _split_dot(l1m, tri_suffix)
                rj = jnp.sum(jnp.where(lane == j, rbuf, 0.0), axis=1, keepdims=True)
                a = jnp.exp(lb + sx + rj)
                if masked:
                    a = jnp.where(strict, a, 0.0)
                da = lax.dot_general(doh, vb, _DN_NT, preferred_element_type=F32)
                g = a * da
                px = _split_dot(g, tri_prefix) + pre
                dz = g * one_m_beta - beta * px
                if masked:
                    dz = jnp.where(strict, dz, 0.0)
                dzb = dz.astype(BF16)
                dk_acc[sl, :] += lax.dot_general(dzb, qh, _DN_TN, preferred_element_type=F32)
                dv_acc[sl, :] += lax.dot_general(a.astype(BF16), doh, _DN_TN, preferred_element_type=F32)
                return pre + jnp.sum(g, axis=1, keepdims=True), dq + jnp.dot(dzb, kb, preferred_element_type=F32)

            reach = jnp.max(rbuf, axis=0, keepdims=True)
            dead = (reach <= SKIP_BELOW) & (lane[0:1, :] <= i)
            heads.append((tile, jnp.sum(jnp.where(dead, 1.0, 0.0)).astype(jnp.int32)))
        first = jnp.minimum(heads[0][1], heads[1][1])
        zero = (jnp.zeros((blk, 1), F32), jnp.zeros((blk, LANES), F32))
        carries = lax.fori_loop(first, i, lambda t, c: tuple(h[0](t, ch, False) for h, ch in zip(heads, c)),
                                (zero, zero))
        dqs = [h[0](i, ch, True)[1] for h, ch in zip(heads, carries)]
        dq_ref[...] = (jnp.where(_head_mask(dqs[0].shape, 0), dqs[0], dqs[1]) * SCALE).astype(BF16)

        @pl.when(i == nq - 1)
        def _():
            dk_ref[...] = dk_acc[...].astype(BF16)
            dv_ref[...] = dv_acc[...].astype(BF16)

        @pl.when((p_ == npair - 1) & (i == nq - 1))
        def _():
            complete()

    seq = lambda b: pl.BlockSpec((s, LANES), lambda p, i: (0, b + p))
    qblk = lambda b: pl.BlockSpec((blk, LANES), lambda p, i: (i, b + p))
    half = jax.ShapeDtypeStruct((s, SB_W), BF16)
    res = pl.pallas_call(
        body, name="sb_bwd_rs", grid=(npair, nq),
        out_shape=(half, half, half) + _same_shapes(parts),
        in_specs=[qblk(base), seq(base + npair), seq(base + 2 * npair), qblk(npair),
                  pl.BlockSpec((2, blk, LANES), lambda p, i: (p, i, 0))] + [_ANY] * ng,
        out_specs=(qblk(0), seq(0), seq(0)) + tuple([_ANY] * ng),
        scratch_shapes=[pltpu.VMEM((s, LANES), F32), pltpu.VMEM((s, LANES), F32)] + _chip_exchange_scratch(ng),
        compiler_params=pltpu.CompilerParams(dimension_semantics=("arbitrary", "arbitrary")),
    )(proj, proj, proj, do, rsave, *parts)
    return res[0], res[1], res[2], list(res[3:])


SWA_ROWS = 512
Q_W = N_Q * HEAD_DIM
KV_W = N_KV * HEAD_DIM
KV_PAIRS = KV_W // LANES
Q_PER_KVPAIR = Q_W // KV_PAIRS


def _lane_swap(x):
    xf = x.astype(F32)
    parts = [pltpu.roll(xf[:, j * LANES:(j + 1) * LANES], HEAD_DIM, 1) for j in range(x.shape[1] // LANES)]
    return (parts[0] if len(parts) == 1 else jnp.concatenate(parts, axis=1)).astype(x.dtype)


def _swa_specs(s):
    w = WINDOW
    ts = min(SWA_ROWS, s)
    nw = ts // w
    qcols = pl.BlockSpec((ts, Q_PER_KVPAIR), lambda kp, i: (i, kp))
    kbase, vbase = Q_W // LANES, (Q_W + KV_W) // LANES
    prev = lambda base: pl.BlockSpec((w, LANES), lambda kp, i: (jnp.maximum(i * nw - 1, 0), base + kp))
    cur = lambda base: pl.BlockSpec((ts, LANES), lambda kp, i: (i, base + kp))
    vec = pl.BlockSpec((1, Q_PER_KVPAIR), lambda kp, i: (0, kp))
    return ts, nw, qcols, prev(kbase), cur(kbase), prev(vbase), cur(vbase), vec


def _swa_fwd(qk, vsrc, sink_row):
    s = qk.shape[0]
    w = WINDOW
    ts, nw, qcols, kprev, kcur, vprev, vcur, vec = _swa_specs(s)

    def body(q_ref, kp_ref, kc_ref, vp_ref, vc_ref, s_ref, o_ref, lse_ref):
        i = pl.program_id(1)
        k2 = jnp.concatenate([kp_ref[...], kc_ref[...]], axis=0)
        v2 = jnp.concatenate([vp_ref[...], vc_ref[...]], axis=0)
        ksw, vsw = _lane_swap(k2), _lane_swap(v2)
        row = lax.broadcasted_iota(jnp.int32, (w, 2 * w), 0)
        col = lax.broadcasted_iota(jnp.int32, (w, 2 * w), 1)
        band = (col > row) & (col <= row + w)
        first_half = _head_mask((w, LANES), 0)
        for u in range(nw):
            valid = band if u > 0 else band & ((col >= w) | (i > 0))
            rows, keys = slice(u * w, (u + 1) * w), slice(u * w, (u + 2) * w)
            for pr in range(Q_PER_KVPAIR // LANES):
                gh = pr // 2
                cols = slice(pr * LANES, (pr + 1) * LANES)
                q2 = q_ref[rows, cols].astype(F32) * SCALE
                outs, lses = [], []
                for hh in range(2):
                    kk = (k2 if hh == gh else ksw)[keys, :]
                    vv = (v2 if hh == gh else vsw)[keys, :]
                    qm = jnp.where(_head_mask(q2.shape, hh), q2, 0.0).astype(BF16)
                    sc = jnp.where(valid, lax.dot_general(qm, kk, _DN_NT, preferred_element_type=F32), NEG_INF)
                    sink = s_ref[:, pr * LANES + HEAD_DIM * hh: pr * LANES + HEAD_DIM * hh + 1]
                    m = jnp.maximum(jnp.max(sc, axis=1, keepdims=True), sink)
                    e = jnp.exp(sc - m)
                    l = jnp.sum(e, axis=1, keepdims=True) + jnp.exp(sink - m)
                    outs.append(jnp.dot(e.astype(BF16), vv, preferred_element_type=F32) / l)
                    lses.append(m + jnp.log(l))
                o_ref[rows, cols] = jnp.where(first_half, outs[0], outs[1]).astype(BF16)
                lse_ref[rows, cols] = jnp.where(first_half, lses[0], lses[1])

    return pl.pallas_call(
        body, name="swa_fwd", grid=(KV_PAIRS, s // ts),
        out_shape=(jax.ShapeDtypeStruct((s, Q_W), BF16), jax.ShapeDtypeStruct((s, Q_W), F32)),
        in_specs=[qcols, kprev, kcur, vprev, vcur, vec], out_specs=(qcols, qcols),
    )(qk, qk, qk, vsrc, vsrc, sink_row)


def _swa_bwd_dq(qk, vsrc, sink_row, do, o, lse):
    s = qk.shape[0]
    w = WINDOW
    ts, nw, qcols, kprev, kcur, vprev, vcur, vec = _swa_specs(s)

    def body(q_ref, kp_ref, kc_ref, vp_ref, vc_ref, s_ref, do_ref, o_ref, lse_ref, dq_ref, dsink_ref):
        i = pl.program_id(1)

        @pl.when(i == 0)
        def _():
            dsink_ref[...] = jnp.zeros_like(dsink_ref)

        k2 = jnp.concatenate([kp_ref[...], kc_ref[...]], axis=0)
        v2 = jnp.concatenate([vp_ref[...], vc_ref[...]], axis=0)
        ksw, vsw = _lane_swap(k2), _lane_swap(v2)
        row = lax.broadcasted_iota(jnp.int32, (w, 2 * w), 0)
        col = lax.broadcasted_iota(jnp.int32, (w, 2 * w), 1)
        band = (col > row) & (col <= row + w)
        first_half = _head_mask((w, LANES), 0)
        lane_all = lax.broadcasted_iota(jnp.int32, (1, Q_PER_KVPAIR), 1)
        dsink = jnp.zeros((1, Q_PER_KVPAIR), F32)
        for u in range(nw):
            valid = band if u > 0 else band & ((col >= w) | (i > 0))
            rows, keys = slice(u * w, (u + 1) * w), slice(u * w, (u + 2) * w)
            for pr in range(Q_PER_KVPAIR // LANES):
                gh = pr // 2
                cols = slice(pr * LANES, (pr + 1) * LANES)
                q2 = q_ref[rows, cols].astype(F32) * SCALE
                do2 = do_ref[rows, cols]
                prod = do2.astype(F32) * o_ref[rows, cols].astype(F32)
                lse2 = lse_ref[rows, cols]
                dqs = []
                for hh in range(2):
                    hm = _head_mask(q2.shape, hh)
                    lo = pr * LANES + HEAD_DIM * hh
                    kk = (k2 if hh == gh else ksw)[keys, :]
                    vv = (v2 if hh == gh else vsw)[keys, :]
                    qm = jnp.where(hm, q2, 0.0).astype(BF16)
                    sc = jnp.where(valid, lax.dot_general(qm, kk, _DN_NT, preferred_element_type=F32), NEG_INF)
                    lse_c = lse2[:, HEAD_DIM * hh: HEAD_DIM * hh + 1]
                    pm = jnp.exp(sc - lse_c)
                    doh = jnp.where(hm, do2, jnp.zeros_like(do2))
                    delta = jnp.sum(jnp.where(hm, prod, 0.0), axis=1, keepdims=True)
                    ds = pm * (lax.dot_general(doh, vv, _DN_NT, preferred_element_type=F32) - delta)
                    dqs.append(jnp.dot(ds.astype(BF16), kk, preferred_element_type=F32))
                    part = jnp.sum(-jnp.exp(s_ref[:, lo:lo + 1] - lse_c) * delta, axis=0, keepdims=True)
                    dsink = dsink + jnp.where((lane_all >= lo) & (lane_all < lo + HEAD_DIM), part, 0.0)
                dq_ref[rows, cols] = (jnp.where(first_half, dqs[0], dqs[1]) * SCALE).astype(BF16)
        dsink_ref[...] += dsink

    return pl.pallas_call(
        body, name="swa_bwd_dq", grid=(KV_PAIRS, s // ts),
        out_shape=(jax.ShapeDtypeStruct((s, Q_W), BF16), jax.ShapeDtypeStruct((1, Q_W), F32)),
        in_specs=[qcols, kprev, kcur, vprev, vcur, vec, qcols, qcols, qcols], out_specs=(qcols, vec),
        compiler_params=pltpu.CompilerParams(dimension_semantics=("parallel", "arbitrary")),
    )(qk, qk, qk, vsrc, vsrc, sink_row, do, o, lse)


def _swa_bwd_dkv(qk, vsrc, do, o, lse):
    s = qk.shape[0]
    w = WINDOW
    ts = min(SWA_ROWS, s)
    nw = ts // w
    nstep = s // ts
    last_window = s // w - 1

    def body(k_ref, v_ref, qc_ref, qn_ref, doc_ref, don_ref, oc_ref, on_ref, lc_ref, ln_ref, dk_ref, dv_ref):
        j = pl.program_id(1)
        cat = lambda a_ref, b_ref: jnp.concatenate([a_ref[...], b_ref[...]], axis=0)
        qcat, docat, ocat, lcat = cat(qc_ref, qn_ref), cat(doc_ref, don_ref), cat(oc_ref, on_ref), cat(lc_ref, ln_ref)
        qsw, dosw = _lane_swap(qcat), _lane_swap(docat)
        row = lax.broadcasted_iota(jnp.int32, (2 * w, w), 0)
        col = lax.broadcasted_iota(jnp.int32, (2 * w, w), 1)
        band = (col <= row) & (row < col + w)
        has_next = j + 1 < nstep
        for wi in range(nw):
            valid = band if wi < nw - 1 else band & ((row < w) | has_next)
            keys, qrows = slice(wi * w, (wi + 1) * w), slice(wi * w, (wi + 2) * w)
            kw, vw = k_ref[keys, :], v_ref[keys, :]
            dk = jnp.zeros((w, LANES), F32)
            dv = jnp.zeros((w, LANES), F32)
            for pr in range(Q_PER_KVPAIR // LANES):
                gh = pr // 2
                cols = slice(pr * LANES, (pr + 1) * LANES)
                prod = docat[qrows, cols].astype(F32) * ocat[qrows, cols].astype(F32)
                lse2 = lcat[qrows, cols]
                to_kv = _head_mask(prod.shape, gh)
                for hh in range(2):
                    q_src, do_src = (qcat, docat) if hh == gh else (qsw, dosw)
                    q_al = jnp.where(to_kv, q_src[qrows, cols].astype(F32) * SCALE, 0.0).astype(BF16)
                    do_al = jnp.where(to_kv, do_src[qrows, cols], jnp.zeros((2 * w, LANES), BF16))
                    sc = jnp.where(valid, lax.dot_general(q_al, kw, _DN_NT, preferred_element_type=F32), NEG_INF)
                    pm = jnp.exp(sc - lse2[:, HEAD_DIM * hh: HEAD_DIM * hh + 1])
                    delta = jnp.sum(jnp.where(_head_mask(prod.shape, hh), prod, 0.0), axis=1, keepdims=True)
                    ds = pm * (lax.dot_general(do_al, vw, _DN_NT, preferred_element_type=F32) - delta)
                    dk = dk + lax.dot_general(ds.astype(BF16), q_al, _DN_TN, preferred_element_type=F32)
                    dv = dv + lax.dot_general(pm.astype(BF16), do_al, _DN_TN, preferred_element_type=F32)
            dk_ref[keys, :] = dk.astype(BF16)
            dv_ref[keys, :] = dv.astype(BF16)

    kbase, vbase = Q_W // LANES, (Q_W + KV_W) // LANES
    kv = lambda base: pl.BlockSpec((ts, LANES), lambda kp, j: (j, base + kp))
    same = pl.BlockSpec((ts, Q_PER_KVPAIR), lambda kp, j: (j, kp))
    nxt = pl.BlockSpec((w, Q_PER_KVPAIR), lambda kp, j: (jnp.minimum((j + 1) * nw, last_window), kp))
    out = pl.BlockSpec((ts, LANES), lambda kp, j: (j, kp))
    shp = jax.ShapeDtypeStruct((s, KV_W), BF16)
    return pl.pallas_call(
        body, name="swa_bwd_dkv", grid=(KV_PAIRS, nstep),
        out_shape=(shp, shp),
        in_specs=[kv(kbase), kv(vbase), same, nxt, same, nxt, same, nxt, same, nxt], out_specs=(out, out),
    )(qk, vsrc, qk, qk, do, do, o, o, lse, lse)


def _ffn_ple_fwd(h1, p_l, g_ffn, g_ple, w_gu, w_down, w_pg, w_pp, tag, next_gain=None):
    hn2 = _rms_fwd(h1, g_ffn, f"rms_ffn_{tag}")
    ab, u = _gate_up_swiglu(hn2, w_gu, f"mm_gate_up_swiglu_{tag}")
    h2 = _mm(u, w_down, out_dtype=F32, res=h1, name=f"mm_down_{tag}")
    hn3 = _rms_fwd(h2, g_ple, f"rms_ple_{tag}")
    gl = _mm(hn3, w_pg, name=f"mm_ple_gate_{tag}")
    pp = _mm(p_l, w_pp, name=f"mm_ple_proj_{tag}")
    h3 = _ple_fwd(h2, gl, pp, f"ple_{tag}", next_gain)
    return h3, dict(h1=h1, hn2=hn2, ab=ab, u=u, h2=h2, hn3=hn3, gl=gl, pp=pp)


def _ffn_ple_bwd(dh3, sv, p_l, g_ffn, g_ple, w_gu, w_down, w_pg, tag):
    dgl, dpp = _ple_bwd(dh3, sv["gl"], sv["pp"], f"ple_bwd_{tag}")
    d_wpp = _mm(p_l, dpp, ta=True, out_dtype=F32, name=f"mm_dw_ple_proj_{tag}")
    d_wpg = _mm(sv["hn3"], dgl, ta=True, out_dtype=F32, name=f"mm_dw_ple_gate_{tag}")
    dhn3 = _mm(dgl, w_pg, tb=True, out_dtype=F32, name=f"mm_dx_ple_gate_{tag}")
    dh2, dh2b, dg_ple = _rms_bwd(sv["h2"], g_ple, [dhn3], dh3, f"rms_ple_bwd_{tag}", True)
    d_wdown = _mm(sv["u"], dh2b, ta=True, out_dtype=F32, name=f"mm_dw_down_{tag}")
    dab = _down_t_swiglu_bwd(dh2b, w_down, sv["ab"], f"mm_dx_down_swiglu_bwd_{tag}")
    d_wgu = _mm(sv["hn2"], dab, ta=True, out_dtype=F32, group="out", group_width=2 * HID_PAD,
                name=f"mm_dw_gate_up_{tag}")
    dhn2 = _mm(dab, w_gu, tb=True, out_dtype=F32, group="k", name=f"mm_dx_gate_up_{tag}")
    dh1, dh1b, dg_ffn = _rms_bwd(sv["h1"], g_ffn, [dhn2], dh2, f"rms_ffn_bwd_{tag}", True)
    return dh1, dh1b, dict(d_wpp=d_wpp, d_wpg=d_wpg, d_wdown=d_wdown, d_wgu=d_wgu, dg_ple=dg_ple, dg_ffn=dg_ffn)


def _row_form(cum, blk):
    s = cum.shape[0]
    return cum[:, :N_FOX].T.reshape(N_FOX, s // blk, 1, blk)


def _col_form(cum):
    s = cum.shape[0]
    return jnp.broadcast_to(cum[:, :N_FOX].T[:, :, None], (N_FOX, s, LANES))


def kernel(x, p, positions, norm_mix, norm_ffn, norm_ple, norm_final, ev_w_in, ev_b_f, ev_w_out, od_w_in, od_sinks, od_w_out, ffn_w_gate, ffn_w_up, ffn_w_down, ple_w_proj, ple_w_gate, loss_target, m_norm_mix, m_norm_ffn, m_norm_ple, m_norm_final, m_ev_w_in, m_ev_b_f, m_ev_w_out, m_od_w_in, m_od_sinks, m_od_w_out, m_ffn_w_gate, m_ffn_w_up, m_ffn_w_down, m_ple_w_proj, m_ple_w_gate, v_norm_mix, v_norm_ffn, v_norm_ple, v_norm_final, v_ev_w_in, v_ev_b_f, v_ev_w_out, v_od_w_in, v_od_sinks, v_od_w_out, v_ffn_w_gate, v_ffn_w_up, v_ffn_w_down, v_ple_w_proj, v_ple_w_gate):
    s = x.shape[1]
    blk = min(ATTN_BLOCK, s)
    big_w = [ev_w_in, ev_w_out, od_w_in, od_w_out, ffn_w_gate, ffn_w_up, ffn_w_down, ple_w_proj, ple_w_gate]
    big_m = [m_ev_w_in, m_ev_w_out, m_od_w_in, m_od_w_out, m_ffn_w_gate, m_ffn_w_up, m_ffn_w_down, m_ple_w_proj, m_ple_w_gate]
    big_v = [v_ev_w_in, v_ev_w_out, v_od_w_in, v_od_w_out, v_ffn_w_gate, v_ffn_w_up, v_ffn_w_down, v_ple_w_proj, v_ple_w_gate]

    s_rows, s_gu, s_evin, s_odin, s_pp = _group_shards(*big_w, BF16)
    (g_evin,) = _all_gather_weights([s_evin])
    d = D_MODEL
    w_in0 = _from_owner_cols(g_evin)
    w_qkv = w_in0[:, :QKV_W]
    w_f = jnp.pad(w_in0[:, QKV_W:], ((0, 0), (0, LANES - N_FOX)))

    h0 = x[0]
    target = loss_target[0]
    p_b = [p[l, 0].astype(BF16) for l in range(2)]
    g_mix = [norm_mix[l][None, :] for l in range(2)]
    g_ffn = [norm_ffn[l][None, :] for l in range(2)]
    g_ple = [norm_ple[l][None, :] for l in range(2)]
    b_f = jnp.pad(ev_b_f, ((0, 0), (0, LANES - N_FOX)))

    half = HEAD_DIM // 2
    inv = ROPE_THETA ** (-jnp.arange(half, dtype=F32) / half)
    ang = positions[0].astype(F32)[:, None] * inv
    cos_t = jnp.tile(jnp.cos(ang), (1, 4))
    sin_t = jnp.tile(jnp.concatenate([-jnp.sin(ang), jnp.sin(ang)], axis=1), (1, 2))

    hn1 = _rms_fwd(h0, g_mix[0], "rms_mix_0")
    proj0 = _mm(hn1, w_qkv, name="mm_in_0")
    flog = _mm(hn1, w_f, out_dtype=F32, name="mm_fgate_0")
    cum = _fgate_fwd(flog, b_f)
    crow = _row_form(cum, blk)
    o_fox, acol, (g_rows, g_gu, g_odin, g_pp) = _fox_fwd(proj0, _col_form(cum), crow, [s_rows, s_gu, s_odin, s_pp])
    w_oi = _from_owner_cols(g_odin)
    w_eo = g_rows[:, 0:128].reshape(d, d)
    w_oo = g_rows[:, 128:256].reshape(d, d)
    w_down = [g_rows[:, 256 + HID_PAD * l: 256 + HID_PAD * (l + 1)].reshape(D_FF_PAD, d) for l in range(2)]
    w_pg = [g_rows[:, 1024 + 128 * l: 1152 + 128 * l].reshape(d, d) for l in range(2)]
    w_gu = [g_gu[:, d * l: d * (l + 1)] for l in range(2)]
    w_pp = [_from_owner_cols(g_pp[:, PLE_DIM * l: PLE_DIM * (l + 1)]) for l in range(2)]
    o_sb, rsave = _sb_fwd(proj0)
    o0 = jnp.concatenate([o_fox, o_sb], axis=1)
    h1 = _mm(o0, w_eo, out_dtype=F32, res=h0, name="mm_out_0")
    (h3, hn1b), sv0 = _ffn_ple_fwd(h1, p_b[0], g_ffn[0], g_ple[0], w_gu[0], w_down[0], w_pg[0], w_pp[0], "0",
                                   g_mix[1])

    proj1 = _mm(hn1b, w_oi, name="mm_in_1")
    qk_r = _rope(proj1, Q_W + KV_W, cos_t, sin_t, 1.0, "rope_fwd")
    sink_row = jnp.repeat(od_sinks[0], HEAD_DIM)[None, :]
    o1, lse1 = _swa_fwd(qk_r, proj1, sink_row)
    h4 = _mm(o1, w_oo, out_dtype=F32, res=h3, name="mm_out_1")
    h6, sv1 = _ffn_ple_fwd(h4, p_b[1], g_ffn[1], g_ple[1], w_gu[1], w_down[1], w_pg[1], w_pp[1], "1")

    loss_part, dh6, dg_final = _final_norm_loss(h6, norm_final[None, :], target)

    dh4, dh4b, gr1 = _ffn_ple_bwd(dh6, sv1, p_b[1], g_ffn[1], g_ple[1], w_gu[1], w_down[1], w_pg[1], "1")
    do1 = _mm(dh4b, w_oo, tb=True, name="mm_dx_out_1")
    d_woo = _mm(o1, dh4b, ta=True, out_dtype=F32, name="mm_dw_out_1")
    dq1, dsink_row = _swa_bwd_dq(qk_r, proj1, sink_row, do1, o1, lse1)
    dk1, dv1 = _swa_bwd_dkv(qk_r, proj1, do1, o1, lse1)
    dqk = _rope(jnp.concatenate([dq1, dk1], axis=1), Q_W + KV_W, cos_t, sin_t, -1.0, "rope_bwd")
    dproj1 = jnp.concatenate([dqk, dv1], axis=1)
    d_woi = _mm(hn1b, dproj1, ta=True, out_dtype=F32, name="mm_dw_in_1")
    dhn1b = _mm(dproj1, w_oi, tb=True, out_dtype=F32, name="mm_dx_in_1")
    dh3, dg_mix1 = _rms_bwd(h3, g_mix[1], [dhn1b], dh4, "rms_mix_bwd_1", False)

    dh1, dh1b, gr0 = _ffn_ple_bwd(dh3, sv0, p_b[0], g_ffn[0], g_ple[0], w_gu[0], w_down[0], w_pg[0], "0")
    do0 = _mm(dh1b, w_eo, tb=True, name="mm_dx_out_0")
    d_weo = _mm(o0, dh1b, ta=True, out_dtype=F32, name="mm_dw_out_0")
    by_rows = lambda g, r: g.reshape(N_DEV, r, d)
    early = [
        jnp.concatenate([by_rows(d_weo, 128), by_rows(d_woo, 128), by_rows(gr0["d_wdown"], HID_PAD),
                         by_rows(gr1["d_wdown"], HID_PAD), by_rows(gr0["d_wpg"], 128), by_rows(gr1["d_wpg"], 128)],
                        axis=1),
        jnp.concatenate([gr0["d_wgu"], gr1["d_wgu"]], axis=1),
        _by_owner_cols(d_woi),
        jnp.concatenate([_by_owner_cols(gr0["d_wpp"]), _by_owner_cols(gr1["d_wpp"])], axis=1),
    ]
    early_tags = ("rows", "gu", "od_in", "pp")
    core = lax.axis_index("c").astype(jnp.int32).reshape(1)
    dq_f, dk_f, dv_f, gc, early_sib = _fox_bwd(proj0, do0, o0, acol, crow, early)
    early_part = [_rs_chip_sum(core, g, r, f"rs_chip_sum_{t}") for g, r, t in zip(early, early_sib, early_tags)]
    dq_s, dk_s, dv_s, early_recv = _sb_bwd(proj0, do0, rsave, early_part)
    dproj0 = jnp.concatenate([dq_f, dk_f, dv_f, dq_s, dk_s, dv_s], axis=1)
    gcum = jnp.pad(gc.reshape(N_FOX, s).T, ((0, 0), (0, LANES - N_FOX)))
    dflog, db_f = _fgate_bwd(gcum, flog, b_f)
    d_wqkv = _mm(hn1, dproj0, ta=True, out_dtype=F32, name="mm_dw_in_0")
    d_wf = _mm(hn1, dflog, ta=True, out_dtype=F32, name="mm_dw_fgate_0")
    dhn1 = _mm(dproj0, w_qkv, tb=True, out_dtype=F32, name="mm_dx_in_0")
    dhn1f = _mm(dflog, w_f, tb=True, out_dtype=F32, name="mm_dx_fgate_0")
    grad_x, dg_mix0 = _rms_bwd(h0, g_mix[0], [dhn1, dhn1f], dh1, "rms_mix_bwd_0", False)

    late = [_by_owner_cols(jnp.concatenate([d_wqkv, d_wf[:, :N_FOX]], axis=1))]
    late_sib = _rs_sibling_exchange(late)
    late_recv = _rs_chip_exchange([_rs_chip_sum(core, late[0], late_sib[0], "rs_chip_sum_ev_in")])
    tags = ("rows", "gu", "ev_in", "od_in", "pp")
    chip_recv = [early_recv[0], early_recv[1], late_recv[0], early_recv[2], early_recv[3]]
    w_grp, m_grp, v_grp = (_group_shards(*ws, F32) for ws in (big_w, big_m, big_v))
    updated = [_rs_sum_adamw(r, w_, m_, v_, f"rs_sum_adamw_{t}")
               for r, w_, m_, v_, t in zip(chip_recv, w_grp, m_grp, v_grp, tags)]
    big_g, big_d, big_nm, big_nv = (_ungroup_shards([u[k] for u in updated]) for k in range(4))

    def small_pack(nmix, nffn, nple, nfin, bf, sk, extra):
        last = jnp.concatenate([bf.reshape(-1), sk.reshape(-1), extra.reshape(-1)])
        last = jnp.pad(last, (0, D_MODEL - last.shape[0]))
        return jnp.concatenate([nmix, nffn, nple, nfin.reshape(1, -1), last[None, :]], axis=0)

    small_g = small_pack(jnp.concatenate([dg_mix0, dg_mix1]), jnp.concatenate([gr0["dg_ffn"], gr1["dg_ffn"]]),
                         jnp.concatenate([gr0["dg_ple"], gr1["dg_ple"]]), dg_final,
                         db_f[0, :N_FOX], dsink_row[0, ::HEAD_DIM], loss_part[0, :1])
    zero1 = jnp.zeros((1,), F32)
    small_w = small_pack(norm_mix, norm_ffn, norm_ple, norm_final, ev_b_f, od_sinks, zero1)
    small_m = small_pack(m_norm_mix, m_norm_ffn, m_norm_ple, m_norm_final, m_ev_b_f, m_od_sinks, zero1)
    small_v = small_pack(v_norm_mix, v_norm_ffn, v_norm_ple, v_norm_final, v_ev_b_f, v_od_sinks, zero1)
    sg, sd, sm, sv_ = _small_allreduce_adamw(small_g, small_w, small_m, small_v)

    def small_unpack(t):
        return [t[0:2], t[2:4], t[4:6], t[6], t[7, :N_FOX][None, :], t[7, N_FOX:N_FOX + N_Q][None, :]]

    loss = sg[7, N_FOX + N_Q]

    def ordered(small, big):
        nm, nf, npl, nfin, bf, sk = small_unpack(small)
        ev_in, ev_out, od_in, od_out, fg, fu, fd, pproj, pgate = big
        return [nm, nf, npl, nfin, ev_in, bf, ev_out, od_in, sk, od_out, fg, fu, fd, pproj, pgate]

    return (loss, grad_x[None], *ordered(sg, big_g), *ordered(sd, big_d),
            *ordered(sm, big_nm), *ordered(sv_, big_nv))
```
